```python
import jax, jax.numpy as jnp
from jax import lax
import numpy as np

D_MODEL = 2048
BATCH = 4
SEQ = 2048
DEPTH = 2

HEAD_DIM = 128
N_HEADS = D_MODEL // HEAD_DIM
MLA_HEADS = N_HEADS // 4
DIL_HEADS = (N_HEADS - MLA_HEADS) // 2
NSA_HEADS = N_HEADS - MLA_HEADS - DIL_HEADS
MLA_Q_RANK = 384
MLA_KV_RANK = 128
MLA_NOPE_DIM = 128
MLA_ROPE_DIM = 64
MLA_V_DIM = HEAD_DIM
DIL_PAIRS = ((128, 1), (512, 4), (2048, 16))
NSA_CMP_LEN = 32
NSA_CMP_STRIDE = 16
NSA_CMP_HIDDEN = 256
NSA_SEL_BLOCK = 64
NSA_TOP_N = 16
NSA_WINDOW = 512
NSA_BRANCHES = 3
NSA_FORCE_SCORE = 1e4
ROPE_THETA = 500000.0
ROT_DIM = HEAD_DIM // 4
N_GROUPS = 4
EXPERTS_PER_GROUP = 8
N_EXPERTS = N_GROUPS * EXPERTS_PER_GROUP
EXPERT_FF = 512
MOE_TOP_K = 2
MOE_ROWS = 256
Q_BLOCK = 128
DN_ALPHA = (2 * DEPTH) ** 0.25
DN_BETA = (8 * DEPTH) ** -0.25
LN_EPS = 1e-5
RMS_EPS = 1e-6
NEG_INF = -1e30
D_MIX = MLA_HEADS * MLA_V_DIM + (DIL_HEADS + NSA_HEADS) * HEAD_DIM
IN_SPLITS = (MLA_Q_RANK, MLA_KV_RANK, MLA_ROPE_DIM,
             DIL_HEADS * HEAD_DIM, DIL_HEADS * HEAD_DIM, DIL_HEADS * HEAD_DIM,
             NSA_HEADS * HEAD_DIM, HEAD_DIM, HEAD_DIM, HEAD_DIM, HEAD_DIM, HEAD_DIM, HEAD_DIM,
             NSA_HEADS * NSA_BRANCHES)
IN_COLS = sum(IN_SPLITS)

kernel_name = 'hybrid_mla_dilated_nsa_hmoe'


def layer_norm(x, g, b):
    xf = x.astype(jnp.float32)
    mu = jnp.mean(xf, -1, keepdims=True)
    var = jnp.mean(jnp.square(xf - mu), -1, keepdims=True)
    return ((xf - mu) * lax.rsqrt(var + LN_EPS) * g.astype(jnp.float32) + b.astype(jnp.float32)).astype(x.dtype)


def rms_norm(x, g):
    xf = x.astype(jnp.float32)
    return (xf * lax.rsqrt(jnp.mean(jnp.square(xf), -1, keepdims=True) + RMS_EPS) * g.astype(jnp.float32)).astype(x.dtype)


def rope(x, pos, rot_dim):
    half = rot_dim // 2
    inv_freq = ROPE_THETA ** (-jnp.arange(half, dtype=jnp.float32) / half)
    ang = pos.astype(jnp.float32)[:, None] * inv_freq[None, :]
    cos = jnp.cos(ang)[:, None, :]
    sin = jnp.sin(ang)[:, None, :]
    x1 = x[..., :half].astype(jnp.float32)
    x2 = x[..., half:rot_dim].astype(jnp.float32)
    rot = jnp.concatenate([x1 * cos - x2 * sin, x2 * cos + x1 * sin], axis=-1).astype(x.dtype)
    return jnp.concatenate([rot, x[..., rot_dim:]], axis=-1)


def split_cols(h, widths):
    cuts = [int(c) for c in np.cumsum(widths)[:-1]]
    return jnp.split(h, cuts, axis=-1)


def causal_block_attention(q, k, v):
    B, S, H, dq = q.shape
    nq = S // Q_BLOCK
    scale = dq ** -0.5
    qb = q.reshape(B, nq, Q_BLOCK, H, dq).swapaxes(0, 1)
    kpos = jnp.arange(S)

    def one_block(args):
        qi, i = args
        s = jnp.einsum('bqhd,bkhd->bhqk', qi, k).astype(jnp.float32) * scale
        qpos = i * Q_BLOCK + jnp.arange(Q_BLOCK)
        mask = kpos[None, :] <= qpos[:, None]
        p = jax.nn.softmax(jnp.where(mask, s, NEG_INF), axis=-1)
        return jnp.einsum('bhqk,bkhd->bqhd', p.astype(v.dtype), v)

    out = lax.map(one_block, (qb, jnp.arange(nq)))
    return out.swapaxes(0, 1).reshape(B, S, H, v.shape[-1])


def banded_attention(q, k, v, max_dist):
    N, L, H, d = q.shape
    Hk = k.shape[2]
    R = H // Hk
    dv = v.shape[-1]
    nblk = -(-L // Q_BLOCK)
    Lp = nblk * Q_BLOCK
    nprev = -(-max_dist // Q_BLOCK)
    tail = Lp - L
    qp = jnp.pad(q, ((0, 0), (0, tail), (0, 0), (0, 0)))
    kp = jnp.pad(k, ((0, 0), (nprev * Q_BLOCK, tail), (0, 0), (0, 0)))
    vp = jnp.pad(v, ((0, 0), (nprev * Q_BLOCK, tail), (0, 0), (0, 0)))
    kb = kp.reshape(N, nprev + nblk, Q_BLOCK, Hk, d)
    vb = vp.reshape(N, nprev + nblk, Q_BLOCK, Hk, dv)
    kw = jnp.concatenate([kb[:, j:j + nblk] for j in range(nprev + 1)], axis=2)
    vw = jnp.concatenate([vb[:, j:j + nblk] for j in range(nprev + 1)], axis=2)
    qb = qp.reshape(N, nblk, Q_BLOCK, Hk, R, d)
    s = jnp.einsum('nbqgrd,nbkgd->nbgrqk', qb, kw).astype(jnp.float32) * d ** -0.5
    qpos = jnp.arange(nblk)[:, None] * Q_BLOCK + jnp.arange(Q_BLOCK)[None, :]
    kpos = jnp.arange(nblk)[:, None] * Q_BLOCK - nprev * Q_BLOCK + jnp.arange((nprev + 1) * Q_BLOCK)[None, :]
    dist = qpos[:, :, None] - kpos[:, None, :]
    mask = (dist >= 0) & (dist <= max_dist) & (kpos[:, None, :] >= 0)
    s = jnp.where(mask[None, :, None, None], s, NEG_INF)
    lse = jax.nn.logsumexp(s, axis=-1, keepdims=True)
    p = jnp.exp(s - lse)
    o = jnp.einsum('nbgrqk,nbkgd->nbqgrd', p.astype(v.dtype), vw).reshape(N, Lp, H, dv)[:, :L]
    lse = lse[..., 0].transpose(0, 1, 4, 2, 3).reshape(N, Lp, H)[:, :L]
    return o, lse


def mla_attention(q_lat, kv_lat, k_rope, q_lat_norm, w_q_up, kv_lat_norm, w_kv_up, pos):
    B, S, _ = q_lat.shape
    q = (rms_norm(q_lat, q_lat_norm) @ w_q_up).reshape(B, S, MLA_HEADS, MLA_NOPE_DIM + MLA_ROPE_DIM)
    q = jnp.concatenate([q[..., :MLA_NOPE_DIM], rope(q[..., MLA_NOPE_DIM:], pos, MLA_ROPE_DIM)], -1)
    kv = (rms_norm(kv_lat, kv_lat_norm) @ w_kv_up).reshape(B, S, MLA_HEADS, MLA_NOPE_DIM + MLA_V_DIM)
    k_nope, v = kv[..., :MLA_NOPE_DIM], kv[..., MLA_NOPE_DIM:]
    k_pe = rope(k_rope[:, :, None, :], pos, MLA_ROPE_DIM)
    k = jnp.concatenate([k_nope, jnp.broadcast_to(k_pe, (B, S, MLA_HEADS, MLA_ROPE_DIM))], -1)
    return causal_block_attention(q, k, v)


def dilated_attention(q, k, v):
    B, S, H, d = q.shape
    outs, lses = [], []
    for window, dil in DIL_PAIRS:
        L = S // dil

        def to_sub(t):
            return t.reshape(B, L, dil, H, d).transpose(0, 2, 1, 3, 4).reshape(B * dil, L, H, d)

        o, lse = banded_attention(to_sub(q), to_sub(k), to_sub(v), window // dil)
        outs.append(o.reshape(B, dil, L, H, d).transpose(0, 2, 1, 3, 4).reshape(B, S, H, d))
        lses.append(lse.reshape(B, dil, L, H).transpose(0, 2, 1, 3).reshape(B, S, H))
    wts = jax.nn.softmax(jnp.stack(lses, 0), axis=0)
    return jnp.einsum('nbsh,nbshd->bshd', wts.astype(q.dtype), jnp.stack(outs, 0))


def selected_block_attention(q, k, v, sel_idx):
    B, S, H, d = q.shape
    n_blocks = S // NSA_SEL_BLOCK
    n = sel_idx.shape[-1]
    kb = k.reshape(B, n_blocks, NSA_SEL_BLOCK, d)
    vb = v.reshape(B, n_blocks, NSA_SEL_BLOCK, v.shape[-1])
    nq = S // Q_BLOCK
    qc = q.reshape(B, nq, Q_BLOCK, H, d).swapaxes(0, 1)
    ic = sel_idx.reshape(B, nq, Q_BLOCK, n).swapaxes(0, 1)
    offs = jnp.arange(NSA_SEL_BLOCK)
    gather = jax.vmap(lambda blocks, ids: blocks[ids])

    def one_chunk(args):
        qi, ii, c = args
        kg = gather(kb, ii).reshape(B, Q_BLOCK, n * NSA_SEL_BLOCK, d)
        vg = gather(vb, ii).reshape(B, Q_BLOCK, n * NSA_SEL_BLOCK, v.shape[-1])
        kpos = (ii[..., None] * NSA_SEL_BLOCK + offs).reshape(B, Q_BLOCK, n * NSA_SEL_BLOCK)
        qpos = c * Q_BLOCK + jnp.arange(Q_BLOCK)
        mask = kpos <= qpos[None, :, None]
        s = jnp.einsum('bqhd,bqkd->bqhk', qi, kg).astype(jnp.float32) * d ** -0.5
        p = jax.nn.softmax(jnp.where(mask[:, :, None, :], s, NEG_INF), axis=-1)
        return jnp.einsum('bqhk,bqkd->bqhd', p.astype(v.dtype), vg)

    out = lax.map(one_chunk, (qc, ic, jnp.arange(nq)))
    return out.swapaxes(0, 1).reshape(B, S, H, v.shape[-1])


def nsa_attention(q, k_cmp, v_cmp, k_slc, v_slc, k_win, v_win, gates,
                  cmp_pos_k, cmp_w1_k, cmp_w2_k, cmp_pos_v, cmp_w1_v, cmp_w2_v, pos):
    B, S, H, d = q.shape
    q = rope(q, pos, ROT_DIM)
    n_chunk = S // NSA_CMP_STRIDE
    n_cmp = n_chunk - 1

    def compress(t, pe, w1, w2):
        ch = t.reshape(B, n_chunk, NSA_CMP_STRIDE, d)
        blocks = jnp.concatenate([ch[:, :-1], ch[:, 1:]], axis=2)
        h = jax.nn.gelu((blocks + pe).reshape(B, n_cmp, NSA_CMP_LEN * d) @ w1)
        return h @ w2

    cmp_start = jnp.arange(n_cmp) * NSA_CMP_STRIDE
    cmp_end = cmp_start + NSA_CMP_LEN - 1
    kc = rope(compress(k_cmp, cmp_pos_k, cmp_w1_k, cmp_w2_k)[:, :, None, :], cmp_end, ROT_DIM)[:, :, 0]
    vc = compress(v_cmp, cmp_pos_v, cmp_w1_v, cmp_w2_v)
    valid_c = cmp_end[None, :] <= pos[:, None]
    s = jnp.einsum('bshd,bcd->bhsc', q, kc).astype(jnp.float32) * d ** -0.5
    p_cmp = jax.nn.softmax(jnp.where(valid_c, s, NEG_INF), axis=-1) * valid_c
    o_cmp = jnp.einsum('bhsc,bcd->bshd', p_cmp.astype(q.dtype), vc)
    n_sel_blocks = S // NSA_SEL_BLOCK
    sel_start = jnp.arange(n_sel_blocks) * NSA_SEL_BLOCK
    cover = ((cmp_start[:, None] < sel_start[None, :] + NSA_SEL_BLOCK) &
             (cmp_start[:, None] + NSA_CMP_LEN > sel_start[None, :])).astype(jnp.float32)
    imp = jnp.einsum('bhsc,cj->bsj', p_cmp, cover)
    jj = jnp.arange(n_sel_blocks)[None, :]
    qblk = (pos // NSA_SEL_BLOCK)[:, None]
    valid_s = sel_start[None, :] <= pos[:, None]
    forced = (jj == 0) | (jj == qblk) | (jj == qblk - 1)
    score = jnp.where(valid_s, jnp.where(forced, NSA_FORCE_SCORE, imp), -1.0)
    _, sel_idx = lax.top_k(score, min(NSA_TOP_N, n_sel_blocks))
    o_slc = selected_block_attention(q, rope(k_slc[:, :, None, :], pos, ROT_DIM)[:, :, 0], v_slc, sel_idx)
    o_win, _ = banded_attention(q, rope(k_win[:, :, None, :], pos, ROT_DIM), v_win[:, :, None, :], NSA_WINDOW - 1)
    g = jax.nn.sigmoid(gates.astype(jnp.float32)).reshape(B, S, H, NSA_BRANCHES).astype(q.dtype)
    return g[..., 0:1] * o_cmp + g[..., 1:2] * o_slc + g[..., 2:3] * o_win


def hybrid_mixer(x, w_in, q_lat_norm, w_q_up, kv_lat_norm, w_kv_up,
                 cmp_pos_k, cmp_w1_k, cmp_w2_k, cmp_pos_v, cmp_w1_v, cmp_w2_v, w_out):
    B, S, _ = x.shape
    pos = jnp.arange(S, dtype=jnp.int32)
    (q_lat, kv_lat, k_rope, dq, dk, dv, nq, nkc, nvc, nks, nvs, nkw, nvw, ngate) = split_cols(x @ w_in, IN_SPLITS)

    def heads(t, h):
        return t.reshape(B, S, h, -1)

    o_mla = mla_attention(q_lat, kv_lat, k_rope, q_lat_norm, w_q_up, kv_lat_norm, w_kv_up, pos)
    o_dil = dilated_attention(rope(heads(dq, DIL_HEADS), pos, ROT_DIM),
                              rope(heads(dk, DIL_HEADS), pos, ROT_DIM), heads(dv, DIL_HEADS))
    o_nsa = nsa_attention(heads(nq, NSA_HEADS), nkc, nvc, nks, nvs, nkw, nvw, ngate,
                          cmp_pos_k, cmp_w1_k, cmp_w2_k, cmp_pos_v, cmp_w1_v, cmp_w2_v, pos)
    o = jnp.concatenate([o_mla.reshape(B, S, -1), o_dil.reshape(B, S, -1), o_nsa.reshape(B, S, -1)], -1)
    return o @ w_out


def hier_moe(x, w_grp, w_exp, w_gate, w_up, w_down):
    B, S, D = x.shape
    T = B * S
    xf = x.reshape(T, D)
    grp_prob = jax.nn.softmax((xf @ w_grp).astype(jnp.float32), axis=-1)
    g_idx = jnp.argmax(grp_prob, axis=-1)
    p_g = jnp.take_along_axis(grp_prob, g_idx[:, None], 1)[:, 0]
    e_logits = (xf @ w_exp).astype(jnp.float32).reshape(T, N_GROUPS, EXPERTS_PER_GROUP)
    e_logits = jnp.take_along_axis(e_logits, g_idx[:, None, None], 1)[:, 0]
    top_v, top_i = lax.top_k(e_logits, MOE_TOP_K)
    gate = p_g[:, None] * jax.nn.softmax(top_v, axis=-1)
    eid = g_idx[:, None].astype(jnp.int32) * EXPERTS_PER_GROUP + top_i.astype(jnp.int32)
    M = T * MOE_TOP_K
    n_blocks = -(-M // MOE_ROWS) + N_EXPERTS
    tok = jnp.repeat(jnp.arange(T, dtype=jnp.int32), MOE_TOP_K)
    ex = eid.reshape(M)
    order = jnp.argsort(ex)
    ex_s, tok_s, gw_s = ex[order], tok[order], gate.reshape(M)[order]
    counts = jnp.zeros((N_EXPERTS,), jnp.int32).at[ex].add(1)
    seg_start = jnp.cumsum(counts) - counts
    padded = (counts + MOE_ROWS - 1) // MOE_ROWS * MOE_ROWS
    pad_end = jnp.cumsum(padded)
    dest = (pad_end - padded)[ex_s] + jnp.arange(M, dtype=jnp.int32) - seg_start[ex_s]
    buf_tok = jnp.full((n_blocks * MOE_ROWS,), T, jnp.int32).at[dest].set(tok_s)
    buf_w = jnp.zeros((n_blocks * MOE_ROWS,), x.dtype).at[dest].set(gw_s.astype(x.dtype))
    blk_start = jnp.arange(n_blocks, dtype=jnp.int32) * MOE_ROWS
    blk_ex = jnp.minimum(jnp.searchsorted(pad_end, blk_start, side='right'), N_EXPERTS - 1)
    x_pad = jnp.concatenate([xf, jnp.zeros((1, D), xf.dtype)], axis=0)
    xb = x_pad[buf_tok].reshape(n_blocks, MOE_ROWS, D)

    def expert_rows(args):
        rows, e = args
        h = jax.nn.silu(rows @ w_gate[e]) * (rows @ w_up[e])
        return h @ w_down[e]

    yb = lax.map(expert_rows, (xb, blk_ex)).reshape(n_blocks * MOE_ROWS, D)
    out = jnp.zeros((T + 1, D), x.dtype).at[buf_tok].add(yb * buf_w[:, None])[:T]
    return out.reshape(B, S, D)


def setup_inputs(seed: int = 0) -> dict:
    key = jax.random.key(seed)
    ks = jax.random.split(key, 24)
    L = DEPTH

    def nrm(k, shape, scale):
        return jax.random.normal(k, shape, jnp.float32) * scale

    cmp_in = NSA_CMP_LEN * HEAD_DIM
    return {
        'x': nrm(ks[0], (BATCH, SEQ, D_MODEL), 1.0),
        'w_in': nrm(ks[1], (L, D_MODEL, IN_COLS), D_MODEL ** -0.5),
        'q_lat_norm': 1.0 + nrm(ks[2], (L, MLA_Q_RANK), 0.02),
        'w_q_up': nrm(ks[3], (L, MLA_Q_RANK, MLA_HEADS * (MLA_NOPE_DIM + MLA_ROPE_DIM)), MLA_Q_RANK ** -0.5),
        'kv_lat_norm': 1.0 + nrm(ks[4], (L, MLA_KV_RANK), 0.02),
        'w_kv_up': nrm(ks[5], (L, MLA_KV_RANK, MLA_HEADS * (MLA_NOPE_DIM + MLA_V_DIM)), MLA_KV_RANK ** -0.5),
        'cmp_pos_k': nrm(ks[6], (L, NSA_CMP_LEN, HEAD_DIM), 0.1),
        'cmp_w1_k': nrm(ks[7], (L, cmp_in, NSA_CMP_HIDDEN), cmp_in ** -0.5),
        'cmp_w2_k': nrm(ks[8], (L, NSA_CMP_HIDDEN, HEAD_DIM), NSA_CMP_HIDDEN ** -0.5),
        'cmp_pos_v': nrm(ks[9], (L, NSA_CMP_LEN, HEAD_DIM), 0.1),
        'cmp_w1_v': nrm(ks[10], (L, cmp_in, NSA_CMP_HIDDEN), cmp_in ** -0.5),
        'cmp_w2_v': nrm(ks[11], (L, NSA_CMP_HIDDEN, HEAD_DIM), NSA_CMP_HIDDEN ** -0.5),
        'w_out': nrm(ks[12], (L, D_MIX, D_MODEL), D_MIX ** -0.5 * DN_BETA),
        'ln1_g': 1.0 + nrm(ks[13], (L, D_MODEL), 0.02),
        'ln1_b': nrm(ks[14], (L, D_MODEL), 0.02),
        'w_grp': nrm(ks[15], (L, D_MODEL, N_GROUPS), D_MODEL ** -0.5),
        'w_exp': nrm(ks[16], (L, D_MODEL, N_EXPERTS), D_MODEL ** -0.5),
        'w_gate': nrm(ks[17], (L, N_EXPERTS, D_MODEL, EXPERT_FF), D_MODEL ** -0.5),
        'w_up': nrm(ks[18], (L, N_EXPERTS, D_MODEL, EXPERT_FF), D_MODEL ** -0.5),
        'w_down': nrm(ks[19], (L, N_EXPERTS, EXPERT_FF, D_MODEL), EXPERT_FF ** -0.5 * DN_BETA),
        'ln2_g': 1.0 + nrm(ks[20], (L, D_MODEL), 0.02),
        'ln2_b': nrm(ks[21], (L, D_MODEL), 0.02),
    }


def reference(x, w_in, q_lat_norm, w_q_up, kv_lat_norm, w_kv_up, cmp_pos_k, cmp_w1_k, cmp_w2_k,
              cmp_pos_v, cmp_w1_v, cmp_w2_v, w_out, ln1_g, ln1_b, w_grp, w_exp, w_gate, w_up, w_down,
              ln2_g, ln2_b):
    for l in range(DEPTH):
        mix = hybrid_mixer(x, w_in[l], q_lat_norm[l], w_q_up[l], kv_lat_norm[l], w_kv_up[l],
                           cmp_pos_k[l], cmp_w1_k[l], cmp_w2_k[l], cmp_pos_v[l], cmp_w1_v[l], cmp_w2_v[l],
                           w_out[l])
        x = layer_norm(DN_ALPHA * x + mix, ln1_g[l], ln1_b[l])
        ffn = hier_moe(x, w_grp[l], w_exp[l], w_gate[l], w_up[l], w_down[l])
        x = layer_norm(DN_ALPHA * x + ffn, ln2_g[l], ln2_b[l])
    return x
```

```python
import functools

import jax
import jax.numpy as jnp
from jax import lax
from jax.experimental import pallas as pl
from jax.experimental.pallas import tpu as pltpu

F32 = jnp.float32
BF16 = jnp.bfloat16

D_MODEL = 2048
SEQ = 2048
DEPTH = 2
HEAD_DIM = 128
LANES = 128
MLA_HEADS = 4
DIL_HEADS = 6
NSA_HEADS = 6
MLA_Q_RANK = 384
MLA_KV_RANK = 128
MLA_NOPE_DIM = 128
MLA_ROPE_DIM = 64
MLA_QK_PAD = 256
DIL_PAIRS = ((128, 1), (512, 4), (2048, 16))
NSA_CMP_LEN = 32
NSA_CMP_STRIDE = 16
NSA_CMP_HIDDEN = 256
NSA_SEL_BLOCK = 64
NSA_TOP_N = 16
NSA_WINDOW = 512
NSA_FORCE_SCORE = 1e4
ROPE_THETA = 500000.0
ROT_DIM = HEAD_DIM // 4
N_GROUPS = 4
EXPERTS_PER_GROUP = 8
N_EXPERTS = N_GROUPS * EXPERTS_PER_GROUP
EXPERT_FF = 512
MOE_TOP_K = 2
MOE_ROWS = 256
DN_ALPHA = (2 * DEPTH) ** 0.25
LN_EPS = 1e-5
RMS_EPS = 1e-6
NEG_INF = -1e30

N_CMP_PAD = SEQ // NSA_CMP_STRIDE
N_SEL_BLOCKS = SEQ // NSA_SEL_BLOCK

BLK_DQ = 0
BLK_NQ = 6
BLK_DK = 12
BLK_NKS = 18
BLK_NKW = 19
BLK_DV = 20
BLK_NKC = 26
BLK_NVC = 27
BLK_NVS = 28
BLK_NVW = 29
BLK_QLAT = 30
BLK_KVLAT = 33
BLK_MISC = 34
N_IN_BLOCKS = 36
IN_COLS_PAD = N_IN_BLOCKS * LANES
GATE_LANE0 = MLA_ROPE_DIM

INPROJ_TM = 512
INPROJ_TN = 512
N_SCALED_TILES = (BLK_DK * LANES) // INPROJ_TN
N_ROPE_TILES = (BLK_DV * LANES) // INPROJ_TN
ATT_T = 256
ROW_TM = 256

VMEM_LIMIT = 56 * 1024 * 1024


def _cparams(sem):
    return pltpu.CompilerParams(dimension_semantics=sem, vmem_limit_bytes=VMEM_LIMIT)


def _rope_tables(pos, rot_dim, keep_rest):
    half = rot_dim // 2
    inv_freq = ROPE_THETA ** (-jnp.arange(half, dtype=F32) / half)
    ang = pos.astype(F32)[:, None] * inv_freq[None, :]
    cos, sin = jnp.cos(ang), jnp.sin(ang)
    n = pos.shape[0]
    rest = jnp.full((n, LANES - rot_dim), 1.0 if keep_rest else 0.0, F32)
    c = jnp.concatenate([cos, cos, rest], axis=-1)
    s1 = jnp.concatenate([jnp.zeros((n, half), F32), sin, jnp.zeros((n, LANES - rot_dim), F32)], axis=-1)
    s2 = jnp.concatenate([-sin, jnp.zeros((n, LANES - half), F32)], axis=-1)
    return c, s1, s2


def _rope_lanes(a, c, s1, s2, half):
    return a * c + pltpu.roll(a, half, 1) * s1 + pltpu.roll(a, LANES - half, 1) * s2


def _distance_bias(mult_fn, n_diag):
    d = jnp.arange(n_diag, dtype=jnp.int32)[:, None, None] * ATT_T
    r = jnp.arange(ATT_T, dtype=jnp.int32)[None, :, None]
    c = jnp.arange(ATT_T, dtype=jnp.int32)[None, None, :]
    dist = d + r - c
    mult = mult_fn(dist)
    return jnp.where(mult > 0, jnp.log(jnp.maximum(mult, 1).astype(F32)), NEG_INF).astype(F32)


def _causal_mult(dist):
    return (dist >= 0).astype(jnp.int32)


def _dilated_mult(dist):
    m = jnp.zeros_like(dist)
    for window, dil in DIL_PAIRS:
        m = m + ((dist >= 0) & (dist % dil == 0) & (dist <= (window // dil) * dil)).astype(jnp.int32)
    return m


def _window_mult(dist):
    return ((dist >= 0) & (dist <= NSA_WINDOW - 1)).astype(jnp.int32)


def _inproj_kernel(x_ref, w_ref, c_ref, s1_ref, s2_ref, o_ref, *, scale):
    j = pl.program_id(0)
    acc = jnp.dot(x_ref[...], w_ref[...], preferred_element_type=F32)

    @pl.when(j < N_ROPE_TILES)
    def _():
        c, s1, s2 = c_ref[...], s1_ref[...], s2_ref[...]
        sc = jnp.where(j < N_SCALED_TILES, scale, 1.0).astype(F32)
        for u in range(INPROJ_TN // LANES):
            a = acc[:, u * LANES:(u + 1) * LANES]
            o_ref[:, u * LANES:(u + 1) * LANES] = (_rope_lanes(a, c, s1, s2, ROT_DIM // 2) * sc).astype(o_ref.dtype)

    @pl.when(j >= N_ROPE_TILES)
    def _():
        o_ref[...] = acc.astype(o_ref.dtype)


def _inproj(xb, w_in_p, tabs):
    t = xb.shape[0]
    tm, tn = INPROJ_TM, INPROJ_TN
    nrow = SEQ // tm
    tab_spec = pl.BlockSpec((tm, LANES), lambda j, i: (i % nrow, 0))
    return pl.pallas_call(
        functools.partial(_inproj_kernel, scale=HEAD_DIM ** -0.5),
        out_shape=jax.ShapeDtypeStruct((t, IN_COLS_PAD), BF16),
        grid=(IN_COLS_PAD // tn, t // tm),
        in_specs=[pl.BlockSpec((tm, D_MODEL), lambda j, i: (i, 0)),
                  pl.BlockSpec((D_MODEL, tn), lambda j, i: (0, j)),
                  tab_spec, tab_spec, tab_spec],
        out_specs=pl.BlockSpec((tm, tn), lambda j, i: (i, j)),
        compiler_params=_cparams(("arbitrary", "arbitrary")),
        name="inproj",
    )(xb, w_in_p, *tabs)


def _rms(xf, g):
    return xf * lax.rsqrt(jnp.mean(jnp.square(xf), axis=-1, keepdims=True) + RMS_EPS) * g


def _mlaprep_kernel(ql_ref, kvl_ref, misc_ref, gq_ref, gkv_ref, wq_ref, wkn_ref, wv_ref,
                    c_ref, s1_ref, s2_ref, q_ref, k_ref, v_ref, *, scale):
    c, s1, s2 = c_ref[...], s1_ref[...], s2_ref[...]
    half = MLA_ROPE_DIM // 2
    qn = _rms(ql_ref[...].astype(F32), gq_ref[...]).astype(BF16)
    q = jnp.dot(qn, wq_ref[...], preferred_element_type=F32)
    for h in range(MLA_HEADS):
        lo = h * MLA_QK_PAD
        q_ref[:, lo:lo + LANES] = (q[:, lo:lo + LANES] * scale).astype(q_ref.dtype)
        a = q[:, lo + LANES:lo + 2 * LANES]
        q_ref[:, lo + LANES:lo + 2 * LANES] = (_rope_lanes(a, c, s1, s2, half) * scale).astype(q_ref.dtype)
    kvn = _rms(kvl_ref[...].astype(F32), gkv_ref[...]).astype(BF16)
    kn = jnp.dot(kvn, wkn_ref[...], preferred_element_type=F32)
    v_ref[...] = jnp.dot(kvn, wv_ref[...], preferred_element_type=F32).astype(v_ref.dtype)
    kpe = _rope_lanes(misc_ref[...].astype(F32), c, s1, s2, half).astype(k_ref.dtype)
    for h in range(MLA_HEADS):
        lo = h * MLA_QK_PAD
        k_ref[:, lo:lo + LANES] = kn[:, h * LANES:(h + 1) * LANES].astype(k_ref.dtype)
        k_ref[:, lo + LANES:lo + 2 * LANES] = kpe


def _mlaprep(hproj, gq, gkv, wq_p, wkn, wv, tabs):
    t = hproj.shape[0]
    tm = ROW_TM
    nrow = SEQ // tm
    tab_spec = pl.BlockSpec((tm, LANES), lambda i: (i % nrow, 0))
    full = lambda shape: pl.BlockSpec(shape, lambda i: (0,) * len(shape))
    return pl.pallas_call(
        functools.partial(_mlaprep_kernel, scale=(MLA_NOPE_DIM + MLA_ROPE_DIM) ** -0.5),
        out_shape=(jax.ShapeDtypeStruct((t, MLA_HEADS * MLA_QK_PAD), BF16),
                   jax.ShapeDtypeStruct((t, MLA_HEADS * MLA_QK_PAD), BF16),
                   jax.ShapeDtypeStruct((t, MLA_HEADS * HEAD_DIM), BF16)),
        grid=(t // tm,),
        in_specs=[pl.BlockSpec((tm, MLA_Q_RANK), lambda i: (i, BLK_QLAT * LANES // MLA_Q_RANK)),
                  pl.BlockSpec((tm, LANES), lambda i: (i, BLK_KVLAT)),
                  pl.BlockSpec((tm, LANES), lambda i: (i, BLK_MISC)),
                  full((1, MLA_Q_RANK)), full((1, MLA_KV_RANK)),
                  full((MLA_Q_RANK, MLA_HEADS * MLA_QK_PAD)),
                  full((MLA_KV_RANK, MLA_HEADS * MLA_NOPE_DIM)),
                  full((MLA_KV_RANK, MLA_HEADS * HEAD_DIM)),
                  tab_spec, tab_spec, tab_spec],
        out_specs=(pl.BlockSpec((tm, MLA_HEADS * MLA_QK_PAD), lambda i: (i, 0)),
                   pl.BlockSpec((tm, MLA_HEADS * MLA_QK_PAD), lambda i: (i, 0)),
                   pl.BlockSpec((tm, MLA_HEADS * HEAD_DIM), lambda i: (i, 0))),
        compiler_params=_cparams(("arbitrary",)),
        name="mlaprep",
    )(hproj, hproj, hproj, gq, gkv, wq_p, wkn, wv, *tabs)


def _gelu_tanh(x):
    return 0.5 * x * (1.0 + jnp.tanh(0.7978845608028654 * (x + 0.044715 * (x * x * x))))


def _compress_kernel(tk_ref, tv_ref, pek_ref, pev_ref, w1k_ref, w2k_ref, w1v_ref, w2v_ref,
                     c_ref, s1_ref, s2_ref, kc_ref, vc_ref):
    half_in = NSA_CMP_STRIDE * HEAD_DIM

    def comp(t_ref, pe_ref, w1_ref, w2_ref):
        t = t_ref[0].astype(F32)
        a = (t + pe_ref[0:1, :]).astype(BF16)
        b = (t + pe_ref[1:2, :]).astype(BF16)
        y0 = jnp.dot(a, w1_ref[0:half_in, :], preferred_element_type=F32)
        y1 = jnp.dot(b, w1_ref[half_in:2 * half_in, :], preferred_element_type=F32)
        hid = _gelu_tanh(y0 + pltpu.roll(y1, N_CMP_PAD - 1, 0))
        return jnp.dot(hid.astype(BF16), w2_ref[...], preferred_element_type=F32)

    kc = comp(tk_ref, pek_ref, w1k_ref, w2k_ref)
    kc_ref[0] = _rope_lanes(kc, c_ref[...], s1_ref[...], s2_ref[...], ROT_DIM // 2).astype(kc_ref.dtype)
    vc_ref[0] = comp(tv_ref, pev_ref, w1v_ref, w2v_ref).astype(vc_ref.dtype)


def _compress(tk2, tv2, pek, pev, w1k, w2k, w1v, w2v, tabs):
    nb = tk2.shape[0]
    wide = NSA_CMP_STRIDE * HEAD_DIM
    full = lambda shape: pl.BlockSpec(shape, lambda b: (0,) * len(shape))
    bspec = pl.BlockSpec((1, N_CMP_PAD, wide), lambda b: (b, 0, 0))
    ospec = pl.BlockSpec((1, N_CMP_PAD, HEAD_DIM), lambda b: (b, 0, 0))
    return pl.pallas_call(
        _compress_kernel,
        out_shape=(jax.ShapeDtypeStruct((nb, N_CMP_PAD, HEAD_DIM), BF16),) * 2,
        grid=(nb,),
        in_specs=[bspec, bspec, full((2, wide)), full((2, wide)),
                  full((2 * wide, NSA_CMP_HIDDEN)), full((NSA_CMP_HIDDEN, HEAD_DIM)),
                  full((2 * wide, NSA_CMP_HIDDEN)), full((NSA_CMP_HIDDEN, HEAD_DIM)),
                  full((N_CMP_PAD, LANES)), full((N_CMP_PAD, LANES)), full((N_CMP_PAD, LANES))],
        out_specs=(ospec, ospec),
        compiler_params=_cparams(("arbitrary",)),
        name="compress",
    )(tk2, tv2, pek, pev, w1k, w2k, w1v, w2v, *tabs)


def _qk(q, k):
    return lax.dot_general(q, k, (((1,), (1,)), ((), ())), preferred_element_type=F32)


def _flash_rows(q, k_ref, v_ref, bias_fn, j_lo, j_hi, dv):
    tq = q.shape[0]

    def body(j, carry):
        m, l, acc = carry
        off = pl.multiple_of(j * ATT_T, ATT_T)
        s = _qk(q, k_ref[pl.ds(off, ATT_T), :]) + bias_fn(j)
        m_new = jnp.maximum(m, jnp.max(s, axis=-1, keepdims=True))
        p = jnp.exp(s - m_new)
        alpha = jnp.exp(m - m_new)
        l = alpha * l + jnp.sum(p, axis=-1, keepdims=True)
        acc = alpha * acc + jnp.dot(p.astype(BF16), v_ref[pl.ds(off, ATT_T), :], preferred_element_type=F32)
        return m_new, l, acc

    init = (jnp.full((tq, 1), NEG_INF, F32), jnp.zeros((tq, 1), F32), jnp.zeros((tq, dv), F32))
    _, l, acc = lax.fori_loop(j_lo, j_hi, body, init)
    return acc / l


def _flash_kernel(q_ref, k_ref, v_ref, bias_ref, o_ref, *, n_diag):
    i = pl.program_id(2)
    dv = v_ref.shape[-1]
    j_lo = jnp.maximum(i - (n_diag - 1), 0)
    out = _flash_rows(q_ref[...], k_ref, v_ref, lambda j: bias_ref[i - j], j_lo, i + 1, dv)
    o_ref[...] = out.astype(o_ref.dtype)


def _flash(q_arr, k_arr, v_arr, bias, *, nb, heads, dqk, dv, q_blk0, k_blk0, v_blk0):
    nq = SEQ // ATT_T
    n_diag = bias.shape[0]
    return pl.pallas_call(
        functools.partial(_flash_kernel, n_diag=n_diag),
        out_shape=jax.ShapeDtypeStruct((nb * SEQ, heads * dv), BF16),
        grid=(nb, heads, nq),
        in_specs=[pl.BlockSpec((ATT_T, dqk), lambda b, h, i: (b * nq + i, q_blk0 + h)),
                  pl.BlockSpec((SEQ, dqk), lambda b, h, i: (b, k_blk0 + h)),
                  pl.BlockSpec((SEQ, dv), lambda b, h, i: (b, v_blk0 + h)),
                  pl.BlockSpec(bias.shape, lambda b, h, i: (0, 0, 0))],
        out_specs=pl.BlockSpec((ATT_T, dv), lambda b, h, i: (b * nq + i, h)),
        compiler_params=_cparams(("arbitrary", "arbitrary", "arbitrary")),
        name="flash",
    )(q_arr, k_arr, v_arr, bias)


def _split3(x):
    hi = x.astype(BF16)
    r1 = x - hi.astype(F32)
    mid = r1.astype(BF16)
    lo = (r1 - mid.astype(F32)).astype(BF16)
    return hi, mid, lo


def _nsa_kernel(q_ref, kc_ref, vc_ref, ks_ref, vs_ref, kw_ref, vw_ref, misc_ref, cover_ref, expand_ref,
                wbias_ref, o_ref, bias_scr):
    i = pl.program_id(1)
    tq = ATT_T
    pos = i * tq + lax.broadcasted_iota(jnp.int32, (tq, 1), 0)

    cidx = lax.broadcasted_iota(jnp.int32, (tq, N_CMP_PAD), 1)
    valid_c = (cidx * NSA_CMP_STRIDE + (NSA_CMP_LEN - 1) <= pos) & (cidx < N_CMP_PAD - 1)
    valid_cf = valid_c.astype(F32)
    cbias = jnp.where(valid_c, 0.0, NEG_INF).astype(F32)
    kc, vc = kc_ref[0], vc_ref[0]
    psum = jnp.zeros((tq, N_CMP_PAD), F32)
    o_cmp = []
    for h in range(NSA_HEADS):
        s = _qk(q_ref[:, h * LANES:(h + 1) * LANES], kc) + cbias
        p = jnp.exp(s - jnp.max(s, axis=-1, keepdims=True)) * valid_cf
        l = jnp.sum(p, axis=-1, keepdims=True)
        p = p / jnp.where(l > 0.0, l, 1.0)
        psum = psum + p
        o_cmp.append(jnp.dot(p.astype(BF16), vc, preferred_element_type=F32))

    cover = cover_ref[...]
    imp = sum(jnp.dot(part, cover, preferred_element_type=F32) for part in _split3(psum))

    jidx = lax.broadcasted_iota(jnp.int32, (tq, N_SEL_BLOCKS), 1)
    qblk = pos // NSA_SEL_BLOCK
    valid_s = jidx * NSA_SEL_BLOCK <= pos
    forced = (jidx == 0) | (jidx == qblk) | (jidx == qblk - 1)
    score = jnp.where(valid_s, jnp.where(forced, NSA_FORCE_SCORE, imp), -1.0)
    rank = jnp.zeros((tq, N_SEL_BLOCKS), jnp.int32)
    for ii in range(N_SEL_BLOCKS):
        si = score[:, ii:ii + 1]
        rank = rank + ((si > score) | ((si == score) & (ii < jidx))).astype(jnp.int32)
    sel = (rank < NSA_TOP_N).astype(BF16)

    def fill(j, carry):
        hit = jnp.dot(sel, expand_ref[j], preferred_element_type=F32)
        kpos = j * tq + lax.broadcasted_iota(jnp.int32, (tq, tq), 1)
        bias_scr[j] = jnp.where((hit > 0.5) & (kpos <= pos), 0.0, NEG_INF).astype(F32)
        return carry

    lax.fori_loop(0, i + 1, fill, 0)

    gates = jax.nn.sigmoid(misc_ref[...].astype(F32))
    w_lo = jnp.maximum(i - (wbias_ref.shape[0] - 1), 0)
    for h in range(NSA_HEADS):
        qh = q_ref[:, h * LANES:(h + 1) * LANES]
        o_slc = _flash_rows(qh, ks_ref, vs_ref, lambda j: bias_scr[j], 0, i + 1, HEAD_DIM)
        o_win = _flash_rows(qh, kw_ref, vw_ref, lambda j: wbias_ref[i - j], w_lo, i + 1, HEAD_DIM)
        g0 = GATE_LANE0 + 3 * h
        out = (gates[:, g0:g0 + 1] * o_cmp[h] + gates[:, g0 + 1:g0 + 2] * o_slc
               + gates[:, g0 + 2:g0 + 3] * o_win)
        o_ref[:, h * LANES:(h + 1) * LANES] = out.astype(o_ref.dtype)


def _nsa(hproj, kc, vc, cover, expand, wbias, nb):
    nq = SEQ // ATT_T
    wide = NSA_HEADS * HEAD_DIM
    kv_spec = lambda blk: pl.BlockSpec((SEQ, LANES), lambda b, i: (b, blk))
    cspec = pl.BlockSpec((1, N_CMP_PAD, HEAD_DIM), lambda b, i: (b, 0, 0))
    full = lambda shape: pl.BlockSpec(shape, lambda b, i: (0,) * len(shape))
    return pl.pallas_call(
        _nsa_kernel,
        out_shape=jax.ShapeDtypeStruct((nb * SEQ, wide), BF16),
        grid=(nb, nq),
        in_specs=[pl.BlockSpec((ATT_T, wide), lambda b, i: (b * nq + i, BLK_NQ * LANES // wide)),
                  cspec, cspec, kv_spec(BLK_NKS), kv_spec(BLK_NVS), kv_spec(BLK_NKW), kv_spec(BLK_NVW),
                  pl.BlockSpec((ATT_T, LANES), lambda b, i: (b * nq + i, BLK_MISC)),
                  full(cover.shape), full(expand.shape), full(wbias.shape)],
        out_specs=pl.BlockSpec((ATT_T, wide), lambda b, i: (b * nq + i, 0)),
        scratch_shapes=[pltpu.VMEM((nq, ATT_T, ATT_T), F32)],
        compiler_params=_cparams(("arbitrary", "arbitrary")),
        name="nsa",
    )(hproj, kc, vc, hproj, hproj, hproj, hproj, hproj, cover, expand, wbias)


def _layer_norm(y, g, b):
    mu = jnp.mean(y, axis=-1, keepdims=True)
    var = jnp.mean(jnp.square(y - mu), axis=-1, keepdims=True)
    return (y - mu) * lax.rsqrt(var + LN_EPS) * g + b


def _lane_min(x):
    return jnp.min(x, axis=-1, keepdims=True)


def _lane_max(x):
    return jnp.max(x, axis=-1, keepdims=True)


def _route(logits):
    lane = lax.broadcasted_iota(jnp.int32, logits.shape, 1)
    lane_f = lane.astype(F32)
    far = float(LANES)
    is_grp = lane < N_GROUPS
    lg = jnp.where(is_grp, logits, NEG_INF)
    eg = jnp.where(is_grp, jnp.exp(lg - _lane_max(lg)), 0.0)
    prob = eg / jnp.sum(eg, axis=-1, keepdims=True)
    p_g = _lane_max(prob)
    g_idx = _lane_min(jnp.where(is_grp & (prob == p_g), lane_f, far))
    e_lo = N_GROUPS + EXPERTS_PER_GROUP * g_idx
    in_grp = (lane_f >= e_lo) & (lane_f < e_lo + EXPERTS_PER_GROUP)
    le = jnp.where(in_grp, logits, NEG_INF)
    v1 = _lane_max(le)
    i1 = _lane_min(jnp.where(in_grp & (le == v1), lane_f, far))
    rest = in_grp & (lane_f != i1)
    le2 = jnp.where(rest, logits, NEG_INF)
    v2 = _lane_max(le2)
    i2 = _lane_min(jnp.where(rest & (le2 == v2), lane_f, far))
    e21 = jnp.exp(v2 - v1)
    den = 1.0 + e21
    gate1 = p_g * (1.0 / den)
    gate2 = p_g * (e21 / den)
    out = jnp.where(lane == 0, i1 - N_GROUPS, 0.0)
    out = jnp.where(lane == 1, i2 - N_GROUPS, out)
    out = jnp.where(lane == 2, gate1, out)
    out = jnp.where(lane == 3, gate2, out)
    return out


def _outproj_kernel(om_ref, od_ref, on_ref, w_ref, x_ref, g_ref, b_ref, wr_ref, x1_ref, route_ref):
    n_mla = MLA_HEADS * HEAD_DIM
    n_dil = DIL_HEADS * HEAD_DIM
    mix = jnp.dot(om_ref[...], w_ref[0:n_mla, :], preferred_element_type=F32)
    mix = mix + jnp.dot(od_ref[...], w_ref[n_mla:n_mla + n_dil, :], preferred_element_type=F32)
    mix = mix + jnp.dot(on_ref[...], w_ref[n_mla + n_dil:, :], preferred_element_type=F32)
    x1 = _layer_norm(DN_ALPHA * x_ref[...] + mix, g_ref[...], b_ref[...])
    x1_ref[...] = x1
    xh, xm, xl = _split3(x1)
    wh, wm, wl = wr_ref[0], wr_ref[1], wr_ref[2]
    dot = functools.partial(jnp.dot, preferred_element_type=F32)
    logits = (dot(xm, wm) + dot(xh, wl) + dot(xl, wh)) + (dot(xh, wm) + dot(xm, wh)) + dot(xh, wh)
    route_ref[...] = _route(logits)


def _outproj(o_mla, o_dil, o_nsa, w_out, x, g, b, wr3):
    t = x.shape[0]
    tm = ROW_TM
    full = lambda shape: pl.BlockSpec(shape, lambda i: (0,) * len(shape))
    row = lambda w: pl.BlockSpec((tm, w), lambda i: (i, 0))
    return pl.pallas_call(
        _outproj_kernel,
        out_shape=(jax.ShapeDtypeStruct((t, D_MODEL), F32), jax.ShapeDtypeStruct((t, LANES), F32)),
        grid=(t // tm,),
        in_specs=[row(o_mla.shape[1]), row(o_dil.shape[1]), row(o_nsa.shape[1]), full(w_out.shape),
                  row(D_MODEL), full((1, D_MODEL)), full((1, D_MODEL)), full(wr3.shape)],
        out_specs=(row(D_MODEL), row(LANES)),
        compiler_params=_cparams(("arbitrary",)),
        name="outproj",
    )(o_mla, o_dil, o_nsa, w_out, x, g, b, wr3)


def _experts_kernel(blk_ex_ref, nused_ref, tok_cur_ref, tok_nxt_ref, x_hbm, wg_ref, wu_ref, wd_ref, y_ref,
                    xbuf, wg_b, wu_b, wd_b, sem):
    b = pl.program_id(0)
    nused = nused_ref[0]

    def issue(tok_ref, slot):
        def body(r, carry):
            t = tok_ref[0, 0, r]
            pltpu.make_async_copy(x_hbm.at[pl.ds(t, 1)], xbuf.at[slot, pl.ds(r, 1)], sem.at[slot]).start()
            return carry

        lax.fori_loop(0, MOE_ROWS, body, 0)

    @pl.when(b == 0)
    def _():
        issue(tok_cur_ref, 0)

    @pl.when(b + 1 < nused)
    def _():
        issue(tok_nxt_ref, (b + 1) % 2)

    @pl.when(b < nused)
    def _():
        slot = b % 2
        pltpu.make_async_copy(x_hbm.at[pl.ds(0, MOE_ROWS)], xbuf.at[slot], sem.at[slot]).wait()
        changed = jnp.logical_or(b == 0, blk_ex_ref[b] != blk_ex_ref[jnp.maximum(b - 1, 0)])

        @pl.when(changed)
        def _():
            wg_b[...] = wg_ref[0].astype(BF16)
            wu_b[...] = wu_ref[0].astype(BF16)
            wd_b[...] = wd_ref[0].astype(BF16)

        rows = xbuf[slot].astype(BF16)
        gate = jnp.dot(rows, wg_b[...], preferred_element_type=F32)
        up = jnp.dot(rows, wu_b[...], preferred_element_type=F32)
        hid = (gate * jax.nn.sigmoid(gate) * up).astype(BF16)
        y_ref[...] = jnp.dot(hid, wd_b[...], preferred_element_type=F32)

    @pl.when(b >= nused)
    def _():
        y_ref[...] = jnp.zeros_like(y_ref)


def _experts(blk_ex, nused, buf_tok3, x1, w_gate, w_up, w_down):
    n_blocks = buf_tok3.shape[0]
    tok_spec = lambda shift: pl.BlockSpec(
        (1, 1, MOE_ROWS), lambda b, ex, nu: (jnp.minimum(b + shift, n_blocks - 1), 0, 0),
        memory_space=pltpu.SMEM)
    w_spec = lambda shape: pl.BlockSpec((1,) + shape, lambda b, ex, nu: (ex[b], 0, 0))
    grid_spec = pltpu.PrefetchScalarGridSpec(
        num_scalar_prefetch=2,
        grid=(n_blocks,),
        in_specs=[tok_spec(0), tok_spec(1), pl.BlockSpec(memory_space=pl.ANY),
                  w_spec((D_MODEL, EXPERT_FF)), w_spec((D_MODEL, EXPERT_FF)), w_spec((EXPERT_FF, D_MODEL))],
        out_specs=pl.BlockSpec((MOE_ROWS, D_MODEL), lambda b, ex, nu: (b, 0)),
        scratch_shapes=[pltpu.VMEM((2, MOE_ROWS, D_MODEL), F32),
                        pltpu.VMEM((D_MODEL, EXPERT_FF), BF16), pltpu.VMEM((D_MODEL, EXPERT_FF), BF16),
                        pltpu.VMEM((EXPERT_FF, D_MODEL), BF16), pltpu.SemaphoreType.DMA((2,))],
    )
    return pl.pallas_call(
        _experts_kernel,
        out_shape=jax.ShapeDtypeStruct((n_blocks * MOE_ROWS, D_MODEL), F32),
        grid_spec=grid_spec,
        compiler_params=_cparams(("arbitrary",)),
        name="experts",
    )(blk_ex, nused, buf_tok3, buf_tok3, x1, w_gate, w_up, w_down)


def _combine_kernel(pos_cur_ref, pos_nxt_ref, y_hbm, x1_ref, route_ref, g_ref, b_ref, o_ref, ob_ref, ybuf, sem):
    i = pl.program_id(0)
    n = pl.num_programs(0)
    tm = ROW_TM

    def issue(pos_ref, slot):
        def body(r, carry):
            for k in range(MOE_TOP_K):
                p = pos_ref[0, 0, MOE_TOP_K * r + k]
                pltpu.make_async_copy(y_hbm.at[pl.ds(p, 1)], ybuf.at[slot, k, pl.ds(r, 1)], sem.at[slot]).start()
            return carry

        lax.fori_loop(0, tm, body, 0)

    @pl.when(i == 0)
    def _():
        issue(pos_cur_ref, 0)

    @pl.when(i + 1 < n)
    def _():
        issue(pos_nxt_ref, (i + 1) % 2)

    slot = i % 2
    for k in range(MOE_TOP_K):
        pltpu.make_async_copy(y_hbm.at[pl.ds(0, tm)], ybuf.at[slot, k], sem.at[slot]).wait()
    route = route_ref[...]
    ffn = route[:, 2:3] * ybuf[slot, 0] + route[:, 3:4] * ybuf[slot, 1]
    x2 = _layer_norm(DN_ALPHA * x1_ref[...] + ffn, g_ref[...], b_ref[...])
    o_ref[...] = x2
    ob_ref[...] = x2.astype(ob_ref.dtype)


def _combine(pos3, yb, x1, route, g, b):
    t = x1.shape[0]
    tm = ROW_TM
    nt = t // tm
    pos_spec = lambda shift: pl.BlockSpec((1, 1, MOE_TOP_K * tm), lambda i: (jnp.minimum(i + shift, nt - 1), 0, 0),
                                          memory_space=pltpu.SMEM)
    row = lambda w: pl.BlockSpec((tm, w), lambda i: (i, 0))
    full = lambda shape: pl.BlockSpec(shape, lambda i: (0,) * len(shape))
    return pl.pallas_call(
        _combine_kernel,
        out_shape=(jax.ShapeDtypeStruct((t, D_MODEL), F32), jax.ShapeDtypeStruct((t, D_MODEL), BF16)),
        grid=(nt,),
        in_specs=[pos_spec(0), pos_spec(1), pl.BlockSpec(memory_space=pl.ANY), row(D_MODEL), row(LANES),
                  full((1, D_MODEL)), full((1, D_MODEL))],
        out_specs=(row(D_MODEL), row(D_MODEL)),
        scratch_shapes=[pltpu.VMEM((2, MOE_TOP_K, tm, D_MODEL), F32), pltpu.SemaphoreType.DMA((2,))],
        compiler_params=_cparams(("arbitrary",)),
        name="combine",
    )(pos3, pos3, yb, x1, route, g, b)


def _dispatch_plan(route, t):
    m = t * MOE_TOP_K
    n_blocks = -(-m // MOE_ROWS) + N_EXPERTS
    ex = route[:, 0:MOE_TOP_K].astype(jnp.int32).reshape(m)
    order = jnp.argsort(ex, stable=True).astype(jnp.int32)
    ex_s = ex[order]
    counts = jnp.sum((ex[:, None] == jnp.arange(N_EXPERTS, dtype=jnp.int32)[None, :]).astype(jnp.int32), axis=0)
    seg_start = jnp.cumsum(counts) - counts
    padded = (counts + MOE_ROWS - 1) // MOE_ROWS * MOE_ROWS
    pad_end = jnp.cumsum(padded)
    pad_start = pad_end - padded
    blk_start = jnp.arange(n_blocks, dtype=jnp.int32) * MOE_ROWS
    blk_ex = jnp.minimum(jnp.searchsorted(pad_end, blk_start, side='right'), N_EXPERTS - 1).astype(jnp.int32)
    nused = (pad_end[-1] // MOE_ROWS).astype(jnp.int32).reshape(1)
    p = jnp.arange(n_blocks * MOE_ROWS, dtype=jnp.int32)
    e_of_p = jnp.repeat(blk_ex, MOE_ROWS)
    within = p - pad_start[e_of_p]
    src = jnp.clip(seg_start[e_of_p] + within, 0, m - 1)
    buf_tok = jnp.where(within < counts[e_of_p], order[src] // MOE_TOP_K, 0).astype(jnp.int32)
    dest_sorted = pad_start[ex_s] + jnp.arange(m, dtype=jnp.int32) - seg_start[ex_s]
    pos = jnp.zeros((m,), jnp.int32).at[order].set(dest_sorted, unique_indices=True)
    return blk_ex, nused, buf_tok.reshape(n_blocks, 1, MOE_ROWS), pos


def _permute_w_in(w_in):
    cuts = [MLA_Q_RANK, MLA_KV_RANK, MLA_ROPE_DIM] + [DIL_HEADS * HEAD_DIM] * 3 + [NSA_HEADS * HEAD_DIM] \
        + [HEAD_DIM] * 6 + [NSA_HEADS * 3]
    offs = [0]
    for c in cuts:
        offs.append(offs[-1] + c)
    (q_lat, kv_lat, k_rope, dq, dk, dv, nq, nkc, nvc, nks, nvs, nkw, nvw, ngate) = [
        w_in[:, offs[n]:offs[n + 1]] for n in range(len(cuts))]
    pad = jnp.zeros((w_in.shape[0], IN_COLS_PAD - offs[-1]), w_in.dtype)
    misc_pad, tail_pad = pad[:, :LANES - MLA_ROPE_DIM - NSA_HEADS * 3], pad[:, LANES - MLA_ROPE_DIM - NSA_HEADS * 3:]
    return jnp.concatenate([dq, nq, dk, nks, nkw, dv, nkc, nvc, nvs, nvw, q_lat, kv_lat,
                            k_rope, ngate, misc_pad, tail_pad], axis=1).astype(BF16)


def _permute_w_q_up(w_q_up):
    w = w_q_up.reshape(MLA_Q_RANK, MLA_HEADS, MLA_NOPE_DIM + MLA_ROPE_DIM)
    w = jnp.pad(w, ((0, 0), (0, 0), (0, MLA_QK_PAD - MLA_NOPE_DIM - MLA_ROPE_DIM)))
    return w.reshape(MLA_Q_RANK, MLA_HEADS * MLA_QK_PAD).astype(BF16)


def _split_w_kv_up(w_kv_up):
    w = w_kv_up.reshape(MLA_KV_RANK, MLA_HEADS, MLA_NOPE_DIM + HEAD_DIM)
    wkn = w[:, :, :MLA_NOPE_DIM].reshape(MLA_KV_RANK, MLA_HEADS * MLA_NOPE_DIM)
    wv = w[:, :, MLA_NOPE_DIM:].reshape(MLA_KV_RANK, MLA_HEADS * HEAD_DIM)
    return wkn.astype(BF16), wv.astype(BF16)


def _router_slices(w_grp, w_exp):
    w = jnp.concatenate([w_grp, w_exp, jnp.zeros((D_MODEL, LANES - N_GROUPS - N_EXPERTS), F32)], axis=1)
    hi = w.astype(BF16)
    r1 = w - hi.astype(F32)
    mid = r1.astype(BF16)
    lo = (r1 - mid.astype(F32)).astype(BF16)
    return jnp.stack([hi, mid, lo], axis=0)


def _mixer(xb, nb, w_in, q_lat_norm, w_q_up, kv_lat_norm, w_kv_up,
           cmp_pos_k, cmp_w1_k, cmp_w2_k, cmp_pos_v, cmp_w1_v, cmp_w2_v, consts):
    hproj = _inproj(xb, _permute_w_in(w_in), consts["rope32"])
    wkn, wv = _split_w_kv_up(w_kv_up)
    q_mla, k_mla, v_mla = _mlaprep(hproj, q_lat_norm.reshape(1, -1), kv_lat_norm.reshape(1, -1),
                                   _permute_w_q_up(w_q_up), wkn, wv, consts["rope64"])
    wide = NSA_CMP_STRIDE * HEAD_DIM
    tk2 = hproj[:, BLK_NKC * LANES:(BLK_NKC + 1) * LANES].reshape(nb, N_CMP_PAD, wide)
    tv2 = hproj[:, BLK_NVC * LANES:(BLK_NVC + 1) * LANES].reshape(nb, N_CMP_PAD, wide)
    kc, vc = _compress(tk2, tv2, cmp_pos_k.reshape(2, wide), cmp_pos_v.reshape(2, wide),
                       cmp_w1_k.astype(BF16), cmp_w2_k.astype(BF16), cmp_w1_v.astype(BF16), cmp_w2_v.astype(BF16),
                       consts["rope_cmp"])
    o_mla = _flash(q_mla, k_mla, v_mla, consts["bias_causal"], nb=nb, heads=MLA_HEADS, dqk=MLA_QK_PAD,
                   dv=HEAD_DIM, q_blk0=0, k_blk0=0, v_blk0=0)
    o_dil = _flash(hproj, hproj, hproj, consts["bias_dil"], nb=nb, heads=DIL_HEADS, dqk=HEAD_DIM, dv=HEAD_DIM,
                   q_blk0=BLK_DQ, k_blk0=BLK_DK, v_blk0=BLK_DV)
    o_nsa = _nsa(hproj, kc, vc, consts["cover"], consts["expand"], consts["bias_win"], nb)
    return o_mla, o_dil, o_nsa


def _make_consts():
    pos = jnp.arange(SEQ, dtype=jnp.int32)
    cmp_end = jnp.arange(N_CMP_PAD, dtype=jnp.int32) * NSA_CMP_STRIDE + NSA_CMP_LEN - 1
    cmp_start = jnp.arange(N_CMP_PAD, dtype=jnp.int32) * NSA_CMP_STRIDE
    sel_start = jnp.arange(N_SEL_BLOCKS, dtype=jnp.int32) * NSA_SEL_BLOCK
    cover = ((cmp_start[:, None] < sel_start[None, :] + NSA_SEL_BLOCK)
             & (cmp_start[:, None] + NSA_CMP_LEN > sel_start[None, :])
             & (jnp.arange(N_CMP_PAD)[:, None] < N_CMP_PAD - 1)).astype(BF16)
    key_blk = jnp.arange(SEQ, dtype=jnp.int32) // NSA_SEL_BLOCK
    expand = (jnp.arange(N_SEL_BLOCKS, dtype=jnp.int32)[:, None] == key_blk[None, :]).astype(BF16)
    expand = expand.reshape(N_SEL_BLOCKS, SEQ // ATT_T, ATT_T).transpose(1, 0, 2)
    return {
        "rope32": _rope_tables(pos, ROT_DIM, True),
        "rope64": _rope_tables(pos, MLA_ROPE_DIM, False),
        "rope_cmp": _rope_tables(cmp_end, ROT_DIM, True),
        "bias_causal": _distance_bias(_causal_mult, SEQ // ATT_T),
        "bias_dil": _distance_bias(_dilated_mult, SEQ // ATT_T),
        "bias_win": _distance_bias(_window_mult, -(-NSA_WINDOW // ATT_T) + 1),
        "cover": cover,
        "expand": expand,
    }


def kernel(x, w_in, q_lat_norm, w_q_up, kv_lat_norm, w_kv_up, cmp_pos_k, cmp_w1_k, cmp_w2_k, cmp_pos_v, cmp_w1_v,
           cmp_w2_v, w_out, ln1_g, ln1_b, w_grp, w_exp, w_gate, w_up, w_down, ln2_g, ln2_b):
    nb, s, d = x.shape
    assert s == SEQ and d == D_MODEL
    t = nb * s
    consts = _make_consts()
    xf = x.reshape(t, d)
    xb = xf.astype(BF16)
    for l in range(DEPTH):
        o_mla, o_dil, o_nsa = _mixer(xb, nb, w_in[l], q_lat_norm[l], w_q_up[l], kv_lat_norm[l], w_kv_up[l],
                                     cmp_pos_k[l], cmp_w1_k[l], cmp_w2_k[l], cmp_pos_v[l], cmp_w1_v[l],
                                     cmp_w2_v[l], consts)
        x1, route = _outproj(o_mla, o_dil, o_nsa, w_out[l].astype(BF16), xf, ln1_g[l].reshape(1, d),
                             ln1_b[l].reshape(1, d), _router_slices(w_grp[l], w_exp[l]))
        blk_ex, nused, buf_tok3, pos = _dispatch_plan(route, t)
        yb = _experts(blk_ex, nused, buf_tok3, x1, w_gate[l], w_up[l], w_down[l])
        xf, xb = _combine(pos.reshape(t // ROW_TM, 1, MOE_TOP_K * ROW_TM), yb, x1, route,
                          ln2_g[l].reshape(1, d), ln2_b[l].reshape(1, d))
    return xf.reshape(nb, s, d)
```

```python
import functools

import numpy as np
import jax
import jax.numpy as jnp
from jax import lax
from jax.experimental import pallas as pl
from jax.experimental.pallas import tpu as pltpu

F32 = jnp.float32
BF16 = jnp.bfloat16

D_MODEL = 2048
SEQ = 2048
DEPTH = 2
HEAD_DIM = 128
LANES = 128
MLA_HEADS = 4
DIL_HEADS = 6
NSA_HEADS = 6
MLA_Q_RANK = 384
MLA_KV_RANK = 128
MLA_NOPE_DIM = 128
MLA_ROPE_DIM = 64
MLA_QK_PAD = 256
DIL_PAIRS = ((128, 1), (512, 4), (2048, 16))
NSA_CMP_LEN = 32
NSA_CMP_STRIDE = 16
NSA_CMP_HIDDEN = 256
NSA_SEL_BLOCK = 64
NSA_TOP_N = 16
NSA_WINDOW = 512
NSA_BRANCHES = 3
NSA_FORCE_SCORE = 1e4
ROPE_THETA = 500000.0
ROT_DIM = HEAD_DIM // 4
N_GROUPS = 4
EXPERTS_PER_GROUP = 8
N_EXPERTS = N_GROUPS * EXPERTS_PER_GROUP
EXPERT_FF = 512
MOE_TOP_K = 2
MOE_ROWS = 256
DN_ALPHA = (2 * DEPTH) ** 0.25
LN_EPS = 1e-5
RMS_EPS = 1e-6
NEG_INF = -1e30

N_CMP_PAD = SEQ // NSA_CMP_STRIDE
N_SEL_BLOCKS = SEQ // NSA_SEL_BLOCK

BLK_QLAT = 0
BLK_KVLAT = 3
BLK_KROPE = 4
BLK_DQ = 5
BLK_DK = 11
BLK_DV = 17
BLK_NQ = 23
BLK_NKC = 29
BLK_NVC = 30
BLK_NKS = 31
BLK_NVS = 32
BLK_NKW = 33
BLK_NVW = 34
BLK_GATE = 35
N_IN_BLOCKS = 36
IN_COLS = 4434
IN_SHIFT_FROM = MLA_Q_RANK + MLA_KV_RANK - MLA_ROPE_DIM
IN_COLS_PAD = N_IN_BLOCKS * LANES
KROPE_LANE0 = LANES - MLA_ROPE_DIM

INPROJ_TM = 512
INPROJ_TN = 512
FLASH_T = 512
NSA_T = 256
ROW_TM = 256

VMEM_LIMIT = 56 * 1024 * 1024


def _cparams(sem):
    return pltpu.CompilerParams(dimension_semantics=sem, vmem_limit_bytes=VMEM_LIMIT)


def _rope_tables(pos, rot_dim, keep_rest, lane0=0):
    half = rot_dim // 2
    inv_freq = np.power(np.float32(ROPE_THETA), -np.arange(half, dtype=np.float32) / np.float32(half))
    ang = (pos.astype(np.float32)[:, None] * inv_freq[None, :]).astype(np.float32)
    cos, sin = np.cos(ang).astype(np.float32), np.sin(ang).astype(np.float32)
    n = pos.shape[0]
    c = np.full((n, LANES), 1.0 if keep_rest else 0.0, np.float32)
    s1 = np.zeros((n, LANES), np.float32)
    s2 = np.zeros((n, LANES), np.float32)
    c[:, lane0:lane0 + half] = cos
    c[:, lane0 + half:lane0 + rot_dim] = cos
    s1[:, lane0 + half:lane0 + rot_dim] = sin
    s2[:, lane0:lane0 + half] = -sin
    return jnp.asarray(c), jnp.asarray(s1), jnp.asarray(s2)


def _rope_lanes(a, c, s1, s2, half):
    return a * c + pltpu.roll(a, half, 1) * s1 + pltpu.roll(a, LANES - half, 1) * s2


def _distance_bias(mult_fn, diags, t):
    d = np.asarray(diags, np.int64)[:, None, None] * t
    r = np.arange(t, dtype=np.int64)[None, :, None]
    c = np.arange(t, dtype=np.int64)[None, None, :]
    mult = mult_fn(d + r - c)
    return jnp.asarray(np.where(mult > 0, np.log(np.maximum(mult, 1).astype(np.float32)), NEG_INF).astype(np.float32))


def _causal_mult(dist):
    return (dist >= 0).astype(np.int64)


def _dilated_mult(dist):
    m = np.zeros_like(dist)
    for window, dil in DIL_PAIRS:
        m = m + ((dist >= 0) & (dist % dil == 0) & (dist <= (window // dil) * dil)).astype(np.int64)
    return m


def _window_mult(dist):
    return ((dist >= 0) & (dist <= NSA_WINDOW - 1)).astype(np.int64)


def _in_block(blk, lo, n):
    return jnp.logical_and(blk >= lo, blk < lo + n)


def _inproj_kernel(x_ref, w_ref, c_ref, s1_ref, s2_ref, o_ref, *, scale):
    j = pl.program_id(0)
    acc = jnp.dot(x_ref[...], w_ref[0], preferred_element_type=F32)
    c, s1, s2 = c_ref[...], s1_ref[...], s2_ref[...]
    for u in range(INPROJ_TN // LANES):
        blk = j * (INPROJ_TN // LANES) + u
        is_q = jnp.logical_or(_in_block(blk, BLK_DQ, DIL_HEADS), _in_block(blk, BLK_NQ, NSA_HEADS))
        is_rope = is_q | _in_block(blk, BLK_DK, DIL_HEADS) | (blk == BLK_NKS) | (blk == BLK_NKW)
        sc = jnp.where(is_q, scale, 1.0).astype(F32)
        a = acc[:, u * LANES:(u + 1) * LANES]
        r = _rope_lanes(a, jnp.where(is_rope, c, 1.0), jnp.where(is_rope, s1, 0.0), jnp.where(is_rope, s2, 0.0),
                        ROT_DIM // 2)
        o_ref[:, u * LANES:(u + 1) * LANES] = (r * sc).astype(o_ref.dtype)


def _inproj(xb, w_in_s, layer, tabs):
    t = xb.shape[0]
    tm, tn = INPROJ_TM, INPROJ_TN
    nrow = SEQ // tm
    tab_spec = pl.BlockSpec((tm, LANES), lambda j, i: (i % nrow, 0))
    return pl.pallas_call(
        functools.partial(_inproj_kernel, scale=HEAD_DIM ** -0.5),
        out_shape=jax.ShapeDtypeStruct((t, IN_COLS_PAD), BF16),
        grid=(IN_COLS_PAD // tn, t // tm),
        in_specs=[pl.BlockSpec((tm, D_MODEL), lambda j, i: (i, 0)),
                  pl.BlockSpec((1, D_MODEL, tn), lambda j, i: (layer, 0, j)),
                  tab_spec, tab_spec, tab_spec],
        out_specs=pl.BlockSpec((tm, tn), lambda j, i: (i, j)),
        compiler_params=_cparams(("arbitrary", "arbitrary")),
        name="inproj",
    )(xb, w_in_s, *tabs)


def _rms(xf, g):
    return xf * lax.rsqrt(jnp.mean(jnp.square(xf), axis=-1, keepdims=True) + RMS_EPS) * g


def _mlaprep_kernel(ql_ref, kvl_ref, kr_ref, gq_ref, gkv_ref, wq_ref, wkn_ref, wv_ref,
                    c_ref, s1_ref, s2_ref, q_ref, k_ref, v_ref, *, scale):
    c, s1, s2 = c_ref[...], s1_ref[...], s2_ref[...]
    half = MLA_ROPE_DIM // 2
    qn = _rms(ql_ref[...].astype(F32), gq_ref[...]).astype(BF16)
    q = jnp.dot(qn, wq_ref[...], preferred_element_type=F32)
    for h in range(MLA_HEADS):
        lo = h * MLA_QK_PAD
        q_ref[:, lo:lo + LANES] = (q[:, lo:lo + LANES] * scale).astype(q_ref.dtype)
        a = q[:, lo + LANES:lo + 2 * LANES]
        q_ref[:, lo + LANES:lo + 2 * LANES] = (_rope_lanes(a, c, s1, s2, half) * scale).astype(q_ref.dtype)
    kvn = _rms(kvl_ref[...].astype(F32), gkv_ref[...]).astype(BF16)
    kn = jnp.dot(kvn, wkn_ref[...], preferred_element_type=F32)
    v_ref[...] = jnp.dot(kvn, wv_ref[...], preferred_element_type=F32).astype(v_ref.dtype)
    kpe = _rope_lanes(kr_ref[...].astype(F32), c, s1, s2, half).astype(k_ref.dtype)
    for h in range(MLA_HEADS):
        lo = h * MLA_QK_PAD
        k_ref[:, lo:lo + LANES] = kn[:, h * LANES:(h + 1) * LANES].astype(k_ref.dtype)
        k_ref[:, lo + LANES:lo + 2 * LANES] = kpe


def _mlaprep(hproj, gq, gkv, wq_p, wkn, wv, tabs):
    t = hproj.shape[0]
    tm = ROW_TM
    nrow = SEQ // tm
    tab_spec = pl.BlockSpec((tm, LANES), lambda i: (i % nrow, 0))
    full = lambda shape: pl.BlockSpec(shape, lambda i: (0,) * len(shape))
    return pl.pallas_call(
        functools.partial(_mlaprep_kernel, scale=(MLA_NOPE_DIM + MLA_ROPE_DIM) ** -0.5),
        out_shape=(jax.ShapeDtypeStruct((t, MLA_HEADS * MLA_QK_PAD), BF16),
                   jax.ShapeDtypeStruct((t, MLA_HEADS * MLA_QK_PAD), BF16),
                   jax.ShapeDtypeStruct((t, MLA_HEADS * HEAD_DIM), BF16)),
        grid=(t // tm,),
        in_specs=[pl.BlockSpec((tm, MLA_Q_RANK), lambda i: (i, BLK_QLAT * LANES // MLA_Q_RANK)),
                  pl.BlockSpec((tm, LANES), lambda i: (i, BLK_KVLAT)),
                  pl.BlockSpec((tm, LANES), lambda i: (i, BLK_KROPE)),
                  full((1, MLA_Q_RANK)), full((1, MLA_KV_RANK)),
                  full((MLA_Q_RANK, MLA_HEADS * MLA_QK_PAD)),
                  full((MLA_KV_RANK, MLA_HEADS * MLA_NOPE_DIM)),
                  full((MLA_KV_RANK, MLA_HEADS * HEAD_DIM)),
                  tab_spec, tab_spec, tab_spec],
        out_specs=(pl.BlockSpec((tm, MLA_HEADS * MLA_QK_PAD), lambda i: (i, 0)),
                   pl.BlockSpec((tm, MLA_HEADS * MLA_QK_PAD), lambda i: (i, 0)),
                   pl.BlockSpec((tm, MLA_HEADS * HEAD_DIM), lambda i: (i, 0))),
        compiler_params=_cparams(("arbitrary",)),
        name="mlaprep",
    )(hproj, hproj, hproj, gq, gkv, wq_p, wkn, wv, *tabs)


def _gelu_tanh(x):
    return 0.5 * x * (1.0 + jnp.tanh(0.7978845608028654 * (x + 0.044715 * (x * x * x))))


def _compress_kernel(tk_ref, tv_ref, pek_ref, pev_ref, w1k_ref, w2k_ref, w1v_ref, w2v_ref,
                     c_ref, s1_ref, s2_ref, kc_ref, vc_ref):
    half_in = NSA_CMP_STRIDE * HEAD_DIM

    def comp(t_ref, pe_ref, w1_ref, w2_ref):
        t = t_ref[0].astype(F32)
        a = (t + pe_ref[0:1, :]).astype(BF16)
        b = (t + pe_ref[1:2, :]).astype(BF16)
        y0 = jnp.dot(a, w1_ref[0:half_in, :], preferred_element_type=F32)
        y1 = jnp.dot(b, w1_ref[half_in:2 * half_in, :], preferred_element_type=F32)
        hid = _gelu_tanh(y0 + pltpu.roll(y1, N_CMP_PAD - 1, 0))
        return jnp.dot(hid.astype(BF16), w2_ref[...], preferred_element_type=F32)

    kc = comp(tk_ref, pek_ref, w1k_ref, w2k_ref)
    kc_ref[0] = _rope_lanes(kc, c_ref[...], s1_ref[...], s2_ref[...], ROT_DIM // 2).astype(kc_ref.dtype)
    vc_ref[0] = comp(tv_ref, pev_ref, w1v_ref, w2v_ref).astype(vc_ref.dtype)


def _compress(tk2, tv2, pek, pev, w1k, w2k, w1v, w2v, tabs):
    nb = tk2.shape[0]
    wide = NSA_CMP_STRIDE * HEAD_DIM
    full = lambda shape: pl.BlockSpec(shape, lambda b: (0,) * len(shape))
    bspec = pl.BlockSpec((1, N_CMP_PAD, wide), lambda b: (b, 0, 0))
    ospec = pl.BlockSpec((1, N_CMP_PAD, HEAD_DIM), lambda b: (b, 0, 0))
    return pl.pallas_call(
        _compress_kernel,
        out_shape=(jax.ShapeDtypeStruct((nb, N_CMP_PAD, HEAD_DIM), BF16),) * 2,
        grid=(nb,),
        in_specs=[bspec, bspec, full((2, wide)), full((2, wide)),
                  full((2 * wide, NSA_CMP_HIDDEN)), full((NSA_CMP_HIDDEN, HEAD_DIM)),
                  full((2 * wide, NSA_CMP_HIDDEN)), full((NSA_CMP_HIDDEN, HEAD_DIM)),
                  full((N_CMP_PAD, LANES)), full((N_CMP_PAD, LANES)), full((N_CMP_PAD, LANES))],
        out_specs=(ospec, ospec),
        compiler_params=_cparams(("arbitrary",)),
        name="compress",
    )(tk2, tv2, pek, pev, w1k, w2k, w1v, w2v, *tabs)


def _qk(q, k):
    return lax.dot_general(q, k, (((1,), (1,)), ((), ())), preferred_element_type=F32)


def _lane_fold(x, op):
    out = x[:, 0:LANES]
    for u in range(1, x.shape[1] // LANES):
        out = op(out, x[:, u * LANES:(u + 1) * LANES])
    return out


def _attend(q, k_ref, v_ref, bias_fn, j_lo, n_tiles, s_scr, *, tk, groups=1):
    rows = q.shape[0]
    dv = v_ref.shape[-1]

    def pass1(t, mx):
        off = pl.multiple_of((j_lo + t) * tk, tk)
        s = _qk(q, k_ref[pl.ds(off, tk), :])
        bias = bias_fn(j_lo + t)
        if groups > 1:
            s = (s.reshape(groups, rows // groups, tk) + bias[None]).reshape(rows, tk)
        else:
            s = s + bias
        s_scr[t] = s
        return jnp.maximum(mx, _lane_fold(s, jnp.maximum))

    mx = lax.fori_loop(0, n_tiles, pass1, jnp.full((rows, LANES), NEG_INF, F32))
    m = jnp.broadcast_to(jnp.max(mx, axis=-1, keepdims=True), (rows, LANES))

    def pass2(t, carry):
        lsum, acc = carry
        off = pl.multiple_of((j_lo + t) * tk, tk)
        s = s_scr[t]
        parts = [jnp.exp(s[:, u * LANES:(u + 1) * LANES] - m) for u in range(tk // LANES)]
        for p in parts:
            lsum = lsum + p
        pb = jnp.concatenate([p.astype(BF16) for p in parts], axis=1)
        acc = acc + jnp.dot(pb, v_ref[pl.ds(off, tk), :], preferred_element_type=F32)
        return lsum, acc

    lsum, acc = lax.fori_loop(0, n_tiles, pass2, (jnp.zeros((rows, LANES), F32), jnp.zeros((rows, dv), F32)))
    return acc / jnp.sum(lsum, axis=-1, keepdims=True)


def _flash_kernel(q_ref, k_ref, v_ref, bias_ref, o_ref, s_scr):
    i = pl.program_id(2)
    last = bias_ref.shape[0] - 1
    out = _attend(q_ref[...], k_ref, v_ref, lambda j: bias_ref[jnp.minimum(i - j, last)], 0, i + 1, s_scr,
                  tk=FLASH_T)
    o_ref[...] = out.astype(o_ref.dtype)


def _flash(q_arr, k_arr, v_arr, bias, *, nb, heads, dqk, dv, q_blk0, k_blk0, v_blk0):
    t = FLASH_T
    nq = SEQ // t
    return pl.pallas_call(
        _flash_kernel,
        out_shape=jax.ShapeDtypeStruct((nb * SEQ, heads * dv), BF16),
        grid=(nb, heads, nq),
        in_specs=[pl.BlockSpec((t, dqk), lambda b, h, i: (b * nq + i, q_blk0 + h)),
                  pl.BlockSpec((SEQ, dqk), lambda b, h, i: (b, k_blk0 + h)),
                  pl.BlockSpec((SEQ, dv), lambda b, h, i: (b, v_blk0 + h)),
                  pl.BlockSpec(bias.shape, lambda b, h, i: (0, 0, 0))],
        out_specs=pl.BlockSpec((t, dv), lambda b, h, i: (b * nq + i, h)),
        scratch_shapes=[pltpu.VMEM((nq, t, t), F32)],
        compiler_params=_cparams(("arbitrary", "arbitrary", "arbitrary")),
        name="flash",
    )(q_arr, k_arr, v_arr, bias)


def _split3(x):
    hi = x.astype(BF16)
    r1 = x - hi.astype(F32)
    mid = r1.astype(BF16)
    lo = (r1 - mid.astype(F32)).astype(BF16)
    return hi, mid, lo


def _nsa_kernel(q0_ref, q1_ref, q2_ref, q3_ref, q4_ref, q5_ref, kc_ref, vc_ref, ks_ref, vs_ref, kw_ref, vw_ref,
                gate_ref, cover_ref, expand_ref, wbias_ref, o_ref, bias_scr, s_scr):
    i = pl.program_id(1)
    tq = NSA_T
    nh = NSA_HEADS
    rows = nh * tq
    pos = i * tq + lax.broadcasted_iota(jnp.int32, (tq, 1), 0)
    q = jnp.concatenate([r[...] for r in (q0_ref, q1_ref, q2_ref, q3_ref, q4_ref, q5_ref)], axis=0)

    cidx = lax.broadcasted_iota(jnp.int32, (tq, N_CMP_PAD), 1)
    valid_c = (cidx * NSA_CMP_STRIDE + (NSA_CMP_LEN - 1) <= pos) & (cidx < N_CMP_PAD - 1)
    valid_cf = valid_c.astype(F32)
    cbias = jnp.where(valid_c, 0.0, NEG_INF).astype(F32)
    s = (_qk(q, kc_ref[0]).reshape(nh, tq, N_CMP_PAD) + cbias[None]).reshape(rows, N_CMP_PAD)
    p = (jnp.exp(s - jnp.max(s, axis=-1, keepdims=True)).reshape(nh, tq, N_CMP_PAD) * valid_cf[None])
    p = p.reshape(rows, N_CMP_PAD)
    l = jnp.sum(p, axis=-1, keepdims=True)
    p = p / jnp.where(l > 0.0, l, 1.0)
    o_cmp = jnp.dot(p.astype(BF16), vc_ref[0], preferred_element_type=F32)
    psum = jnp.sum(p.reshape(nh, tq, N_CMP_PAD), axis=0)

    cover = cover_ref[...]
    imp = sum(jnp.dot(part, cover, preferred_element_type=F32) for part in _split3(psum))

    jidx = lax.broadcasted_iota(jnp.int32, (tq, N_SEL_BLOCKS), 1)
    qblk = pos // NSA_SEL_BLOCK
    valid_s = jidx * NSA_SEL_BLOCK <= pos
    forced = (jidx == 0) | (jidx == qblk) | (jidx == qblk - 1)
    score = jnp.where(valid_s, jnp.where(forced, NSA_FORCE_SCORE, imp), -1.0)
    rank = jnp.zeros((tq, N_SEL_BLOCKS), jnp.int32)
    for ii in range(N_SEL_BLOCKS):
        si = score[:, ii:ii + 1]
        rank = rank + ((si > score) | ((si == score) & (ii < jidx))).astype(jnp.int32)
    sel = (rank < NSA_TOP_N).astype(BF16)

    def fill(j, carry):
        hit = jnp.dot(sel, expand_ref[j], preferred_element_type=F32)
        kpos = j * tq + lax.broadcasted_iota(jnp.int32, (tq, tq), 1)
        bias_scr[j] = jnp.where((hit > 0.5) & (kpos <= pos), 0.0, NEG_INF).astype(F32)
        return carry

    lax.fori_loop(0, i + 1, fill, 0)

    o_slc = _attend(q, ks_ref, vs_ref, lambda j: bias_scr[j], 0, i + 1, s_scr, tk=tq, groups=nh)
    w_lo = jnp.maximum(i - (wbias_ref.shape[0] - 1), 0)
    o_win = _attend(q, kw_ref, vw_ref, lambda j: wbias_ref[i - j], w_lo, i + 1 - w_lo, s_scr, tk=tq, groups=nh)

    gates = jax.nn.sigmoid(gate_ref[...].astype(F32))
    for h in range(nh):
        sl = slice(h * tq, (h + 1) * tq)
        g0 = NSA_BRANCHES * h
        out = (gates[:, g0:g0 + 1] * o_cmp[sl] + gates[:, g0 + 1:g0 + 2] * o_slc[sl]
               + gates[:, g0 + 2:g0 + 3] * o_win[sl])
        o_ref[:, h * LANES:(h + 1) * LANES] = out.astype(o_ref.dtype)


def _nsa(hproj, kc, vc, cover, expand, wbias, nb):
    t = NSA_T
    nq = SEQ // t
    wide = NSA_HEADS * HEAD_DIM
    q_spec = lambda h: pl.BlockSpec((t, LANES), lambda b, i: (b * nq + i, BLK_NQ + h))
    kv_spec = lambda blk: pl.BlockSpec((SEQ, LANES), lambda b, i: (b, blk))
    cspec = pl.BlockSpec((1, N_CMP_PAD, HEAD_DIM), lambda b, i: (b, 0, 0))
    full = lambda shape: pl.BlockSpec(shape, lambda b, i: (0,) * len(shape))
    return pl.pallas_call(
        _nsa_kernel,
        out_shape=jax.ShapeDtypeStruct((nb * SEQ, wide), BF16),
        grid=(nb, nq),
        in_specs=[q_spec(h) for h in range(NSA_HEADS)]
        + [cspec, cspec, kv_spec(BLK_NKS), kv_spec(BLK_NVS), kv_spec(BLK_NKW), kv_spec(BLK_NVW),
           pl.BlockSpec((t, LANES), lambda b, i: (b * nq + i, BLK_GATE)),
           full(cover.shape), full(expand.shape), full(wbias.shape)],
        out_specs=pl.BlockSpec((t, wide), lambda b, i: (b * nq + i, 0)),
        scratch_shapes=[pltpu.VMEM((nq, t, t), F32), pltpu.VMEM((nq, NSA_HEADS * t, t), F32)],
        compiler_params=_cparams(("arbitrary", "arbitrary")),
        name="nsa",
    )(*([hproj] * NSA_HEADS), kc, vc, hproj, hproj, hproj, hproj, hproj, cover, expand, wbias)


def _layer_norm(y, g, b):
    mu = jnp.mean(y, axis=-1, keepdims=True)
    var = jnp.mean(jnp.square(y - mu), axis=-1, keepdims=True)
    return (y - mu) * lax.rsqrt(var + LN_EPS) * g + b


def _lane_min(x):
    return jnp.min(x, axis=-1, keepdims=True)


def _lane_max(x):
    return jnp.max(x, axis=-1, keepdims=True)


def _route(logits):
    lane = lax.broadcasted_iota(jnp.int32, logits.shape, 1)
    lane_f = lane.astype(F32)
    far = float(LANES)
    is_grp = lane < N_GROUPS
    lg = jnp.where(is_grp, logits, NEG_INF)
    eg = jnp.where(is_grp, jnp.exp(lg - _lane_max(lg)), 0.0)
    prob = eg / jnp.sum(eg, axis=-1, keepdims=True)
    p_g = _lane_max(prob)
    g_idx = _lane_min(jnp.where(is_grp & (prob == p_g), lane_f, far))
    e_lo = N_GROUPS + EXPERTS_PER_GROUP * g_idx
    in_grp = (lane_f >= e_lo) & (lane_f < e_lo + EXPERTS_PER_GROUP)
    le = jnp.where(in_grp, logits, NEG_INF)
    v1 = _lane_max(le)
    i1 = _lane_min(jnp.where(in_grp & (le == v1), lane_f, far))
    rest = in_grp & (lane_f != i1)
    le2 = jnp.where(rest, logits, NEG_INF)
    v2 = _lane_max(le2)
    i2 = _lane_min(jnp.where(rest & (le2 == v2), lane_f, far))
    e21 = jnp.exp(v2 - v1)
    den = 1.0 + e21
    gate1 = p_g * (1.0 / den)
    gate2 = p_g * (e21 / den)
    out = jnp.where(lane == 0, i1 - N_GROUPS, 0.0)
    out = jnp.where(lane == 1, i2 - N_GROUPS, out)
    out = jnp.where(lane == 2, gate1, out)
    out = jnp.where(lane == 3, gate2, out)
    return out


def _outproj_kernel(om_ref, od_ref, on_ref, w_ref, x_ref, g_ref, b_ref, wr_ref, x1_ref, route_ref):
    n_mla = MLA_HEADS * HEAD_DIM
    n_dil = DIL_HEADS * HEAD_DIM
    mix = jnp.dot(om_ref[...], w_ref[0, 0:n_mla, :], preferred_element_type=F32)
    mix = mix + jnp.dot(od_ref[...], w_ref[0, n_mla:n_mla + n_dil, :], preferred_element_type=F32)
    mix = mix + jnp.dot(on_ref[...], w_ref[0, n_mla + n_dil:, :], preferred_element_type=F32)
    x1 = _layer_norm(DN_ALPHA * x_ref[...] + mix, g_ref[...], b_ref[...])
    x1_ref[...] = x1
    xh, xm, xl = _split3(x1)
    wh, wm, wl = wr_ref[0], wr_ref[1], wr_ref[2]
    dot = functools.partial(jnp.dot, preferred_element_type=F32)
    logits = (dot(xm, wm) + dot(xh, wl) + dot(xl, wh)) + (dot(xh, wm) + dot(xm, wh)) + dot(xh, wh)
    route_ref[...] = _route(logits)


def _outproj(o_mla, o_dil, o_nsa, w_out_s, layer, x, g, b, wr3):
    t = x.shape[0]
    tm = ROW_TM
    full = lambda shape: pl.BlockSpec(shape, lambda i: (0,) * len(shape))
    row = lambda w: pl.BlockSpec((tm, w), lambda i: (i, 0))
    return pl.pallas_call(
        _outproj_kernel,
        out_shape=(jax.ShapeDtypeStruct((t, D_MODEL), F32), jax.ShapeDtypeStruct((t, LANES), F32)),
        grid=(t // tm,),
        in_specs=[row(o_mla.shape[1]), row(o_dil.shape[1]), row(o_nsa.shape[1]),
                  pl.BlockSpec((1,) + w_out_s.shape[1:], lambda i: (layer, 0, 0)),
                  row(D_MODEL), full((1, D_MODEL)), full((1, D_MODEL)), full(wr3.shape)],
        out_specs=(row(D_MODEL), row(LANES)),
        compiler_params=_cparams(("arbitrary",)),
        name="outproj",
    )(o_mla, o_dil, o_nsa, w_out_s, x, g, b, wr3)


def _experts_kernel(blk_ex_ref, nused_ref, tok_cur_ref, tok_nxt_ref, x_hbm, wg_ref, wu_ref, wd_ref, y_ref,
                    xbuf, wg_b, wu_b, wd_b, sem):
    b = pl.program_id(0)
    nused = nused_ref[0]

    def issue(tok_ref, slot):
        def body(r, carry):
            t = tok_ref[0, 0, r]
            pltpu.make_async_copy(x_hbm.at[pl.ds(t, 1)], xbuf.at[slot, pl.ds(r, 1)], sem.at[slot]).start()
            return carry

        lax.fori_loop(0, MOE_ROWS, body, 0)

    @pl.when(b == 0)
    def _():
        issue(tok_cur_ref, 0)

    @pl.when(b + 1 < nused)
    def _():
        issue(tok_nxt_ref, (b + 1) % 2)

    @pl.when(b < nused)
    def _():
        slot = b % 2
        pltpu.make_async_copy(x_hbm.at[pl.ds(0, MOE_ROWS)], xbuf.at[slot], sem.at[slot]).wait()
        changed = jnp.logical_or(b == 0, blk_ex_ref[b] != blk_ex_ref[jnp.maximum(b - 1, 0)])

        @pl.when(changed)
        def _():
            wg_b[...] = wg_ref[0, 0].astype(BF16)
            wu_b[...] = wu_ref[0, 0].astype(BF16)
            wd_b[...] = wd_ref[0, 0].astype(BF16)

        rows = xbuf[slot].astype(BF16)
        gate = jnp.dot(rows, wg_b[...], preferred_element_type=F32)
        up = jnp.dot(rows, wu_b[...], preferred_element_type=F32)
        hid = (gate * jax.nn.sigmoid(gate) * up).astype(BF16)
        y_ref[...] = jnp.dot(hid, wd_b[...], preferred_element_type=F32)

    @pl.when(b >= nused)
    def _():
        y_ref[...] = jnp.zeros_like(y_ref)


def _experts(blk_ex, nused, buf_tok3, x1, w_gate, w_up, w_down, layer):
    n_blocks = buf_tok3.shape[0]
    tok_spec = lambda shift: pl.BlockSpec(
        (1, 1, MOE_ROWS), lambda b, ex, nu: (jnp.minimum(b + shift, n_blocks - 1), 0, 0),
        memory_space=pltpu.SMEM)
    w_spec = lambda shape: pl.BlockSpec((1, 1) + shape, lambda b, ex, nu: (layer, ex[b], 0, 0))
    grid_spec = pltpu.PrefetchScalarGridSpec(
        num_scalar_prefetch=2,
        grid=(n_blocks,),
        in_specs=[tok_spec(0), tok_spec(1), pl.BlockSpec(memory_space=pl.ANY),
                  w_spec((D_MODEL, EXPERT_FF)), w_spec((D_MODEL, EXPERT_FF)), w_spec((EXPERT_FF, D_MODEL))],
        out_specs=pl.BlockSpec((MOE_ROWS, D_MODEL), lambda b, ex, nu: (b, 0)),
        scratch_shapes=[pltpu.VMEM((2, MOE_ROWS, D_MODEL), F32),
                        pltpu.VMEM((D_MODEL, EXPERT_FF), BF16), pltpu.VMEM((D_MODEL, EXPERT_FF), BF16),
                        pltpu.VMEM((EXPERT_FF, D_MODEL), BF16), pltpu.SemaphoreType.DMA((2,))],
    )
    return pl.pallas_call(
        _experts_kernel,
        out_shape=jax.ShapeDtypeStruct((n_blocks * MOE_ROWS, D_MODEL), F32),
        grid_spec=grid_spec,
        compiler_params=_cparams(("arbitrary",)),
        name="experts",
    )(blk_ex, nused, buf_tok3, buf_tok3, x1, w_gate, w_up, w_down)


def _combine_kernel(pos_cur_ref, pos_nxt_ref, y_hbm, x1_ref, route_ref, g_ref, b_ref, o_ref, ob_ref, ybuf, sem):
    i = pl.program_id(0)
    n = pl.num_programs(0)
    tm = ROW_TM

    def issue(pos_ref, slot):
        def body(r, carry):
            for k in range(MOE_TOP_K):
                p = pos_ref[0, 0, MOE_TOP_K * r + k]
                pltpu.make_async_copy(y_hbm.at[pl.ds(p, 1)], ybuf.at[slot, k, pl.ds(r, 1)], sem.at[slot]).start()
            return carry

        lax.fori_loop(0, tm, body, 0)

    @pl.when(i == 0)
    def _():
        issue(pos_cur_ref, 0)

    @pl.when(i + 1 < n)
    def _():
        issue(pos_nxt_ref, (i + 1) % 2)

    slot = i % 2
    for k in range(MOE_TOP_K):
        pltpu.make_async_copy(y_hbm.at[pl.ds(0, tm)], ybuf.at[slot, k], sem.at[slot]).wait()
    route = route_ref[...]
    ffn = route[:, 2:3] * ybuf[slot, 0] + route[:, 3:4] * ybuf[slot, 1]
    x2 = _layer_norm(DN_ALPHA * x1_ref[...] + ffn, g_ref[...], b_ref[...])
    o_ref[...] = x2
    ob_ref[...] = x2.astype(ob_ref.dtype)


def _combine(pos3, yb, x1, route, g, b):
    t = x1.shape[0]
    tm = ROW_TM
    nt = t // tm
    pos_spec = lambda shift: pl.BlockSpec((1, 1, MOE_TOP_K * tm), lambda i: (jnp.minimum(i + shift, nt - 1), 0, 0),
                                          memory_space=pltpu.SMEM)
    row = lambda w: pl.BlockSpec((tm, w), lambda i: (i, 0))
    full = lambda shape: pl.BlockSpec(shape, lambda i: (0,) * len(shape))
    return pl.pallas_call(
        _combine_kernel,
        out_shape=(jax.ShapeDtypeStruct((t, D_MODEL), F32), jax.ShapeDtypeStruct((t, D_MODEL), BF16)),
        grid=(nt,),
        in_specs=[pos_spec(0), pos_spec(1), pl.BlockSpec(memory_space=pl.ANY), row(D_MODEL), row(LANES),
                  full((1, D_MODEL)), full((1, D_MODEL))],
        out_specs=(row(D_MODEL), row(D_MODEL)),
        scratch_shapes=[pltpu.VMEM((2, MOE_TOP_K, tm, D_MODEL), F32), pltpu.SemaphoreType.DMA((2,))],
        compiler_params=_cparams(("arbitrary",)),
        name="combine",
    )(pos3, pos3, yb, x1, route, g, b)


def _dispatch_plan(route, t):
    m = t * MOE_TOP_K
    n_blocks = -(-m // MOE_ROWS) + N_EXPERTS
    ex = route[:, 0:MOE_TOP_K].astype(jnp.int32).reshape(m)
    order = jnp.argsort(ex, stable=True).astype(jnp.int32)
    inv = jnp.argsort(order).astype(jnp.int32)
    counts = jnp.sum((ex[:, None] == jnp.arange(N_EXPERTS, dtype=jnp.int32)[None, :]).astype(jnp.int32), axis=0)
    seg_start = jnp.cumsum(counts) - counts
    padded = (counts + MOE_ROWS - 1) // MOE_ROWS * MOE_ROWS
    pad_end = jnp.cumsum(padded)
    pad_start = pad_end - padded
    blk_start = jnp.arange(n_blocks, dtype=jnp.int32) * MOE_ROWS
    blk_ex = jnp.minimum(jnp.sum((pad_end[None, :] <= blk_start[:, None]).astype(jnp.int32), axis=1),
                         N_EXPERTS - 1).astype(jnp.int32)
    nused = (pad_end[-1] // MOE_ROWS).astype(jnp.int32).reshape(1)
    p = jnp.arange(n_blocks * MOE_ROWS, dtype=jnp.int32)
    e_of_p = jnp.repeat(blk_ex, MOE_ROWS)
    within = p - pad_start[e_of_p]
    src = jnp.clip(seg_start[e_of_p] + within, 0, m - 1)
    buf_tok = jnp.where(within < counts[e_of_p], order[src] // MOE_TOP_K, 0).astype(jnp.int32)
    pos = pad_start[ex] + inv - seg_start[ex]
    return blk_ex, nused, buf_tok.reshape(n_blocks, 1, MOE_ROWS), pos.astype(jnp.int32)


def _shift_kernel(w_ref, tail_ref, o_ref):
    n_keep = (IN_SHIFT_FROM + MLA_ROPE_DIM) // LANES
    n_src = IN_COLS // LANES
    o_ref[0, :, 0:n_keep * LANES] = w_ref[0, :, 0:n_keep * LANES].astype(o_ref.dtype)
    low = lax.broadcasted_iota(jnp.int32, (w_ref.shape[1], LANES), 1) < MLA_ROPE_DIM

    def src(m):
        if m < n_src:
            return w_ref[0, :, m * LANES:(m + 1) * LANES]
        return tail_ref[0] if m == n_src else jnp.zeros((w_ref.shape[1], LANES), F32)

    for m in range(n_keep, N_IN_BLOCKS):
        blk = jnp.where(low, pltpu.roll(src(m - 1), MLA_ROPE_DIM, 1), pltpu.roll(src(m), MLA_ROPE_DIM, 1))
        o_ref[0, :, m * LANES:(m + 1) * LANES] = blk.astype(o_ref.dtype)


def _shift_w_in(w_in):
    nl, d, _ = w_in.shape
    n_src = IN_COLS // LANES
    tail = jnp.pad(w_in[:, :, n_src * LANES:], ((0, 0), (0, 0), (0, (n_src + 1) * LANES - IN_COLS)))
    tk = ROW_TM
    return pl.pallas_call(
        _shift_kernel,
        out_shape=jax.ShapeDtypeStruct((nl, d, IN_COLS_PAD), BF16),
        grid=(nl, d // tk),
        in_specs=[pl.BlockSpec((1, tk, IN_COLS), lambda l, r: (l, r, 0)),
                  pl.BlockSpec((1, tk, LANES), lambda l, r: (l, r, 0))],
        out_specs=pl.BlockSpec((1, tk, IN_COLS_PAD), lambda l, r: (l, r, 0)),
        compiler_params=_cparams(("arbitrary", "arbitrary")),
        name="shiftw",
    )(w_in, tail)


def _permute_w_q_up(w_q_up):
    w = w_q_up.reshape(MLA_Q_RANK, MLA_HEADS, MLA_NOPE_DIM + MLA_ROPE_DIM)
    zero = jnp.zeros((MLA_Q_RANK, MLA_HEADS, MLA_QK_PAD - MLA_NOPE_DIM - MLA_ROPE_DIM), w.dtype)
    w = jnp.concatenate([w[:, :, :MLA_NOPE_DIM], zero, w[:, :, MLA_NOPE_DIM:]], axis=-1)
    return w.reshape(MLA_Q_RANK, MLA_HEADS * MLA_QK_PAD).astype(BF16)


def _split_w_kv_up(w_kv_up):
    w = w_kv_up.reshape(MLA_KV_RANK, MLA_HEADS, MLA_NOPE_DIM + HEAD_DIM)
    wkn = w[:, :, :MLA_NOPE_DIM].reshape(MLA_KV_RANK, MLA_HEADS * MLA_NOPE_DIM)
    wv = w[:, :, MLA_NOPE_DIM:].reshape(MLA_KV_RANK, MLA_HEADS * HEAD_DIM)
    return wkn.astype(BF16), wv.astype(BF16)


def _router_slices(w_grp, w_exp):
    w = jnp.concatenate([w_grp, w_exp, jnp.zeros((D_MODEL, LANES - N_GROUPS - N_EXPERTS), F32)], axis=1)
    hi = w.astype(BF16)
    r1 = w - hi.astype(F32)
    mid = r1.astype(BF16)
    lo = (r1 - mid.astype(F32)).astype(BF16)
    return jnp.stack([hi, mid, lo], axis=0)


def _mixer(xb, nb, layer, w_in_s, q_lat_norm, w_q_up, kv_lat_norm, w_kv_up,
           cmp_pos_k, cmp_w1_k, cmp_w2_k, cmp_pos_v, cmp_w1_v, cmp_w2_v, consts):
    hproj = _inproj(xb, w_in_s, layer, consts["rope32"])
    wkn, wv = _split_w_kv_up(w_kv_up)
    q_mla, k_mla, v_mla = _mlaprep(hproj, q_lat_norm.reshape(1, -1), kv_lat_norm.reshape(1, -1),
                                   _permute_w_q_up(w_q_up), wkn, wv, consts["rope64"])
    wide = NSA_CMP_STRIDE * HEAD_DIM
    tk2 = hproj[:, BLK_NKC * LANES:(BLK_NKC + 1) * LANES].reshape(nb, N_CMP_PAD, wide)
    tv2 = hproj[:, BLK_NVC * LANES:(BLK_NVC + 1) * LANES].reshape(nb, N_CMP_PAD, wide)
    kc, vc = _compress(tk2, tv2, cmp_pos_k.reshape(2, wide), cmp_pos_v.reshape(2, wide),
                       cmp_w1_k.astype(BF16), cmp_w2_k.astype(BF16), cmp_w1_v.astype(BF16), cmp_w2_v.astype(BF16),
                       consts["rope_cmp"])
    o_mla = _flash(q_mla, k_mla, v_mla, consts["bias_causal"], nb=nb, heads=MLA_HEADS, dqk=MLA_QK_PAD,
                   dv=HEAD_DIM, q_blk0=0, k_blk0=0, v_blk0=0)
    o_dil = _flash(hproj, hproj, hproj, consts["bias_dil"], nb=nb, heads=DIL_HEADS, dqk=HEAD_DIM, dv=HEAD_DIM,
                   q_blk0=BLK_DQ, k_blk0=BLK_DK, v_blk0=BLK_DV)
    o_nsa = _nsa(hproj, kc, vc, consts["cover"], consts["expand"], consts["bias_win"], nb)
    return o_mla, o_dil, o_nsa


def _make_consts():
    pos = np.arange(SEQ)
    slot = np.arange(N_CMP_PAD)
    cmp_start = slot * NSA_CMP_STRIDE
    cmp_end = cmp_start + NSA_CMP_LEN - 1
    sel_start = np.arange(N_SEL_BLOCKS) * NSA_SEL_BLOCK
    cover = ((cmp_start[:, None] < sel_start[None, :] + NSA_SEL_BLOCK)
             & (cmp_start[:, None] + NSA_CMP_LEN > sel_start[None, :])
             & (slot[:, None] < N_CMP_PAD - 1)).astype(np.float32)
    expand = (np.arange(N_SEL_BLOCKS)[:, None] == (pos // NSA_SEL_BLOCK)[None, :]).astype(np.float32)
    expand = expand.reshape(N_SEL_BLOCKS, SEQ // NSA_T, NSA_T).transpose(1, 0, 2)
    n_win = -(-NSA_WINDOW // NSA_T) + 1
    return {
        "rope32": _rope_tables(pos, ROT_DIM, True),
        "rope64": _rope_tables(pos, MLA_ROPE_DIM, False, KROPE_LANE0),
        "rope_cmp": _rope_tables(cmp_end, ROT_DIM, True),
        "bias_causal": _distance_bias(_causal_mult, (0, 1), FLASH_T),
        "bias_dil": _distance_bias(_dilated_mult, range(SEQ // FLASH_T), FLASH_T),
        "bias_win": _distance_bias(_window_mult, range(n_win), NSA_T),
        "cover": jnp.asarray(cover, BF16),
        "expand": jnp.asarray(expand, BF16),
    }


def kernel(x, w_in, q_lat_norm, w_q_up, kv_lat_norm, w_kv_up, cmp_pos_k, cmp_w1_k, cmp_w2_k, cmp_pos_v, cmp_w1_v,
           cmp_w2_v, w_out, ln1_g, ln1_b, w_grp, w_exp, w_gate, w_up, w_down, ln2_g, ln2_b):
    nb, s, d = x.shape
    assert s == SEQ and d == D_MODEL
    t = nb * s
    consts = _make_consts()
    w_in_s = _shift_w_in(w_in)
    w_out_s = w_out.astype(BF16)
    xf = x.reshape(t, d)
    xb = xf.astype(BF16)
    for l in range(DEPTH):
        o_mla, o_dil, o_nsa = _mixer(xb, nb, l, w_in_s, q_lat_norm[l], w_q_up[l], kv_lat_norm[l], w_kv_up[l],
                                     cmp_pos_k[l], cmp_w1_k[l], cmp_w2_k[l], cmp_pos_v[l], cmp_w1_v[l],
                                     cmp_w2_v[l], consts)
        x1, route = _outproj(o_mla, o_dil, o_nsa, w_out_s, l, xf, ln1_g[l].reshape(1, d),
                             ln1_b[l].reshape(1, d), _router_slices(w_grp[l], w_exp[l]))
        blk_ex, nused, buf_tok3, pos = _dispatch_plan(route, t)
        yb = _experts(blk_ex, nused, buf_tok3, x1, w_gate, w_up, w_down, l)
        xf, xb = _combine(pos.reshape(t // ROW_TM, 1, MOE_TOP_K * ROW_TM), yb, x1, route,
                          ln2_g[l].reshape(1, d), ln2_b[l].reshape(1, d))
    return xf.reshape(nb, s, d)
```

```python
import functools

import numpy as np
import jax
import jax.numpy as jnp
from jax import lax
from jax.experimental import pallas as pl
from jax.experimental.pallas import tpu as pltpu

F32 = jnp.float32
BF16 = jnp.bfloat16

D_MODEL = 2048
SEQ = 2048
DEPTH = 2
HEAD_DIM = 128
LANES = 128
MLA_HEADS = 4
DIL_HEADS = 6
NSA_HEADS = 6
MLA_Q_RANK = 384
MLA_KV_RANK = 128
MLA_NOPE_DIM = 128
MLA_ROPE_DIM = 64
MLA_QK_PAD = 256
DIL_PAIRS = ((128, 1), (512, 4), (2048, 16))
NSA_CMP_LEN = 32
NSA_CMP_STRIDE = 16
NSA_CMP_HIDDEN = 256
NSA_SEL_BLOCK = 64
NSA_TOP_N = 16
NSA_WINDOW = 512
NSA_BRANCHES = 3
NSA_FORCE_SCORE = 1e4
ROPE_THETA = 500000.0
ROT_DIM = HEAD_DIM // 4
N_GROUPS = 4
EXPERTS_PER_GROUP = 8
N_EXPERTS = N_GROUPS * EXPERTS_PER_GROUP
EXPERT_FF = 512
MOE_TOP_K = 2
MOE_ROWS = 256
DN_ALPHA = (2 * DEPTH) ** 0.25
LN_EPS = 1e-5
RMS_EPS = 1e-6
NEG_INF = -1e30

N_CMP_PAD = SEQ // NSA_CMP_STRIDE
N_SEL_BLOCKS = SEQ // NSA_SEL_BLOCK

BLK_QLAT = 0
BLK_KVLAT = 3
BLK_KROPE = 4
BLK_DQ = 5
BLK_DK = 11
BLK_DV = 17
BLK_NQ = 23
BLK_NKC = 29
BLK_NVC = 30
BLK_NKS = 31
BLK_NVS = 32
BLK_NKW = 33
BLK_NVW = 34
BLK_GATE = 35
N_IN_BLOCKS = 36
IN_COLS = 4434
IN_SHIFT_FROM = MLA_Q_RANK + MLA_KV_RANK - MLA_ROPE_DIM
IN_COLS_PAD = N_IN_BLOCKS * LANES
KROPE_LANE0 = LANES - MLA_ROPE_DIM

INPROJ_TM = 512
INPROJ_TN = 512
FLASH_T = 512
NSA_T = 256
ROW_TM = 256
DMA_UNROLL = 8

VMEM_LIMIT = 56 * 1024 * 1024


def _cparams(sem):
    return pltpu.CompilerParams(dimension_semantics=sem, vmem_limit_bytes=VMEM_LIMIT)


def _rope_tables(pos, rot_dim, keep_rest, lane0=0):
    half = rot_dim // 2
    inv_freq = np.power(np.float32(ROPE_THETA), -np.arange(half, dtype=np.float32) / np.float32(half))
    ang = (pos.astype(np.float32)[:, None] * inv_freq[None, :]).astype(np.float32)
    cos, sin = np.cos(ang).astype(np.float32), np.sin(ang).astype(np.float32)
    n = pos.shape[0]
    c = np.full((n, LANES), 1.0 if keep_rest else 0.0, np.float32)
    s1 = np.zeros((n, LANES), np.float32)
    s2 = np.zeros((n, LANES), np.float32)
    c[:, lane0:lane0 + half] = cos
    c[:, lane0 + half:lane0 + rot_dim] = cos
    s1[:, lane0 + half:lane0 + rot_dim] = sin
    s2[:, lane0:lane0 + half] = -sin
    return jnp.asarray(c), jnp.asarray(s1), jnp.asarray(s2)


def _rope_lanes(a, c, s1, s2, half):
    return a * c + pltpu.roll(a, half, 1) * s1 + pltpu.roll(a, LANES - half, 1) * s2


def _distance_bias(mult_fn, diags, t):
    d = np.asarray(diags, np.int64)[:, None, None] * t
    r = np.arange(t, dtype=np.int64)[None, :, None]
    c = np.arange(t, dtype=np.int64)[None, None, :]
    mult = mult_fn(d + r - c)
    return jnp.asarray(np.where(mult > 0, np.log(np.maximum(mult, 1).astype(np.float32)), NEG_INF).astype(np.float32))


def _causal_mult(dist):
    return (dist >= 0).astype(np.int64)


def _dilated_mult(dist):
    m = np.zeros_like(dist)
    for window, dil in DIL_PAIRS:
        m = m + ((dist >= 0) & (dist % dil == 0) & (dist <= (window // dil) * dil)).astype(np.int64)
    return m


def _window_mult(dist):
    return ((dist >= 0) & (dist <= NSA_WINDOW - 1)).astype(np.int64)


def _in_block(blk, lo, n):
    return jnp.logical_and(blk >= lo, blk < lo + n)


def _inproj_kernel(x_ref, w_ref, c_ref, s1_ref, s2_ref, o_ref, *, scale):
    j = pl.program_id(0)
    acc = jnp.dot(x_ref[...], w_ref[0], preferred_element_type=F32)
    c, s1, s2 = c_ref[...], s1_ref[...], s2_ref[...]
    for u in range(INPROJ_TN // LANES):
        blk = j * (INPROJ_TN // LANES) + u
        is_q = jnp.logical_or(_in_block(blk, BLK_DQ, DIL_HEADS), _in_block(blk, BLK_NQ, NSA_HEADS))
        is_rope = is_q | _in_block(blk, BLK_DK, DIL_HEADS) | (blk == BLK_NKS) | (blk == BLK_NKW)
        sc = jnp.where(is_q, scale, 1.0).astype(F32)
        a = acc[:, u * LANES:(u + 1) * LANES]
        r = _rope_lanes(a, jnp.where(is_rope, c, 1.0), jnp.where(is_rope, s1, 0.0), jnp.where(is_rope, s2, 0.0),
                        ROT_DIM // 2)
        o_ref[:, u * LANES:(u + 1) * LANES] = (r * sc).astype(o_ref.dtype)


def _inproj(xb, w_in_s, layer, tabs):
    t = xb.shape[0]
    tm, tn = INPROJ_TM, INPROJ_TN
    nrow = SEQ // tm
    tab_spec = pl.BlockSpec((tm, LANES), lambda j, i: (i % nrow, 0))
    return pl.pallas_call(
        functools.partial(_inproj_kernel, scale=HEAD_DIM ** -0.5),
        out_shape=jax.ShapeDtypeStruct((t, IN_COLS_PAD), BF16),
        grid=(IN_COLS_PAD // tn, t // tm),
        in_specs=[pl.BlockSpec((tm, D_MODEL), lambda j, i: (i, 0)),
                  pl.BlockSpec((1, D_MODEL, tn), lambda j, i: (layer, 0, j)),
                  tab_spec, tab_spec, tab_spec],
        out_specs=pl.BlockSpec((tm, tn), lambda j, i: (i, j)),
        compiler_params=_cparams(("arbitrary", "arbitrary")),
        name="inproj",
    )(xb, w_in_s, *tabs)


def _rms(xf, g):
    return xf * lax.rsqrt(jnp.mean(jnp.square(xf), axis=-1, keepdims=True) + RMS_EPS) * g


def _mlaprep_kernel(ql_ref, kvl_ref, kr_ref, gq_ref, gkv_ref, wq_ref, wkn_ref, wv_ref,
                    c_ref, s1_ref, s2_ref, q_ref, k_ref, v_ref, *, scale):
    c, s1, s2 = c_ref[...], s1_ref[...], s2_ref[...]
    half = MLA_ROPE_DIM // 2
    qn = _rms(ql_ref[...].astype(F32), gq_ref[...]).astype(BF16)
    q = jnp.dot(qn, wq_ref[...], preferred_element_type=F32)
    for h in range(MLA_HEADS):
        lo = h * MLA_QK_PAD
        q_ref[:, lo:lo + LANES] = (q[:, lo:lo + LANES] * scale).astype(q_ref.dtype)
        a = q[:, lo + LANES:lo + 2 * LANES]
        q_ref[:, lo + LANES:lo + 2 * LANES] = (_rope_lanes(a, c, s1, s2, half) * scale).astype(q_ref.dtype)
    kvn = _rms(kvl_ref[...].astype(F32), gkv_ref[...]).astype(BF16)
    kn = jnp.dot(kvn, wkn_ref[...], preferred_element_type=F32)
    v_ref[...] = jnp.dot(kvn, wv_ref[...], preferred_element_type=F32).astype(v_ref.dtype)
    kpe = _rope_lanes(kr_ref[...].astype(F32), c, s1, s2, half).astype(k_ref.dtype)
    for h in range(MLA_HEADS):
        lo = h * MLA_QK_PAD
        k_ref[:, lo:lo + LANES] = kn[:, h * LANES:(h + 1) * LANES].astype(k_ref.dtype)
        k_ref[:, lo + LANES:lo + 2 * LANES] = kpe


def _mlaprep(hproj, gq, gkv, wq_p, wkn, wv, tabs):
    t = hproj.shape[0]
    tm = ROW_TM
    nrow = SEQ // tm
    tab_spec = pl.BlockSpec((tm, LANES), lambda i: (i % nrow, 0))
    full = lambda shape: pl.BlockSpec(shape, lambda i: (0,) * len(shape))
    return pl.pallas_call(
        functools.partial(_mlaprep_kernel, scale=(MLA_NOPE_DIM + MLA_ROPE_DIM) ** -0.5),
        out_shape=(jax.ShapeDtypeStruct((t, MLA_HEADS * MLA_QK_PAD), BF16),
                   jax.ShapeDtypeStruct((t, MLA_HEADS * MLA_QK_PAD), BF16),
                   jax.ShapeDtypeStruct((t, MLA_HEADS * HEAD_DIM), BF16)),
        grid=(t // tm,),
        in_specs=[pl.BlockSpec((tm, MLA_Q_RANK), lambda i: (i, BLK_QLAT * LANES // MLA_Q_RANK)),
                  pl.BlockSpec((tm, LANES), lambda i: (i, BLK_KVLAT)),
                  pl.BlockSpec((tm, LANES), lambda i: (i, BLK_KROPE)),
                  full((1, MLA_Q_RANK)), full((1, MLA_KV_RANK)),
                  full((MLA_Q_RANK, MLA_HEADS * MLA_QK_PAD)),
                  full((MLA_KV_RANK, MLA_HEADS * MLA_NOPE_DIM)),
                  full((MLA_KV_RANK, MLA_HEADS * HEAD_DIM)),
                  tab_spec, tab_spec, tab_spec],
        out_specs=(pl.BlockSpec((tm, MLA_HEADS * MLA_QK_PAD), lambda i: (i, 0)),
                   pl.BlockSpec((tm, MLA_HEADS * MLA_QK_PAD), lambda i: (i, 0)),
                   pl.BlockSpec((tm, MLA_HEADS * HEAD_DIM), lambda i: (i, 0))),
        compiler_params=_cparams(("arbitrary",)),
        name="mlaprep",
    )(hproj, hproj, hproj, gq, gkv, wq_p, wkn, wv, *tabs)


def _gelu_tanh(x):
    return 0.5 * x * (1.0 + jnp.tanh(0.7978845608028654 * (x + 0.044715 * (x * x * x))))


def _compress_kernel(tk_ref, tv_ref, pek_ref, pev_ref, w1k_ref, w2k_ref, w1v_ref, w2v_ref,
                     c_ref, s1_ref, s2_ref, kc_ref, vc_ref):
    half_in = NSA_CMP_STRIDE * HEAD_DIM

    def comp(t_ref, pe_ref, w1_ref, w2_ref):
        t = t_ref[0].astype(F32)
        a = (t + pe_ref[0:1, :]).astype(BF16)
        b = (t + pe_ref[1:2, :]).astype(BF16)
        y0 = jnp.dot(a, w1_ref[0:half_in, :], preferred_element_type=F32)
        y1 = jnp.dot(b, w1_ref[half_in:2 * half_in, :], preferred_element_type=F32)
        hid = _gelu_tanh(y0 + pltpu.roll(y1, N_CMP_PAD - 1, 0))
        return jnp.dot(hid.astype(BF16), w2_ref[...], preferred_element_type=F32)

    kc = comp(tk_ref, pek_ref, w1k_ref, w2k_ref)
    kc_ref[0] = _rope_lanes(kc, c_ref[...], s1_ref[...], s2_ref[...], ROT_DIM // 2).astype(kc_ref.dtype)
    vc_ref[0] = comp(tv_ref, pev_ref, w1v_ref, w2v_ref).astype(vc_ref.dtype)


def _compress(tk2, tv2, pek, pev, w1k, w2k, w1v, w2v, tabs):
    nb = tk2.shape[0]
    wide = NSA_CMP_STRIDE * HEAD_DIM
    full = lambda shape: pl.BlockSpec(shape, lambda b: (0,) * len(shape))
    bspec = pl.BlockSpec((1, N_CMP_PAD, wide), lambda b: (b, 0, 0))
    ospec = pl.BlockSpec((1, N_CMP_PAD, HEAD_DIM), lambda b: (b, 0, 0))
    return pl.pallas_call(
        _compress_kernel,
        out_shape=(jax.ShapeDtypeStruct((nb, N_CMP_PAD, HEAD_DIM), BF16),) * 2,
        grid=(nb,),
        in_specs=[bspec, bspec, full((2, wide)), full((2, wide)),
                  full((2 * wide, NSA_CMP_HIDDEN)), full((NSA_CMP_HIDDEN, HEAD_DIM)),
                  full((2 * wide, NSA_CMP_HIDDEN)), full((NSA_CMP_HIDDEN, HEAD_DIM)),
                  full((N_CMP_PAD, LANES)), full((N_CMP_PAD, LANES)), full((N_CMP_PAD, LANES))],
        out_specs=(ospec, ospec),
        compiler_params=_cparams(("arbitrary",)),
        name="compress",
    )(tk2, tv2, pek, pev, w1k, w2k, w1v, w2v, *tabs)


def _qk(q, k):
    return lax.dot_general(q, k, (((1,), (1,)), ((), ())), preferred_element_type=F32)


def _lane_fold(x, op):
    out = x[:, 0:LANES]
    for u in range(1, x.shape[1] // LANES):
        out = op(out, x[:, u * LANES:(u + 1) * LANES])
    return out


def _attend(q, k_ref, v_ref, bias_fn, j_lo, n_tiles, s_scr, *, tk, groups=1):
    rows = q.shape[0]
    dv = v_ref.shape[-1]

    def pass1(t, mx):
        off = pl.multiple_of((j_lo + t) * tk, tk)
        s = _qk(q, k_ref[pl.ds(off, tk), :])
        bias = bias_fn(j_lo + t)
        if groups > 1:
            s = (s.reshape(groups, rows // groups, tk) + bias[None]).reshape(rows, tk)
        else:
            s = s + bias
        s_scr[t] = s
        return jnp.maximum(mx, _lane_fold(s, jnp.maximum))

    mx = lax.fori_loop(0, n_tiles, pass1, jnp.full((rows, LANES), NEG_INF, F32))
    m = jnp.broadcast_to(jnp.max(mx, axis=-1, keepdims=True), (rows, LANES))

    def pass2(t, carry):
        lsum, acc = carry
        off = pl.multiple_of((j_lo + t) * tk, tk)
        s = s_scr[t]
        parts = [jnp.exp(s[:, u * LANES:(u + 1) * LANES] - m) for u in range(tk // LANES)]
        for p in parts:
            lsum = lsum + p
        pb = jnp.concatenate([p.astype(BF16) for p in parts], axis=1)
        acc = acc + jnp.dot(pb, v_ref[pl.ds(off, tk), :], preferred_element_type=F32)
        return lsum, acc

    lsum, acc = lax.fori_loop(0, n_tiles, pass2, (jnp.zeros((rows, LANES), F32), jnp.zeros((rows, dv), F32)))
    return acc / jnp.sum(lsum, axis=-1, keepdims=True)


def _flash_kernel(q_ref, k_ref, v_ref, bias_ref, o_ref, s_scr):
    i = pl.program_id(2)
    last = bias_ref.shape[0] - 1
    out = _attend(q_ref[...], k_ref, v_ref, lambda j: bias_ref[jnp.minimum(i - j, last)], 0, i + 1, s_scr,
                  tk=FLASH_T)
    o_ref[...] = out.astype(o_ref.dtype)


def _flash(q_arr, k_arr, v_arr, bias, *, nb, heads, dqk, dv, q_blk0, k_blk0, v_blk0):
    t = FLASH_T
    nq = SEQ // t
    return pl.pallas_call(
        _flash_kernel,
        out_shape=jax.ShapeDtypeStruct((nb * SEQ, heads * dv), BF16),
        grid=(nb, heads, nq),
        in_specs=[pl.BlockSpec((t, dqk), lambda b, h, i: (b * nq + i, q_blk0 + h)),
                  pl.BlockSpec((SEQ, dqk), lambda b, h, i: (b, k_blk0 + h)),
                  pl.BlockSpec((SEQ, dv), lambda b, h, i: (b, v_blk0 + h)),
                  pl.BlockSpec(bias.shape, lambda b, h, i: (0, 0, 0))],
        out_specs=pl.BlockSpec((t, dv), lambda b, h, i: (b * nq + i, h)),
        scratch_shapes=[pltpu.VMEM((nq, t, t), F32)],
        compiler_params=_cparams(("arbitrary", "arbitrary", "arbitrary")),
        name="flash",
    )(q_arr, k_arr, v_arr, bias)


def _split3(x):
    hi = x.astype(BF16)
    r1 = x - hi.astype(F32)
    mid = r1.astype(BF16)
    lo = (r1 - mid.astype(F32)).astype(BF16)
    return hi, mid, lo


def _nsa_kernel(q0_ref, q1_ref, q2_ref, q3_ref, q4_ref, q5_ref, kc_ref, vc_ref, ks_ref, vs_ref, kw_ref, vw_ref,
                gate_ref, cover_ref, expand_ref, wbias_ref, o_ref, bias_scr, s_scr):
    i = pl.program_id(1)
    tq = NSA_T
    nh = NSA_HEADS
    rows = nh * tq
    pos = i * tq + lax.broadcasted_iota(jnp.int32, (tq, 1), 0)
    q = jnp.concatenate([r[...] for r in (q0_ref, q1_ref, q2_ref, q3_ref, q4_ref, q5_ref)], axis=0)

    cidx = lax.broadcasted_iota(jnp.int32, (tq, N_CMP_PAD), 1)
    valid_c = (cidx * NSA_CMP_STRIDE + (NSA_CMP_LEN - 1) <= pos) & (cidx < N_CMP_PAD - 1)
    valid_cf = valid_c.astype(F32)
    cbias = jnp.where(valid_c, 0.0, NEG_INF).astype(F32)
    s = (_qk(q, kc_ref[0]).reshape(nh, tq, N_CMP_PAD) + cbias[None]).reshape(rows, N_CMP_PAD)
    p = (jnp.exp(s - jnp.max(s, axis=-1, keepdims=True)).reshape(nh, tq, N_CMP_PAD) * valid_cf[None])
    p = p.reshape(rows, N_CMP_PAD)
    l = jnp.sum(p, axis=-1, keepdims=True)
    p = p / jnp.where(l > 0.0, l, 1.0)
    o_cmp = jnp.dot(p.astype(BF16), vc_ref[0], preferred_element_type=F32)
    psum = jnp.sum(p.reshape(nh, tq, N_CMP_PAD), axis=0)

    cover_t = cover_ref[...]
    imp = sum(jnp.dot(cover_t, part, preferred_element_type=F32) for part in _split3(psum.T))

    pos_t = i * tq + lax.broadcasted_iota(jnp.int32, (N_SEL_BLOCKS, tq), 1)
    jidx = lax.broadcasted_iota(jnp.int32, (N_SEL_BLOCKS, tq), 0)
    qblk = pos_t // NSA_SEL_BLOCK
    valid_s = jidx * NSA_SEL_BLOCK <= pos_t
    forced = (jidx == 0) | (jidx == qblk) | (jidx == qblk - 1)
    score = jnp.where(valid_s, jnp.where(forced, NSA_FORCE_SCORE, imp), -1.0)
    rank = jnp.zeros((N_SEL_BLOCKS, tq), jnp.int32)
    for ii in range(N_SEL_BLOCKS):
        si = score[ii:ii + 1, :]
        rank = rank + ((si > score) | ((si == score) & (ii < jidx))).astype(jnp.int32)
    sel_t = (rank < NSA_TOP_N).astype(F32)
    sel = jnp.concatenate([sel_t, jnp.zeros((LANES - N_SEL_BLOCKS, tq), F32)], axis=0).T.astype(BF16)

    def fill(j, carry):
        hit = jnp.dot(sel, expand_ref[j], preferred_element_type=F32)
        kpos = j * tq + lax.broadcasted_iota(jnp.int32, (tq, tq), 1)
        bias_scr[j] = jnp.where((hit > 0.5) & (kpos <= pos), 0.0, NEG_INF).astype(F32)
        return carry

    lax.fori_loop(0, i + 1, fill, 0)

    o_slc = _attend(q, ks_ref, vs_ref, lambda j: bias_scr[j], 0, i + 1, s_scr, tk=tq, groups=nh)
    w_lo = jnp.maximum(i - (wbias_ref.shape[0] - 1), 0)
    o_win = _attend(q, kw_ref, vw_ref, lambda j: wbias_ref[i - j], w_lo, i + 1 - w_lo, s_scr, tk=tq, groups=nh)

    gates = jax.nn.sigmoid(gate_ref[...].astype(F32))
    for h in range(nh):
        sl = slice(h * tq, (h + 1) * tq)
        g0 = NSA_BRANCHES * h
        out = (gates[:, g0:g0 + 1] * o_cmp[sl] + gates[:, g0 + 1:g0 + 2] * o_slc[sl]
               + gates[:, g0 + 2:g0 + 3] * o_win[sl])
        o_ref[:, h * LANES:(h + 1) * LANES] = out.astype(o_ref.dtype)


def _nsa(hproj, kc, vc, cover, expand, wbias, nb):
    t = NSA_T
    nq = SEQ // t
    wide = NSA_HEADS * HEAD_DIM
    q_spec = lambda h: pl.BlockSpec((t, LANES), lambda b, i: (b * nq + i, BLK_NQ + h))
    kv_spec = lambda blk: pl.BlockSpec((SEQ, LANES), lambda b, i: (b, blk))
    cspec = pl.BlockSpec((1, N_CMP_PAD, HEAD_DIM), lambda b, i: (b, 0, 0))
    full = lambda shape: pl.BlockSpec(shape, lambda b, i: (0,) * len(shape))
    return pl.pallas_call(
        _nsa_kernel,
        out_shape=jax.ShapeDtypeStruct((nb * SEQ, wide), BF16),
        grid=(nb, nq),
        in_specs=[q_spec(h) for h in range(NSA_HEADS)]
        + [cspec, cspec, kv_spec(BLK_NKS), kv_spec(BLK_NVS), kv_spec(BLK_NKW), kv_spec(BLK_NVW),
           pl.BlockSpec((t, LANES), lambda b, i: (b * nq + i, BLK_GATE)),
           full(cover.shape), full(expand.shape), full(wbias.shape)],
        out_specs=pl.BlockSpec((t, wide), lambda b, i: (b * nq + i, 0)),
        scratch_shapes=[pltpu.VMEM((nq, t, t), F32), pltpu.VMEM((nq, NSA_HEADS * t, t), F32)],
        compiler_params=_cparams(("arbitrary", "arbitrary")),
        name="nsa",
    )(*([hproj] * NSA_HEADS), kc, vc, hproj, hproj, hproj, hproj, hproj, cover, expand, wbias)


def _layer_norm(y, g, b):
    mu = jnp.mean(y, axis=-1, keepdims=True)
    var = jnp.mean(jnp.square(y - mu), axis=-1, keepdims=True)
    return (y - mu) * lax.rsqrt(var + LN_EPS) * g + b


def _lane_min(x):
    return jnp.min(x, axis=-1, keepdims=True)


def _lane_max(x):
    return jnp.max(x, axis=-1, keepdims=True)


def _route(logits):
    lane = lax.broadcasted_iota(jnp.int32, logits.shape, 1)
    lane_f = lane.astype(F32)
    far = float(LANES)
    is_grp = lane < N_GROUPS
    lg = jnp.where(is_grp, logits, NEG_INF)
    eg = jnp.where(is_grp, jnp.exp(lg - _lane_max(lg)), 0.0)
    prob = eg / jnp.sum(eg, axis=-1, keepdims=True)
    p_g = _lane_max(prob)
    g_idx = _lane_min(jnp.where(is_grp & (prob == p_g), lane_f, far))
    e_lo = N_GROUPS + EXPERTS_PER_GROUP * g_idx
    in_grp = (lane_f >= e_lo) & (lane_f < e_lo + EXPERTS_PER_GROUP)
    le = jnp.where(in_grp, logits, NEG_INF)
    v1 = _lane_max(le)
    i1 = _lane_min(jnp.where(in_grp & (le == v1), lane_f, far))
    rest = in_grp & (lane_f != i1)
    le2 = jnp.where(rest, logits, NEG_INF)
    v2 = _lane_max(le2)
    i2 = _lane_min(jnp.where(rest & (le2 == v2), lane_f, far))
    e21 = jnp.exp(v2 - v1)
    den = 1.0 + e21
    gate1 = p_g * (1.0 / den)
    gate2 = p_g * (e21 / den)
    out = jnp.where(lane == 0, i1 - N_GROUPS, 0.0)
    out = jnp.where(lane == 1, i2 - N_GROUPS, out)
    out = jnp.where(lane == 2, gate1, out)
    out = jnp.where(lane == 3, gate2, out)
    return out


def _outproj_kernel(om_ref, od_ref, on_ref, w_ref, x_ref, g_ref, b_ref, wr_ref, x1_ref, route_ref):
    n_mla = MLA_HEADS * HEAD_DIM
    n_dil = DIL_HEADS * HEAD_DIM
    mix = jnp.dot(om_ref[...], w_ref[0, 0:n_mla, :], preferred_element_type=F32)
    mix = mix + jnp.dot(od_ref[...], w_ref[0, n_mla:n_mla + n_dil, :], preferred_element_type=F32)
    mix = mix + jnp.dot(on_ref[...], w_ref[0, n_mla + n_dil:, :], preferred_element_type=F32)
    x1 = _layer_norm(DN_ALPHA * x_ref[...] + mix, g_ref[...], b_ref[...])
    x1_ref[...] = x1
    xh = x1.astype(BF16)
    xm = (x1 - xh.astype(F32)).astype(BF16)
    both = jnp.dot(xh, wr_ref[...], preferred_element_type=F32)
    logits = both[:, 0:LANES] + (both[:, LANES:2 * LANES]
                                 + jnp.dot(xm, wr_ref[:, 0:LANES], preferred_element_type=F32))
    route_ref[...] = _route(logits)


def _outproj(o_mla, o_dil, o_nsa, w_out_s, layer, x, g, b, wr3):
    t = x.shape[0]
    tm = ROW_TM
    full = lambda shape: pl.BlockSpec(shape, lambda i: (0,) * len(shape))
    row = lambda w: pl.BlockSpec((tm, w), lambda i: (i, 0))
    return pl.pallas_call(
        _outproj_kernel,
        out_shape=(jax.ShapeDtypeStruct((t, D_MODEL), F32), jax.ShapeDtypeStruct((t, LANES), F32)),
        grid=(t // tm,),
        in_specs=[row(o_mla.shape[1]), row(o_dil.shape[1]), row(o_nsa.shape[1]),
                  pl.BlockSpec((1,) + w_out_s.shape[1:], lambda i: (layer, 0, 0)),
                  row(D_MODEL), full((1, D_MODEL)), full((1, D_MODEL)), full(wr3.shape)],
        out_specs=(row(D_MODEL), row(LANES)),
        compiler_params=_cparams(("arbitrary",)),
        name="outproj",
    )(o_mla, o_dil, o_nsa, w_out_s, x, g, b, wr3)


def _dispatch_kernel(fill_ref, pos_ref, x1_ref, xb_hbm, stage, zbuf, sem, zsem):
    i = pl.program_id(0)
    n = pl.num_programs(0)
    tm = ROW_TM
    slot = i % 2

    def drain(s):
        for _ in range(MOE_TOP_K):
            pltpu.make_async_copy(stage.at[s], xb_hbm.at[pl.ds(0, tm)], sem.at[s]).wait()

    def zero_copy(e):
        start = pl.multiple_of(fill_ref[e], MOE_ROWS)
        return pltpu.make_async_copy(zbuf, xb_hbm.at[pl.ds(start, MOE_ROWS)], zsem)

    @pl.when(i == 0)
    def _():
        zbuf[...] = jnp.zeros_like(zbuf)
        for e in range(N_EXPERTS):
            @pl.when(fill_ref[e] >= 0)
            def _():
                zero_copy(e).start()
        for e in range(N_EXPERTS):
            @pl.when(fill_ref[e] >= 0)
            def _():
                zero_copy(e).wait()

        def tail_copy(b):
            start = pl.multiple_of(b * MOE_ROWS, MOE_ROWS)
            return pltpu.make_async_copy(zbuf, xb_hbm.at[pl.ds(start, MOE_ROWS)], zsem)

        n_blocks = xb_hbm.shape[0] // MOE_ROWS
        lax.fori_loop(fill_ref[N_EXPERTS], n_blocks, lambda b, c: (tail_copy(b).start(), c)[1], 0)
        lax.fori_loop(fill_ref[N_EXPERTS], n_blocks, lambda b, c: (tail_copy(b).wait(), c)[1], 0)

    @pl.when(i >= 2)
    def _():
        drain(slot)

    stage[slot] = x1_ref[...]

    def body(g, carry):
        for u in range(DMA_UNROLL):
            r = g * DMA_UNROLL + u
            for k in range(MOE_TOP_K):
                p = pos_ref[0, 0, MOE_TOP_K * r + k]
                pltpu.make_async_copy(stage.at[slot, pl.ds(r, 1)], xb_hbm.at[pl.ds(p, 1)], sem.at[slot]).start()
        return carry

    lax.fori_loop(0, tm // DMA_UNROLL, body, 0)

    @pl.when(i == n - 1)
    def _():
        drain(1 - slot)
        drain(slot)


def _dispatch(fill, pos3, x1, n_rows):
    t = x1.shape[0]
    tm = ROW_TM
    grid_spec = pltpu.PrefetchScalarGridSpec(
        num_scalar_prefetch=1,
        grid=(t // tm,),
        in_specs=[pl.BlockSpec((1, 1, MOE_TOP_K * tm), lambda i, fl: (i, 0, 0), memory_space=pltpu.SMEM),
                  pl.BlockSpec((tm, D_MODEL), lambda i, fl: (i, 0))],
        out_specs=pl.BlockSpec(memory_space=pl.ANY),
        scratch_shapes=[pltpu.VMEM((2, tm, D_MODEL), F32), pltpu.VMEM((MOE_ROWS, D_MODEL), F32),
                        pltpu.SemaphoreType.DMA((2,)), pltpu.SemaphoreType.DMA(())],
    )
    return pl.pallas_call(
        _dispatch_kernel,
        out_shape=jax.ShapeDtypeStruct((n_rows, D_MODEL), F32),
        grid_spec=grid_spec,
        compiler_params=_cparams(("arbitrary",)),
        name="dispatch",
    )(fill, pos3, x1)


def _experts_kernel(blk_ex_ref, nused_ref, x_ref, wg_ref, wu_ref, wd_ref, y_ref, wg_b, wu_b, wd_b):
    b = pl.program_id(0)
    nused = nused_ref[0]

    @pl.when(b < nused)
    def _():
        changed = jnp.logical_or(b == 0, blk_ex_ref[b] != blk_ex_ref[jnp.maximum(b - 1, 0)])

        @pl.when(changed)
        def _():
            wg_b[...] = wg_ref[0, 0].astype(BF16)
            wu_b[...] = wu_ref[0, 0].astype(BF16)
            wd_b[...] = wd_ref[0, 0].astype(BF16)

        rows = x_ref[...].astype(BF16)
        gate = jnp.dot(rows, wg_b[...], preferred_element_type=F32)
        up = jnp.dot(rows, wu_b[...], preferred_element_type=F32)
        hid = (gate * jax.nn.sigmoid(gate) * up).astype(BF16)
        y_ref[...] = jnp.dot(hid, wd_b[...], preferred_element_type=F32)

    @pl.when(b >= nused)
    def _():
        y_ref[...] = jnp.zeros_like(y_ref)


def _experts(blk_ex, nused, xb_rows, w_gate, w_up, w_down, layer):
    n_blocks = xb_rows.shape[0] // MOE_ROWS
    w_spec = lambda shape: pl.BlockSpec((1, 1) + shape, lambda b, ex, nu: (layer, ex[b], 0, 0))
    grid_spec = pltpu.PrefetchScalarGridSpec(
        num_scalar_prefetch=2,
        grid=(n_blocks,),
        in_specs=[pl.BlockSpec((MOE_ROWS, D_MODEL), lambda b, ex, nu: (jnp.minimum(b, nu[0] - 1), 0)),
                  w_spec((D_MODEL, EXPERT_FF)), w_spec((D_MODEL, EXPERT_FF)), w_spec((EXPERT_FF, D_MODEL))],
        out_specs=pl.BlockSpec((MOE_ROWS, D_MODEL), lambda b, ex, nu: (b, 0)),
        scratch_shapes=[pltpu.VMEM((D_MODEL, EXPERT_FF), BF16), pltpu.VMEM((D_MODEL, EXPERT_FF), BF16),
                        pltpu.VMEM((EXPERT_FF, D_MODEL), BF16)],
    )
    return pl.pallas_call(
        _experts_kernel,
        out_shape=jax.ShapeDtypeStruct((n_blocks * MOE_ROWS, D_MODEL), F32),
        grid_spec=grid_spec,
        compiler_params=_cparams(("arbitrary",)),
        name="experts",
    )(blk_ex, nused, xb_rows, w_gate, w_up, w_down)


def _combine_kernel(pos_cur_ref, pos_nxt_ref, y_hbm, x1_ref, route_ref, g_ref, b_ref, o_ref, ob_ref, ybuf, sem):
    i = pl.program_id(0)
    n = pl.num_programs(0)
    tm = ROW_TM

    def issue(pos_ref, slot):
        def body(g, carry):
            for u in range(DMA_UNROLL):
                r = g * DMA_UNROLL + u
                for k in range(MOE_TOP_K):
                    p = pos_ref[0, 0, MOE_TOP_K * r + k]
                    pltpu.make_async_copy(y_hbm.at[pl.ds(p, 1)], ybuf.at[slot, k, pl.ds(r, 1)],
                                          sem.at[slot]).start()
            return carry

        lax.fori_loop(0, tm // DMA_UNROLL, body, 0)

    @pl.when(i == 0)
    def _():
        issue(pos_cur_ref, 0)

    @pl.when(i + 1 < n)
    def _():
        issue(pos_nxt_ref, (i + 1) % 2)

    slot = i % 2
    for k in range(MOE_TOP_K):
        pltpu.make_async_copy(y_hbm.at[pl.ds(0, tm)], ybuf.at[slot, k], sem.at[slot]).wait()
    route = route_ref[...]
    ffn = route[:, 2:3] * ybuf[slot, 0] + route[:, 3:4] * ybuf[slot, 1]
    x2 = _layer_norm(DN_ALPHA * x1_ref[...] + ffn, g_ref[...], b_ref[...])
    o_ref[...] = x2
    ob_ref[...] = x2.astype(ob_ref.dtype)


def _combine(pos3, yb, x1, route, g, b):
    t = x1.shape[0]
    tm = ROW_TM
    nt = t // tm
    pos_spec = lambda shift: pl.BlockSpec((1, 1, MOE_TOP_K * tm), lambda i: (jnp.minimum(i + shift, nt - 1), 0, 0),
                                          memory_space=pltpu.SMEM)
    row = lambda w: pl.BlockSpec((tm, w), lambda i: (i, 0))
    full = lambda shape: pl.BlockSpec(shape, lambda i: (0,) * len(shape))
    return pl.pallas_call(
        _combine_kernel,
        out_shape=(jax.ShapeDtypeStruct((t, D_MODEL), F32), jax.ShapeDtypeStruct((t, D_MODEL), BF16)),
        grid=(nt,),
        in_specs=[pos_spec(0), pos_spec(1), pl.BlockSpec(memory_space=pl.ANY), row(D_MODEL), row(LANES),
                  full((1, D_MODEL)), full((1, D_MODEL))],
        out_specs=(row(D_MODEL), row(D_MODEL)),
        scratch_shapes=[pltpu.VMEM((2, MOE_TOP_K, tm, D_MODEL), F32), pltpu.SemaphoreType.DMA((2,))],
        compiler_params=_cparams(("arbitrary",)),
        name="combine",
    )(pos3, pos3, yb, x1, route, g, b)


def _dispatch_plan(route, t):
    m = t * MOE_TOP_K
    n_blocks = -(-m // MOE_ROWS) + N_EXPERTS
    chunk = MOE_ROWS
    ex = route[:, 0:MOE_TOP_K].reshape(m // chunk, chunk, 1)
    onehot = (ex == jnp.arange(N_EXPERTS, dtype=F32)[None, None, :]).astype(F32)
    tri = jnp.asarray(np.tril(np.ones((chunk, chunk), np.float32)))
    within = jnp.einsum('ij,cje->cie', tri, onehot)
    totals = jnp.sum(onehot, axis=1)
    before = jnp.cumsum(totals, axis=0) - totals
    counts = jnp.sum(totals, axis=0)
    padded = jnp.ceil(counts / MOE_ROWS) * MOE_ROWS
    pad_end = jnp.cumsum(padded)
    pad_start = pad_end - padded
    row = jnp.sum(onehot * (within - 1.0 + before[:, None, :] + pad_start[None, None, :]), axis=-1)
    pos = row.astype(jnp.int32).reshape(m)
    blk_start = jnp.arange(n_blocks, dtype=F32) * MOE_ROWS
    blk_ex = jnp.minimum(jnp.sum((pad_end[None, :] <= blk_start[:, None]).astype(jnp.int32), axis=1),
                         N_EXPERTS - 1).astype(jnp.int32)
    nused = (pad_end[-1] / MOE_ROWS).astype(jnp.int32).reshape(1)
    fill = jnp.concatenate([jnp.where(padded > 0, pad_end - MOE_ROWS, -1.0).astype(jnp.int32), nused])
    return blk_ex, nused, fill, pos


def _shift_kernel(w_ref, tail_ref, o_ref):
    n_keep = (IN_SHIFT_FROM + MLA_ROPE_DIM) // LANES
    n_src = IN_COLS // LANES
    o_ref[0, :, 0:n_keep * LANES] = w_ref[0, :, 0:n_keep * LANES].astype(o_ref.dtype)
    low = lax.broadcasted_iota(jnp.int32, (w_ref.shape[1], LANES), 1) < MLA_ROPE_DIM

    def src(m):
        if m < n_src:
            return w_ref[0, :, m * LANES:(m + 1) * LANES]
        return tail_ref[0] if m == n_src else jnp.zeros((w_ref.shape[1], LANES), F32)

    for m in range(n_keep, N_IN_BLOCKS):
        blk = jnp.where(low, pltpu.roll(src(m - 1), MLA_ROPE_DIM, 1), pltpu.roll(src(m), MLA_ROPE_DIM, 1))
        o_ref[0, :, m * LANES:(m + 1) * LANES] = blk.astype(o_ref.dtype)


def _shift_w_in(w_in):
    nl, d, _ = w_in.shape
    n_src = IN_COLS // LANES
    tail = jnp.pad(w_in[:, :, n_src * LANES:], ((0, 0), (0, 0), (0, (n_src + 1) * LANES - IN_COLS)))
    tk = ROW_TM
    return pl.pallas_call(
        _shift_kernel,
        out_shape=jax.ShapeDtypeStruct((nl, d, IN_COLS_PAD), BF16),
        grid=(nl, d // tk),
        in_specs=[pl.BlockSpec((1, tk, IN_COLS), lambda l, r: (l, r, 0)),
                  pl.BlockSpec((1, tk, LANES), lambda l, r: (l, r, 0))],
        out_specs=pl.BlockSpec((1, tk, IN_COLS_PAD), lambda l, r: (l, r, 0)),
        compiler_params=_cparams(("arbitrary", "arbitrary")),
        name="shiftw",
    )(w_in, tail)


def _permute_w_q_up(w_q_up):
    w = w_q_up.reshape(MLA_Q_RANK, MLA_HEADS, MLA_NOPE_DIM + MLA_ROPE_DIM)
    zero = jnp.zeros((MLA_Q_RANK, MLA_HEADS, MLA_QK_PAD - MLA_NOPE_DIM - MLA_ROPE_DIM), w.dtype)
    w = jnp.concatenate([w[:, :, :MLA_NOPE_DIM], zero, w[:, :, MLA_NOPE_DIM:]], axis=-1)
    return w.reshape(MLA_Q_RANK, MLA_HEADS * MLA_QK_PAD).astype(BF16)


def _split_w_kv_up(w_kv_up):
    w = w_kv_up.reshape(MLA_KV_RANK, MLA_HEADS, MLA_NOPE_DIM + HEAD_DIM)
    wkn = w[:, :, :MLA_NOPE_DIM].reshape(MLA_KV_RANK, MLA_HEADS * MLA_NOPE_DIM)
    wv = w[:, :, MLA_NOPE_DIM:].reshape(MLA_KV_RANK, MLA_HEADS * HEAD_DIM)
    return wkn.astype(BF16), wv.astype(BF16)


def _router_slices(w_grp, w_exp):
    w = jnp.concatenate([w_grp, w_exp, jnp.zeros((D_MODEL, LANES - N_GROUPS - N_EXPERTS), F32)], axis=1)
    hi = w.astype(BF16)
    mid = (w - hi.astype(F32)).astype(BF16)
    return jnp.concatenate([hi, mid], axis=1)


def _mixer(xb, nb, layer, w_in_s, q_lat_norm, w_q_up, kv_lat_norm, w_kv_up,
           cmp_pos_k, cmp_w1_k, cmp_w2_k, cmp_pos_v, cmp_w1_v, cmp_w2_v, consts):
    hproj = _inproj(xb, w_in_s, layer, consts["rope32"])
    wkn, wv = _split_w_kv_up(w_kv_up)
    q_mla, k_mla, v_mla = _mlaprep(hproj, q_lat_norm.reshape(1, -1), kv_lat_norm.reshape(1, -1),
                                   _permute_w_q_up(w_q_up), wkn, wv, consts["rope64"])
    wide = NSA_CMP_STRIDE * HEAD_DIM
    tk2 = hproj[:, BLK_NKC * LANES:(BLK_NKC + 1) * LANES].reshape(nb, N_CMP_PAD, wide)
    tv2 = hproj[:, BLK_NVC * LANES:(BLK_NVC + 1) * LANES].reshape(nb, N_CMP_PAD, wide)
    kc, vc = _compress(tk2, tv2, cmp_pos_k.reshape(2, wide), cmp_pos_v.reshape(2, wide),
                       cmp_w1_k.astype(BF16), cmp_w2_k.astype(BF16), cmp_w1_v.astype(BF16), cmp_w2_v.astype(BF16),
                       consts["rope_cmp"])
    o_mla = _flash(q_mla, k_mla, v_mla, consts["bias_causal"], nb=nb, heads=MLA_HEADS, dqk=MLA_QK_PAD,
                   dv=HEAD_DIM, q_blk0=0, k_blk0=0, v_blk0=0)
    o_dil = _flash(hproj, hproj, hproj, consts["bias_dil"], nb=nb, heads=DIL_HEADS, dqk=HEAD_DIM, dv=HEAD_DIM,
                   q_blk0=BLK_DQ, k_blk0=BLK_DK, v_blk0=BLK_DV)
    o_nsa = _nsa(hproj, kc, vc, consts["cover"], consts["expand"], consts["bias_win"], nb)
    return o_mla, o_dil, o_nsa


def _make_consts():
    pos = np.arange(SEQ)
    slot = np.arange(N_CMP_PAD)
    cmp_start = slot * NSA_CMP_STRIDE
    cmp_end = cmp_start + NSA_CMP_LEN - 1
    sel_start = np.arange(N_SEL_BLOCKS) * NSA_SEL_BLOCK
    cover = ((cmp_start[:, None] < sel_start[None, :] + NSA_SEL_BLOCK)
             & (cmp_start[:, None] + NSA_CMP_LEN > sel_start[None, :])
             & (slot[:, None] < N_CMP_PAD - 1)).astype(np.float32)
    expand = (np.arange(LANES)[:, None] == (pos // NSA_SEL_BLOCK)[None, :]).astype(np.float32)
    expand = expand.reshape(LANES, SEQ // NSA_T, NSA_T).transpose(1, 0, 2)
    n_win = -(-NSA_WINDOW // NSA_T) + 1
    return {
        "rope32": _rope_tables(pos, ROT_DIM, True),
        "rope64": _rope_tables(pos, MLA_ROPE_DIM, False, KROPE_LANE0),
        "rope_cmp": _rope_tables(cmp_end, ROT_DIM, True),
        "bias_causal": _distance_bias(_causal_mult, (0, 1), FLASH_T),
        "bias_dil": _distance_bias(_dilated_mult, range(SEQ // FLASH_T), FLASH_T),
        "bias_win": _distance_bias(_window_mult, range(n_win), NSA_T),
        "cover": jnp.asarray(cover.T, BF16),
        "expand": jnp.asarray(expand, BF16),
    }


def kernel(x, w_in, q_lat_norm, w_q_up, kv_lat_norm, w_kv_up, cmp_pos_k, cmp_w1_k, cmp_w2_k, cmp_pos_v, cmp_w1_v,
           cmp_w2_v, w_out, ln1_g, ln1_b, w_grp, w_exp, w_gate, w_up, w_down, ln2_g, ln2_b):
    nb, s, d = x.shape
    assert s == SEQ and d == D_MODEL
    t = nb * s
    consts = _make_consts()
    w_in_s = _shift_w_in(w_in)
    w_out_s = w_out.astype(BF16)
    xf = x.reshape(t, d)
    xb = xf.astype(BF16)
    for l in range(DEPTH):
        o_mla, o_dil, o_nsa = _mixer(xb, nb, l, w_in_s, q_lat_norm[l], w_q_up[l], kv_lat_norm[l], w_kv_up[l],
                                     cmp_pos_k[l], cmp_w1_k[l], cmp_w2_k[l], cmp_pos_v[l], cmp_w1_v[l],
                                     cmp_w2_v[l], consts)
        x1, route = _outproj(o_mla, o_dil, o_nsa, w_out_s, l, xf, ln1_g[l].reshape(1, d),
                             ln1_b[l].reshape(1, d), _router_slices(w_grp[l], w_exp[l]))
        blk_ex, nused, fill, pos = _dispatch_plan(route, t)
        pos3 = pos.reshape(t // ROW_TM, 1, MOE_TOP_K * ROW_TM)
        n_rows = (-(-t * MOE_TOP_K // MOE_ROWS) + N_EXPERTS) * MOE_ROWS
        yb = _experts(blk_ex, nused, _dispatch(fill, pos3, x1, n_rows), w_gate, w_up, w_down, l)
        xf, xb = _combine(pos3, yb, x1, route,
                          ln2_g[l].reshape(1, d), ln2_b[l].reshape(1, d))
    return xf.reshape(nb, s, d)
```

```python
import functools

import numpy as np
import jax
import jax.numpy as jnp
from jax import lax
from jax.experimental import pallas as pl
from jax.experimental.pallas import tpu as pltpu

F32 = jnp.float32
BF16 = jnp.bfloat16

D_MODEL = 2048
SEQ = 2048
DEPTH = 2
HEAD_DIM = 128
LANES = 128
MLA_HEADS = 4
DIL_HEADS = 6
NSA_HEADS = 6
MLA_Q_RANK = 384
MLA_KV_RANK = 128
MLA_NOPE_DIM = 128
MLA_ROPE_DIM = 64
MLA_QK_PAD = 256
DIL_PAIRS = ((128, 1), (512, 4), (2048, 16))
NSA_CMP_LEN = 32
NSA_CMP_STRIDE = 16
NSA_CMP_HIDDEN = 256
NSA_SEL_BLOCK = 64
NSA_TOP_N = 16
NSA_WINDOW = 512
NSA_BRANCHES = 3
NSA_FORCE_SCORE = 1e4
ROPE_THETA = 500000.0
ROT_DIM = HEAD_DIM // 4
N_GROUPS = 4
EXPERTS_PER_GROUP = 8
N_EXPERTS = N_GROUPS * EXPERTS_PER_GROUP
EXPERT_FF = 512
MOE_TOP_K = 2
MOE_ROWS = 256
DN_ALPHA = (2 * DEPTH) ** 0.25
LN_EPS = 1e-5
RMS_EPS = 1e-6
NEG_INF = -1e30

N_CMP_PAD = SEQ // NSA_CMP_STRIDE
N_SEL_BLOCKS = SEQ // NSA_SEL_BLOCK

BLK_QLAT = 0
BLK_KVLAT = 3
BLK_KROPE = 4
BLK_DQ = 5
BLK_DK = 11
BLK_DV = 17
BLK_NQ = 23
BLK_NKC = 29
BLK_NVC = 30
BLK_NKS = 31
BLK_NVS = 32
BLK_NKW = 33
BLK_NVW = 34
BLK_GATE = 35
N_IN_BLOCKS = 36
IN_COLS = 4434
IN_SHIFT_FROM = MLA_Q_RANK + MLA_KV_RANK - MLA_ROPE_DIM
IN_COLS_PAD = N_IN_BLOCKS * LANES
KROPE_LANE0 = LANES - MLA_ROPE_DIM

INPROJ_TM = 2048
INPROJ_SUB = 512
INPROJ_TN = 512
FLASH_T = 512
NSA_T = 256
ROW_TM = 256
OUTPROJ_TM = 512
DMA_UNROLL = 8

VMEM_LIMIT = 56 * 1024 * 1024


def _cparams(sem):
    return pltpu.CompilerParams(dimension_semantics=sem, vmem_limit_bytes=VMEM_LIMIT)


def _rope_tables(pos, rot_dim, keep_rest, lane0=0):
    half = rot_dim // 2
    inv_freq = np.power(np.float32(ROPE_THETA), -np.arange(half, dtype=np.float32) / np.float32(half))
    ang = (pos.astype(np.float32)[:, None] * inv_freq[None, :]).astype(np.float32)
    cos, sin = np.cos(ang).astype(np.float32), np.sin(ang).astype(np.float32)
    n = pos.shape[0]
    c = np.full((n, LANES), 1.0 if keep_rest else 0.0, np.float32)
    s1 = np.zeros((n, LANES), np.float32)
    s2 = np.zeros((n, LANES), np.float32)
    c[:, lane0:lane0 + half] = cos
    c[:, lane0 + half:lane0 + rot_dim] = cos
    s1[:, lane0 + half:lane0 + rot_dim] = sin
    s2[:, lane0:lane0 + half] = -sin
    return jnp.asarray(c), jnp.asarray(s1), jnp.asarray(s2)


def _rope_lanes(a, c, s1, s2, half):
    return a * c + pltpu.roll(a, half, 1) * s1 + pltpu.roll(a, LANES - half, 1) * s2


def _distance_bias(mult_fn, diags, t):
    d = np.asarray(diags, np.int64)[:, None, None] * t
    r = np.arange(t, dtype=np.int64)[None, :, None]
    c = np.arange(t, dtype=np.int64)[None, None, :]
    mult = mult_fn(d + r - c)
    return jnp.asarray(np.where(mult > 0, np.log(np.maximum(mult, 1).astype(np.float32)), NEG_INF).astype(np.float32))


def _causal_mult(dist):
    return (dist >= 0).astype(np.int64)


def _dilated_mult(dist):
    m = np.zeros_like(dist)
    for window, dil in DIL_PAIRS:
        m = m + ((dist >= 0) & (dist % dil == 0) & (dist <= (window // dil) * dil)).astype(np.int64)
    return m


def _window_mult(dist):
    return ((dist >= 0) & (dist <= NSA_WINDOW - 1)).astype(np.int64)


def _in_block(blk, lo, n):
    return jnp.logical_and(blk >= lo, blk < lo + n)


def _inproj_kernel(x_ref, w_ref, c_ref, s1_ref, s2_ref, o_ref, *, scale):
    j = pl.program_id(0)
    for hh in range(INPROJ_TM // INPROJ_SUB):
        rows = slice(hh * INPROJ_SUB, (hh + 1) * INPROJ_SUB)
        acc = jnp.dot(x_ref[rows, :], w_ref[0], preferred_element_type=F32)
        c, s1, s2 = c_ref[rows, :], s1_ref[rows, :], s2_ref[rows, :]
        for u in range(INPROJ_TN // LANES):
            blk = j * (INPROJ_TN // LANES) + u
            is_q = jnp.logical_or(_in_block(blk, BLK_DQ, DIL_HEADS), _in_block(blk, BLK_NQ, NSA_HEADS))
            is_rope = is_q | _in_block(blk, BLK_DK, DIL_HEADS) | (blk == BLK_NKS) | (blk == BLK_NKW)
            sc = jnp.where(is_q, scale, 1.0).astype(F32)
            a = acc[:, u * LANES:(u + 1) * LANES]
            r = _rope_lanes(a, jnp.where(is_rope, c, 1.0), jnp.where(is_rope, s1, 0.0),
                            jnp.where(is_rope, s2, 0.0), ROT_DIM // 2)
            o_ref[rows, u * LANES:(u + 1) * LANES] = (r * sc).astype(o_ref.dtype)


def _inproj(xb, w_in_s, layer, tabs):
    t = xb.shape[0]
    tm, tn = INPROJ_TM, INPROJ_TN
    nrow = SEQ // tm
    tab_spec = pl.BlockSpec((tm, LANES), lambda j, i: (i % nrow, 0))
    return pl.pallas_call(
        functools.partial(_inproj_kernel, scale=HEAD_DIM ** -0.5),
        out_shape=jax.ShapeDtypeStruct((t, IN_COLS_PAD), BF16),
        grid=(IN_COLS_PAD // tn, t // tm),
        in_specs=[pl.BlockSpec((tm, D_MODEL), lambda j, i: (i, 0)),
                  pl.BlockSpec((1, D_MODEL, tn), lambda j, i: (layer, 0, j)),
                  tab_spec, tab_spec, tab_spec],
        out_specs=pl.BlockSpec((tm, tn), lambda j, i: (i, j)),
        compiler_params=_cparams(("arbitrary", "arbitrary")),
        name="inproj",
    )(xb, w_in_s, *tabs)


def _rms(xf, g):
    return xf * lax.rsqrt(jnp.mean(jnp.square(xf), axis=-1, keepdims=True) + RMS_EPS) * g


def _mlaprep_kernel(ql_ref, kvl_ref, kr_ref, gq_ref, gkv_ref, wq_ref, wkn_ref, wv_ref,
                    c_ref, s1_ref, s2_ref, q_ref, k_ref, v_ref, *, scale):
    c, s1, s2 = c_ref[...], s1_ref[...], s2_ref[...]
    half = MLA_ROPE_DIM // 2
    qn = _rms(ql_ref[...].astype(F32), gq_ref[...]).astype(BF16)
    q = jnp.dot(qn, wq_ref[...], preferred_element_type=F32)
    for h in range(MLA_HEADS):
        lo = h * MLA_QK_PAD
        q_ref[:, lo:lo + LANES] = (q[:, lo:lo + LANES] * scale).astype(q_ref.dtype)
        a = q[:, lo + LANES:lo + 2 * LANES]
        q_ref[:, lo + LANES:lo + 2 * LANES] = (_rope_lanes(a, c, s1, s2, half) * scale).astype(q_ref.dtype)
    kvn = _rms(kvl_ref[...].astype(F32), gkv_ref[...]).astype(BF16)
    kn = jnp.dot(kvn, wkn_ref[...], preferred_element_type=F32)
    v_ref[...] = jnp.dot(kvn, wv_ref[...], preferred_element_type=F32).astype(v_ref.dtype)
    kpe = _rope_lanes(kr_ref[...].astype(F32), c, s1, s2, half).astype(k_ref.dtype)
    for h in range(MLA_HEADS):
        lo = h * MLA_QK_PAD
        k_ref[:, lo:lo + LANES] = kn[:, h * LANES:(h + 1) * LANES].astype(k_ref.dtype)
        k_ref[:, lo + LANES:lo + 2 * LANES] = kpe


def _mlaprep(hproj, gq, gkv, wq_p, wkn, wv, tabs):
    t = hproj.shape[0]
    tm = ROW_TM
    nrow = SEQ // tm
    tab_spec = pl.BlockSpec((tm, LANES), lambda i: (i % nrow, 0))
    full = lambda shape: pl.BlockSpec(shape, lambda i: (0,) * len(shape))
    return pl.pallas_call(
        functools.partial(_mlaprep_kernel, scale=(MLA_NOPE_DIM + MLA_ROPE_DIM) ** -0.5),
        out_shape=(jax.ShapeDtypeStruct((t, MLA_HEADS * MLA_QK_PAD), BF16),
                   jax.ShapeDtypeStruct((t, MLA_HEADS * MLA_QK_PAD), BF16),
                   jax.ShapeDtypeStruct((t, MLA_HEADS * HEAD_DIM), BF16)),
        grid=(t // tm,),
        in_specs=[pl.BlockSpec((tm, MLA_Q_RANK), lambda i: (i, BLK_QLAT * LANES // MLA_Q_RANK)),
                  pl.BlockSpec((tm, LANES), lambda i: (i, BLK_KVLAT)),
                  pl.BlockSpec((tm, LANES), lambda i: (i, BLK_KROPE)),
                  full((1, MLA_Q_RANK)), full((1, MLA_KV_RANK)),
                  full((MLA_Q_RANK, MLA_HEADS * MLA_QK_PAD)),
                  full((MLA_KV_RANK, MLA_HEADS * MLA_NOPE_DIM)),
                  full((MLA_KV_RANK, MLA_HEADS * HEAD_DIM)),
                  tab_spec, tab_spec, tab_spec],
        out_specs=(pl.BlockSpec((tm, MLA_HEADS * MLA_QK_PAD), lambda i: (i, 0)),
                   pl.BlockSpec((tm, MLA_HEADS * MLA_QK_PAD), lambda i: (i, 0)),
                   pl.BlockSpec((tm, MLA_HEADS * HEAD_DIM), lambda i: (i, 0))),
        compiler_params=_cparams(("arbitrary",)),
        name="mlaprep",
    )(hproj, hproj, hproj, gq, gkv, wq_p, wkn, wv, *tabs)


def _gelu_tanh(x):
    return 0.5 * x * (1.0 + jnp.tanh(0.7978845608028654 * (x + 0.044715 * (x * x * x))))


def _compress_kernel(tk_ref, tv_ref, pek_ref, pev_ref, w1k_ref, w2k_ref, w1v_ref, w2v_ref,
                     c_ref, s1_ref, s2_ref, kc_ref, vc_ref):
    half_in = NSA_CMP_STRIDE * HEAD_DIM

    def comp(t_ref, pe_ref, w1_ref, w2_ref):
        t = t_ref[0].astype(F32)
        a = (t + pe_ref[0:1, :]).astype(BF16)
        b = (t + pe_ref[1:2, :]).astype(BF16)
        y0 = jnp.dot(a, w1_ref[0:half_in, :], preferred_element_type=F32)
        y1 = jnp.dot(b, w1_ref[half_in:2 * half_in, :], preferred_element_type=F32)
        hid = _gelu_tanh(y0 + pltpu.roll(y1, N_CMP_PAD - 1, 0))
        return jnp.dot(hid.astype(BF16), w2_ref[...], preferred_element_type=F32)

    kc = comp(tk_ref, pek_ref, w1k_ref, w2k_ref)
    kc_ref[0] = _rope_lanes(kc, c_ref[...], s1_ref[...], s2_ref[...], ROT_DIM // 2).astype(kc_ref.dtype)
    vc_ref[0] = comp(tv_ref, pev_ref, w1v_ref, w2v_ref).astype(vc_ref.dtype)


def _compress(tk2, tv2, pek, pev, w1k, w2k, w1v, w2v, tabs):
    nb = tk2.shape[0]
    wide = NSA_CMP_STRIDE * HEAD_DIM
    full = lambda shape: pl.BlockSpec(shape, lambda b: (0,) * len(shape))
    bspec = pl.BlockSpec((1, N_CMP_PAD, wide), lambda b: (b, 0, 0))
    ospec = pl.BlockSpec((1, N_CMP_PAD, HEAD_DIM), lambda b: (b, 0, 0))
    return pl.pallas_call(
        _compress_kernel,
        out_shape=(jax.ShapeDtypeStruct((nb, N_CMP_PAD, HEAD_DIM), BF16),) * 2,
        grid=(nb,),
        in_specs=[bspec, bspec, full((2, wide)), full((2, wide)),
                  full((2 * wide, NSA_CMP_HIDDEN)), full((NSA_CMP_HIDDEN, HEAD_DIM)),
                  full((2 * wide, NSA_CMP_HIDDEN)), full((NSA_CMP_HIDDEN, HEAD_DIM)),
                  full((N_CMP_PAD, LANES)), full((N_CMP_PAD, LANES)), full((N_CMP_PAD, LANES))],
        out_specs=(ospec, ospec),
        compiler_params=_cparams(("arbitrary",)),
        name="compress",
    )(tk2, tv2, pek, pev, w1k, w2k, w1v, w2v, *tabs)


def _qk(q, k):
    return lax.dot_general(q, k, (((1,), (1,)), ((), ())), preferred_element_type=F32)


def _lane_fold(x, op):
    out = x[:, 0:LANES]
    for u in range(1, x.shape[1] // LANES):
        out = op(out, x[:, u * LANES:(u + 1) * LANES])
    return out


def _split3(x):
    hi = x.astype(BF16)
    r1 = x - hi.astype(F32)
    mid = r1.astype(BF16)
    lo = (r1 - mid.astype(F32)).astype(BF16)
    return hi, mid, lo


def _attend(q, k_ref, v_ref, bias_fn, j_lo, n_tiles, s_scr, *, tk, groups=1):
    rows = q.shape[0]
    dv = v_ref.shape[-1]

    def pass1(t, mx):
        off = pl.multiple_of((j_lo + t) * tk, tk)
        s = _qk(q, k_ref[pl.ds(off, tk), :])
        bias = bias_fn(j_lo + t)
        if groups > 1:
            s = (s.reshape(groups, rows // groups, tk) + bias[None]).reshape(rows, tk)
        else:
            s = s + bias
        s_scr[t] = s
        return jnp.maximum(mx, _lane_fold(s, jnp.maximum))

    mx = lax.fori_loop(0, n_tiles, pass1, jnp.full((rows, LANES), NEG_INF, F32))
    m = jnp.broadcast_to(jnp.max(mx, axis=-1, keepdims=True), (rows, LANES))

    def pass2(t, carry):
        lsum, acc = carry
        off = pl.multiple_of((j_lo + t) * tk, tk)
        s = s_scr[t]
        parts = [jnp.exp(s[:, u * LANES:(u + 1) * LANES] - m) for u in range(tk // LANES)]
        for p in parts:
            lsum = lsum + p
        pb = jnp.concatenate([p.astype(BF16) for p in parts], axis=1)
        acc = acc + jnp.dot(pb, v_ref[pl.ds(off, tk), :], preferred_element_type=F32)
        return lsum, acc

    lsum, acc = lax.fori_loop(0, n_tiles, pass2, (jnp.zeros((rows, LANES), F32), jnp.zeros((rows, dv), F32)))
    return acc / jnp.sum(lsum, axis=-1, keepdims=True)


def _flash_kernel(q_ref, k_ref, v_ref, bias_ref, o_ref, s_scr):
    i = pl.program_id(2)
    last = bias_ref.shape[0] - 1
    out = _attend(q_ref[...], k_ref, v_ref, lambda j: bias_ref[jnp.minimum(i - j, last)], 0, i + 1, s_scr,
                  tk=FLASH_T)
    o_ref[...] = out.astype(o_ref.dtype)


def _flash(q_arr, k_arr, v_arr, bias, *, nb, heads, dqk, dv, q_blk0, k_blk0, v_blk0):
    t = FLASH_T
    nq = SEQ // t
    return pl.pallas_call(
        _flash_kernel,
        out_shape=jax.ShapeDtypeStruct((nb * SEQ, heads * dv), BF16),
        grid=(nb, heads, nq),
        in_specs=[pl.BlockSpec((t, dqk), lambda b, h, i: (b * nq + i, q_blk0 + h)),
                  pl.BlockSpec((SEQ, dqk), lambda b, h, i: (b, k_blk0 + h)),
                  pl.BlockSpec((SEQ, dv), lambda b, h, i: (b, v_blk0 + h)),
                  pl.BlockSpec(bias.shape, lambda b, h, i: (0, 0, 0))],
        out_specs=pl.BlockSpec((t, dv), lambda b, h, i: (b * nq + i, h)),
        scratch_shapes=[pltpu.VMEM((nq, t, t), F32)],
        compiler_params=_cparams(("arbitrary", "arbitrary", "arbitrary")),
        name="flash",
    )(q_arr, k_arr, v_arr, bias)


def _nsa_kernel(q0_ref, q1_ref, q2_ref, q3_ref, q4_ref, q5_ref, kc_ref, vc_ref, ks_ref, vs_ref, kw_ref, vw_ref,
                gate_ref, cover_ref, expand_ref, wbias_ref, o_ref, bias_scr, s_scr):
    i = pl.program_id(1)
    tq = NSA_T
    nh = NSA_HEADS
    rows = nh * tq
    pos = i * tq + lax.broadcasted_iota(jnp.int32, (tq, 1), 0)
    q = jnp.concatenate([r[...] for r in (q0_ref, q1_ref, q2_ref, q3_ref, q4_ref, q5_ref)], axis=0)

    cidx = lax.broadcasted_iota(jnp.int32, (tq, N_CMP_PAD), 1)
    valid_c = (cidx * NSA_CMP_STRIDE + (NSA_CMP_LEN - 1) <= pos) & (cidx < N_CMP_PAD - 1)
    valid_cf = valid_c.astype(F32)
    cbias = jnp.where(valid_c, 0.0, NEG_INF).astype(F32)
    s = (_qk(q, kc_ref[0]).reshape(nh, tq, N_CMP_PAD) + cbias[None]).reshape(rows, N_CMP_PAD)
    p = (jnp.exp(s - jnp.max(s, axis=-1, keepdims=True)).reshape(nh, tq, N_CMP_PAD) * valid_cf[None])
    p = p.reshape(rows, N_CMP_PAD)
    l = jnp.sum(p, axis=-1, keepdims=True)
    p = p / jnp.where(l > 0.0, l, 1.0)
    o_cmp = jnp.dot(p.astype(BF16), vc_ref[0], preferred_element_type=F32)
    psum = jnp.sum(p.reshape(nh, tq, N_CMP_PAD), axis=0)

    cover_t = cover_ref[...]
    imp = sum(jnp.dot(cover_t, part, preferred_element_type=F32) for part in _split3(psum.T))

    pos_t = i * tq + lax.broadcasted_iota(jnp.int32, (N_SEL_BLOCKS, tq), 1)
    jidx = lax.broadcasted_iota(jnp.int32, (N_SEL_BLOCKS, tq), 0)
    qblk = pos_t // NSA_SEL_BLOCK
    valid_s = jidx * NSA_SEL_BLOCK <= pos_t
    forced = (jidx == 0) | (jidx == qblk) | (jidx == qblk - 1)
    score = jnp.where(valid_s, jnp.where(forced, NSA_FORCE_SCORE, imp), -1.0)
    rank = jnp.zeros((N_SEL_BLOCKS, tq), jnp.int32)
    for ii in range(N_SEL_BLOCKS):
        si = score[ii:ii + 1, :]
        rank = rank + ((si > score) | ((si == score) & (ii < jidx))).astype(jnp.int32)
    sel_t = (rank < NSA_TOP_N).astype(F32)
    sel = jnp.concatenate([sel_t, jnp.zeros((LANES - N_SEL_BLOCKS, tq), F32)], axis=0).T.astype(BF16)

    def fill(j, carry):
        hit = jnp.dot(sel, expand_ref[j], preferred_element_type=F32)
        kpos = j * tq + lax.broadcasted_iota(jnp.int32, (tq, tq), 1)
        bias_scr[j] = jnp.where((hit > 0.5) & (kpos <= pos), 0.0, NEG_INF).astype(F32)
        return carry

    lax.fori_loop(0, i + 1, fill, 0)

    o_slc = _attend(q, ks_ref, vs_ref, lambda j: bias_scr[j], 0, i + 1, s_scr, tk=tq, groups=nh)
    w_lo = jnp.maximum(i - (wbias_ref.shape[0] - 1), 0)
    o_win = _attend(q, kw_ref, vw_ref, lambda j: wbias_ref[i - j], w_lo, i + 1 - w_lo, s_scr, tk=tq, groups=nh)

    gates = jax.nn.sigmoid(gate_ref[...].astype(F32))
    for h in range(nh):
        sl = slice(h * tq, (h + 1) * tq)
        g0 = NSA_BRANCHES * h
        out = (gates[:, g0:g0 + 1] * o_cmp[sl] + gates[:, g0 + 1:g0 + 2] * o_slc[sl]
               + gates[:, g0 + 2:g0 + 3] * o_win[sl])
        o_ref[:, h * LANES:(h + 1) * LANES] = out.astype(o_ref.dtype)


def _nsa(hproj, kc, vc, cover, expand, wbias, nb):
    t = NSA_T
    nq = SEQ // t
    wide = NSA_HEADS * HEAD_DIM
    q_spec = lambda h: pl.BlockSpec((t, LANES), lambda b, i: (b * nq + i, BLK_NQ + h))
    kv_spec = lambda blk: pl.BlockSpec((SEQ, LANES), lambda b, i: (b, blk))
    cspec = pl.BlockSpec((1, N_CMP_PAD, HEAD_DIM), lambda b, i: (b, 0, 0))
    full = lambda shape: pl.BlockSpec(shape, lambda b, i: (0,) * len(shape))
    return pl.pallas_call(
        _nsa_kernel,
        out_shape=jax.ShapeDtypeStruct((nb * SEQ, wide), BF16),
        grid=(nb, nq),
        in_specs=[q_spec(h) for h in range(NSA_HEADS)]
        + [cspec, cspec, kv_spec(BLK_NKS), kv_spec(BLK_NVS), kv_spec(BLK_NKW), kv_spec(BLK_NVW),
           pl.BlockSpec((t, LANES), lambda b, i: (b * nq + i, BLK_GATE)),
           full(cover.shape), full(expand.shape), full(wbias.shape)],
        out_specs=pl.BlockSpec((t, wide), lambda b, i: (b * nq + i, 0)),
        scratch_shapes=[pltpu.VMEM((nq, t, t), F32), pltpu.VMEM((nq, NSA_HEADS * t, t), F32)],
        compiler_params=_cparams(("arbitrary", "arbitrary")),
        name="nsa",
    )(*([hproj] * NSA_HEADS), kc, vc, hproj, hproj, hproj, hproj, hproj, cover, expand, wbias)


def _layer_norm(y, g, b):
    mu = jnp.mean(y, axis=-1, keepdims=True)
    var = jnp.mean(jnp.square(y - mu), axis=-1, keepdims=True)
    return (y - mu) * lax.rsqrt(var + LN_EPS) * g + b


def _lane_min(x):
    return jnp.min(x, axis=-1, keepdims=True)


def _lane_max(x):
    return jnp.max(x, axis=-1, keepdims=True)


def _route(logits):
    lane = lax.broadcasted_iota(jnp.int32, logits.shape, 1)
    lane_f = lane.astype(F32)
    far = float(LANES)
    is_grp = lane < N_GROUPS
    lg = jnp.where(is_grp, logits, NEG_INF)
    eg = jnp.where(is_grp, jnp.exp(lg - _lane_max(lg)), 0.0)
    prob = eg / jnp.sum(eg, axis=-1, keepdims=True)
    p_g = _lane_max(prob)
    g_idx = _lane_min(jnp.where(is_grp & (prob == p_g), lane_f, far))
    e_lo = N_GROUPS + EXPERTS_PER_GROUP * g_idx
    in_grp = (lane_f >= e_lo) & (lane_f < e_lo + EXPERTS_PER_GROUP)
    le = jnp.where(in_grp, logits, NEG_INF)
    v1 = _lane_max(le)
    i1 = _lane_min(jnp.where(in_grp & (le == v1), lane_f, far))
    rest = in_grp & (lane_f != i1)
    le2 = jnp.where(rest, logits, NEG_INF)
    v2 = _lane_max(le2)
    i2 = _lane_min(jnp.where(rest & (le2 == v2), lane_f, far))
    e21 = jnp.exp(v2 - v1)
    den = 1.0 + e21
    gate1 = p_g * (1.0 / den)
    gate2 = p_g * (e21 / den)
    out = jnp.where(lane == 0, i1 - N_GROUPS, 0.0)
    out = jnp.where(lane == 1, i2 - N_GROUPS, out)
    out = jnp.where(lane == 2, gate1, out)
    out = jnp.where(lane == 3, gate2, out)
    return out


def _outproj_kernel(om_ref, od_ref, on_ref, w_ref, x_ref, g_ref, b_ref, wr_ref, x1_ref, route_ref):
    n_mla = MLA_HEADS * HEAD_DIM
    n_dil = DIL_HEADS * HEAD_DIM
    for hh in range(OUTPROJ_TM // ROW_TM):
        rows = slice(hh * ROW_TM, (hh + 1) * ROW_TM)
        mix = jnp.dot(om_ref[rows, :], w_ref[0, 0:n_mla, :], preferred_element_type=F32)
        mix = mix + jnp.dot(od_ref[rows, :], w_ref[0, n_mla:n_mla + n_dil, :], preferred_element_type=F32)
        mix = mix + jnp.dot(on_ref[rows, :], w_ref[0, n_mla + n_dil:, :], preferred_element_type=F32)
        x1 = _layer_norm(DN_ALPHA * x_ref[rows, :] + mix, g_ref[...], b_ref[...])
        x1_ref[rows, :] = x1
        xh = x1.astype(BF16)
        xm = (x1 - xh.astype(F32)).astype(BF16)
        both = jnp.dot(xh, wr_ref[...], preferred_element_type=F32)
        logits = both[:, 0:LANES] + (both[:, LANES:2 * LANES]
                                     + jnp.dot(xm, wr_ref[:, 0:LANES], preferred_element_type=F32))
        route_ref[rows, :] = _route(logits)


def _outproj(o_mla, o_dil, o_nsa, w_out_s, layer, x, g, b, wr3):
    t = x.shape[0]
    tm = OUTPROJ_TM
    full = lambda shape: pl.BlockSpec(shape, lambda i: (0,) * len(shape))
    row = lambda w: pl.BlockSpec((tm, w), lambda i: (i, 0))
    return pl.pallas_call(
        _outproj_kernel,
        out_shape=(jax.ShapeDtypeStruct((t, D_MODEL), F32), jax.ShapeDtypeStruct((t, LANES), F32)),
        grid=(t // tm,),
        in_specs=[row(o_mla.shape[1]), row(o_dil.shape[1]), row(o_nsa.shape[1]),
                  pl.BlockSpec((1,) + w_out_s.shape[1:], lambda i: (layer, 0, 0)),
                  row(D_MODEL), full((1, D_MODEL)), full((1, D_MODEL)), full(wr3.shape)],
        out_specs=(row(D_MODEL), row(LANES)),
        compiler_params=_cparams(("arbitrary",)),
        name="outproj",
    )(o_mla, o_dil, o_nsa, w_out_s, x, g, b, wr3)


def _dispatch_kernel(fill_ref, pos_ref, x1_ref, xb_hbm, stage, zbuf, sem, zsem):
    i = pl.program_id(0)
    n = pl.num_programs(0)
    tm = ROW_TM
    slot = i % 2

    def drain(s):
        for _ in range(MOE_TOP_K):
            pltpu.make_async_copy(stage.at[s], xb_hbm.at[pl.ds(0, tm)], sem.at[s]).wait()

    def zero_copy(e):
        start = pl.multiple_of(fill_ref[e], MOE_ROWS)
        return pltpu.make_async_copy(zbuf, xb_hbm.at[pl.ds(start, MOE_ROWS)], zsem)

    @pl.when(i == 0)
    def _():
        zbuf[...] = jnp.zeros_like(zbuf)
        for e in range(N_EXPERTS):
            @pl.when(fill_ref[e] >= 0)
            def _():
                zero_copy(e).start()
        for e in range(N_EXPERTS):
            @pl.when(fill_ref[e] >= 0)
            def _():
                zero_copy(e).wait()

        def tail_copy(b):
            start = pl.multiple_of(b * MOE_ROWS, MOE_ROWS)
            return pltpu.make_async_copy(zbuf, xb_hbm.at[pl.ds(start, MOE_ROWS)], zsem)

        n_blocks = xb_hbm.shape[0] // MOE_ROWS
        lax.fori_loop(fill_ref[N_EXPERTS], n_blocks, lambda b, c: (tail_copy(b).start(), c)[1], 0)
        lax.fori_loop(fill_ref[N_EXPERTS], n_blocks, lambda b, c: (tail_copy(b).wait(), c)[1], 0)

    @pl.when(i >= 2)
    def _():
        drain(slot)

    stage[slot] = x1_ref[...]

    def body(g, carry):
        for u in range(DMA_UNROLL):
            r = g * DMA_UNROLL + u
            for k in range(MOE_TOP_K):
                p = pos_ref[0, 0, MOE_TOP_K * r + k]
                pltpu.make_async_copy(stage.at[slot, pl.ds(r, 1)], xb_hbm.at[pl.ds(p, 1)], sem.at[slot]).start()
        return carry

    lax.fori_loop(0, tm // DMA_UNROLL, body, 0)

    @pl.when(i == n - 1)
    def _():
        drain(1 - slot)
        drain(slot)


def _dispatch(fill, pos3, x1, n_rows):
    t = x1.shape[0]
    tm = ROW_TM
    grid_spec = pltpu.PrefetchScalarGridSpec(
        num_scalar_prefetch=1,
        grid=(t // tm,),
        in_specs=[pl.BlockSpec((1, 1, MOE_TOP_K * tm), lambda i, fl: (i, 0, 0), memory_space=pltpu.SMEM),
                  pl.BlockSpec((tm, D_MODEL), lambda i, fl: (i, 0))],
        out_specs=pl.BlockSpec(memory_space=pl.ANY),
        scratch_shapes=[pltpu.VMEM((2, tm, D_MODEL), F32), pltpu.VMEM((MOE_ROWS, D_MODEL), F32),
                        pltpu.SemaphoreType.DMA((2,)), pltpu.SemaphoreType.DMA(())],
    )
    return pl.pallas_call(
        _dispatch_kernel,
        out_shape=jax.ShapeDtypeStruct((n_rows, D_MODEL), F32),
        grid_spec=grid_spec,
        compiler_params=_cparams(("arbitrary",)),
        name="dispatch",
    )(fill, pos3, x1)


def _experts_kernel(blk_ex_ref, first_ref, nxt_ref, nused_ref, x_ref, wg_hbm, wu_hbm, wd_hbm, y_ref,
                    wg_f, wu_f, wd_f, wg_b, wu_b, wd_b, sem, *, layer):
    b = pl.program_id(0)
    nused = nused_ref[0]

    def weight_copies(e, s):
        return (pltpu.make_async_copy(wg_hbm.at[layer, e], wg_f.at[s], sem.at[s]),
                pltpu.make_async_copy(wu_hbm.at[layer, e], wu_f.at[s], sem.at[s]),
                pltpu.make_async_copy(wd_hbm.at[layer, e], wd_f.at[s], sem.at[s]))

    @pl.when(b == 0)
    def _():
        for cp in weight_copies(blk_ex_ref[0], 0):
            cp.start()

    @pl.when(b < nused)
    def _():
        first = first_ref[b]

        @pl.when(first > 0)
        def _():
            s = first - 1
            for cp in weight_copies(blk_ex_ref[b], s):
                cp.wait()

            @pl.when(nxt_ref[b] >= 0)
            def _():
                for cp in weight_copies(nxt_ref[b], 1 - s):
                    cp.start()

            wg_b[...] = wg_f[s].astype(BF16)
            wu_b[...] = wu_f[s].astype(BF16)
            wd_b[...] = wd_f[s].astype(BF16)

        rows = x_ref[...].astype(BF16)
        gate = jnp.dot(rows, wg_b[...], preferred_element_type=F32)
        up = jnp.dot(rows, wu_b[...], preferred_element_type=F32)
        hid = (gate * jax.nn.sigmoid(gate) * up).astype(BF16)
        y_ref[...] = jnp.dot(hid, wd_b[...], preferred_element_type=F32)

    @pl.when(b >= nused)
    def _():
        y_ref[...] = jnp.zeros_like(y_ref)


def _experts(blk_ex, first, nxt, nused, xb_rows, w_gate, w_up, w_down, layer):
    n_blocks = xb_rows.shape[0] // MOE_ROWS
    hbm = pl.BlockSpec(memory_space=pl.ANY)
    grid_spec = pltpu.PrefetchScalarGridSpec(
        num_scalar_prefetch=4,
        grid=(n_blocks,),
        in_specs=[pl.BlockSpec((MOE_ROWS, D_MODEL), lambda b, ex, fi, nx, nu: (jnp.minimum(b, nu[0] - 1), 0)),
                  hbm, hbm, hbm],
        out_specs=pl.BlockSpec((MOE_ROWS, D_MODEL), lambda b, ex, fi, nx, nu: (b, 0)),
        scratch_shapes=[pltpu.VMEM((2, D_MODEL, EXPERT_FF), F32), pltpu.VMEM((2, D_MODEL, EXPERT_FF), F32),
                        pltpu.VMEM((2, EXPERT_FF, D_MODEL), F32),
                        pltpu.VMEM((D_MODEL, EXPERT_FF), BF16), pltpu.VMEM((D_MODEL, EXPERT_FF), BF16),
                        pltpu.VMEM((EXPERT_FF, D_MODEL), BF16), pltpu.SemaphoreType.DMA((2,))],
    )
    return pl.pallas_call(
        functools.partial(_experts_kernel, layer=layer),
        out_shape=jax.ShapeDtypeStruct((n_blocks * MOE_ROWS, D_MODEL), F32),
        grid_spec=grid_spec,
        compiler_params=_cparams(("arbitrary",)),
        name="experts",
    )(blk_ex, first, nxt, nused, xb_rows, w_gate, w_up, w_down)


def _combine_kernel(pos_cur_ref, pos_nxt_ref, y_hbm, x1_ref, route_ref, g_ref, b_ref, o_ref, ob_ref, ybuf, sem):
    i = pl.program_id(0)
    n = pl.num_programs(0)
    tm = ROW_TM

    def issue(pos_ref, slot):
        def body(g, carry):
            for u in range(DMA_UNROLL):
                r = g * DMA_UNROLL + u
                for k in range(MOE_TOP_K):
                    p = pos_ref[0, 0, MOE_TOP_K * r + k]
                    pltpu.make_async_copy(y_hbm.at[pl.ds(p, 1)], ybuf.at[slot, k, pl.ds(r, 1)],
                                          sem.at[slot]).start()
            return carry

        lax.fori_loop(0, tm // DMA_UNROLL, body, 0)

    @pl.when(i == 0)
    def _():
        issue(pos_cur_ref, 0)

    @pl.when(i + 1 < n)
    def _():
        issue(pos_nxt_ref, (i + 1) % 2)

    slot = i % 2
    for k in range(MOE_TOP_K):
        pltpu.make_async_copy(y_hbm.at[pl.ds(0, tm)], ybuf.at[slot, k], sem.at[slot]).wait()
    route = route_ref[...]
    ffn = route[:, 2:3] * ybuf[slot, 0] + route[:, 3:4] * ybuf[slot, 1]
    x2 = _layer_norm(DN_ALPHA * x1_ref[...] + ffn, g_ref[...], b_ref[...])
    o_ref[...] = x2
    ob_ref[...] = x2.astype(ob_ref.dtype)


def _combine(pos3, yb, x1, route, g, b):
    t = x1.shape[0]
    tm = ROW_TM
    nt = t // tm
    pos_spec = lambda shift: pl.BlockSpec((1, 1, MOE_TOP_K * tm), lambda i: (jnp.minimum(i + shift, nt - 1), 0, 0),
                                          memory_space=pltpu.SMEM)
    row = lambda w: pl.BlockSpec((tm, w), lambda i: (i, 0))
    full = lambda shape: pl.BlockSpec(shape, lambda i: (0,) * len(shape))
    return pl.pallas_call(
        _combine_kernel,
        out_shape=(jax.ShapeDtypeStruct((t, D_MODEL), F32), jax.ShapeDtypeStruct((t, D_MODEL), BF16)),
        grid=(nt,),
        in_specs=[pos_spec(0), pos_spec(1), pl.BlockSpec(memory_space=pl.ANY), row(D_MODEL), row(LANES),
                  full((1, D_MODEL)), full((1, D_MODEL))],
        out_specs=(row(D_MODEL), row(D_MODEL)),
        scratch_shapes=[pltpu.VMEM((2, MOE_TOP_K, tm, D_MODEL), F32), pltpu.SemaphoreType.DMA((2,))],
        compiler_params=_cparams(("arbitrary",)),
        name="combine",
    )(pos3, pos3, yb, x1, route, g, b)


def _dispatch_plan(route, t):
    m = t * MOE_TOP_K
    n_blocks = -(-m // MOE_ROWS) + N_EXPERTS
    chunk = MOE_ROWS
    ex = route[:, 0:MOE_TOP_K].reshape(m // chunk, chunk, 1)
    onehot = (ex == jnp.arange(N_EXPERTS, dtype=F32)[None, None, :]).astype(F32)
    tri = jnp.asarray(np.tril(np.ones((chunk, chunk), np.float32)))
    within = jnp.einsum('ij,cje->cie', tri, onehot)
    totals = jnp.sum(onehot, axis=1)
    before = jnp.cumsum(totals, axis=0) - totals
    counts = jnp.sum(totals, axis=0)
    padded = jnp.ceil(counts / MOE_ROWS) * MOE_ROWS
    pad_end = jnp.cumsum(padded)
    pad_start = pad_end - padded
    row = jnp.sum(onehot * (within - 1.0 + before[:, None, :] + pad_start[None, None, :]), axis=-1)
    pos = row.astype(jnp.int32).reshape(m)
    blk_start = jnp.arange(n_blocks, dtype=F32) * MOE_ROWS
    blk_ex = jnp.minimum(jnp.sum((pad_end[None, :] <= blk_start[:, None]).astype(jnp.int32), axis=1),
                         N_EXPERTS - 1).astype(jnp.int32)
    nused = (pad_end[-1] / MOE_ROWS).astype(jnp.int32).reshape(1)
    fill = jnp.concatenate([jnp.where(padded > 0, pad_end - MOE_ROWS, -1.0).astype(jnp.int32), nused])
    nonempty = padded > 0
    slot_e = (jnp.cumsum(nonempty.astype(jnp.int32)) - 1) % 2
    expert_ids = jnp.arange(N_EXPERTS, dtype=jnp.int32)
    later = lax.cummin(jnp.where(nonempty, expert_ids, N_EXPERTS), axis=0, reverse=True)
    nxt_e = jnp.concatenate([later[1:], jnp.full((1,), N_EXPERTS, jnp.int32)])
    nxt_e = jnp.where(nxt_e >= N_EXPERTS, -1, nxt_e)
    is_first = (blk_start == pad_start[blk_ex]) & (jnp.arange(n_blocks) < nused[0])
    first = jnp.where(is_first, 1 + slot_e[blk_ex], 0).astype(jnp.int32)
    return blk_ex, first, nxt_e[blk_ex].astype(jnp.int32), nused, fill, pos


def _shift_kernel(w_ref, tail_ref, o_ref):
    n_keep = (IN_SHIFT_FROM + MLA_ROPE_DIM) // LANES
    n_src = IN_COLS // LANES
    o_ref[0, :, 0:n_keep * LANES] = w_ref[0, :, 0:n_keep * LANES].astype(o_ref.dtype)
    low = lax.broadcasted_iota(jnp.int32, (w_ref.shape[1], LANES), 1) < MLA_ROPE_DIM

    def src(m):
        if m < n_src:
            return w_ref[0, :, m * LANES:(m + 1) * LANES]
        return tail_ref[0] if m == n_src else jnp.zeros((w_ref.shape[1], LANES), F32)

    for m in range(n_keep, N_IN_BLOCKS):
        blk = jnp.where(low, pltpu.roll(src(m - 1), MLA_ROPE_DIM, 1), pltpu.roll(src(m), MLA_ROPE_DIM, 1))
        o_ref[0, :, m * LANES:(m + 1) * LANES] = blk.astype(o_ref.dtype)


def _shift_w_in(w_in):
    nl, d, _ = w_in.shape
    n_src = IN_COLS // LANES
    tail = jnp.pad(w_in[:, :, n_src * LANES:], ((0, 0), (0, 0), (0, (n_src + 1) * LANES - IN_COLS)))
    tk = ROW_TM
    return pl.pallas_call(
        _shift_kernel,
        out_shape=jax.ShapeDtypeStruct((nl, d, IN_COLS_PAD), BF16),
        grid=(nl, d // tk),
        in_specs=[pl.BlockSpec((1, tk, IN_COLS), lambda l, r: (l, r, 0)),
                  pl.BlockSpec((1, tk, LANES), lambda l, r: (l, r, 0))],
        out_specs=pl.BlockSpec((1, tk, IN_COLS_PAD), lambda l, r: (l, r, 0)),
        compiler_params=_cparams(("arbitrary", "arbitrary")),
        name="shiftw",
    )(w_in, tail)


def _permute_w_q_up(w_q_up):
    w = w_q_up.reshape(MLA_Q_RANK, MLA_HEADS, MLA_NOPE_DIM + MLA_ROPE_DIM)
    zero = jnp.zeros((MLA_Q_RANK, MLA_HEADS, MLA_QK_PAD - MLA_NOPE_DIM - MLA_ROPE_DIM), w.dtype)
    w = jnp.concatenate([w[:, :, :MLA_NOPE_DIM], zero, w[:, :, MLA_NOPE_DIM:]], axis=-1)
    return w.reshape(MLA_Q_RANK, MLA_HEADS * MLA_QK_PAD).astype(BF16)


def _split_w_kv_up(w_kv_up):
    w = w_kv_up.reshape(MLA_KV_RANK, MLA_HEADS, MLA_NOPE_DIM + HEAD_DIM)
    wkn = w[:, :, :MLA_NOPE_DIM].reshape(MLA_KV_RANK, MLA_HEADS * MLA_NOPE_DIM)
    wv = w[:, :, MLA_NOPE_DIM:].reshape(MLA_KV_RANK, MLA_HEADS * HEAD_DIM)
    return wkn.astype(BF16), wv.astype(BF16)


def _router_slices(w_grp, w_exp):
    w = jnp.concatenate([w_grp, w_exp, jnp.zeros((D_MODEL, LANES - N_GROUPS - N_EXPERTS), F32)], axis=1)
    hi = w.astype(BF16)
    mid = (w - hi.astype(F32)).astype(BF16)
    return jnp.concatenate([hi, mid], axis=1)


def _mixer(xb, nb, layer, w_in_s, q_lat_norm, w_q_up, kv_lat_norm, w_kv_up,
           cmp_pos_k, cmp_w1_k, cmp_w2_k, cmp_pos_v, cmp_w1_v, cmp_w2_v, consts):
    hproj = _inproj(xb, w_in_s, layer, consts["rope32"])
    wkn, wv = _split_w_kv_up(w_kv_up)
    q_mla, k_mla, v_mla = _mlaprep(hproj, q_lat_norm.reshape(1, -1), kv_lat_norm.reshape(1, -1),
                                   _permute_w_q_up(w_q_up), wkn, wv, consts["rope64"])
    wide = NSA_CMP_STRIDE * HEAD_DIM
    tk2 = hproj[:, BLK_NKC * LANES:(BLK_NKC + 1) * LANES].reshape(nb, N_CMP_PAD, wide)
    tv2 = hproj[:, BLK_NVC * LANES:(BLK_NVC + 1) * LANES].reshape(nb, N_CMP_PAD, wide)
    kc, vc = _compress(tk2, tv2, cmp_pos_k.reshape(2, wide), cmp_pos_v.reshape(2, wide),
                       cmp_w1_k.astype(BF16), cmp_w2_k.astype(BF16), cmp_w1_v.astype(BF16), cmp_w2_v.astype(BF16),
                       consts["rope_cmp"])
    o_mla = _flash(q_mla, k_mla, v_mla, consts["bias_causal"], nb=nb, heads=MLA_HEADS, dqk=MLA_QK_PAD,
                   dv=HEAD_DIM, q_blk0=0, k_blk0=0, v_blk0=0)
    o_dil = _flash(hproj, hproj, hproj, consts["bias_dil"], nb=nb, heads=DIL_HEADS, dqk=HEAD_DIM, dv=HEAD_DIM,
                   q_blk0=BLK_DQ, k_blk0=BLK_DK, v_blk0=BLK_DV)
    o_nsa = _nsa(hproj, kc, vc, consts["cover"], consts["expand"], consts["bias_win"], nb)
    return o_mla, o_dil, o_nsa


def _make_consts():
    pos = np.arange(SEQ)
    slot = np.arange(N_CMP_PAD)
    cmp_start = slot * NSA_CMP_STRIDE
    cmp_end = cmp_start + NSA_CMP_LEN - 1
    sel_start = np.arange(N_SEL_BLOCKS) * NSA_SEL_BLOCK
    cover = ((cmp_start[:, None] < sel_start[None, :] + NSA_SEL_BLOCK)
             & (cmp_start[:, None] + NSA_CMP_LEN > sel_start[None, :])
             & (slot[:, None] < N_CMP_PAD - 1)).astype(np.float32)
    expand = (np.arange(LANES)[:, None] == (pos // NSA_SEL_BLOCK)[None, :]).astype(np.float32)
    expand = expand.reshape(LANES, SEQ // NSA_T, NSA_T).transpose(1, 0, 2)
    n_win = -(-NSA_WINDOW // NSA_T) + 1
    return {
        "rope32": _rope_tables(pos, ROT_DIM, True),
        "rope64": _rope_tables(pos, MLA_ROPE_DIM, False, KROPE_LANE0),
        "rope_cmp": _rope_tables(cmp_end, ROT_DIM, True),
        "bias_causal": _distance_bias(_causal_mult, (0, 1), FLASH_T),
        "bias_dil": _distance_bias(_dilated_mult, range(SEQ // FLASH_T), FLASH_T),
        "bias_win": _distance_bias(_window_mult, range(n_win), NSA_T),
        "cover": jnp.asarray(cover.T, BF16),
        "expand": jnp.asarray(expand, BF16),
    }


def kernel(x, w_in, q_lat_norm, w_q_up, kv_lat_norm, w_kv_up, cmp_pos_k, cmp_w1_k, cmp_w2_k, cmp_pos_v, cmp_w1_v,
           cmp_w2_v, w_out, ln1_g, ln1_b, w_grp, w_exp, w_gate, w_up, w_down, ln2_g, ln2_b):
    nb, s, d = x.shape
    assert s == SEQ and d == D_MODEL
    t = nb * s
    consts = _make_consts()
    w_in_s = _shift_w_in(w_in)
    w_out_s = w_out.astype(BF16)
    xf = x.reshape(t, d)
    xb = xf.astype(BF16)
    for l in range(DEPTH):
        o_mla, o_dil, o_nsa = _mixer(xb, nb, l, w_in_s, q_lat_norm[l], w_q_up[l], kv_lat_norm[l], w_kv_up[l],
                                     cmp_pos_k[l], cmp_w1_k[l], cmp_w2_k[l], cmp_pos_v[l], cmp_w1_v[l],
                                     cmp_w2_v[l], consts)
        x1, route = _outproj(o_mla, o_dil, o_nsa, w_out_s, l, xf, ln1_g[l].reshape(1, d),
                             ln1_b[l].reshape(1, d), _router_slices(w_grp[l], w_exp[l]))
        blk_ex, first, nxt, nused, fill, pos = _dispatch_plan(route, t)
        pos3 = pos.reshape(t // ROW_TM, 1, MOE_TOP_K * ROW_TM)
        n_rows = (-(-t * MOE_TOP_K // MOE_ROWS) + N_EXPERTS) * MOE_ROWS
        yb = _experts(blk_ex, first, nxt, nused, _dispatch(fill, pos3, x1, n_rows), w_gate, w_up, w_down, l)
        xf, xb = _combine(pos3, yb, x1, route,
                          ln2_g[l].reshape(1, d), ln2_b[l].reshape(1, d))
    return xf.reshape(nb, s, d)
```

```python
import functools

import numpy as np
import jax
import jax.numpy as jnp
from jax import lax
from jax.experimental import pallas as pl
from jax.experimental.pallas import tpu as pltpu

F32 = jnp.float32
BF16 = jnp.bfloat16

D_MODEL = 2048
SEQ = 2048
DEPTH = 2
HEAD_DIM = 128
LANES = 128
MLA_HEADS = 4
DIL_HEADS = 6
NSA_HEADS = 6
MLA_Q_RANK = 384
MLA_KV_RANK = 128
MLA_NOPE_DIM = 128
MLA_ROPE_DIM = 64
MLA_QK_PAD = 256
DIL_PAIRS = ((128, 1), (512, 4), (2048, 16))
NSA_CMP_LEN = 32
NSA_CMP_STRIDE = 16
NSA_CMP_HIDDEN = 256
NSA_SEL_BLOCK = 64
NSA_TOP_N = 16
NSA_WINDOW = 512
NSA_BRANCHES = 3
NSA_FORCE_SCORE = 1e4
ROPE_THETA = 500000.0
ROT_DIM = HEAD_DIM // 4
N_GROUPS = 4
EXPERTS_PER_GROUP = 8
N_EXPERTS = N_GROUPS * EXPERTS_PER_GROUP
EXPERT_FF = 512
MOE_TOP_K = 2
MOE_ROWS = 256
DN_ALPHA = (2 * DEPTH) ** 0.25
LN_EPS = 1e-5
RMS_EPS = 1e-6
NEG_INF = -1e30

N_CMP_PAD = SEQ // NSA_CMP_STRIDE
N_SEL_BLOCKS = SEQ // NSA_SEL_BLOCK

BLK_QLAT = 0
BLK_KVLAT = 3
BLK_KROPE = 4
BLK_DQ = 5
BLK_DK = 11
BLK_DV = 17
BLK_NQ = 23
BLK_NKC = 29
BLK_NVC = 30
BLK_NKS = 31
BLK_NVS = 32
BLK_NKW = 33
BLK_NVW = 34
BLK_GATE = 35
N_IN_BLOCKS = 36
IN_COLS = 4434
IN_SHIFT_FROM = MLA_Q_RANK + MLA_KV_RANK - MLA_ROPE_DIM
IN_COLS_PAD = N_IN_BLOCKS * LANES
KROPE_LANE0 = LANES - MLA_ROPE_DIM

INPROJ_TM = 2048
INPROJ_SUB = 512
INPROJ_TN = 512
FLASH_T = 512
NSA_T = 256
ROW_TM = 256
OUTPROJ_TM = 512
DMA_UNROLL = 8

VMEM_LIMIT = 56 * 1024 * 1024


def _cparams(sem):
    return pltpu.CompilerParams(dimension_semantics=sem, vmem_limit_bytes=VMEM_LIMIT)


def _rope_tables(pos, rot_dim, keep_rest, lane0=0):
    half = rot_dim // 2
    inv_freq = np.power(np.float32(ROPE_THETA), -np.arange(half, dtype=np.float32) / np.float32(half))
    ang = (pos.astype(np.float32)[:, None] * inv_freq[None, :]).astype(np.float32)
    cos, sin = np.cos(ang).astype(np.float32), np.sin(ang).astype(np.float32)
    n = pos.shape[0]
    c = np.full((n, LANES), 1.0 if keep_rest else 0.0, np.float32)
    s1 = np.zeros((n, LANES), np.float32)
    s2 = np.zeros((n, LANES), np.float32)
    c[:, lane0:lane0 + half] = cos
    c[:, lane0 + half:lane0 + rot_dim] = cos
    s1[:, lane0 + half:lane0 + rot_dim] = sin
    s2[:, lane0:lane0 + half] = -sin
    return jnp.asarray(c), jnp.asarray(s1), jnp.asarray(s2)


def _rope_lanes(a, c, s1, s2, half):
    return a * c + pltpu.roll(a, half, 1) * s1 + pltpu.roll(a, LANES - half, 1) * s2


def _distance_bias(mult_fn, diags, t):
    d = np.asarray(diags, np.int64)[:, None, None] * t
    r = np.arange(t, dtype=np.int64)[None, :, None]
    c = np.arange(t, dtype=np.int64)[None, None, :]
    mult = mult_fn(d + r - c)
    return jnp.asarray(np.where(mult > 0, np.log(np.maximum(mult, 1).astype(np.float32)), NEG_INF).astype(np.float32))


def _causal_mult(dist):
    return (dist >= 0).astype(np.int64)


def _dilated_mult(dist):
    m = np.zeros_like(dist)
    for window, dil in DIL_PAIRS:
        m = m + ((dist >= 0) & (dist % dil == 0) & (dist <= (window // dil) * dil)).astype(np.int64)
    return m


def _window_mult(dist):
    return ((dist >= 0) & (dist <= NSA_WINDOW - 1)).astype(np.int64)


def _in_block(blk, lo, n):
    return jnp.logical_and(blk >= lo, blk < lo + n)


def _inproj_kernel(x_ref, w_ref, c_ref, s1_ref, s2_ref, o_ref, *, scale):
    j = pl.program_id(0)
    for hh in range(INPROJ_TM // INPROJ_SUB):
        rows = slice(hh * INPROJ_SUB, (hh + 1) * INPROJ_SUB)
        acc = _qk(x_ref[rows, :], w_ref[0])
        c, s1, s2 = c_ref[rows, :], s1_ref[rows, :], s2_ref[rows, :]
        for u in range(INPROJ_TN // LANES):
            blk = j * (INPROJ_TN // LANES) + u
            is_q = jnp.logical_or(_in_block(blk, BLK_DQ, DIL_HEADS), _in_block(blk, BLK_NQ, NSA_HEADS))
            is_rope = is_q | _in_block(blk, BLK_DK, DIL_HEADS) | (blk == BLK_NKS) | (blk == BLK_NKW)
            sc = jnp.where(is_q, scale, 1.0).astype(F32)
            a = acc[:, u * LANES:(u + 1) * LANES]
            r = _rope_lanes(a, jnp.where(is_rope, c, 1.0), jnp.where(is_rope, s1, 0.0),
                            jnp.where(is_rope, s2, 0.0), ROT_DIM // 2)
            o_ref[rows, u * LANES:(u + 1) * LANES] = (r * sc).astype(o_ref.dtype)


def _inproj(xb, w_in_s, layer, tabs):
    t = xb.shape[0]
    tm, tn = INPROJ_TM, INPROJ_TN
    nrow = SEQ // tm
    tab_spec = pl.BlockSpec((tm, LANES), lambda j, i: (i % nrow, 0))
    return pl.pallas_call(
        functools.partial(_inproj_kernel, scale=HEAD_DIM ** -0.5),
        out_shape=jax.ShapeDtypeStruct((t, IN_COLS_PAD), BF16),
        grid=(IN_COLS_PAD // tn, t // tm),
        in_specs=[pl.BlockSpec((tm, D_MODEL), lambda j, i: (i, 0)),
                  pl.BlockSpec((1, tn, D_MODEL), lambda j, i: (layer, j, 0)),
                  tab_spec, tab_spec, tab_spec],
        out_specs=pl.BlockSpec((tm, tn), lambda j, i: (i, j)),
        compiler_params=_cparams(("arbitrary", "arbitrary")),
        name="inproj",
    )(xb, w_in_s, *tabs)


def _rms(xf, g):
    return xf * lax.rsqrt(jnp.mean(jnp.square(xf), axis=-1, keepdims=True) + RMS_EPS) * g


def _mlaprep_kernel(ql_ref, kvl_ref, kr_ref, gq_ref, gkv_ref, wq_ref, wkn_ref, wv_ref,
                    c_ref, s1_ref, s2_ref, q_ref, k_ref, v_ref, *, scale):
    c, s1, s2 = c_ref[...], s1_ref[...], s2_ref[...]
    half = MLA_ROPE_DIM // 2
    qn = _rms(ql_ref[...].astype(F32), gq_ref[...]).astype(BF16)
    q = jnp.dot(qn, wq_ref[...], preferred_element_type=F32)
    for h in range(MLA_HEADS):
        lo = h * MLA_QK_PAD
        q_ref[:, lo:lo + LANES] = (q[:, lo:lo + LANES] * scale).astype(q_ref.dtype)
        a = q[:, lo + LANES:lo + 2 * LANES]
        q_ref[:, lo + LANES:lo + 2 * LANES] = (_rope_lanes(a, c, s1, s2, half) * scale).astype(q_ref.dtype)
    kvn = _rms(kvl_ref[...].astype(F32), gkv_ref[...]).astype(BF16)
    kn = jnp.dot(kvn, wkn_ref[...], preferred_element_type=F32)
    v_ref[...] = jnp.dot(kvn, wv_ref[...], preferred_element_type=F32).astype(v_ref.dtype)
    kpe = _rope_lanes(kr_ref[...].astype(F32), c, s1, s2, half).astype(k_ref.dtype)
    for h in range(MLA_HEADS):
        lo = h * MLA_QK_PAD
        k_ref[:, lo:lo + LANES] = kn[:, h * LANES:(h + 1) * LANES].astype(k_ref.dtype)
        k_ref[:, lo + LANES:lo + 2 * LANES] = kpe


def _mlaprep(hproj, gq, gkv, wq_p, wkn, wv, tabs):
    t = hproj.shape[0]
    tm = ROW_TM
    nrow = SEQ // tm
    tab_spec = pl.BlockSpec((tm, LANES), lambda i: (i % nrow, 0))
    full = lambda shape: pl.BlockSpec(shape, lambda i: (0,) * len(shape))
    return pl.pallas_call(
        functools.partial(_mlaprep_kernel, scale=(MLA_NOPE_DIM + MLA_ROPE_DIM) ** -0.5),
        out_shape=(jax.ShapeDtypeStruct((t, MLA_HEADS * MLA_QK_PAD), BF16),
                   jax.ShapeDtypeStruct((t, MLA_HEADS * MLA_QK_PAD), BF16),
                   jax.ShapeDtypeStruct((t, MLA_HEADS * HEAD_DIM), BF16)),
        grid=(t // tm,),
        in_specs=[pl.BlockSpec((tm, MLA_Q_RANK), lambda i: (i, BLK_QLAT * LANES // MLA_Q_RANK)),
                  pl.BlockSpec((tm, LANES), lambda i: (i, BLK_KVLAT)),
                  pl.BlockSpec((tm, LANES), lambda i: (i, BLK_KROPE)),
                  full((1, MLA_Q_RANK)), full((1, MLA_KV_RANK)),
                  full((MLA_Q_RANK, MLA_HEADS * MLA_QK_PAD)),
                  full((MLA_KV_RANK, MLA_HEADS * MLA_NOPE_DIM)),
                  full((MLA_KV_RANK, MLA_HEADS * HEAD_DIM)),
                  tab_spec, tab_spec, tab_spec],
        out_specs=(pl.BlockSpec((tm, MLA_HEADS * MLA_QK_PAD), lambda i: (i, 0)),
                   pl.BlockSpec((tm, MLA_HEADS * MLA_QK_PAD), lambda i: (i, 0)),
                   pl.BlockSpec((tm, MLA_HEADS * HEAD_DIM), lambda i: (i, 0))),
        compiler_params=_cparams(("arbitrary",)),
        name="mlaprep",
    )(hproj, hproj, hproj, gq, gkv, wq_p, wkn, wv, *tabs)


def _gelu_tanh(x):
    return 0.5 * x * (1.0 + jnp.tanh(0.7978845608028654 * (x + 0.044715 * (x * x * x))))


def _compress_kernel(tk_ref, tv_ref, pek_ref, pev_ref, w1k_ref, w2k_ref, w1v_ref, w2v_ref,
                     c_ref, s1_ref, s2_ref, kc_ref, vc_ref):
    half_in = NSA_CMP_STRIDE * HEAD_DIM

    def comp(t_ref, pe_ref, w1_ref, w2_ref):
        t = t_ref[0].astype(F32)
        a = (t + pe_ref[0:1, :]).astype(BF16)
        b = (t + pe_ref[1:2, :]).astype(BF16)
        y0 = jnp.dot(a, w1_ref[0:half_in, :], preferred_element_type=F32)
        y1 = jnp.dot(b, w1_ref[half_in:2 * half_in, :], preferred_element_type=F32)
        hid = _gelu_tanh(y0 + pltpu.roll(y1, N_CMP_PAD - 1, 0))
        return jnp.dot(hid.astype(BF16), w2_ref[...], preferred_element_type=F32)

    kc = comp(tk_ref, pek_ref, w1k_ref, w2k_ref)
    kc_ref[0] = _rope_lanes(kc, c_ref[...], s1_ref[...], s2_ref[...], ROT_DIM // 2).astype(kc_ref.dtype)
    vc_ref[0] = comp(tv_ref, pev_ref, w1v_ref, w2v_ref).astype(vc_ref.dtype)


def _compress(tk2, tv2, pek, pev, w1k, w2k, w1v, w2v, tabs):
    nb = tk2.shape[0]
    wide = NSA_CMP_STRIDE * HEAD_DIM
    full = lambda shape: pl.BlockSpec(shape, lambda b: (0,) * len(shape))
    bspec = pl.BlockSpec((1, N_CMP_PAD, wide), lambda b: (b, 0, 0))
    ospec = pl.BlockSpec((1, N_CMP_PAD, HEAD_DIM), lambda b: (b, 0, 0))
    return pl.pallas_call(
        _compress_kernel,
        out_shape=(jax.ShapeDtypeStruct((nb, N_CMP_PAD, HEAD_DIM), BF16),) * 2,
        grid=(nb,),
        in_specs=[bspec, bspec, full((2, wide)), full((2, wide)),
                  full((2 * wide, NSA_CMP_HIDDEN)), full((NSA_CMP_HIDDEN, HEAD_DIM)),
                  full((2 * wide, NSA_CMP_HIDDEN)), full((NSA_CMP_HIDDEN, HEAD_DIM)),
                  full((N_CMP_PAD, LANES)), full((N_CMP_PAD, LANES)), full((N_CMP_PAD, LANES))],
        out_specs=(ospec, ospec),
        compiler_params=_cparams(("arbitrary",)),
        name="compress",
    )(tk2, tv2, pek, pev, w1k, w2k, w1v, w2v, *tabs)


def _qk(q, k):
    return lax.dot_general(q, k, (((1,), (1,)), ((), ())), preferred_element_type=F32)


def _lane_fold(x, op):
    out = x[:, 0:LANES]
    for u in range(1, x.shape[1] // LANES):
        out = op(out, x[:, u * LANES:(u + 1) * LANES])
    return out


def _split3(x):
    hi = x.astype(BF16)
    r1 = x - hi.astype(F32)
    mid = r1.astype(BF16)
    lo = (r1 - mid.astype(F32)).astype(BF16)
    return hi, mid, lo


def _attend(q, k_ref, v_ref, bias_fn, j_lo, n_tiles, s_scr, *, tk, groups=1):
    rows = q.shape[0]
    dv = v_ref.shape[-1]
    assert dv == LANES
    ones = jnp.ones((tk, LANES), BF16)

    def pass1(t, mx):
        off = pl.multiple_of((j_lo + t) * tk, tk)
        s = _qk(q, k_ref[pl.ds(off, tk), :])
        bias = bias_fn(j_lo + t)
        if groups > 1:
            s = (s.reshape(groups, rows // groups, tk) + bias[None]).reshape(rows, tk)
        else:
            s = s + bias
        s_scr[t] = s.astype(BF16)
        return jnp.maximum(mx, _lane_fold(s, jnp.maximum))

    mx = lax.fori_loop(0, n_tiles, pass1, jnp.full((rows, LANES), NEG_INF, F32))
    m = jnp.broadcast_to(jnp.max(mx, axis=-1, keepdims=True), (rows, LANES)).astype(BF16)

    def pass2(t, acc):
        off = pl.multiple_of((j_lo + t) * tk, tk)
        s = s_scr[t]
        p = jnp.concatenate([jnp.exp(s[:, u * LANES:(u + 1) * LANES] - m) for u in range(tk // LANES)], axis=1)
        v_aug = jnp.concatenate([v_ref[pl.ds(off, tk), :], ones], axis=1)
        return acc + jnp.dot(p, v_aug, preferred_element_type=F32)

    acc = lax.fori_loop(0, n_tiles, pass2, jnp.zeros((rows, dv + LANES), F32))
    return acc[:, 0:dv] / acc[:, dv:dv + LANES]


def _flash_kernel(q_ref, k_ref, v_ref, bias_ref, o_ref, s_scr):
    i = pl.program_id(2)
    last = bias_ref.shape[0] - 1
    out = _attend(q_ref[...], k_ref, v_ref, lambda j: bias_ref[jnp.minimum(i - j, last)], 0, i + 1, s_scr,
                  tk=FLASH_T)
    o_ref[...] = out.astype(o_ref.dtype)


def _flash(q_arr, k_arr, v_arr, bias, *, nb, heads, dqk, dv, q_blk0, k_blk0, v_blk0):
    t = FLASH_T
    nq = SEQ // t
    return pl.pallas_call(
        _flash_kernel,
        out_shape=jax.ShapeDtypeStruct((nb * SEQ, heads * dv), BF16),
        grid=(nb, heads, nq),
        in_specs=[pl.BlockSpec((t, dqk), lambda b, h, i: (b * nq + i, q_blk0 + h)),
                  pl.BlockSpec((SEQ, dqk), lambda b, h, i: (b, k_blk0 + h)),
                  pl.BlockSpec((SEQ, dv), lambda b, h, i: (b, v_blk0 + h)),
                  pl.BlockSpec(bias.shape, lambda b, h, i: (0, 0, 0))],
        out_specs=pl.BlockSpec((t, dv), lambda b, h, i: (b * nq + i, h)),
        scratch_shapes=[pltpu.VMEM((nq, t, t), BF16)],
        compiler_params=_cparams(("arbitrary", "arbitrary", "arbitrary")),
        name="flash",
    )(q_arr, k_arr, v_arr, bias)


def _nsa_kernel(q0_ref, q1_ref, q2_ref, q3_ref, q4_ref, q5_ref, kc_ref, vc_ref, ks_ref, vs_ref, kw_ref, vw_ref,
                gate_ref, cover_ref, expand_ref, wbias_ref, o_ref, bias_scr, s_scr):
    i = pl.program_id(1)
    tq = NSA_T
    nh = NSA_HEADS
    rows = nh * tq
    pos = i * tq + lax.broadcasted_iota(jnp.int32, (tq, 1), 0)
    q = jnp.concatenate([r[...] for r in (q0_ref, q1_ref, q2_ref, q3_ref, q4_ref, q5_ref)], axis=0)

    cidx = lax.broadcasted_iota(jnp.int32, (tq, N_CMP_PAD), 1)
    valid_c = (cidx * NSA_CMP_STRIDE + (NSA_CMP_LEN - 1) <= pos) & (cidx < N_CMP_PAD - 1)
    valid_cf = valid_c.astype(F32)
    cbias = jnp.where(valid_c, 0.0, NEG_INF).astype(F32)
    s = (_qk(q, kc_ref[0]).reshape(nh, tq, N_CMP_PAD) + cbias[None]).reshape(rows, N_CMP_PAD)
    p = (jnp.exp(s - jnp.max(s, axis=-1, keepdims=True)).reshape(nh, tq, N_CMP_PAD) * valid_cf[None])
    p = p.reshape(rows, N_CMP_PAD)
    l = jnp.sum(p, axis=-1, keepdims=True)
    p = p / jnp.where(l > 0.0, l, 1.0)
    o_cmp = jnp.dot(p.astype(BF16), vc_ref[0], preferred_element_type=F32)
    psum = jnp.sum(p.reshape(nh, tq, N_CMP_PAD), axis=0)

    cover_t = cover_ref[...]
    imp = sum(jnp.dot(cover_t, part, preferred_element_type=F32) for part in _split3(psum.T))

    pos_t = i * tq + lax.broadcasted_iota(jnp.int32, (N_SEL_BLOCKS, tq), 1)
    jidx = lax.broadcasted_iota(jnp.int32, (N_SEL_BLOCKS, tq), 0)
    qblk = pos_t // NSA_SEL_BLOCK
    valid_s = jidx * NSA_SEL_BLOCK <= pos_t
    forced = (jidx == 0) | (jidx == qblk) | (jidx == qblk - 1)
    score = jnp.where(valid_s, jnp.where(forced, NSA_FORCE_SCORE, imp), -1.0)
    rank = jnp.zeros((N_SEL_BLOCKS, tq), jnp.int32)
    for ii in range(N_SEL_BLOCKS):
        si = score[ii:ii + 1, :]
        rank = rank + ((si > score) | ((si == score) & (ii < jidx))).astype(jnp.int32)
    sel_t = (rank < NSA_TOP_N).astype(F32)
    sel = jnp.concatenate([sel_t, jnp.zeros((LANES - N_SEL_BLOCKS, tq), F32)], axis=0).T.astype(BF16)

    def fill(j, carry):
        hit = jnp.dot(sel, expand_ref[j], preferred_element_type=F32)
        kpos = j * tq + lax.broadcasted_iota(jnp.int32, (tq, tq), 1)
        bias_scr[j] = jnp.where((hit > 0.5) & (kpos <= pos), 0.0, NEG_INF).astype(F32)
        return carry

    lax.fori_loop(0, i + 1, fill, 0)

    o_slc = _attend(q, ks_ref, vs_ref, lambda j: bias_scr[j], 0, i + 1, s_scr, tk=tq, groups=nh)
    w_lo = jnp.maximum(i - (wbias_ref.shape[0] - 1), 0)
    o_win = _attend(q, kw_ref, vw_ref, lambda j: wbias_ref[i - j], w_lo, i + 1 - w_lo, s_scr, tk=tq, groups=nh)

    gates = jax.nn.sigmoid(gate_ref[...].astype(F32))
    for h in range(nh):
        sl = slice(h * tq, (h + 1) * tq)
        g0 = NSA_BRANCHES * h
        out = (gates[:, g0:g0 + 1] * o_cmp[sl] + gates[:, g0 + 1:g0 + 2] * o_slc[sl]
               + gates[:, g0 + 2:g0 + 3] * o_win[sl])
        o_ref[:, h * LANES:(h + 1) * LANES] = out.astype(o_ref.dtype)


def _nsa(hproj, kc, vc, cover, expand, wbias, nb):
    t = NSA_T
    nq = SEQ // t
    wide = NSA_HEADS * HEAD_DIM
    q_spec = lambda h: pl.BlockSpec((t, LANES), lambda b, i: (b * nq + i, BLK_NQ + h))
    kv_spec = lambda blk: pl.BlockSpec((SEQ, LANES), lambda b, i: (b, blk))
    cspec = pl.BlockSpec((1, N_CMP_PAD, HEAD_DIM), lambda b, i: (b, 0, 0))
    full = lambda shape: pl.BlockSpec(shape, lambda b, i: (0,) * len(shape))
    return pl.pallas_call(
        _nsa_kernel,
        out_shape=jax.ShapeDtypeStruct((nb * SEQ, wide), BF16),
        grid=(nb, nq),
        in_specs=[q_spec(h) for h in range(NSA_HEADS)]
        + [cspec, cspec, kv_spec(BLK_NKS), kv_spec(BLK_NVS), kv_spec(BLK_NKW), kv_spec(BLK_NVW),
           pl.BlockSpec((t, LANES), lambda b, i: (b * nq + i, BLK_GATE)),
           full(cover.shape), full(expand.shape), full(wbias.shape)],
        out_specs=pl.BlockSpec((t, wide), lambda b, i: (b * nq + i, 0)),
        scratch_shapes=[pltpu.VMEM((nq, t, t), F32), pltpu.VMEM((nq, NSA_HEADS * t, t), BF16)],
        compiler_params=_cparams(("arbitrary", "arbitrary")),
        name="nsa",
    )(*([hproj] * NSA_HEADS), kc, vc, hproj, hproj, hproj, hproj, hproj, cover, expand, wbias)


def _layer_norm(y, g, b):
    mu = jnp.mean(y, axis=-1, keepdims=True)
    var = jnp.mean(jnp.square(y - mu), axis=-1, keepdims=True)
    return (y - mu) * lax.rsqrt(var + LN_EPS) * g + b


def _lane_min(x):
    return jnp.min(x, axis=-1, keepdims=True)


def _lane_max(x):
    return jnp.max(x, axis=-1, keepdims=True)


def _route(logits):
    lane = lax.broadcasted_iota(jnp.int32, logits.shape, 1)
    lane_f = lane.astype(F32)
    far = float(LANES)
    is_grp = lane < N_GROUPS
    lg = jnp.where(is_grp, logits, NEG_INF)
    eg = jnp.where(is_grp, jnp.exp(lg - _lane_max(lg)), 0.0)
    prob = eg / jnp.sum(eg, axis=-1, keepdims=True)
    p_g = _lane_max(prob)
    g_idx = _lane_min(jnp.where(is_grp & (prob == p_g), lane_f, far))
    e_lo = N_GROUPS + EXPERTS_PER_GROUP * g_idx
    in_grp = (lane_f >= e_lo) & (lane_f < e_lo + EXPERTS_PER_GROUP)
    le = jnp.where(in_grp, logits, NEG_INF)
    v1 = _lane_max(le)
    i1 = _lane_min(jnp.where(in_grp & (le == v1), lane_f, far))
    rest = in_grp & (lane_f != i1)
    le2 = jnp.where(rest, logits, NEG_INF)
    v2 = _lane_max(le2)
    i2 = _lane_min(jnp.where(rest & (le2 == v2), lane_f, far))
    e21 = jnp.exp(v2 - v1)
    den = 1.0 + e21
    gate1 = p_g * (1.0 / den)
    gate2 = p_g * (e21 / den)
    out = jnp.where(lane == 0, i1 - N_GROUPS, 0.0)
    out = jnp.where(lane == 1, i2 - N_GROUPS, out)
    out = jnp.where(lane == 2, gate1, out)
    out = jnp.where(lane == 3, gate2, out)
    return out


def _outproj_kernel(om_ref, od_ref, on_ref, w_ref, x_ref, g_ref, b_ref, wr_ref, x1_ref, route_ref):
    n_mla = MLA_HEADS * HEAD_DIM
    n_dil = DIL_HEADS * HEAD_DIM
    for hh in range(OUTPROJ_TM // ROW_TM):
        rows = slice(hh * ROW_TM, (hh + 1) * ROW_TM)
        mix = jnp.dot(om_ref[rows, :], w_ref[0, 0:n_mla, :], preferred_element_type=F32)
        mix = mix + jnp.dot(od_ref[rows, :], w_ref[0, n_mla:n_mla + n_dil, :], preferred_element_type=F32)
        mix = mix + jnp.dot(on_ref[rows, :], w_ref[0, n_mla + n_dil:, :], preferred_element_type=F32)
        x1 = _layer_norm(DN_ALPHA * x_ref[rows, :] + mix, g_ref[...], b_ref[...])
        x1_ref[rows, :] = x1
        xh = x1.astype(BF16)
        xm = (x1 - xh.astype(F32)).astype(BF16)
        both = jnp.dot(xh, wr_ref[...], preferred_element_type=F32)
        logits = both[:, 0:LANES] + (both[:, LANES:2 * LANES]
                                     + jnp.dot(xm, wr_ref[:, 0:LANES], preferred_element_type=F32))
        route_ref[rows, :] = _route(logits)


def _outproj(o_mla, o_dil, o_nsa, w_out_s, layer, x, g, b, wr3):
    t = x.shape[0]
    tm = OUTPROJ_TM
    full = lambda shape: pl.BlockSpec(shape, lambda i: (0,) * len(shape))
    row = lambda w: pl.BlockSpec((tm, w), lambda i: (i, 0))
    return pl.pallas_call(
        _outproj_kernel,
        out_shape=(jax.ShapeDtypeStruct((t, D_MODEL), F32), jax.ShapeDtypeStruct((t, LANES), F32)),
        grid=(t // tm,),
        in_specs=[row(o_mla.shape[1]), row(o_dil.shape[1]), row(o_nsa.shape[1]),
                  pl.BlockSpec((1,) + w_out_s.shape[1:], lambda i: (layer, 0, 0)),
                  row(D_MODEL), full((1, D_MODEL)), full((1, D_MODEL)), full(wr3.shape)],
        out_specs=(row(D_MODEL), row(LANES)),
        compiler_params=_cparams(("arbitrary",)),
        name="outproj",
    )(o_mla, o_dil, o_nsa, w_out_s, x, g, b, wr3)


def _dispatch_kernel(fill_ref, pos_ref, x1_ref, xb_hbm, stage, zbuf, sem, zsem):
    i = pl.program_id(0)
    n = pl.num_programs(0)
    tm = ROW_TM
    slot = i % 2

    def drain(s):
        for _ in range(MOE_TOP_K):
            pltpu.make_async_copy(stage.at[s], xb_hbm.at[pl.ds(0, tm)], sem.at[s]).wait()

    def zero_copy(e):
        start = pl.multiple_of(fill_ref[e], MOE_ROWS)
        return pltpu.make_async_copy(zbuf, xb_hbm.at[pl.ds(start, MOE_ROWS)], zsem)

    @pl.when(i == 0)
    def _():
        zbuf[...] = jnp.zeros_like(zbuf)
        for e in range(N_EXPERTS):
            @pl.when(fill_ref[e] >= 0)
            def _():
                zero_copy(e).start()
        for e in range(N_EXPERTS):
            @pl.when(fill_ref[e] >= 0)
            def _():
                zero_copy(e).wait()

        def tail_copy(b):
            start = pl.multiple_of(b * MOE_ROWS, MOE_ROWS)
            return pltpu.make_async_copy(zbuf, xb_hbm.at[pl.ds(start, MOE_ROWS)], zsem)

        n_blocks = xb_hbm.shape[0] // MOE_ROWS
        lax.fori_loop(fill_ref[N_EXPERTS], n_blocks, lambda b, c: (tail_copy(b).start(), c)[1], 0)
        lax.fori_loop(fill_ref[N_EXPERTS], n_blocks, lambda b, c: (tail_copy(b).wait(), c)[1], 0)

    @pl.when(i >= 2)
    def _():
        drain(slot)

    stage[slot] = x1_ref[...]

    def body(g, carry):
        for u in range(DMA_UNROLL):
            r = g * DMA_UNROLL + u
            for k in range(MOE_TOP_K):
                p = pos_ref[0, 0, MOE_TOP_K * r + k]
                pltpu.make_async_copy(stage.at[slot, pl.ds(r, 1)], xb_hbm.at[pl.ds(p, 1)], sem.at[slot]).start()
        return carry

    lax.fori_loop(0, tm // DMA_UNROLL, body, 0)

    @pl.when(i == n - 1)
    def _():
        drain(1 - slot)
        drain(slot)


def _dispatch(fill, pos3, x1, n_rows):
    t = x1.shape[0]
    tm = ROW_TM
    grid_spec = pltpu.PrefetchScalarGridSpec(
        num_scalar_prefetch=1,
        grid=(t // tm,),
        in_specs=[pl.BlockSpec((1, 1, MOE_TOP_K * tm), lambda i, fl: (i, 0, 0), memory_space=pltpu.SMEM),
                  pl.BlockSpec((tm, D_MODEL), lambda i, fl: (i, 0))],
        out_specs=pl.BlockSpec(memory_space=pl.ANY),
        scratch_shapes=[pltpu.VMEM((2, tm, D_MODEL), F32), pltpu.VMEM((MOE_ROWS, D_MODEL), F32),
                        pltpu.SemaphoreType.DMA((2,)), pltpu.SemaphoreType.DMA(())],
    )
    return pl.pallas_call(
        _dispatch_kernel,
        out_shape=jax.ShapeDtypeStruct((n_rows, D_MODEL), F32),
        grid_spec=grid_spec,
        compiler_params=_cparams(("arbitrary",)),
        name="dispatch",
    )(fill, pos3, x1)


def _experts_kernel(blk_ex_ref, first_ref, nxt_ref, nused_ref, x_ref, wg_hbm, wu_hbm, wd_hbm, y_ref,
                    wg_f, wu_f, wd_f, wg_b, wu_b, wd_b, sem, *, layer):
    b = pl.program_id(0)
    nused = nused_ref[0]

    def weight_copies(e, s):
        return (pltpu.make_async_copy(wg_hbm.at[layer, e], wg_f.at[s], sem.at[s]),
                pltpu.make_async_copy(wu_hbm.at[layer, e], wu_f.at[s], sem.at[s]),
                pltpu.make_async_copy(wd_hbm.at[layer, e], wd_f.at[s], sem.at[s]))

    @pl.when(b == 0)
    def _():
        for cp in weight_copies(blk_ex_ref[0], 0):
            cp.start()

    @pl.when(b < nused)
    def _():
        first = first_ref[b]

        @pl.when(first > 0)
        def _():
            s = first - 1
            for cp in weight_copies(blk_ex_ref[b], s):
                cp.wait()

            @pl.when(nxt_ref[b] >= 0)
            def _():
                for cp in weight_copies(nxt_ref[b], 1 - s):
                    cp.start()

            wg_b[...] = wg_f[s].astype(BF16)
            wu_b[...] = wu_f[s].astype(BF16)
            wd_b[...] = wd_f[s].astype(BF16)

        rows = x_ref[...].astype(BF16)
        gate = jnp.dot(rows, wg_b[...], preferred_element_type=F32)
        up = jnp.dot(rows, wu_b[...], preferred_element_type=F32)
        hid = (gate * jax.nn.sigmoid(gate) * up).astype(BF16)
        y_ref[...] = jnp.dot(hid, wd_b[...], preferred_element_type=F32)

    @pl.when(b >= nused)
    def _():
        y_ref[...] = jnp.zeros_like(y_ref)


def _experts(blk_ex, first, nxt, nused, xb_rows, w_gate, w_up, w_down, layer):
    n_blocks = xb_rows.shape[0] // MOE_ROWS
    hbm = pl.BlockSpec(memory_space=pl.ANY)
    grid_spec = pltpu.PrefetchScalarGridSpec(
        num_scalar_prefetch=4,
        grid=(n_blocks,),
        in_specs=[pl.BlockSpec((MOE_ROWS, D_MODEL), lambda b, ex, fi, nx, nu: (jnp.minimum(b, nu[0] - 1), 0)),
                  hbm, hbm, hbm],
        out_specs=pl.BlockSpec((MOE_ROWS, D_MODEL), lambda b, ex, fi, nx, nu: (b, 0)),
        scratch_shapes=[pltpu.VMEM((2, D_MODEL, EXPERT_FF), F32), pltpu.VMEM((2, D_MODEL, EXPERT_FF), F32),
                        pltpu.VMEM((2, EXPERT_FF, D_MODEL), F32),
                        pltpu.VMEM((D_MODEL, EXPERT_FF), BF16), pltpu.VMEM((D_MODEL, EXPERT_FF), BF16),
                        pltpu.VMEM((EXPERT_FF, D_MODEL), BF16), pltpu.SemaphoreType.DMA((2,))],
    )
    return pl.pallas_call(
        functools.partial(_experts_kernel, layer=layer),
        out_shape=jax.ShapeDtypeStruct((n_blocks * MOE_ROWS, D_MODEL), F32),
        grid_spec=grid_spec,
        compiler_params=_cparams(("arbitrary",)),
        name="experts",
    )(blk_ex, first, nxt, nused, xb_rows, w_gate, w_up, w_down)


def _combine_kernel(pos_cur_ref, pos_nxt_ref, y_hbm, x1_ref, route_ref, g_ref, b_ref, o_ref, ob_ref, ybuf, sem):
    i = pl.program_id(0)
    n = pl.num_programs(0)
    tm = ROW_TM

    def issue(pos_ref, slot):
        def body(g, carry):
            for u in range(DMA_UNROLL):
                r = g * DMA_UNROLL + u
                for k in range(MOE_TOP_K):
                    p = pos_ref[0, 0, MOE_TOP_K * r + k]
                    pltpu.make_async_copy(y_hbm.at[pl.ds(p, 1)], ybuf.at[slot, k, pl.ds(r, 1)],
                                          sem.at[slot]).start()
            return carry

        lax.fori_loop(0, tm // DMA_UNROLL, body, 0)

    @pl.when(i == 0)
    def _():
        issue(pos_cur_ref, 0)

    @pl.when(i + 1 < n)
    def _():
        issue(pos_nxt_ref, (i + 1) % 2)

    slot = i % 2
    for k in range(MOE_TOP_K):
        pltpu.make_async_copy(y_hbm.at[pl.ds(0, tm)], ybuf.at[slot, k], sem.at[slot]).wait()
    route = route_ref[...]
    ffn = route[:, 2:3] * ybuf[slot, 0] + route[:, 3:4] * ybuf[slot, 1]
    x2 = _layer_norm(DN_ALPHA * x1_ref[...] + ffn, g_ref[...], b_ref[...])
    o_ref[...] = x2
    ob_ref[...] = x2.astype(ob_ref.dtype)


def _combine(pos3, yb, x1, route, g, b):
    t = x1.shape[0]
    tm = ROW_TM
    nt = t // tm
    pos_spec = lambda shift: pl.BlockSpec((1, 1, MOE_TOP_K * tm), lambda i: (jnp.minimum(i + shift, nt - 1), 0, 0),
                                          memory_space=pltpu.SMEM)
    row = lambda w: pl.BlockSpec((tm, w), lambda i: (i, 0))
    full = lambda shape: pl.BlockSpec(shape, lambda i: (0,) * len(shape))
    return pl.pallas_call(
        _combine_kernel,
        out_shape=(jax.ShapeDtypeStruct((t, D_MODEL), F32), jax.ShapeDtypeStruct((t, D_MODEL), BF16)),
        grid=(nt,),
        in_specs=[pos_spec(0), pos_spec(1), pl.BlockSpec(memory_space=pl.ANY), row(D_MODEL), row(LANES),
                  full((1, D_MODEL)), full((1, D_MODEL))],
        out_specs=(row(D_MODEL), row(D_MODEL)),
        scratch_shapes=[pltpu.VMEM((2, MOE_TOP_K, tm, D_MODEL), F32), pltpu.SemaphoreType.DMA((2,))],
        compiler_params=_cparams(("arbitrary",)),
        name="combine",
    )(pos3, pos3, yb, x1, route, g, b)


def _dispatch_plan(route, t):
    m = t * MOE_TOP_K
    n_blocks = -(-m // MOE_ROWS) + N_EXPERTS
    chunk = MOE_ROWS
    ex = route[:, 0:MOE_TOP_K].reshape(m // chunk, chunk, 1)
    onehot = (ex == jnp.arange(N_EXPERTS, dtype=F32)[None, None, :]).astype(F32)
    tri = jnp.asarray(np.tril(np.ones((chunk, chunk), np.float32)))
    within = jnp.einsum('ij,cje->cie', tri, onehot)
    totals = jnp.sum(onehot, axis=1)
    before = jnp.cumsum(totals, axis=0) - totals
    counts = jnp.sum(totals, axis=0)
    padded = jnp.ceil(counts / MOE_ROWS) * MOE_ROWS
    pad_end = jnp.cumsum(padded)
    pad_start = pad_end - padded
    row = jnp.sum(onehot * (within - 1.0 + before[:, None, :] + pad_start[None, None, :]), axis=-1)
    pos = row.astype(jnp.int32).reshape(m)
    blk_start = jnp.arange(n_blocks, dtype=F32) * MOE_ROWS
    blk_ex = jnp.minimum(jnp.sum((pad_end[None, :] <= blk_start[:, None]).astype(jnp.int32), axis=1),
                         N_EXPERTS - 1).astype(jnp.int32)
    nused = (pad_end[-1] / MOE_ROWS).astype(jnp.int32).reshape(1)
    fill = jnp.concatenate([jnp.where(padded > 0, pad_end - MOE_ROWS, -1.0).astype(jnp.int32), nused])
    nonempty = padded > 0
    slot_e = (jnp.cumsum(nonempty.astype(jnp.int32)) - 1) % 2
    expert_ids = jnp.arange(N_EXPERTS, dtype=jnp.int32)
    later = lax.cummin(jnp.where(nonempty, expert_ids, N_EXPERTS), axis=0, reverse=True)
    nxt_e = jnp.concatenate([later[1:], jnp.full((1,), N_EXPERTS, jnp.int32)])
    nxt_e = jnp.where(nxt_e >= N_EXPERTS, -1, nxt_e)
    is_first = (blk_start == pad_start[blk_ex]) & (jnp.arange(n_blocks) < nused[0])
    first = jnp.where(is_first, 1 + slot_e[blk_ex], 0).astype(jnp.int32)
    return blk_ex, first, nxt_e[blk_ex].astype(jnp.int32), nused, fill, pos


W_HALF = LANES // 2


def _shift_kernel(a_ref, b_ref, tail_ref, o_ref):
    m = pl.program_id(0)
    n_src = IN_COLS // LANES
    nl = o_ref.shape[0]
    d = o_ref.shape[2]
    nc = d // LANES
    a2, b2, t2 = a_ref, b_ref, tail_ref

    def slab(ref4, col0, c, l):
        return ref4[pl.ds(col0, W_HALF), c, l, :].astype(o_ref.dtype)

    def copy(ref_top, top0, ref_bot, bot0):
        for l in range(nl):
            for c in range(nc):
                o_ref[l, 0:W_HALF, c * LANES:(c + 1) * LANES] = slab(ref_top, top0, c, l)
                if ref_bot is None:
                    o_ref[l, W_HALF:LANES, c * LANES:(c + 1) * LANES] = jnp.zeros((W_HALF, LANES), o_ref.dtype)
                else:
                    o_ref[l, W_HALF:LANES, c * LANES:(c + 1) * LANES] = slab(ref_bot, bot0, c, l)

    @pl.when(m < n_src)
    def _():
        copy(a2, 0, b2, 0)

    @pl.when(m == n_src)
    def _():
        copy(a2, 0, t2, 0)

    @pl.when(m == n_src + 1)
    def _():
        copy(t2, W_HALF, None, 0)


def _shift_w_in(w_in):
    nl, d, _ = w_in.shape
    n_src = IN_COLS // LANES
    n_keep = (IN_SHIFT_FROM + MLA_ROPE_DIM) // LANES
    last = n_src * (LANES // W_HALF) - 1
    nc = d // LANES
    view = lambda w: jnp.transpose(w.reshape(nl, nc, LANES, w.shape[-1]), (3, 1, 0, 2))
    w_t = view(w_in)
    tail_t = view(jnp.pad(w_in[:, :, n_src * LANES:], ((0, 0), (0, 0), (0, (n_src + 1) * LANES - IN_COLS))))
    first = lambda m: jnp.where(m < n_keep, 2 * m, 2 * m - 1)
    return pl.pallas_call(
        _shift_kernel,
        out_shape=jax.ShapeDtypeStruct((nl, IN_COLS_PAD, d), BF16),
        grid=(N_IN_BLOCKS,),
        in_specs=[pl.BlockSpec((W_HALF, nc, nl, LANES), lambda m: (jnp.minimum(first(m), last), 0, 0, 0)),
                  pl.BlockSpec((W_HALF, nc, nl, LANES), lambda m: (jnp.minimum(first(m) + 1, last), 0, 0, 0)),
                  pl.BlockSpec((LANES, nc, nl, LANES), lambda m: (0, 0, 0, 0))],
        out_specs=pl.BlockSpec((nl, LANES, d), lambda m: (0, m, 0)),
        compiler_params=_cparams(("arbitrary",)),
        name="shiftw",
    )(w_t, w_t, tail_t)


def _permute_w_q_up(w_q_up):
    w = w_q_up.reshape(MLA_Q_RANK, MLA_HEADS, MLA_NOPE_DIM + MLA_ROPE_DIM)
    zero = jnp.zeros((MLA_Q_RANK, MLA_HEADS, MLA_QK_PAD - MLA_NOPE_DIM - MLA_ROPE_DIM), w.dtype)
    w = jnp.concatenate([w[:, :, :MLA_NOPE_DIM], zero, w[:, :, MLA_NOPE_DIM:]], axis=-1)
    return w.reshape(MLA_Q_RANK, MLA_HEADS * MLA_QK_PAD).astype(BF16)


def _split_w_kv_up(w_kv_up):
    w = w_kv_up.reshape(MLA_KV_RANK, MLA_HEADS, MLA_NOPE_DIM + HEAD_DIM)
    wkn = w[:, :, :MLA_NOPE_DIM].reshape(MLA_KV_RANK, MLA_HEADS * MLA_NOPE_DIM)
    wv = w[:, :, MLA_NOPE_DIM:].reshape(MLA_KV_RANK, MLA_HEADS * HEAD_DIM)
    return wkn.astype(BF16), wv.astype(BF16)


def _router_slices(w_grp, w_exp):
    w = jnp.concatenate([w_grp, w_exp, jnp.zeros((D_MODEL, LANES - N_GROUPS - N_EXPERTS), F32)], axis=1)
    hi = w.astype(BF16)
    mid = (w - hi.astype(F32)).astype(BF16)
    return jnp.concatenate([hi, mid], axis=1)


def _mixer(xb, nb, layer, w_in_s, q_lat_norm, w_q_up, kv_lat_norm, w_kv_up,
           cmp_pos_k, cmp_w1_k, cmp_w2_k, cmp_pos_v, cmp_w1_v, cmp_w2_v, consts):
    hproj = _inproj(xb, w_in_s, layer, consts["rope32"])
    wkn, wv = _split_w_kv_up(w_kv_up)
    q_mla, k_mla, v_mla = _mlaprep(hproj, q_lat_norm.reshape(1, -1), kv_lat_norm.reshape(1, -1),
                                   _permute_w_q_up(w_q_up), wkn, wv, consts["rope64"])
    wide = NSA_CMP_STRIDE * HEAD_DIM
    tk2 = hproj[:, BLK_NKC * LANES:(BLK_NKC + 1) * LANES].reshape(nb, N_CMP_PAD, wide)
    tv2 = hproj[:, BLK_NVC * LANES:(BLK_NVC + 1) * LANES].reshape(nb, N_CMP_PAD, wide)
    kc, vc = _compress(tk2, tv2, cmp_pos_k.reshape(2, wide), cmp_pos_v.reshape(2, wide),
                       cmp_w1_k.astype(BF16), cmp_w2_k.astype(BF16), cmp_w1_v.astype(BF16), cmp_w2_v.astype(BF16),
                       consts["rope_cmp"])
    o_mla = _flash(q_mla, k_mla, v_mla, consts["bias_causal"], nb=nb, heads=MLA_HEADS, dqk=MLA_QK_PAD,
                   dv=HEAD_DIM, q_blk0=0, k_blk0=0, v_blk0=0)
    o_dil = _flash(hproj, hproj, hproj, consts["bias_dil"], nb=nb, heads=DIL_HEADS, dqk=HEAD_DIM, dv=HEAD_DIM,
                   q_blk0=BLK_DQ, k_blk0=BLK_DK, v_blk0=BLK_DV)
    o_nsa = _nsa(hproj, kc, vc, consts["cover"], consts["expand"], consts["bias_win"], nb)
    return o_mla, o_dil, o_nsa


def _make_consts():
    pos = np.arange(SEQ)
    slot = np.arange(N_CMP_PAD)
    cmp_start = slot * NSA_CMP_STRIDE
    cmp_end = cmp_start + NSA_CMP_LEN - 1
    sel_start = np.arange(N_SEL_BLOCKS) * NSA_SEL_BLOCK
    cover = ((cmp_start[:, None] < sel_start[None, :] + NSA_SEL_BLOCK)
             & (cmp_start[:, None] + NSA_CMP_LEN > sel_start[None, :])
             & (slot[:, None] < N_CMP_PAD - 1)).astype(np.float32)
    expand = (np.arange(LANES)[:, None] == (pos // NSA_SEL_BLOCK)[None, :]).astype(np.float32)
    expand = expand.reshape(LANES, SEQ // NSA_T, NSA_T).transpose(1, 0, 2)
    n_win = -(-NSA_WINDOW // NSA_T) + 1
    return {
        "rope32": _rope_tables(pos, ROT_DIM, True),
        "rope64": _rope_tables(pos, MLA_ROPE_DIM, False, KROPE_LANE0),
        "rope_cmp": _rope_tables(cmp_end, ROT_DIM, True),
        "bias_causal": _distance_bias(_causal_mult, (0, 1), FLASH_T),
        "bias_dil": _distance_bias(_dilated_mult, range(SEQ // FLASH_T), FLASH_T),
        "bias_win": _distance_bias(_window_mult, range(n_win), NSA_T),
        "cover": jnp.asarray(cover.T, BF16),
        "expand": jnp.asarray(expand, BF16),
    }


def kernel(x, w_in, q_lat_norm, w_q_up, kv_lat_norm, w_kv_up, cmp_pos_k, cmp_w1_k, cmp_w2_k, cmp_pos_v, cmp_w1_v,
           cmp_w2_v, w_out, ln1_g, ln1_b, w_grp, w_exp, w_gate, w_up, w_down, ln2_g, ln2_b):
    nb, s, d = x.shape
    assert s == SEQ and d == D_MODEL
    t = nb * s
    consts = _make_consts()
    w_in_s = _shift_w_in(w_in)
    w_out_s = w_out.astype(BF16)
    xf = x.reshape(t, d)
    xb = xf.astype(BF16)
    for l in range(DEPTH):
        o_mla, o_dil, o_nsa = _mixer(xb, nb, l, w_in_s, q_lat_norm[l], w_q_up[l], kv_lat_norm[l], w_kv_up[l],
                                     cmp_pos_k[l], cmp_w1_k[l], cmp_w2_k[l], cmp_pos_v[l], cmp_w1_v[l],
                                     cmp_w2_v[l], consts)
        x1, route = _outproj(o_mla, o_dil, o_nsa, w_out_s, l, xf, ln1_g[l].reshape(1, d),
                             ln1_b[l].reshape(1, d), _router_slices(w_grp[l], w_exp[l]))
        blk_ex, first, nxt, nused, fill, pos = _dispatch_plan(route, t)
        pos3 = pos.reshape(t // ROW_TM, 1, MOE_TOP_K * ROW_TM)
        n_rows = (-(-t * MOE_TOP_K // MOE_ROWS) + N_EXPERTS) * MOE_ROWS
        yb = _experts(blk_ex, first, nxt, nused, _dispatch(fill, pos3, x1, n_rows), w_gate, w_up, w_down, l)
        xf, xb = _combine(pos3, yb, x1, route,
                          ln2_g[l].reshape(1, d), ln2_b[l].reshape(1, d))
    return xf.reshape(nb, s, d)
```

```python
import functools

import numpy as np
import jax
import jax.numpy as jnp
from jax import lax
from jax.experimental import pallas as pl
from jax.experimental.pallas import tpu as pltpu

F32 = jnp.float32
BF16 = jnp.bfloat16

D_MODEL = 2048
SEQ = 2048
DEPTH = 2
HEAD_DIM = 128
LANES = 128
MLA_HEADS = 4
DIL_HEADS = 6
NSA_HEADS = 6
MLA_Q_RANK = 384
MLA_KV_RANK = 128
MLA_NOPE_DIM = 128
MLA_ROPE_DIM = 64
MLA_QK_PAD = 256
DIL_PAIRS = ((128, 1), (512, 4), (2048, 16))
NSA_CMP_LEN = 32
NSA_CMP_STRIDE = 16
NSA_CMP_HIDDEN = 256
NSA_SEL_BLOCK = 64
NSA_TOP_N = 16
NSA_WINDOW = 512
NSA_BRANCHES = 3
NSA_FORCE_SCORE = 1e4
ROPE_THETA = 500000.0
ROT_DIM = HEAD_DIM // 4
N_GROUPS = 4
EXPERTS_PER_GROUP = 8
N_EXPERTS = N_GROUPS * EXPERTS_PER_GROUP
EXPERT_FF = 512
MOE_TOP_K = 2
MOE_ROWS = 256
DN_ALPHA = (2 * DEPTH) ** 0.25
LN_EPS = 1e-5
RMS_EPS = 1e-6
NEG_INF = -1e30

N_CMP_PAD = SEQ // NSA_CMP_STRIDE
N_SEL_BLOCKS = SEQ // NSA_SEL_BLOCK

BLK_QLAT = 0
BLK_KVLAT = 3
BLK_KROPE = 4
BLK_DQ = 5
BLK_DK = 11
BLK_DV = 17
BLK_NQ = 23
BLK_NKC = 29
BLK_NVC = 30
BLK_NKS = 31
BLK_NVS = 32
BLK_NKW = 33
BLK_NVW = 34
BLK_GATE = 35
N_IN_BLOCKS = 36
IN_COLS = 4434
IN_SHIFT_FROM = MLA_Q_RANK + MLA_KV_RANK - MLA_ROPE_DIM
IN_COLS_PAD = N_IN_BLOCKS * LANES
KROPE_LANE0 = LANES - MLA_ROPE_DIM

INPROJ_TM = 2048
INPROJ_SUB = 512
INPROJ_TN = 512
FLASH_T = 512
NSA_T = 256
ROW_TM = 256
OUTPROJ_TM = 512
DMA_UNROLL = 8

VMEM_LIMIT = 56 * 1024 * 1024


def _cparams(sem):
    return pltpu.CompilerParams(dimension_semantics=sem, vmem_limit_bytes=VMEM_LIMIT)


def _rope_tables(pos, rot_dim, keep_rest, lane0=0):
    half = rot_dim // 2
    inv_freq = np.power(np.float32(ROPE_THETA), -np.arange(half, dtype=np.float32) / np.float32(half))
    ang = (pos.astype(np.float32)[:, None] * inv_freq[None, :]).astype(np.float32)
    cos, sin = np.cos(ang).astype(np.float32), np.sin(ang).astype(np.float32)
    n = pos.shape[0]
    c = np.full((n, LANES), 1.0 if keep_rest else 0.0, np.float32)
    s1 = np.zeros((n, LANES), np.float32)
    s2 = np.zeros((n, LANES), np.float32)
    c[:, lane0:lane0 + half] = cos
    c[:, lane0 + half:lane0 + rot_dim] = cos
    s1[:, lane0 + half:lane0 + rot_dim] = sin
    s2[:, lane0:lane0 + half] = -sin
    return jnp.asarray(c), jnp.asarray(s1), jnp.asarray(s2)


def _rope_lanes(a, c, s1, s2, half):
    return a * c + pltpu.roll(a, half, 1) * s1 + pltpu.roll(a, LANES - half, 1) * s2


def _distance_bias(mult_fn, diags, t):
    d = np.asarray(diags, np.int64)[:, None, None] * t
    r = np.arange(t, dtype=np.int64)[None, :, None]
    c = np.arange(t, dtype=np.int64)[None, None, :]
    mult = mult_fn(d + r - c)
    return jnp.asarray(np.where(mult > 0, np.log(np.maximum(mult, 1).astype(np.float32)), NEG_INF).astype(np.float32))


def _causal_mult(dist):
    return (dist >= 0).astype(np.int64)


def _dilated_mult(dist):
    m = np.zeros_like(dist)
    for window, dil in DIL_PAIRS:
        m = m + ((dist >= 0) & (dist % dil == 0) & (dist <= (window // dil) * dil)).astype(np.int64)
    return m


def _window_mult(dist):
    return ((dist >= 0) & (dist <= NSA_WINDOW - 1)).astype(np.int64)


def _in_block(blk, lo, n):
    return jnp.logical_and(blk >= lo, blk < lo + n)


def _inproj_kernel(x_ref, w_ref, c_ref, s1_ref, s2_ref, o_ref, *, scale):
    j = pl.program_id(0)
    for hh in range(INPROJ_TM // INPROJ_SUB):
        rows = slice(hh * INPROJ_SUB, (hh + 1) * INPROJ_SUB)
        acc = _qk(x_ref[rows, :], w_ref[0])
        c, s1, s2 = c_ref[rows, :], s1_ref[rows, :], s2_ref[rows, :]
        for u in range(INPROJ_TN // LANES):
            blk = j * (INPROJ_TN // LANES) + u
            is_q = jnp.logical_or(_in_block(blk, BLK_DQ, DIL_HEADS), _in_block(blk, BLK_NQ, NSA_HEADS))
            is_rope = is_q | _in_block(blk, BLK_DK, DIL_HEADS) | (blk == BLK_NKS) | (blk == BLK_NKW)
            sc = jnp.where(is_q, scale, 1.0).astype(F32)
            a = acc[:, u * LANES:(u + 1) * LANES]
            r = _rope_lanes(a, jnp.where(is_rope, c, 1.0), jnp.where(is_rope, s1, 0.0),
                            jnp.where(is_rope, s2, 0.0), ROT_DIM // 2)
            o_ref[rows, u * LANES:(u + 1) * LANES] = (r * sc).astype(o_ref.dtype)


def _inproj(xb, w_in_s, layer, tabs):
    t = xb.shape[0]
    tm, tn = INPROJ_TM, INPROJ_TN
    nrow = SEQ // tm
    tab_spec = pl.BlockSpec((tm, LANES), lambda j, i: (i % nrow, 0))
    return pl.pallas_call(
        functools.partial(_inproj_kernel, scale=HEAD_DIM ** -0.5),
        out_shape=jax.ShapeDtypeStruct((t, IN_COLS_PAD), BF16),
        grid=(IN_COLS_PAD // tn, t // tm),
        in_specs=[pl.BlockSpec((tm, D_MODEL), lambda j, i: (i, 0)),
                  pl.BlockSpec((1, tn, D_MODEL), lambda j, i: (layer, j, 0)),
                  tab_spec, tab_spec, tab_spec],
        out_specs=pl.BlockSpec((tm, tn), lambda j, i: (i, j)),
        compiler_params=_cparams(("arbitrary", "arbitrary")),
        name="inproj",
    )(xb, w_in_s, *tabs)


def _rms(xf, g):
    return xf * lax.rsqrt(jnp.mean(jnp.square(xf), axis=-1, keepdims=True) + RMS_EPS) * g


def _mlaprep_kernel(ql_ref, kvl_ref, kr_ref, gq_ref, gkv_ref, wq_ref, wkn_ref, wv_ref,
                    c_ref, s1_ref, s2_ref, q_ref, k_ref, v_ref, *, scale):
    c, s1, s2 = c_ref[...], s1_ref[...], s2_ref[...]
    half = MLA_ROPE_DIM // 2
    qn = _rms(ql_ref[...].astype(F32), gq_ref[...]).astype(BF16)
    q = jnp.dot(qn, wq_ref[...], preferred_element_type=F32)
    for h in range(MLA_HEADS):
        lo = h * MLA_QK_PAD
        q_ref[:, lo:lo + LANES] = (q[:, lo:lo + LANES] * scale).astype(q_ref.dtype)
        a = q[:, lo + LANES:lo + 2 * LANES]
        q_ref[:, lo + LANES:lo + 2 * LANES] = (_rope_lanes(a, c, s1, s2, half) * scale).astype(q_ref.dtype)
    kvn = _rms(kvl_ref[...].astype(F32), gkv_ref[...]).astype(BF16)
    kn = jnp.dot(kvn, wkn_ref[...], preferred_element_type=F32)
    v_ref[...] = jnp.dot(kvn, wv_ref[...], preferred_element_type=F32).astype(v_ref.dtype)
    kpe = _rope_lanes(kr_ref[...].astype(F32), c, s1, s2, half).astype(k_ref.dtype)
    for h in range(MLA_HEADS):
        lo = h * MLA_QK_PAD
        k_ref[:, lo:lo + LANES] = kn[:, h * LANES:(h + 1) * LANES].astype(k_ref.dtype)
        k_ref[:, lo + LANES:lo + 2 * LANES] = kpe


def _mlaprep(hproj, gq, gkv, wq_p, wkn, wv, tabs):
    t = hproj.shape[0]
    tm = ROW_TM
    nrow = SEQ // tm
    tab_spec = pl.BlockSpec((tm, LANES), lambda i: (i % nrow, 0))
    full = lambda shape: pl.BlockSpec(shape, lambda i: (0,) * len(shape))
    return pl.pallas_call(
        functools.partial(_mlaprep_kernel, scale=(MLA_NOPE_DIM + MLA_ROPE_DIM) ** -0.5),
        out_shape=(jax.ShapeDtypeStruct((t, MLA_HEADS * MLA_QK_PAD), BF16),
                   jax.ShapeDtypeStruct((t, MLA_HEADS * MLA_QK_PAD), BF16),
                   jax.ShapeDtypeStruct((t, MLA_HEADS * HEAD_DIM), BF16)),
        grid=(t // tm,),
        in_specs=[pl.BlockSpec((tm, MLA_Q_RANK), lambda i: (i, BLK_QLAT * LANES // MLA_Q_RANK)),
                  pl.BlockSpec((tm, LANES), lambda i: (i, BLK_KVLAT)),
                  pl.BlockSpec((tm, LANES), lambda i: (i, BLK_KROPE)),
                  full((1, MLA_Q_RANK)), full((1, MLA_KV_RANK)),
                  full((MLA_Q_RANK, MLA_HEADS * MLA_QK_PAD)),
                  full((MLA_KV_RANK, MLA_HEADS * MLA_NOPE_DIM)),
                  full((MLA_KV_RANK, MLA_HEADS * HEAD_DIM)),
                  tab_spec, tab_spec, tab_spec],
        out_specs=(pl.BlockSpec((tm, MLA_HEADS * MLA_QK_PAD), lambda i: (i, 0)),
                   pl.BlockSpec((tm, MLA_HEADS * MLA_QK_PAD), lambda i: (i, 0)),
                   pl.BlockSpec((tm, MLA_HEADS * HEAD_DIM), lambda i: (i, 0))),
        compiler_params=_cparams(("arbitrary",)),
        name="mlaprep",
    )(hproj, hproj, hproj, gq, gkv, wq_p, wkn, wv, *tabs)


def _gelu_tanh(x):
    return 0.5 * x * (1.0 + jnp.tanh(0.7978845608028654 * (x + 0.044715 * (x * x * x))))


def _compress_kernel(tk_ref, tv_ref, pek_ref, pev_ref, w1k_ref, w2k_ref, w1v_ref, w2v_ref,
                     c_ref, s1_ref, s2_ref, kc_ref, vc_ref):
    half_in = NSA_CMP_STRIDE * HEAD_DIM

    def comp(t_ref, pe_ref, w1_ref, w2_ref):
        t = t_ref[0].astype(F32)
        a = (t + pe_ref[0:1, :]).astype(BF16)
        b = (t + pe_ref[1:2, :]).astype(BF16)
        y0 = jnp.dot(a, w1_ref[0:half_in, :], preferred_element_type=F32)
        y1 = jnp.dot(b, w1_ref[half_in:2 * half_in, :], preferred_element_type=F32)
        hid = _gelu_tanh(y0 + pltpu.roll(y1, N_CMP_PAD - 1, 0))
        return jnp.dot(hid.astype(BF16), w2_ref[...], preferred_element_type=F32)

    kc = comp(tk_ref, pek_ref, w1k_ref, w2k_ref)
    kc_ref[0] = _rope_lanes(kc, c_ref[...], s1_ref[...], s2_ref[...], ROT_DIM // 2).astype(kc_ref.dtype)
    vc_ref[0] = comp(tv_ref, pev_ref, w1v_ref, w2v_ref).astype(vc_ref.dtype)


def _compress(tk2, tv2, pek, pev, w1k, w2k, w1v, w2v, tabs):
    nb = tk2.shape[0]
    wide = NSA_CMP_STRIDE * HEAD_DIM
    full = lambda shape: pl.BlockSpec(shape, lambda b: (0,) * len(shape))
    bspec = pl.BlockSpec((1, N_CMP_PAD, wide), lambda b: (b, 0, 0))
    ospec = pl.BlockSpec((1, N_CMP_PAD, HEAD_DIM), lambda b: (b, 0, 0))
    return pl.pallas_call(
        _compress_kernel,
        out_shape=(jax.ShapeDtypeStruct((nb, N_CMP_PAD, HEAD_DIM), BF16),) * 2,
        grid=(nb,),
        in_specs=[bspec, bspec, full((2, wide)), full((2, wide)),
                  full((2 * wide, NSA_CMP_HIDDEN)), full((NSA_CMP_HIDDEN, HEAD_DIM)),
                  full((2 * wide, NSA_CMP_HIDDEN)), full((NSA_CMP_HIDDEN, HEAD_DIM)),
                  full((N_CMP_PAD, LANES)), full((N_CMP_PAD, LANES)), full((N_CMP_PAD, LANES))],
        out_specs=(ospec, ospec),
        compiler_params=_cparams(("arbitrary",)),
        name="compress",
    )(tk2, tv2, pek, pev, w1k, w2k, w1v, w2v, *tabs)


def _qk(q, k):
    return lax.dot_general(q, k, (((1,), (1,)), ((), ())), preferred_element_type=F32)


def _lane_fold(x, op):
    out = x[:, 0:LANES]
    for u in range(1, x.shape[1] // LANES):
        out = op(out, x[:, u * LANES:(u + 1) * LANES])
    return out


def _split3(x):
    hi = x.astype(BF16)
    r1 = x - hi.astype(F32)
    mid = r1.astype(BF16)
    lo = (r1 - mid.astype(F32)).astype(BF16)
    return hi, mid, lo


def _attend(q, k_ref, v_ref, bias_fn, j_lo, n_tiles, s_scr, *, tk, groups=1):
    rows = q.shape[0]
    dv = v_ref.shape[-1]

    def pass1(t, mx):
        off = pl.multiple_of((j_lo + t) * tk, tk)
        s = _qk(q, k_ref[pl.ds(off, tk), :])
        bias = bias_fn(j_lo + t)
        if groups > 1:
            s = (s.reshape(groups, rows // groups, tk) + bias[None]).reshape(rows, tk)
        else:
            s = s + bias
        s_scr[t] = s
        return jnp.maximum(mx, _lane_fold(s, jnp.maximum))

    mx = lax.fori_loop(0, n_tiles, pass1, jnp.full((rows, LANES), NEG_INF, F32))
    m = jnp.broadcast_to(jnp.max(mx, axis=-1, keepdims=True), (rows, LANES))

    def pass2(t, carry):
        lsum, acc = carry
        off = pl.multiple_of((j_lo + t) * tk, tk)
        s = s_scr[t]
        parts = [jnp.exp(s[:, u * LANES:(u + 1) * LANES] - m) for u in range(tk // LANES)]
        for p in parts:
            lsum = lsum + p
        pb = jnp.concatenate([p.astype(BF16) for p in parts], axis=1)
        acc = acc + jnp.dot(pb, v_ref[pl.ds(off, tk), :], preferred_element_type=F32)
        return lsum, acc

    lsum, acc = lax.fori_loop(0, n_tiles, pass2, (jnp.zeros((rows, LANES), F32), jnp.zeros((rows, dv), F32)))
    return acc / jnp.sum(lsum, axis=-1, keepdims=True)


def _flash_kernel(q_ref, k_ref, v_ref, bias_ref, o_ref, s_scr):
    i = pl.program_id(2)
    last = bias_ref.shape[0] - 1
    out = _attend(q_ref[...], k_ref, v_ref, lambda j: bias_ref[jnp.minimum(i - j, last)], 0, i + 1, s_scr,
                  tk=FLASH_T)
    o_ref[...] = out.astype(o_ref.dtype)


def _flash(q_arr, k_arr, v_arr, bias, *, nb, heads, dqk, dv, q_blk0, k_blk0, v_blk0):
    t = FLASH_T
    nq = SEQ // t
    return pl.pallas_call(
        _flash_kernel,
        out_shape=jax.ShapeDtypeStruct((nb * SEQ, heads * dv), BF16),
        grid=(nb, heads, nq),
        in_specs=[pl.BlockSpec((t, dqk), lambda b, h, i: (b * nq + i, q_blk0 + h)),
                  pl.BlockSpec((SEQ, dqk), lambda b, h, i: (b, k_blk0 + h)),
                  pl.BlockSpec((SEQ, dv), lambda b, h, i: (b, v_blk0 + h)),
                  pl.BlockSpec(bias.shape, lambda b, h, i: (0, 0, 0))],
        out_specs=pl.BlockSpec((t, dv), lambda b, h, i: (b * nq + i, h)),
        scratch_shapes=[pltpu.VMEM((nq, t, t), F32)],
        compiler_params=_cparams(("arbitrary", "arbitrary", "arbitrary")),
        name="flash",
    )(q_arr, k_arr, v_arr, bias)


def _nsa_kernel(q0_ref, q1_ref, q2_ref, q3_ref, q4_ref, q5_ref, kc_ref, vc_ref, ks_ref, vs_ref, kw_ref, vw_ref,
                gate_ref, cover_ref, expand_ref, wbias_ref, o_ref, bias_scr, s_scr):
    i = pl.program_id(1)
    tq = NSA_T
    nh = NSA_HEADS
    rows = nh * tq
    pos = i * tq + lax.broadcasted_iota(jnp.int32, (tq, 1), 0)
    q = jnp.concatenate([r[...] for r in (q0_ref, q1_ref, q2_ref, q3_ref, q4_ref, q5_ref)], axis=0)

    cidx = lax.broadcasted_iota(jnp.int32, (tq, N_CMP_PAD), 1)
    valid_c = (cidx * NSA_CMP_STRIDE + (NSA_CMP_LEN - 1) <= pos) & (cidx < N_CMP_PAD - 1)
    valid_cf = valid_c.astype(F32)
    cbias = jnp.where(valid_c, 0.0, NEG_INF).astype(F32)
    s = (_qk(q, kc_ref[0]).reshape(nh, tq, N_CMP_PAD) + cbias[None]).reshape(rows, N_CMP_PAD)
    p = (jnp.exp(s - jnp.max(s, axis=-1, keepdims=True)).reshape(nh, tq, N_CMP_PAD) * valid_cf[None])
    p = p.reshape(rows, N_CMP_PAD)
    l = jnp.sum(p, axis=-1, keepdims=True)
    p = p / jnp.where(l > 0.0, l, 1.0)
    o_cmp = jnp.dot(p.astype(BF16), vc_ref[0], preferred_element_type=F32)
    psum = jnp.sum(p.reshape(nh, tq, N_CMP_PAD), axis=0)

    cover_t = cover_ref[...]
    imp = sum(jnp.dot(cover_t, part, preferred_element_type=F32) for part in _split3(psum.T))

    pos_t = i * tq + lax.broadcasted_iota(jnp.int32, (N_SEL_BLOCKS, tq), 1)
    jidx = lax.broadcasted_iota(jnp.int32, (N_SEL_BLOCKS, tq), 0)
    qblk = pos_t // NSA_SEL_BLOCK
    valid_s = jidx * NSA_SEL_BLOCK <= pos_t
    forced = (jidx == 0) | (jidx == qblk) | (jidx == qblk - 1)
    score = jnp.where(valid_s, jnp.where(forced, NSA_FORCE_SCORE, imp), -1.0)
    rank = jnp.zeros((N_SEL_BLOCKS, tq), jnp.int32)
    for ii in range(N_SEL_BLOCKS):
        si = score[ii:ii + 1, :]
        rank = rank + ((si > score) | ((si == score) & (ii < jidx))).astype(jnp.int32)
    sel_t = (rank < NSA_TOP_N).astype(F32)
    sel = jnp.concatenate([sel_t, jnp.zeros((LANES - N_SEL_BLOCKS, tq), F32)], axis=0).T.astype(BF16)

    def fill(j, carry):
        hit = jnp.dot(sel, expand_ref[j], preferred_element_type=F32)
        kpos = j * tq + lax.broadcasted_iota(jnp.int32, (tq, tq), 1)
        bias_scr[j] = jnp.where((hit > 0.5) & (kpos <= pos), 0.0, NEG_INF).astype(F32)
        return carry

    lax.fori_loop(0, i + 1, fill, 0)

    o_slc = _attend(q, ks_ref, vs_ref, lambda j: bias_scr[j], 0, i + 1, s_scr, tk=tq, groups=nh)
    w_lo = jnp.maximum(i - (wbias_ref.shape[0] - 1), 0)
    o_win = _attend(q, kw_ref, vw_ref, lambda j: wbias_ref[i - j], w_lo, i + 1 - w_lo, s_scr, tk=tq, groups=nh)

    gates = jax.nn.sigmoid(gate_ref[...].astype(F32))
    for h in range(nh):
        sl = slice(h * tq, (h + 1) * tq)
        g0 = NSA_BRANCHES * h
        out = (gates[:, g0:g0 + 1] * o_cmp[sl] + gates[:, g0 + 1:g0 + 2] * o_slc[sl]
               + gates[:, g0 + 2:g0 + 3] * o_win[sl])
        o_ref[:, h * LANES:(h + 1) * LANES] = out.astype(o_ref.dtype)


def _nsa(hproj, kc, vc, cover, expand, wbias, nb):
    t = NSA_T
    nq = SEQ // t
    wide = NSA_HEADS * HEAD_DIM
    q_spec = lambda h: pl.BlockSpec((t, LANES), lambda b, i: (b * nq + i, BLK_NQ + h))
    kv_spec = lambda blk: pl.BlockSpec((SEQ, LANES), lambda b, i: (b, blk))
    cspec = pl.BlockSpec((1, N_CMP_PAD, HEAD_DIM), lambda b, i: (b, 0, 0))
    full = lambda shape: pl.BlockSpec(shape, lambda b, i: (0,) * len(shape))
    return pl.pallas_call(
        _nsa_kernel,
        out_shape=jax.ShapeDtypeStruct((nb * SEQ, wide), BF16),
        grid=(nb, nq),
        in_specs=[q_spec(h) for h in range(NSA_HEADS)]
        + [cspec, cspec, kv_spec(BLK_NKS), kv_spec(BLK_NVS), kv_spec(BLK_NKW), kv_spec(BLK_NVW),
           pl.BlockSpec((t, LANES), lambda b, i: (b * nq + i, BLK_GATE)),
           full(cover.shape), full(expand.shape), full(wbias.shape)],
        out_specs=pl.BlockSpec((t, wide), lambda b, i: (b * nq + i, 0)),
        scratch_shapes=[pltpu.VMEM((nq, t, t), F32), pltpu.VMEM((nq, NSA_HEADS * t, t), F32)],
        compiler_params=_cparams(("arbitrary", "arbitrary")),
        name="nsa",
    )(*([hproj] * NSA_HEADS), kc, vc, hproj, hproj, hproj, hproj, hproj, cover, expand, wbias)


def _layer_norm(y, g, b):
    mu = jnp.mean(y, axis=-1, keepdims=True)
    var = jnp.mean(jnp.square(y - mu), axis=-1, keepdims=True)
    return (y - mu) * lax.rsqrt(var + LN_EPS) * g + b


def _lane_min(x):
    return jnp.min(x, axis=-1, keepdims=True)


def _lane_max(x):
    return jnp.max(x, axis=-1, keepdims=True)


def _route(logits):
    lane = lax.broadcasted_iota(jnp.int32, logits.shape, 1)
    lane_f = lane.astype(F32)
    far = float(LANES)
    is_grp = lane < N_GROUPS
    lg = jnp.where(is_grp, logits, NEG_INF)
    eg = jnp.where(is_grp, jnp.exp(lg - _lane_max(lg)), 0.0)
    prob = eg / jnp.sum(eg, axis=-1, keepdims=True)
    p_g = _lane_max(prob)
    g_idx = _lane_min(jnp.where(is_grp & (prob == p_g), lane_f, far))
    e_lo = N_GROUPS + EXPERTS_PER_GROUP * g_idx
    in_grp = (lane_f >= e_lo) & (lane_f < e_lo + EXPERTS_PER_GROUP)
    le = jnp.where(in_grp, logits, NEG_INF)
    v1 = _lane_max(le)
    i1 = _lane_min(jnp.where(in_grp & (le == v1), lane_f, far))
    rest = in_grp & (lane_f != i1)
    le2 = jnp.where(rest, logits, NEG_INF)
    v2 = _lane_max(le2)
    i2 = _lane_min(jnp.where(rest & (le2 == v2), lane_f, far))
    e21 = jnp.exp(v2 - v1)
    den = 1.0 + e21
    gate1 = p_g * (1.0 / den)
    gate2 = p_g * (e21 / den)
    out = jnp.where(lane == 0, i1 - N_GROUPS, 0.0)
    out = jnp.where(lane == 1, i2 - N_GROUPS, out)
    out = jnp.where(lane == 2, gate1, out)
    out = jnp.where(lane == 3, gate2, out)
    return out


def _outproj_kernel(om_ref, od_ref, on_ref, w_ref, x_ref, g_ref, b_ref, wr_ref, x1_ref, route_ref):
    n_mla = MLA_HEADS * HEAD_DIM
    n_dil = DIL_HEADS * HEAD_DIM
    for hh in range(OUTPROJ_TM // ROW_TM):
        rows = slice(hh * ROW_TM, (hh + 1) * ROW_TM)
        mix = jnp.dot(om_ref[rows, :], w_ref[0, 0:n_mla, :], preferred_element_type=F32)
        mix = mix + jnp.dot(od_ref[rows, :], w_ref[0, n_mla:n_mla + n_dil, :], preferred_element_type=F32)
        mix = mix + jnp.dot(on_ref[rows, :], w_ref[0, n_mla + n_dil:, :], preferred_element_type=F32)
        x1 = _layer_norm(DN_ALPHA * x_ref[rows, :] + mix, g_ref[...], b_ref[...])
        x1_ref[rows, :] = x1
        xh = x1.astype(BF16)
        xm = (x1 - xh.astype(F32)).astype(BF16)
        both = jnp.dot(xh, wr_ref[...], preferred_element_type=F32)
        logits = both[:, 0:LANES] + (both[:, LANES:2 * LANES]
                                     + jnp.dot(xm, wr_ref[:, 0:LANES], preferred_element_type=F32))
        route_ref[rows, :] = _route(logits)


def _outproj(o_mla, o_dil, o_nsa, w_out_s, layer, x, g, b, wr3):
    t = x.shape[0]
    tm = OUTPROJ_TM
    full = lambda shape: pl.BlockSpec(shape, lambda i: (0,) * len(shape))
    row = lambda w: pl.BlockSpec((tm, w), lambda i: (i, 0))
    return pl.pallas_call(
        _outproj_kernel,
        out_shape=(jax.ShapeDtypeStruct((t, D_MODEL), F32), jax.ShapeDtypeStruct((t, LANES), F32)),
        grid=(t // tm,),
        in_specs=[row(o_mla.shape[1]), row(o_dil.shape[1]), row(o_nsa.shape[1]),
                  pl.BlockSpec((1,) + w_out_s.shape[1:], lambda i: (layer, 0, 0)),
                  row(D_MODEL), full((1, D_MODEL)), full((1, D_MODEL)), full(wr3.shape)],
        out_specs=(row(D_MODEL), row(LANES)),
        compiler_params=_cparams(("arbitrary",)),
        name="outproj",
    )(o_mla, o_dil, o_nsa, w_out_s, x, g, b, wr3)


def _experts_kernel(blk_ex_ref, first_ref, nxt_ref, nused_ref, tok_cur_ref, tok_nxt_ref, x_hbm,
                    wg_hbm, wu_hbm, wd_hbm, y_ref, xbuf, wg_f, wu_f, wd_f, wg_b, wu_b, wd_b, gsem, sem, *, layer):
    b = pl.program_id(0)
    nused = nused_ref[0]
    slot = b % 2

    def weight_copies(e, s):
        return (pltpu.make_async_copy(wg_hbm.at[layer, e], wg_f.at[s], sem.at[s]),
                pltpu.make_async_copy(wu_hbm.at[layer, e], wu_f.at[s], sem.at[s]),
                pltpu.make_async_copy(wd_hbm.at[layer, e], wd_f.at[s], sem.at[s]))

    def row_copy(tok_ref, s, r):
        return pltpu.make_async_copy(x_hbm.at[pl.ds(tok_ref[0, 0, r], 1)], xbuf.at[s, pl.ds(r, 1)], gsem.at[s])

    def rows_wait(s):
        pltpu.make_async_copy(x_hbm.at[pl.ds(0, MOE_ROWS)], xbuf.at[s], gsem.at[s]).wait()

    @pl.when(b == 0)
    def _():
        for cp in weight_copies(blk_ex_ref[0], 0):
            cp.start()
        lax.fori_loop(0, MOE_ROWS, lambda r, c: (row_copy(tok_cur_ref, 0, r).start(), c)[1], 0)

    @pl.when(b < nused)
    def _():
        rows_wait(slot)
        first = first_ref[b]

        @pl.when(first > 0)
        def _():
            s = first - 1
            for cp in weight_copies(blk_ex_ref[b], s):
                cp.wait()

            @pl.when(nxt_ref[b] >= 0)
            def _():
                for cp in weight_copies(nxt_ref[b], 1 - s):
                    cp.start()

            wg_b[...] = wg_f[s].astype(BF16)
            wu_b[...] = wu_f[s].astype(BF16)
            wd_b[...] = wd_f[s].astype(BF16)

        for cur in range(2):
            @pl.when(slot == cur)
            def _():
                for r in range(MOE_ROWS):
                    row_copy(tok_nxt_ref, 1 - cur, r).start()
                rows = xbuf[cur].astype(BF16)
                gate = jnp.dot(rows, wg_b[...], preferred_element_type=F32)
                up = jnp.dot(rows, wu_b[...], preferred_element_type=F32)
                hid = (gate * jax.nn.sigmoid(gate) * up).astype(BF16)
                y_ref[...] = jnp.dot(hid, wd_b[...], preferred_element_type=F32)

        @pl.when(b + 1 == nused)
        def _():
            rows_wait(1 - slot)

    @pl.when(b >= nused)
    def _():
        y_ref[...] = jnp.zeros_like(y_ref)


def _experts(blk_ex, first, nxt, nused, buf_tok3, x1, w_gate, w_up, w_down, layer):
    n_blocks = buf_tok3.shape[0]
    hbm = pl.BlockSpec(memory_space=pl.ANY)
    tok_spec = lambda shift: pl.BlockSpec(
        (1, 1, MOE_ROWS), lambda b, ex, fi, nx, nu: (jnp.minimum(b + shift, n_blocks - 1), 0, 0),
        memory_space=pltpu.SMEM)
    grid_spec = pltpu.PrefetchScalarGridSpec(
        num_scalar_prefetch=4,
        grid=(n_blocks,),
        in_specs=[tok_spec(0), tok_spec(1), hbm, hbm, hbm, hbm],
        out_specs=pl.BlockSpec((MOE_ROWS, D_MODEL), lambda b, ex, fi, nx, nu: (b, 0)),
        scratch_shapes=[pltpu.VMEM((2, MOE_ROWS, D_MODEL), F32),
                        pltpu.VMEM((2, D_MODEL, EXPERT_FF), F32), pltpu.VMEM((2, D_MODEL, EXPERT_FF), F32),
                        pltpu.VMEM((2, EXPERT_FF, D_MODEL), F32),
                        pltpu.VMEM((D_MODEL, EXPERT_FF), BF16), pltpu.VMEM((D_MODEL, EXPERT_FF), BF16),
                        pltpu.VMEM((EXPERT_FF, D_MODEL), BF16),
                        pltpu.SemaphoreType.DMA((2,)), pltpu.SemaphoreType.DMA((2,))],
    )
    return pl.pallas_call(
        functools.partial(_experts_kernel, layer=layer),
        out_shape=jax.ShapeDtypeStruct((n_blocks * MOE_ROWS, D_MODEL), F32),
        grid_spec=grid_spec,
        compiler_params=_cparams(("arbitrary",)),
        name="experts",
    )(blk_ex, first, nxt, nused, buf_tok3, buf_tok3, x1, w_gate, w_up, w_down)


def _combine_kernel(pos_cur_ref, pos_nxt_ref, y_hbm, x1_ref, route_ref, g_ref, b_ref, o_ref, ob_ref, ybuf, sem):
    i = pl.program_id(0)
    n = pl.num_programs(0)
    tm = ROW_TM

    def issue(pos_ref, slot):
        def body(g, carry):
            for u in range(DMA_UNROLL):
                r = g * DMA_UNROLL + u
                for k in range(MOE_TOP_K):
                    p = pos_ref[0, 0, MOE_TOP_K * r + k]
                    pltpu.make_async_copy(y_hbm.at[pl.ds(p, 1)], ybuf.at[slot, k, pl.ds(r, 1)],
                                          sem.at[slot]).start()
            return carry

        lax.fori_loop(0, tm // DMA_UNROLL, body, 0)

    @pl.when(i == 0)
    def _():
        issue(pos_cur_ref, 0)

    @pl.when(i + 1 < n)
    def _():
        issue(pos_nxt_ref, (i + 1) % 2)

    slot = i % 2
    for k in range(MOE_TOP_K):
        pltpu.make_async_copy(y_hbm.at[pl.ds(0, tm)], ybuf.at[slot, k], sem.at[slot]).wait()
    route = route_ref[...]
    ffn = route[:, 2:3] * ybuf[slot, 0] + route[:, 3:4] * ybuf[slot, 1]
    x2 = _layer_norm(DN_ALPHA * x1_ref[...] + ffn, g_ref[...], b_ref[...])
    o_ref[...] = x2
    ob_ref[...] = x2.astype(ob_ref.dtype)


def _combine(pos3, yb, x1, route, g, b):
    t = x1.shape[0]
    tm = ROW_TM
    nt = t // tm
    pos_spec = lambda shift: pl.BlockSpec((1, 1, MOE_TOP_K * tm), lambda i: (jnp.minimum(i + shift, nt - 1), 0, 0),
                                          memory_space=pltpu.SMEM)
    row = lambda w: pl.BlockSpec((tm, w), lambda i: (i, 0))
    full = lambda shape: pl.BlockSpec(shape, lambda i: (0,) * len(shape))
    return pl.pallas_call(
        _combine_kernel,
        out_shape=(jax.ShapeDtypeStruct((t, D_MODEL), F32), jax.ShapeDtypeStruct((t, D_MODEL), BF16)),
        grid=(nt,),
        in_specs=[pos_spec(0), pos_spec(1), pl.BlockSpec(memory_space=pl.ANY), row(D_MODEL), row(LANES),
                  full((1, D_MODEL)), full((1, D_MODEL))],
        out_specs=(row(D_MODEL), row(D_MODEL)),
        scratch_shapes=[pltpu.VMEM((2, MOE_TOP_K, tm, D_MODEL), F32), pltpu.SemaphoreType.DMA((2,))],
        compiler_params=_cparams(("arbitrary",)),
        name="combine",
    )(pos3, pos3, yb, x1, route, g, b)


def _dispatch_plan(route, t):
    m = t * MOE_TOP_K
    n_blocks = -(-m // MOE_ROWS) + N_EXPERTS
    chunk = MOE_ROWS
    ex = route[:, 0:MOE_TOP_K].reshape(m // chunk, chunk, 1)
    onehot = (ex == jnp.arange(N_EXPERTS, dtype=F32)[None, None, :]).astype(F32)
    tri = jnp.asarray(np.tril(np.ones((chunk, chunk), np.float32)))
    within = jnp.einsum('ij,cje->cie', tri, onehot)
    totals = jnp.sum(onehot, axis=1)
    before = jnp.cumsum(totals, axis=0) - totals
    counts = jnp.sum(totals, axis=0)
    padded = jnp.ceil(counts / MOE_ROWS) * MOE_ROWS
    pad_end = jnp.cumsum(padded)
    pad_start = pad_end - padded
    row = jnp.sum(onehot * (within - 1.0 + before[:, None, :] + pad_start[None, None, :]), axis=-1)
    pos = row.astype(jnp.int32).reshape(m)
    blk_start = jnp.arange(n_blocks, dtype=F32) * MOE_ROWS
    blk_ex = jnp.minimum(jnp.sum((pad_end[None, :] <= blk_start[:, None]).astype(jnp.int32), axis=1),
                         N_EXPERTS - 1).astype(jnp.int32)
    nused = (pad_end[-1] / MOE_ROWS).astype(jnp.int32).reshape(1)
    order = jnp.argsort(ex.reshape(m).astype(jnp.int32), stable=True).astype(jnp.int32)
    seg_start = jnp.cumsum(counts) - counts
    into = blk_start - pad_start[blk_ex]
    src = (seg_start[blk_ex] + into)[:, None] + jnp.arange(MOE_ROWS, dtype=F32)[None, :]
    valid = (into[:, None] + jnp.arange(MOE_ROWS, dtype=F32)[None, :]) < counts[blk_ex][:, None]
    pair = order[jnp.clip(src, 0, m - 1).astype(jnp.int32)]
    buf_tok = jnp.where(valid, pair // MOE_TOP_K, 0).astype(jnp.int32).reshape(n_blocks, 1, MOE_ROWS)
    nonempty = padded > 0
    slot_e = (jnp.cumsum(nonempty.astype(jnp.int32)) - 1) % 2
    expert_ids = jnp.arange(N_EXPERTS, dtype=jnp.int32)
    later = lax.cummin(jnp.where(nonempty, expert_ids, N_EXPERTS), axis=0, reverse=True)
    nxt_e = jnp.concatenate([later[1:], jnp.full((1,), N_EXPERTS, jnp.int32)])
    nxt_e = jnp.where(nxt_e >= N_EXPERTS, -1, nxt_e)
    is_first = (blk_start == pad_start[blk_ex]) & (jnp.arange(n_blocks) < nused[0])
    first = jnp.where(is_first, 1 + slot_e[blk_ex], 0).astype(jnp.int32)
    return blk_ex, first, nxt_e[blk_ex].astype(jnp.int32), nused, buf_tok, pos


W_HALF = LANES // 2


def _shift_kernel(a_ref, b_ref, tail_ref, o_ref):
    m = pl.program_id(0)
    n_src = IN_COLS // LANES
    nl = o_ref.shape[0]
    d = o_ref.shape[2]
    nc = d // LANES
    a2, b2, t2 = a_ref, b_ref, tail_ref

    def slab(ref4, col0, c, l):
        return ref4[pl.ds(col0, W_HALF), c, l, :].astype(o_ref.dtype)

    def copy(ref_top, top0, ref_bot, bot0):
        for l in range(nl):
            for c in range(nc):
                o_ref[l, 0:W_HALF, c * LANES:(c + 1) * LANES] = slab(ref_top, top0, c, l)
                if ref_bot is None:
                    o_ref[l, W_HALF:LANES, c * LANES:(c + 1) * LANES] = jnp.zeros((W_HALF, LANES), o_ref.dtype)
                else:
                    o_ref[l, W_HALF:LANES, c * LANES:(c + 1) * LANES] = slab(ref_bot, bot0, c, l)

    @pl.when(m < n_src)
    def _():
        copy(a2, 0, b2, 0)

    @pl.when(m == n_src)
    def _():
        copy(a2, 0, t2, 0)

    @pl.when(m == n_src + 1)
    def _():
        copy(t2, W_HALF, None, 0)


def _shift_w_in(w_in):
    nl, d, _ = w_in.shape
    n_src = IN_COLS // LANES
    n_keep = (IN_SHIFT_FROM + MLA_ROPE_DIM) // LANES
    last = n_src * (LANES // W_HALF) - 1
    nc = d // LANES
    view = lambda w: jnp.transpose(w.reshape(nl, nc, LANES, w.shape[-1]), (3, 1, 0, 2))
    w_t = view(w_in)
    tail_t = view(jnp.pad(w_in[:, :, n_src * LANES:], ((0, 0), (0, 0), (0, (n_src + 1) * LANES - IN_COLS))))
    first = lambda m: jnp.where(m < n_keep, 2 * m, 2 * m - 1)
    return pl.pallas_call(
        _shift_kernel,
        out_shape=jax.ShapeDtypeStruct((nl, IN_COLS_PAD, d), BF16),
        grid=(N_IN_BLOCKS,),
        in_specs=[pl.BlockSpec((W_HALF, nc, nl, LANES), lambda m: (jnp.minimum(first(m), last), 0, 0, 0)),
                  pl.BlockSpec((W_HALF, nc, nl, LANES), lambda m: (jnp.minimum(first(m) + 1, last), 0, 0, 0)),
                  pl.BlockSpec((LANES, nc, nl, LANES), lambda m: (0, 0, 0, 0))],
        out_specs=pl.BlockSpec((nl, LANES, d), lambda m: (0, m, 0)),
        compiler_params=_cparams(("arbitrary",)),
        name="shiftw",
    )(w_t, w_t, tail_t)


def _permute_w_q_up(w_q_up):
    w = w_q_up.reshape(MLA_Q_RANK, MLA_HEADS, MLA_NOPE_DIM + MLA_ROPE_DIM)
    zero = jnp.zeros((MLA_Q_RANK, MLA_HEADS, MLA_QK_PAD - MLA_NOPE_DIM - MLA_ROPE_DIM), w.dtype)
    w = jnp.concatenate([w[:, :, :MLA_NOPE_DIM], zero, w[:, :, MLA_NOPE_DIM:]], axis=-1)
    return w.reshape(MLA_Q_RANK, MLA_HEADS * MLA_QK_PAD).astype(BF16)


def _split_w_kv_up(w_kv_up):
    w = w_kv_up.reshape(MLA_KV_RANK, MLA_HEADS, MLA_NOPE_DIM + HEAD_DIM)
    wkn = w[:, :, :MLA_NOPE_DIM].reshape(MLA_KV_RANK, MLA_HEADS * MLA_NOPE_DIM)
    wv = w[:, :, MLA_NOPE_DIM:].reshape(MLA_KV_RANK, MLA_HEADS * HEAD_DIM)
    return wkn.astype(BF16), wv.astype(BF16)


def _router_slices(w_grp, w_exp):
    w = jnp.concatenate([w_grp, w_exp, jnp.zeros((D_MODEL, LANES - N_GROUPS - N_EXPERTS), F32)], axis=1)
    hi = w.astype(BF16)
    mid = (w - hi.astype(F32)).astype(BF16)
    return jnp.concatenate([hi, mid], axis=1)


def _mixer(xb, nb, layer, w_in_s, q_lat_norm, w_q_up, kv_lat_norm, w_kv_up,
           cmp_pos_k, cmp_w1_k, cmp_w2_k, cmp_pos_v, cmp_w1_v, cmp_w2_v, consts):
    hproj = _inproj(xb, w_in_s, layer, consts["rope32"])
    wkn, wv = _split_w_kv_up(w_kv_up)
    q_mla, k_mla, v_mla = _mlaprep(hproj, q_lat_norm.reshape(1, -1), kv_lat_norm.reshape(1, -1),
                                   _permute_w_q_up(w_q_up), wkn, wv, consts["rope64"])
    wide = NSA_CMP_STRIDE * HEAD_DIM
    tk2 = hproj[:, BLK_NKC * LANES:(BLK_NKC + 1) * LANES].reshape(nb, N_CMP_PAD, wide)
    tv2 = hproj[:, BLK_NVC * LANES:(BLK_NVC + 1) * LANES].reshape(nb, N_CMP_PAD, wide)
    kc, vc = _compress(tk2, tv2, cmp_pos_k.reshape(2, wide), cmp_pos_v.reshape(2, wide),
                       cmp_w1_k.astype(BF16), cmp_w2_k.astype(BF16), cmp_w1_v.astype(BF16), cmp_w2_v.astype(BF16),
                       consts["rope_cmp"])
    o_mla = _flash(q_mla, k_mla, v_mla, consts["bias_causal"], nb=nb, heads=MLA_HEADS, dqk=MLA_QK_PAD,
                   dv=HEAD_DIM, q_blk0=0, k_blk0=0, v_blk0=0)
    o_dil = _flash(hproj, hproj, hproj, consts["bias_dil"], nb=nb, heads=DIL_HEADS, dqk=HEAD_DIM, dv=HEAD_DIM,
                   q_blk0=BLK_DQ, k_blk0=BLK_DK, v_blk0=BLK_DV)
    o_nsa = _nsa(hproj, kc, vc, consts["cover"], consts["expand"], consts["bias_win"], nb)
    return o_mla, o_dil, o_nsa


def _make_consts():
    pos = np.arange(SEQ)
    slot = np.arange(N_CMP_PAD)
    cmp_start = slot * NSA_CMP_STRIDE
    cmp_end = cmp_start + NSA_CMP_LEN - 1
    sel_start = np.arange(N_SEL_BLOCKS) * NSA_SEL_BLOCK
    cover = ((cmp_start[:, None] < sel_start[None, :] + NSA_SEL_BLOCK)
             & (cmp_start[:, None] + NSA_CMP_LEN > sel_start[None, :])
             & (slot[:, None] < N_CMP_PAD - 1)).astype(np.float32)
    expand = (np.arange(LANES)[:, None] == (pos // NSA_SEL_BLOCK)[None, :]).astype(np.float32)
    expand = expand.reshape(LANES, SEQ // NSA_T, NSA_T).transpose(1, 0, 2)
    n_win = -(-NSA_WINDOW // NSA_T) + 1
    return {
        "rope32": _rope_tables(pos, ROT_DIM, True),
        "rope64": _rope_tables(pos, MLA_ROPE_DIM, False, KROPE_LANE0),
        "rope_cmp": _rope_tables(cmp_end, ROT_DIM, True),
        "bias_causal": _distance_bias(_causal_mult, (0, 1), FLASH_T),
        "bias_dil": _distance_bias(_dilated_mult, range(SEQ // FLASH_T), FLASH_T),
        "bias_win": _distance_bias(_window_mult, range(n_win), NSA_T),
        "cover": jnp.asarray(cover.T, BF16),
        "expand": jnp.asarray(expand, BF16),
    }


def kernel(x, w_in, q_lat_norm, w_q_up, kv_lat_norm, w_kv_up, cmp_pos_k, cmp_w1_k, cmp_w2_k, cmp_pos_v, cmp_w1_v,
           cmp_w2_v, w_out, ln1_g, ln1_b, w_grp, w_exp, w_gate, w_up, w_down, ln2_g, ln2_b):
    nb, s, d = x.shape
    assert s == SEQ and d == D_MODEL
    t = nb * s
    consts = _make_consts()
    w_in_s = _shift_w_in(w_in)
    w_out_s = w_out.astype(BF16)
    xf = x.reshape(t, d)
    xb = xf.astype(BF16)
    for l in range(DEPTH):
        o_mla, o_dil, o_nsa = _mixer(xb, nb, l, w_in_s, q_lat_norm[l], w_q_up[l], kv_lat_norm[l], w_kv_up[l],
                                     cmp_pos_k[l], cmp_w1_k[l], cmp_w2_k[l], cmp_pos_v[l], cmp_w1_v[l],
                                     cmp_w2_v[l], consts)
        x1, route = _outproj(o_mla, o_dil, o_nsa, w_out_s, l, xf, ln1_g[l].reshape(1, d),
                             ln1_b[l].reshape(1, d), _router_slices(w_grp[l], w_exp[l]))
        blk_ex, first, nxt, nused, buf_tok, pos = _dispatch_plan(route, t)
        pos3 = pos.reshape(t // ROW_TM, 1, MOE_TOP_K * ROW_TM)
        yb = _experts(blk_ex, first, nxt, nused, buf_tok, x1, w_gate, w_up, w_down, l)
        xf, xb = _combine(pos3, yb, x1, route,
                          ln2_g[l].reshape(1, d), ln2_b[l].reshape(1, d))
    return xf.reshape(nb, s, d)
```

```python
import functools

import numpy as np
import jax
import jax.numpy as jnp
from jax import lax
from jax.experimental import pallas as pl
from jax.experimental.pallas import tpu as pltpu

F32 = jnp.float32
BF16 = jnp.bfloat16

D_MODEL = 2048
SEQ = 2048
DEPTH = 2
HEAD_DIM = 128
LANES = 128
MLA_HEADS = 4
DIL_HEADS = 6
NSA_HEADS = 6
MLA_Q_RANK = 384
MLA_KV_RANK = 128
MLA_NOPE_DIM = 128
MLA_ROPE_DIM = 64
MLA_QK_PAD = 256
DIL_PAIRS = ((128, 1), (512, 4), (2048, 16))
NSA_CMP_LEN = 32
NSA_CMP_STRIDE = 16
NSA_CMP_HIDDEN = 256
NSA_SEL_BLOCK = 64
NSA_TOP_N = 16
NSA_WINDOW = 512
NSA_BRANCHES = 3
NSA_FORCE_SCORE = 1e4
ROPE_THETA = 500000.0
ROT_DIM = HEAD_DIM // 4
N_GROUPS = 4
EXPERTS_PER_GROUP = 8
N_EXPERTS = N_GROUPS * EXPERTS_PER_GROUP
EXPERT_FF = 512
MOE_TOP_K = 2
MOE_ROWS = 256
DN_ALPHA = (2 * DEPTH) ** 0.25
LN_EPS = 1e-5
RMS_EPS = 1e-6
NEG_INF = -1e30

N_CMP_PAD = SEQ // NSA_CMP_STRIDE
N_SEL_BLOCKS = SEQ // NSA_SEL_BLOCK

BLK_QLAT = 0
BLK_KVLAT = 3
BLK_KROPE = 4
BLK_DQ = 5
BLK_DK = 11
BLK_DV = 17
BLK_NQ = 23
BLK_NKC = 29
BLK_NVC = 30
BLK_NKS = 31
BLK_NVS = 32
BLK_NKW = 33
BLK_NVW = 34
BLK_GATE = 35
N_IN_BLOCKS = 36
IN_COLS = 4434
IN_SHIFT_FROM = MLA_Q_RANK + MLA_KV_RANK - MLA_ROPE_DIM
IN_COLS_PAD = N_IN_BLOCKS * LANES
KROPE_LANE0 = LANES - MLA_ROPE_DIM

INPROJ_TM = 2048
INPROJ_SUB = 512
INPROJ_TN = 512
FLASH_T = 512
NSA_T = 256
NSA_TQ = 256
ROW_TM = 256
MLAPREP_TM = 512
OUTPROJ_TM = 512

VMEM_LIMIT = 56 * 1024 * 1024


def _cparams(sem):
    return pltpu.CompilerParams(dimension_semantics=sem, vmem_limit_bytes=VMEM_LIMIT)


def _rope_tables(pos, rot_dim, keep_rest, lane0=0):
    half = rot_dim // 2
    inv_freq = np.power(np.float32(ROPE_THETA), -np.arange(half, dtype=np.float32) / np.float32(half))
    ang = (pos.astype(np.float32)[:, None] * inv_freq[None, :]).astype(np.float32)
    cos, sin = np.cos(ang).astype(np.float32), np.sin(ang).astype(np.float32)
    n = pos.shape[0]
    c = np.full((n, LANES), 1.0 if keep_rest else 0.0, np.float32)
    s1 = np.zeros((n, LANES), np.float32)
    s2 = np.zeros((n, LANES), np.float32)
    c[:, lane0:lane0 + half] = cos
    c[:, lane0 + half:lane0 + rot_dim] = cos
    s1[:, lane0 + half:lane0 + rot_dim] = sin
    s2[:, lane0:lane0 + half] = -sin
    return jnp.asarray(c), jnp.asarray(s1), jnp.asarray(s2)


def _rope_lanes(a, c, s1, s2, half):
    return a * c + pltpu.roll(a, half, 1) * s1 + pltpu.roll(a, LANES - half, 1) * s2


def _distance_bias(mult_fn, diags, t, rows=None):
    d = np.asarray(diags, np.int64)[:, None, None] * t
    r = np.arange(t if rows is None else rows, dtype=np.int64)[None, :, None]
    c = np.arange(t, dtype=np.int64)[None, None, :]
    mult = mult_fn(d + r - c)
    return jnp.asarray(np.where(mult > 0, np.log(np.maximum(mult, 1).astype(np.float32)), NEG_INF).astype(np.float32))


def _causal_mult(dist):
    return (dist >= 0).astype(np.int64)


def _dilated_mult(dist):
    m = np.zeros_like(dist)
    for window, dil in DIL_PAIRS:
        m = m + ((dist >= 0) & (dist % dil == 0) & (dist <= (window // dil) * dil)).astype(np.int64)
    return m


def _window_mult(dist):
    return ((dist >= 0) & (dist <= NSA_WINDOW - 1)).astype(np.int64)


def _in_block(blk, lo, n):
    return jnp.logical_and(blk >= lo, blk < lo + n)


def _inproj_kernel(x_ref, w_ref, c_ref, s1_ref, s2_ref, o_ref, *, scale):
    j = pl.program_id(0)
    for hh in range(INPROJ_TM // INPROJ_SUB):
        rows = slice(hh * INPROJ_SUB, (hh + 1) * INPROJ_SUB)
        acc = _qk(x_ref[rows, :], w_ref[0])
        c, s1, s2 = c_ref[rows, :], s1_ref[rows, :], s2_ref[rows, :]
        for u in range(INPROJ_TN // LANES):
            blk = j * (INPROJ_TN // LANES) + u
            is_q = jnp.logical_or(_in_block(blk, BLK_DQ, DIL_HEADS), _in_block(blk, BLK_NQ, NSA_HEADS))
            is_rope = is_q | _in_block(blk, BLK_DK, DIL_HEADS) | (blk == BLK_NKS) | (blk == BLK_NKW)
            sc = jnp.where(is_q, scale, 1.0).astype(F32)
            a = acc[:, u * LANES:(u + 1) * LANES]
            r = _rope_lanes(a, jnp.where(is_rope, c, 1.0), jnp.where(is_rope, s1, 0.0),
                            jnp.where(is_rope, s2, 0.0), ROT_DIM // 2)
            o_ref[rows, u * LANES:(u + 1) * LANES] = (r * sc).astype(o_ref.dtype)


def _inproj(xb, w_in_s, layer, tabs):
    t = xb.shape[0]
    tm, tn = INPROJ_TM, INPROJ_TN
    nrow = SEQ // tm
    tab_spec = pl.BlockSpec((tm, LANES), lambda j, i: (i % nrow, 0))
    return pl.pallas_call(
        functools.partial(_inproj_kernel, scale=HEAD_DIM ** -0.5),
        out_shape=jax.ShapeDtypeStruct((t, IN_COLS_PAD), BF16),
        grid=(IN_COLS_PAD // tn, t // tm),
        in_specs=[pl.BlockSpec((tm, D_MODEL), lambda j, i: (i, 0)),
                  pl.BlockSpec((1, tn, D_MODEL), lambda j, i: (layer, j, 0)),
                  tab_spec, tab_spec, tab_spec],
        out_specs=pl.BlockSpec((tm, tn), lambda j, i: (i, j)),
        compiler_params=_cparams(("arbitrary", "arbitrary")),
        name="inproj",
    )(xb, w_in_s, *tabs)


def _rms(xf, g):
    return xf * lax.rsqrt(jnp.mean(jnp.square(xf), axis=-1, keepdims=True) + RMS_EPS) * g


def _mlaprep_kernel(ql_ref, kvl_ref, kr_ref, gq_ref, gkv_ref, wq_ref, wkn_ref, wv_ref,
                    c_ref, s1_ref, s2_ref, q_ref, k_ref, v_ref, *, scale):
    c, s1, s2 = c_ref[...], s1_ref[...], s2_ref[...]
    half = MLA_ROPE_DIM // 2
    qn = _rms(ql_ref[...].astype(F32), gq_ref[...]).astype(BF16)
    q = jnp.dot(qn, wq_ref[...], preferred_element_type=F32)
    for h in range(MLA_HEADS):
        lo = h * MLA_QK_PAD
        q_ref[:, lo:lo + LANES] = (q[:, lo:lo + LANES] * scale).astype(q_ref.dtype)
        a = q[:, lo + LANES:lo + 2 * LANES]
        q_ref[:, lo + LANES:lo + 2 * LANES] = (_rope_lanes(a, c, s1, s2, half) * scale).astype(q_ref.dtype)
    kvn = _rms(kvl_ref[...].astype(F32), gkv_ref[...]).astype(BF16)
    kn = jnp.dot(kvn, wkn_ref[...], preferred_element_type=F32)
    v_ref[...] = jnp.dot(kvn, wv_ref[...], preferred_element_type=F32).astype(v_ref.dtype)
    kpe = _rope_lanes(kr_ref[...].astype(F32), c, s1, s2, half).astype(k_ref.dtype)
    for h in range(MLA_HEADS):
        lo = h * MLA_QK_PAD
        k_ref[:, lo:lo + LANES] = kn[:, h * LANES:(h + 1) * LANES].astype(k_ref.dtype)
        k_ref[:, lo + LANES:lo + 2 * LANES] = kpe


def _mlaprep(hproj, gq, gkv, wq_p, wkn, wv, tabs):
    t = hproj.shape[0]
    tm = MLAPREP_TM
    nrow = SEQ // tm
    tab_spec = pl.BlockSpec((tm, LANES), lambda i: (i % nrow, 0))
    full = lambda shape: pl.BlockSpec(shape, lambda i: (0,) * len(shape))
    return pl.pallas_call(
        functools.partial(_mlaprep_kernel, scale=(MLA_NOPE_DIM + MLA_ROPE_DIM) ** -0.5),
        out_shape=(jax.ShapeDtypeStruct((t, MLA_HEADS * MLA_QK_PAD), BF16),
                   jax.ShapeDtypeStruct((t, MLA_HEADS * MLA_QK_PAD), BF16),
                   jax.ShapeDtypeStruct((t, MLA_HEADS * HEAD_DIM), BF16)),
        grid=(t // tm,),
        in_specs=[pl.BlockSpec((tm, MLA_Q_RANK), lambda i: (i, BLK_QLAT * LANES // MLA_Q_RANK)),
                  pl.BlockSpec((tm, LANES), lambda i: (i, BLK_KVLAT)),
                  pl.BlockSpec((tm, LANES), lambda i: (i, BLK_KROPE)),
                  full((1, MLA_Q_RANK)), full((1, MLA_KV_RANK)),
                  full((MLA_Q_RANK, MLA_HEADS * MLA_QK_PAD)),
                  full((MLA_KV_RANK, MLA_HEADS * MLA_NOPE_DIM)),
                  full((MLA_KV_RANK, MLA_HEADS * HEAD_DIM)),
                  tab_spec, tab_spec, tab_spec],
        out_specs=(pl.BlockSpec((tm, MLA_HEADS * MLA_QK_PAD), lambda i: (i, 0)),
                   pl.BlockSpec((tm, MLA_HEADS * MLA_QK_PAD), lambda i: (i, 0)),
                   pl.BlockSpec((tm, MLA_HEADS * HEAD_DIM), lambda i: (i, 0))),
        compiler_params=_cparams(("arbitrary",)),
        name="mlaprep",
    )(hproj, hproj, hproj, gq, gkv, wq_p, wkn, wv, *tabs)


def _gelu_tanh(x):
    return 0.5 * x * (1.0 + jnp.tanh(0.7978845608028654 * (x + 0.044715 * (x * x * x))))


def _compress_kernel(tk_ref, tv_ref, pek_ref, pev_ref, w1k_ref, w2k_ref, w1v_ref, w2v_ref,
                     c_ref, s1_ref, s2_ref, kc_ref, vc_ref):
    half_in = NSA_CMP_STRIDE * HEAD_DIM

    def comp(t_ref, pe_ref, w1_ref, w2_ref):
        t = t_ref[0].astype(F32)
        a = (t + pe_ref[0:1, :]).astype(BF16)
        b = (t + pe_ref[1:2, :]).astype(BF16)
        y0 = jnp.dot(a, w1_ref[0:half_in, :], preferred_element_type=F32)
        y1 = jnp.dot(b, w1_ref[half_in:2 * half_in, :], preferred_element_type=F32)
        hid = _gelu_tanh(y0 + pltpu.roll(y1, N_CMP_PAD - 1, 0))
        return jnp.dot(hid.astype(BF16), w2_ref[...], preferred_element_type=F32)

    kc = comp(tk_ref, pek_ref, w1k_ref, w2k_ref)
    kc_ref[0] = _rope_lanes(kc, c_ref[...], s1_ref[...], s2_ref[...], ROT_DIM // 2).astype(kc_ref.dtype)
    vc_ref[0] = comp(tv_ref, pev_ref, w1v_ref, w2v_ref).astype(vc_ref.dtype)


def _compress(tk2, tv2, pek, pev, w1k, w2k, w1v, w2v, tabs):
    nb = tk2.shape[0]
    wide = NSA_CMP_STRIDE * HEAD_DIM
    full = lambda shape: pl.BlockSpec(shape, lambda b: (0,) * len(shape))
    bspec = pl.BlockSpec((1, N_CMP_PAD, wide), lambda b: (b, 0, 0))
    ospec = pl.BlockSpec((1, N_CMP_PAD, HEAD_DIM), lambda b: (b, 0, 0))
    return pl.pallas_call(
        _compress_kernel,
        out_shape=(jax.ShapeDtypeStruct((nb, N_CMP_PAD, HEAD_DIM), BF16),) * 2,
        grid=(nb,),
        in_specs=[bspec, bspec, full((2, wide)), full((2, wide)),
                  full((2 * wide, NSA_CMP_HIDDEN)), full((NSA_CMP_HIDDEN, HEAD_DIM)),
                  full((2 * wide, NSA_CMP_HIDDEN)), full((NSA_CMP_HIDDEN, HEAD_DIM)),
                  full((N_CMP_PAD, LANES)), full((N_CMP_PAD, LANES)), full((N_CMP_PAD, LANES))],
        out_specs=(ospec, ospec),
        compiler_params=_cparams(("arbitrary",)),
        name="compress",
    )(tk2, tv2, pek, pev, w1k, w2k, w1v, w2v, *tabs)


def _qk(q, k):
    return lax.dot_general(q, k, (((1,), (1,)), ((), ())), preferred_element_type=F32)


def _lane_fold(x, op):
    out = x[:, 0:LANES]
    for u in range(1, x.shape[1] // LANES):
        out = op(out, x[:, u * LANES:(u + 1) * LANES])
    return out


def _split3(x):
    hi = x.astype(BF16)
    r1 = x - hi.astype(F32)
    mid = r1.astype(BF16)
    lo = (r1 - mid.astype(F32)).astype(BF16)
    return hi, mid, lo


def _attend(q, k_ref, v_ref, bias_fn, j_lo, n_tiles, s_scr, *, tk, groups=1):
    rows = q.shape[0]
    dv = v_ref.shape[-1]

    def pass1(t, mx):
        off = pl.multiple_of((j_lo + t) * tk, tk)
        s = _qk(q, k_ref[pl.ds(off, tk), :])
        bias = bias_fn(j_lo + t)
        if groups > 1:
            s = (s.reshape(groups, rows // groups, tk) + bias[None]).reshape(rows, tk)
        else:
            s = s + bias
        s_scr[t] = s
        return jnp.maximum(mx, _lane_fold(s, jnp.maximum))

    mx = lax.fori_loop(0, n_tiles, pass1, jnp.full((rows, LANES), NEG_INF, F32))
    m = jnp.broadcast_to(jnp.max(mx, axis=-1, keepdims=True), (rows, LANES))

    def pass2(t, carry):
        lsum, acc = carry
        off = pl.multiple_of((j_lo + t) * tk, tk)
        s = s_scr[t]
        parts = [jnp.exp(s[:, u * LANES:(u + 1) * LANES] - m) for u in range(tk // LANES)]
        for p in parts:
            lsum = lsum + p
        pb = jnp.concatenate([p.astype(BF16) for p in parts], axis=1)
        acc = acc + jnp.dot(pb, v_ref[pl.ds(off, tk), :], preferred_element_type=F32)
        return lsum, acc

    lsum, acc = lax.fori_loop(0, n_tiles, pass2, (jnp.zeros((rows, LANES), F32), jnp.zeros((rows, dv), F32)))
    return acc / jnp.sum(lsum, axis=-1, keepdims=True)


def _flash_kernel(q_ref, k_ref, v_ref, bias_ref, o_ref, s_scr):
    i = pl.program_id(2)
    last = bias_ref.shape[0] - 1
    out = _attend(q_ref[...], k_ref, v_ref, lambda j: bias_ref[jnp.minimum(i - j, last)], 0, i + 1, s_scr,
                  tk=FLASH_T)
    o_ref[...] = out.astype(o_ref.dtype)


def _flash(q_arr, k_arr, v_arr, bias, *, nb, heads, dqk, dv, q_blk0, k_blk0, v_blk0):
    t = FLASH_T
    nq = SEQ // t
    return pl.pallas_call(
        _flash_kernel,
        out_shape=jax.ShapeDtypeStruct((nb * SEQ, heads * dv), BF16),
        grid=(nb, heads, nq),
        in_specs=[pl.BlockSpec((t, dqk), lambda b, h, i: (b * nq + i, q_blk0 + h)),
                  pl.BlockSpec((SEQ, dqk), lambda b, h, i: (b, k_blk0 + h)),
                  pl.BlockSpec((SEQ, dv), lambda b, h, i: (b, v_blk0 + h)),
                  pl.BlockSpec(bias.shape, lambda b, h, i: (0, 0, 0))],
        out_specs=pl.BlockSpec((t, dv), lambda b, h, i: (b * nq + i, h)),
        scratch_shapes=[pltpu.VMEM((nq, t, t), F32)],
        compiler_params=_cparams(("arbitrary", "arbitrary", "arbitrary")),
        name="flash",
    )(q_arr, k_arr, v_arr, bias)


def _nsa_kernel(q0_ref, q1_ref, q2_ref, q3_ref, q4_ref, q5_ref, kc_ref, vc_ref, ks_ref, vs_ref, kw_ref, vw_ref,
                gate_ref, cover_ref, expand_ref, wbias_ref, o_ref, bias_scr, s_scr):
    i = pl.program_id(1)
    tq, tk = NSA_TQ, NSA_T
    per = tq // tk
    nh = NSA_HEADS
    rows = nh * tq
    pos = i * tq + lax.broadcasted_iota(jnp.int32, (tq, 1), 0)
    q = jnp.concatenate([r[...] for r in (q0_ref, q1_ref, q2_ref, q3_ref, q4_ref, q5_ref)], axis=0)

    cidx = lax.broadcasted_iota(jnp.int32, (tq, N_CMP_PAD), 1)
    valid_c = (cidx * NSA_CMP_STRIDE + (NSA_CMP_LEN - 1) <= pos) & (cidx < N_CMP_PAD - 1)
    valid_cf = valid_c.astype(F32)
    cbias = jnp.where(valid_c, 0.0, NEG_INF).astype(F32)
    s = (_qk(q, kc_ref[0]).reshape(nh, tq, N_CMP_PAD) + cbias[None]).reshape(rows, N_CMP_PAD)
    p = (jnp.exp(s - jnp.max(s, axis=-1, keepdims=True)).reshape(nh, tq, N_CMP_PAD) * valid_cf[None])
    p = p.reshape(rows, N_CMP_PAD)
    l = jnp.sum(p, axis=-1, keepdims=True)
    p = p / jnp.where(l > 0.0, l, 1.0)
    o_cmp = jnp.dot(p.astype(BF16), vc_ref[0], preferred_element_type=F32)
    psum = jnp.sum(p.reshape(nh, tq, N_CMP_PAD), axis=0)

    cover_t = cover_ref[...]
    imp = sum(jnp.dot(cover_t, part, preferred_element_type=F32) for part in _split3(psum.T))

    pos_t = i * tq + lax.broadcasted_iota(jnp.int32, (N_SEL_BLOCKS, tq), 1)
    jidx = lax.broadcasted_iota(jnp.int32, (N_SEL_BLOCKS, tq), 0)
    qblk = pos_t // NSA_SEL_BLOCK
    valid_s = jidx * NSA_SEL_BLOCK <= pos_t
    forced = (jidx == 0) | (jidx == qblk) | (jidx == qblk - 1)
    score = jnp.where(valid_s, jnp.where(forced, NSA_FORCE_SCORE, imp), -1.0)
    rank = jnp.zeros((N_SEL_BLOCKS, tq), jnp.int32)
    for ii in range(N_SEL_BLOCKS):
        si = score[ii:ii + 1, :]
        rank = rank + ((si > score) | ((si == score) & (ii < jidx))).astype(jnp.int32)
    sel_t = (rank < NSA_TOP_N).astype(F32)
    sel = jnp.concatenate([sel_t, jnp.zeros((LANES - N_SEL_BLOCKS, tq), F32)], axis=0).T.astype(BF16)

    n_causal = per * (i + 1)

    def fill(j, carry):
        hit = jnp.dot(sel, expand_ref[j], preferred_element_type=F32)
        kpos = j * tk + lax.broadcasted_iota(jnp.int32, (tq, tk), 1)
        bias_scr[j] = jnp.where((hit > 0.5) & (kpos <= pos), 0.0, NEG_INF).astype(F32)
        return carry

    lax.fori_loop(0, n_causal, fill, 0)

    o_slc = _attend(q, ks_ref, vs_ref, lambda j: bias_scr[j], 0, n_causal, s_scr, tk=tk, groups=nh)
    far = wbias_ref.shape[0] - per
    w_lo = jnp.maximum(per * i - far, 0)
    o_win = _attend(q, kw_ref, vw_ref, lambda j: wbias_ref[per * i - j + per - 1], w_lo, n_causal - w_lo, s_scr,
                    tk=tk, groups=nh)

    gates = jax.nn.sigmoid(gate_ref[...].astype(F32))
    for h in range(nh):
        sl = slice(h * tq, (h + 1) * tq)
        g0 = NSA_BRANCHES * h
        out = (gates[:, g0:g0 + 1] * o_cmp[sl] + gates[:, g0 + 1:g0 + 2] * o_slc[sl]
               + gates[:, g0 + 2:g0 + 3] * o_win[sl])
        o_ref[:, h * LANES:(h + 1) * LANES] = out.astype(o_ref.dtype)


def _nsa(hproj, kc, vc, cover, expand, wbias, nb):
    t = NSA_TQ
    nq = SEQ // t
    nk = SEQ // NSA_T
    wide = NSA_HEADS * HEAD_DIM
    q_spec = lambda h: pl.BlockSpec((t, LANES), lambda b, i: (b * nq + i, BLK_NQ + h))
    kv_spec = lambda blk: pl.BlockSpec((SEQ, LANES), lambda b, i: (b, blk))
    cspec = pl.BlockSpec((1, N_CMP_PAD, HEAD_DIM), lambda b, i: (b, 0, 0))
    full = lambda shape: pl.BlockSpec(shape, lambda b, i: (0,) * len(shape))
    return pl.pallas_call(
        _nsa_kernel,
        out_shape=jax.ShapeDtypeStruct((nb * SEQ, wide), BF16),
        grid=(nb, nq),
        in_specs=[q_spec(h) for h in range(NSA_HEADS)]
        + [cspec, cspec, kv_spec(BLK_NKS), kv_spec(BLK_NVS), kv_spec(BLK_NKW), kv_spec(BLK_NVW),
           pl.BlockSpec((t, LANES), lambda b, i: (b * nq + i, BLK_GATE)),
           full(cover.shape), full(expand.shape), full(wbias.shape)],
        out_specs=pl.BlockSpec((t, wide), lambda b, i: (b * nq + i, 0)),
        scratch_shapes=[pltpu.VMEM((nk, t, NSA_T), F32), pltpu.VMEM((nk, NSA_HEADS * t, NSA_T), F32)],
        compiler_params=_cparams(("arbitrary", "arbitrary")),
        name="nsa",
    )(*([hproj] * NSA_HEADS), kc, vc, hproj, hproj, hproj, hproj, hproj, cover, expand, wbias)


def _layer_norm(y, g, b):
    mu = jnp.mean(y, axis=-1, keepdims=True)
    var = jnp.mean(jnp.square(y - mu), axis=-1, keepdims=True)
    return (y - mu) * lax.rsqrt(var + LN_EPS) * g + b


def _lane_min(x):
    return jnp.min(x, axis=-1, keepdims=True)


def _lane_max(x):
    return jnp.max(x, axis=-1, keepdims=True)


def _route(logits):
    lane = lax.broadcasted_iota(jnp.int32, logits.shape, 1)
    lane_f = lane.astype(F32)
    far = float(LANES)
    is_grp = lane < N_GROUPS
    lg = jnp.where(is_grp, logits, NEG_INF)
    eg = jnp.where(is_grp, jnp.exp(lg - _lane_max(lg)), 0.0)
    prob = eg / jnp.sum(eg, axis=-1, keepdims=True)
    p_g = _lane_max(prob)
    g_idx = _lane_min(jnp.where(is_grp & (prob == p_g), lane_f, far))
    e_lo = N_GROUPS + EXPERTS_PER_GROUP * g_idx
    in_grp = (lane_f >= e_lo) & (lane_f < e_lo + EXPERTS_PER_GROUP)
    le = jnp.where(in_grp, logits, NEG_INF)
    v1 = _lane_max(le)
    i1 = _lane_min(jnp.where(in_grp & (le == v1), lane_f, far))
    rest = in_grp & (lane_f != i1)
    le2 = jnp.where(rest, logits, NEG_INF)
    v2 = _lane_max(le2)
    i2 = _lane_min(jnp.where(rest & (le2 == v2), lane_f, far))
    e21 = jnp.exp(v2 - v1)
    den = 1.0 + e21
    gate1 = p_g * (1.0 / den)
    gate2 = p_g * (e21 / den)
    out = jnp.where(lane == 0, i1 - N_GROUPS, 0.0)
    out = jnp.where(lane == 1, i2 - N_GROUPS, out)
    out = jnp.where(lane == 2, gate1, out)
    out = jnp.where(lane == 3, gate2, out)
    return out


def _outproj_kernel(om_ref, od_ref, on_ref, w_ref, x_ref, g_ref, b_ref, wr_ref, x1_ref, route_ref):
    n_mla = MLA_HEADS * HEAD_DIM
    n_dil = DIL_HEADS * HEAD_DIM
    for hh in range(OUTPROJ_TM // ROW_TM):
        rows = slice(hh * ROW_TM, (hh + 1) * ROW_TM)
        mix = jnp.dot(om_ref[rows, :], w_ref[0, 0:n_mla, :], preferred_element_type=F32)
        mix = mix + jnp.dot(od_ref[rows, :], w_ref[0, n_mla:n_mla + n_dil, :], preferred_element_type=F32)
        mix = mix + jnp.dot(on_ref[rows, :], w_ref[0, n_mla + n_dil:, :], preferred_element_type=F32)
        x1 = _layer_norm(DN_ALPHA * x_ref[rows, :] + mix, g_ref[...], b_ref[...])
        x1_ref[rows, :] = x1
        xh = x1.astype(BF16)
        xm = (x1 - xh.astype(F32)).astype(BF16)
        both = jnp.dot(xh, wr_ref[...], preferred_element_type=F32)
        logits = both[:, 0:LANES] + (both[:, LANES:2 * LANES]
                                     + jnp.dot(xm, wr_ref[:, 0:LANES], preferred_element_type=F32))
        route_ref[rows, :] = _route(logits)


def _outproj(o_mla, o_dil, o_nsa, w_out_s, layer, x, g, b, wr3):
    t = x.shape[0]
    tm = OUTPROJ_TM
    full = lambda shape: pl.BlockSpec(shape, lambda i: (0,) * len(shape))
    row = lambda w: pl.BlockSpec((tm, w), lambda i: (i, 0))
    return pl.pallas_call(
        _outproj_kernel,
        out_shape=(jax.ShapeDtypeStruct((t, D_MODEL), F32), jax.ShapeDtypeStruct((t, LANES), F32)),
        grid=(t // tm,),
        in_specs=[row(o_mla.shape[1]), row(o_dil.shape[1]), row(o_nsa.shape[1]),
                  pl.BlockSpec((1,) + w_out_s.shape[1:], lambda i: (layer, 0, 0)),
                  row(D_MODEL), full((1, D_MODEL)), full((1, D_MODEL)), full(wr3.shape)],
        out_specs=(row(D_MODEL), row(LANES)),
        compiler_params=_cparams(("arbitrary",)),
        name="outproj",
    )(o_mla, o_dil, o_nsa, w_out_s, x, g, b, wr3)


def _dispatch_kernel(fill_ref, pos_ref, x1_ref, xb_hbm, stage, zbuf, sem, zsem):
    i = pl.program_id(0)
    n = pl.num_programs(0)
    tm = ROW_TM
    slot = i % 2

    def drain(s):
        for _ in range(MOE_TOP_K):
            pltpu.make_async_copy(stage.at[s], xb_hbm.at[pl.ds(0, tm)], sem.at[s]).wait()

    def zero_copy(e):
        start = pl.multiple_of(fill_ref[e], MOE_ROWS)
        return pltpu.make_async_copy(zbuf, xb_hbm.at[pl.ds(start, MOE_ROWS)], zsem)

    @pl.when(i == 0)
    def _():
        zbuf[...] = jnp.zeros_like(zbuf)
        for e in range(N_EXPERTS):
            @pl.when(fill_ref[e] >= 0)
            def _():
                zero_copy(e).start()
        for e in range(N_EXPERTS):
            @pl.when(fill_ref[e] >= 0)
            def _():
                zero_copy(e).wait()

        def tail_copy(b):
            start = pl.multiple_of(b * MOE_ROWS, MOE_ROWS)
            return pltpu.make_async_copy(zbuf, xb_hbm.at[pl.ds(start, MOE_ROWS)], zsem)

        n_blocks = xb_hbm.shape[0] // MOE_ROWS
        lax.fori_loop(fill_ref[N_EXPERTS], n_blocks, lambda b, c: (tail_copy(b).start(), c)[1], 0)
        lax.fori_loop(fill_ref[N_EXPERTS], n_blocks, lambda b, c: (tail_copy(b).wait(), c)[1], 0)

    @pl.when(i >= 2)
    def _():
        drain(slot)

    stage[slot] = x1_ref[...]

    for r in range(tm):
        for k in range(MOE_TOP_K):
            p = pos_ref[0, 0, MOE_TOP_K * r + k]
            pltpu.make_async_copy(stage.at[slot, pl.ds(r, 1)], xb_hbm.at[pl.ds(p, 1)],
                                  sem.at[slot]).start(priority=k % 2)

    @pl.when(i == n - 1)
    def _():
        drain(1 - slot)
        drain(slot)


def _dispatch(fill, pos3, x1, n_rows):
    t = x1.shape[0]
    tm = ROW_TM
    grid_spec = pltpu.PrefetchScalarGridSpec(
        num_scalar_prefetch=1,
        grid=(t // tm,),
        in_specs=[pl.BlockSpec((1, 1, MOE_TOP_K * tm), lambda i, fl: (i, 0, 0), memory_space=pltpu.SMEM),
                  pl.BlockSpec((tm, D_MODEL), lambda i, fl: (i, 0))],
        out_specs=pl.BlockSpec(memory_space=pl.ANY),
        scratch_shapes=[pltpu.VMEM((2, tm, D_MODEL), F32), pltpu.VMEM((MOE_ROWS, D_MODEL), F32),
                        pltpu.SemaphoreType.DMA((2,)), pltpu.SemaphoreType.DMA(())],
    )
    return pl.pallas_call(
        _dispatch_kernel,
        out_shape=jax.ShapeDtypeStruct((n_rows, D_MODEL), F32),
        grid_spec=grid_spec,
        compiler_params=_cparams(("arbitrary",)),
        name="dispatch",
    )(fill, pos3, x1)


def _experts_kernel(blk_ex_ref, first_ref, nxt_ref, nused_ref, x_ref, wg_hbm, wu_hbm, wd_hbm, y_ref,
                    wg_f, wu_f, wd_f, wg_b, wu_b, wd_b, sem, *, layer):
    b = pl.program_id(0)
    nused = nused_ref[0]

    def weight_copies(e, s):
        return (pltpu.make_async_copy(wg_hbm.at[layer, e], wg_f.at[s], sem.at[s]),
                pltpu.make_async_copy(wu_hbm.at[layer, e], wu_f.at[s], sem.at[s]),
                pltpu.make_async_copy(wd_hbm.at[layer, e], wd_f.at[s], sem.at[s]))

    @pl.when(b == 0)
    def _():
        for cp in weight_copies(blk_ex_ref[0], 0):
            cp.start()

    @pl.when(b < nused)
    def _():
        first = first_ref[b]

        @pl.when(first > 0)
        def _():
            s = first - 1
            for cp in weight_copies(blk_ex_ref[b], s):
                cp.wait()

            @pl.when(nxt_ref[b] >= 0)
            def _():
                for cp in weight_copies(nxt_ref[b], 1 - s):
                    cp.start()

            wg_b[...] = wg_f[s].astype(BF16)
            wu_b[...] = wu_f[s].astype(BF16)
            wd_b[...] = wd_f[s].astype(BF16)

        rows = x_ref[...].astype(BF16)
        gate = jnp.dot(rows, wg_b[...], preferred_element_type=F32)
        up = jnp.dot(rows, wu_b[...], preferred_element_type=F32)
        hid = (gate * jax.nn.sigmoid(gate) * up).astype(BF16)
        y_ref[...] = jnp.dot(hid, wd_b[...], preferred_element_type=F32)

    @pl.when(b >= nused)
    def _():
        y_ref[...] = jnp.zeros_like(y_ref)


def _experts(blk_ex, first, nxt, nused, xb_rows, w_gate, w_up, w_down, layer):
    n_blocks = xb_rows.shape[0] // MOE_ROWS
    hbm = pl.BlockSpec(memory_space=pl.ANY)
    grid_spec = pltpu.PrefetchScalarGridSpec(
        num_scalar_prefetch=4,
        grid=(n_blocks,),
        in_specs=[pl.BlockSpec((MOE_ROWS, D_MODEL), lambda b, ex, fi, nx, nu: (jnp.minimum(b, nu[0] - 1), 0)),
                  hbm, hbm, hbm],
        out_specs=pl.BlockSpec((MOE_ROWS, D_MODEL), lambda b, ex, fi, nx, nu: (b, 0)),
        scratch_shapes=[pltpu.VMEM((2, D_MODEL, EXPERT_FF), F32), pltpu.VMEM((2, D_MODEL, EXPERT_FF), F32),
                        pltpu.VMEM((2, EXPERT_FF, D_MODEL), F32),
                        pltpu.VMEM((D_MODEL, EXPERT_FF), BF16), pltpu.VMEM((D_MODEL, EXPERT_FF), BF16),
                        pltpu.VMEM((EXPERT_FF, D_MODEL), BF16), pltpu.SemaphoreType.DMA((2,))],
    )
    return pl.pallas_call(
        functools.partial(_experts_kernel, layer=layer),
        out_shape=jax.ShapeDtypeStruct((n_blocks * MOE_ROWS, D_MODEL), F32),
        grid_spec=grid_spec,
        compiler_params=_cparams(("arbitrary",)),
        name="experts",
    )(blk_ex, first, nxt, nused, xb_rows, w_gate, w_up, w_down)


def _combine_kernel(pos_cur_ref, pos_nxt_ref, y_hbm, x1_ref, route_ref, g_ref, b_ref, o_ref, ob_ref, ybuf, sem):
    i = pl.program_id(0)
    n = pl.num_programs(0)
    tm = ROW_TM

    def issue(pos_ref, slot):
        for r in range(tm):
            for k in range(MOE_TOP_K):
                p = pos_ref[0, 0, MOE_TOP_K * r + k]
                pltpu.make_async_copy(y_hbm.at[pl.ds(p, 1)], ybuf.at[slot, k, pl.ds(r, 1)], sem.at[slot]).start()

    @pl.when(i == 0)
    def _():
        issue(pos_cur_ref, 0)

    @pl.when(i + 1 < n)
    def _():
        issue(pos_nxt_ref, (i + 1) % 2)

    slot = i % 2
    for k in range(MOE_TOP_K):
        pltpu.make_async_copy(y_hbm.at[pl.ds(0, tm)], ybuf.at[slot, k], sem.at[slot]).wait()
    route = route_ref[...]
    ffn = route[:, 2:3] * ybuf[slot, 0] + route[:, 3:4] * ybuf[slot, 1]
    x2 = _layer_norm(DN_ALPHA * x1_ref[...] + ffn, g_ref[...], b_ref[...])
    o_ref[...] = x2
    ob_ref[...] = x2.astype(ob_ref.dtype)


def _combine(pos3, yb, x1, route, g, b):
    t = x1.shape[0]
    tm = ROW_TM
    nt = t // tm
    pos_spec = lambda shift: pl.BlockSpec((1, 1, MOE_TOP_K * tm), lambda i: (jnp.minimum(i + shift, nt - 1), 0, 0),
                                          memory_space=pltpu.SMEM)
    row = lambda w: pl.BlockSpec((tm, w), lambda i: (i, 0))
    full = lambda shape: pl.BlockSpec(shape, lambda i: (0,) * len(shape))
    return pl.pallas_call(
        _combine_kernel,
        out_shape=(jax.ShapeDtypeStruct((t, D_MODEL), F32), jax.ShapeDtypeStruct((t, D_MODEL), BF16)),
        grid=(nt,),
        in_specs=[pos_spec(0), pos_spec(1), pl.BlockSpec(memory_space=pl.ANY), row(D_MODEL), row(LANES),
                  full((1, D_MODEL)), full((1, D_MODEL))],
        out_specs=(row(D_MODEL), row(D_MODEL)),
        scratch_shapes=[pltpu.VMEM((2, MOE_TOP_K, tm, D_MODEL), F32), pltpu.SemaphoreType.DMA((2,))],
        compiler_params=_cparams(("arbitrary",)),
        name="combine",
    )(pos3, pos3, yb, x1, route, g, b)


def _dispatch_plan(route, t):
    m = t * MOE_TOP_K
    n_blocks = -(-m // MOE_ROWS) + N_EXPERTS
    chunk = MOE_ROWS
    ex = route[:, 0:MOE_TOP_K].reshape(m // chunk, chunk, 1)
    onehot = (ex == jnp.arange(N_EXPERTS, dtype=F32)[None, None, :]).astype(F32)
    tri = jnp.asarray(np.tril(np.ones((chunk, chunk), np.float32)))
    within = jnp.einsum('ij,cje->cie', tri, onehot)
    totals = jnp.sum(onehot, axis=1)
    before = jnp.cumsum(totals, axis=0) - totals
    counts = jnp.sum(totals, axis=0)
    padded = jnp.ceil(counts / MOE_ROWS) * MOE_ROWS
    pad_end = jnp.cumsum(padded)
    pad_start = pad_end - padded
    row = jnp.sum(onehot * (within - 1.0 + before[:, None, :] + pad_start[None, None, :]), axis=-1)
    pos = row.astype(jnp.int32).reshape(m)
    blk_start = jnp.arange(n_blocks, dtype=F32) * MOE_ROWS
    blk_ex = jnp.minimum(jnp.sum((pad_end[None, :] <= blk_start[:, None]).astype(jnp.int32), axis=1),
                         N_EXPERTS - 1).astype(jnp.int32)
    nused = (pad_end[-1] / MOE_ROWS).astype(jnp.int32).reshape(1)
    fill = jnp.concatenate([jnp.where(padded > 0, pad_end - MOE_ROWS, -1.0).astype(jnp.int32), nused])
    nonempty = padded > 0
    slot_e = (jnp.cumsum(nonempty.astype(jnp.int32)) - 1) % 2
    expert_ids = jnp.arange(N_EXPERTS, dtype=jnp.int32)
    later = lax.cummin(jnp.where(nonempty, expert_ids, N_EXPERTS), axis=0, reverse=True)
    nxt_e = jnp.concatenate([later[1:], jnp.full((1,), N_EXPERTS, jnp.int32)])
    nxt_e = jnp.where(nxt_e >= N_EXPERTS, -1, nxt_e)
    is_first = (blk_start == pad_start[blk_ex]) & (jnp.arange(n_blocks) < nused[0])
    first = jnp.where(is_first, 1 + slot_e[blk_ex], 0).astype(jnp.int32)
    return blk_ex, first, nxt_e[blk_ex].astype(jnp.int32), nused, fill, pos


W_HALF = LANES // 2


def _shift_kernel(a_ref, b_ref, tail_ref, o_ref):
    m = pl.program_id(0)
    n_src = IN_COLS // LANES
    nl = o_ref.shape[0]
    d = o_ref.shape[2]
    nc = d // LANES
    a2, b2, t2 = a_ref, b_ref, tail_ref

    def slab(ref4, col0, c, l):
        return ref4[pl.ds(col0, W_HALF), c, l, :].astype(o_ref.dtype)

    def copy(ref_top, top0, ref_bot, bot0):
        for l in range(nl):
            for c in range(nc):
                o_ref[l, 0:W_HALF, c * LANES:(c + 1) * LANES] = slab(ref_top, top0, c, l)
                if ref_bot is None:
                    o_ref[l, W_HALF:LANES, c * LANES:(c + 1) * LANES] = jnp.zeros((W_HALF, LANES), o_ref.dtype)
                else:
                    o_ref[l, W_HALF:LANES, c * LANES:(c + 1) * LANES] = slab(ref_bot, bot0, c, l)

    @pl.when(m < n_src)
    def _():
        copy(a2, 0, b2, 0)

    @pl.when(m == n_src)
    def _():
        copy(a2, 0, t2, 0)

    @pl.when(m == n_src + 1)
    def _():
        copy(t2, W_HALF, None, 0)


def _shift_w_in(w_in):
    nl, d, _ = w_in.shape
    n_src = IN_COLS // LANES
    n_keep = (IN_SHIFT_FROM + MLA_ROPE_DIM) // LANES
    last = n_src * (LANES // W_HALF) - 1
    nc = d // LANES
    view = lambda w: jnp.transpose(w.reshape(nl, nc, LANES, w.shape[-1]), (3, 1, 0, 2))
    w_t = view(w_in)
    tail_t = view(jnp.pad(w_in[:, :, n_src * LANES:], ((0, 0), (0, 0), (0, (n_src + 1) * LANES - IN_COLS))))
    first = lambda m: jnp.where(m < n_keep, 2 * m, 2 * m - 1)
    return pl.pallas_call(
        _shift_kernel,
        out_shape=jax.ShapeDtypeStruct((nl, IN_COLS_PAD, d), BF16),
        grid=(N_IN_BLOCKS,),
        in_specs=[pl.BlockSpec((W_HALF, nc, nl, LANES), lambda m: (jnp.minimum(first(m), last), 0, 0, 0)),
                  pl.BlockSpec((W_HALF, nc, nl, LANES), lambda m: (jnp.minimum(first(m) + 1, last), 0, 0, 0)),
                  pl.BlockSpec((LANES, nc, nl, LANES), lambda m: (0, 0, 0, 0))],
        out_specs=pl.BlockSpec((nl, LANES, d), lambda m: (0, m, 0)),
        compiler_params=_cparams(("arbitrary",)),
        name="shiftw",
    )(w_t, w_t, tail_t)


def _permute_w_q_up(w_q_up):
    w = w_q_up.reshape(MLA_Q_RANK, MLA_HEADS, MLA_NOPE_DIM + MLA_ROPE_DIM)
    zero = jnp.zeros((MLA_Q_RANK, MLA_HEADS, MLA_QK_PAD - MLA_NOPE_DIM - MLA_ROPE_DIM), w.dtype)
    w = jnp.concatenate([w[:, :, :MLA_NOPE_DIM], zero, w[:, :, MLA_NOPE_DIM:]], axis=-1)
    return w.reshape(MLA_Q_RANK, MLA_HEADS * MLA_QK_PAD).astype(BF16)


def _split_w_kv_up(w_kv_up):
    w = w_kv_up.reshape(MLA_KV_RANK, MLA_HEADS, MLA_NOPE_DIM + HEAD_DIM)
    wkn = w[:, :, :MLA_NOPE_DIM].reshape(MLA_KV_RANK, MLA_HEADS * MLA_NOPE_DIM)
    wv = w[:, :, MLA_NOPE_DIM:].reshape(MLA_KV_RANK, MLA_HEADS * HEAD_DIM)
    return wkn.astype(BF16), wv.astype(BF16)


def _router_slices(w_grp, w_exp):
    w = jnp.concatenate([w_grp, w_exp, jnp.zeros((D_MODEL, LANES - N_GROUPS - N_EXPERTS), F32)], axis=1)
    hi = w.astype(BF16)
    mid = (w - hi.astype(F32)).astype(BF16)
    return jnp.concatenate([hi, mid], axis=1)


def _mixer(xb, nb, layer, w_in_s, q_lat_norm, w_q_up, kv_lat_norm, w_kv_up,
           cmp_pos_k, cmp_w1_k, cmp_w2_k, cmp_pos_v, cmp_w1_v, cmp_w2_v, consts):
    hproj = _inproj(xb, w_in_s, layer, consts["rope32"])
    wkn, wv = _split_w_kv_up(w_kv_up)
    q_mla, k_mla, v_mla = _mlaprep(hproj, q_lat_norm.reshape(1, -1), kv_lat_norm.reshape(1, -1),
                                   _permute_w_q_up(w_q_up), wkn, wv, consts["rope64"])
    wide = NSA_CMP_STRIDE * HEAD_DIM
    tk2 = hproj[:, BLK_NKC * LANES:(BLK_NKC + 1) * LANES].reshape(nb, N_CMP_PAD, wide)
    tv2 = hproj[:, BLK_NVC * LANES:(BLK_NVC + 1) * LANES].reshape(nb, N_CMP_PAD, wide)
    kc, vc = _compress(tk2, tv2, cmp_pos_k.reshape(2, wide), cmp_pos_v.reshape(2, wide),
                       cmp_w1_k.astype(BF16), cmp_w2_k.astype(BF16), cmp_w1_v.astype(BF16), cmp_w2_v.astype(BF16),
                       consts["rope_cmp"])
    o_mla = _flash(q_mla, k_mla, v_mla, consts["bias_causal"], nb=nb, heads=MLA_HEADS, dqk=MLA_QK_PAD,
                   dv=HEAD_DIM, q_blk0=0, k_blk0=0, v_blk0=0)
    o_dil = _flash(hproj, hproj, hproj, consts["bias_dil"], nb=nb, heads=DIL_HEADS, dqk=HEAD_DIM, dv=HEAD_DIM,
                   q_blk0=BLK_DQ, k_blk0=BLK_DK, v_blk0=BLK_DV)
    o_nsa = _nsa(hproj, kc, vc, consts["cover"], consts["expand"], consts["bias_win"], nb)
    return o_mla, o_dil, o_nsa


def _make_consts():
    pos = np.arange(SEQ)
    slot = np.arange(N_CMP_PAD)
    cmp_start = slot * NSA_CMP_STRIDE
    cmp_end = cmp_start + NSA_CMP_LEN - 1
    sel_start = np.arange(N_SEL_BLOCKS) * NSA_SEL_BLOCK
    cover = ((cmp_start[:, None] < sel_start[None, :] + NSA_SEL_BLOCK)
             & (cmp_start[:, None] + NSA_CMP_LEN > sel_start[None, :])
             & (slot[:, None] < N_CMP_PAD - 1)).astype(np.float32)
    expand = (np.arange(LANES)[:, None] == (pos // NSA_SEL_BLOCK)[None, :]).astype(np.float32)
    expand = expand.reshape(LANES, SEQ // NSA_T, NSA_T).transpose(1, 0, 2)
    win_diags = range(-(NSA_TQ // NSA_T - 1), -(-NSA_WINDOW // NSA_T) + 1)
    return {
        "rope32": _rope_tables(pos, ROT_DIM, True),
        "rope64": _rope_tables(pos, MLA_ROPE_DIM, False, KROPE_LANE0),
        "rope_cmp": _rope_tables(cmp_end, ROT_DIM, True),
        "bias_causal": _distance_bias(_causal_mult, (0, 1), FLASH_T),
        "bias_dil": _distance_bias(_dilated_mult, range(SEQ // FLASH_T), FLASH_T),
        "bias_win": _distance_bias(_window_mult, win_diags, NSA_T, NSA_TQ),
        "cover": jnp.asarray(cover.T, BF16),
        "expand": jnp.asarray(expand, BF16),
    }


def kernel(x, w_in, q_lat_norm, w_q_up, kv_lat_norm, w_kv_up, cmp_pos_k, cmp_w1_k, cmp_w2_k, cmp_pos_v, cmp_w1_v,
           cmp_w2_v, w_out, ln1_g, ln1_b, w_grp, w_exp, w_gate, w_up, w_down, ln2_g, ln2_b):
    nb, s, d = x.shape
    assert s == SEQ and d == D_MODEL
    t = nb * s
    consts = _make_consts()
    w_in_s = _shift_w_in(w_in)
    w_out_s = w_out.astype(BF16)
    xf = x.reshape(t, d)
    xb = xf.astype(BF16)
    for l in range(DEPTH):
        o_mla, o_dil, o_nsa = _mixer(xb, nb, l, w_in_s, q_lat_norm[l], w_q_up[l], kv_lat_norm[l], w_kv_up[l],
                                     cmp_pos_k[l], cmp_w1_k[l], cmp_w2_k[l], cmp_pos_v[l], cmp_w1_v[l],
                                     cmp_w2_v[l], consts)
        x1, route = _outproj(o_mla, o_dil, o_nsa, w_out_s, l, xf, ln1_g[l].reshape(1, d),
                             ln1_b[l].reshape(1, d), _router_slices(w_grp[l], w_exp[l]))
        blk_ex, first, nxt, nused, fill, pos = _dispatch_plan(route, t)
        pos3 = pos.reshape(t // ROW_TM, 1, MOE_TOP_K * ROW_TM)
        n_rows = (-(-t * MOE_TOP_K // MOE_ROWS) + N_EXPERTS) * MOE_ROWS
        yb = _experts(blk_ex, first, nxt, nused, _dispatch(fill, pos3, x1, n_rows), w_gate, w_up, w_down, l)
        xf, xb = _combine(pos3, yb, x1, route,
                          ln2_g[l].reshape(1, d), ln2_b[l].reshape(1, d))
    return xf.reshape(nb, s, d)
```

```python
import functools

import numpy as np
import jax
import jax.numpy as jnp
from jax import lax
from jax.experimental import pallas as pl
from jax.experimental.pallas import tpu as pltpu

F32 = jnp.float32
BF16 = jnp.bfloat16

D_MODEL = 2048
SEQ = 2048
DEPTH = 2
HEAD_DIM = 128
LANES = 128
MLA_HEADS = 4
DIL_HEADS = 6
NSA_HEADS = 6
MLA_Q_RANK = 384
MLA_KV_RANK = 128
MLA_NOPE_DIM = 128
MLA_ROPE_DIM = 64
MLA_QK_PAD = 256
DIL_PAIRS = ((128, 1), (512, 4), (2048, 16))
NSA_CMP_LEN = 32
NSA_CMP_STRIDE = 16
NSA_CMP_HIDDEN = 256
NSA_SEL_BLOCK = 64
NSA_TOP_N = 16
NSA_WINDOW = 512
NSA_BRANCHES = 3
NSA_FORCE_SCORE = 1e4
ROPE_THETA = 500000.0
ROT_DIM = HEAD_DIM // 4
N_GROUPS = 4
EXPERTS_PER_GROUP = 8
N_EXPERTS = N_GROUPS * EXPERTS_PER_GROUP
EXPERT_FF = 512
MOE_TOP_K = 2
MOE_ROWS = 256
DN_ALPHA = (2 * DEPTH) ** 0.25
LN_EPS = 1e-5
RMS_EPS = 1e-6
NEG_INF = -1e30

N_CMP_PAD = SEQ // NSA_CMP_STRIDE
N_SEL_BLOCKS = SEQ // NSA_SEL_BLOCK

BLK_QLAT = 0
BLK_KVLAT = 3
BLK_KROPE = 4
BLK_DQ = 5
BLK_DK = 11
BLK_DV = 17
BLK_NQ = 23
BLK_NKC = 29
BLK_NVC = 30
BLK_NKS = 31
BLK_NVS = 32
BLK_NKW = 33
BLK_NVW = 34
BLK_GATE = 35
N_IN_BLOCKS = 36
IN_COLS = 4434
IN_SHIFT_FROM = MLA_Q_RANK + MLA_KV_RANK - MLA_ROPE_DIM
IN_COLS_PAD = N_IN_BLOCKS * LANES
KROPE_LANE0 = LANES - MLA_ROPE_DIM

INPROJ_TM = 2048
INPROJ_SUB = 512
INPROJ_TN = 512
FLASH_T = 512
NSA_T = 256
NSA_TQ = 256
ROW_TM = 256
MLAPREP_TM = 512
OUTPROJ_TM = 512

VMEM_LIMIT = 56 * 1024 * 1024


def _cparams(sem):
    return pltpu.CompilerParams(dimension_semantics=sem, vmem_limit_bytes=VMEM_LIMIT)


def _rope_tables(pos, rot_dim, keep_rest, lane0=0):
    half = rot_dim // 2
    inv_freq = np.power(np.float32(ROPE_THETA), -np.arange(half, dtype=np.float32) / np.float32(half))
    ang = (pos.astype(np.float32)[:, None] * inv_freq[None, :]).astype(np.float32)
    cos, sin = np.cos(ang).astype(np.float32), np.sin(ang).astype(np.float32)
    n = pos.shape[0]
    c = np.full((n, LANES), 1.0 if keep_rest else 0.0, np.float32)
    s1 = np.zeros((n, LANES), np.float32)
    s2 = np.zeros((n, LANES), np.float32)
    c[:, lane0:lane0 + half] = cos
    c[:, lane0 + half:lane0 + rot_dim] = cos
    s1[:, lane0 + half:lane0 + rot_dim] = sin
    s2[:, lane0:lane0 + half] = -sin
    return jnp.asarray(c), jnp.asarray(s1), jnp.asarray(s2)


def _rope_lanes(a, c, s1, s2, half):
    return a * c + pltpu.roll(a, half, 1) * s1 + pltpu.roll(a, LANES - half, 1) * s2


def _distance_bias(mult_fn, diags, t, rows=None):
    d = np.asarray(diags, np.int64)[:, None, None] * t
    r = np.arange(t if rows is None else rows, dtype=np.int64)[None, :, None]
    c = np.arange(t, dtype=np.int64)[None, None, :]
    mult = mult_fn(d + r - c)
    return jnp.asarray(np.where(mult > 0, np.log(np.maximum(mult, 1).astype(np.float32)), NEG_INF).astype(np.float32))


def _causal_mult(dist):
    return (dist >= 0).astype(np.int64)


def _dilated_mult(dist):
    m = np.zeros_like(dist)
    for window, dil in DIL_PAIRS:
        m = m + ((dist >= 0) & (dist % dil == 0) & (dist <= (window // dil) * dil)).astype(np.int64)
    return m


def _window_mult(dist):
    return ((dist >= 0) & (dist <= NSA_WINDOW - 1)).astype(np.int64)


def _in_block(blk, lo, n):
    return jnp.logical_and(blk >= lo, blk < lo + n)


def _inproj_kernel(x_ref, w_ref, c_ref, s1_ref, s2_ref, o_ref, *, scale):
    j = pl.program_id(0)
    for hh in range(INPROJ_TM // INPROJ_SUB):
        rows = slice(hh * INPROJ_SUB, (hh + 1) * INPROJ_SUB)
        acc = _qk(x_ref[rows, :], w_ref[0])
        c, s1, s2 = c_ref[rows, :], s1_ref[rows, :], s2_ref[rows, :]
        for u in range(INPROJ_TN // LANES):
            blk = j * (INPROJ_TN // LANES) + u
            is_q = jnp.logical_or(_in_block(blk, BLK_DQ, DIL_HEADS), _in_block(blk, BLK_NQ, NSA_HEADS))
            is_rope = is_q | _in_block(blk, BLK_DK, DIL_HEADS) | (blk == BLK_NKS) | (blk == BLK_NKW)
            sc = jnp.where(is_q, scale, 1.0).astype(F32)
            a = acc[:, u * LANES:(u + 1) * LANES]
            r = _rope_lanes(a, jnp.where(is_rope, c, 1.0), jnp.where(is_rope, s1, 0.0),
                            jnp.where(is_rope, s2, 0.0), ROT_DIM // 2)
            o_ref[rows, u * LANES:(u + 1) * LANES] = (r * sc).astype(o_ref.dtype)


def _inproj(xb, w_in_s, layer, tabs):
    t = xb.shape[0]
    tm, tn = INPROJ_TM, INPROJ_TN
    nrow = SEQ // tm
    tab_spec = pl.BlockSpec((tm, LANES), lambda j, i: (i % nrow, 0))
    return pl.pallas_call(
        functools.partial(_inproj_kernel, scale=HEAD_DIM ** -0.5),
        out_shape=jax.ShapeDtypeStruct((t, IN_COLS_PAD), BF16),
        grid=(IN_COLS_PAD // tn, t // tm),
        in_specs=[pl.BlockSpec((tm, D_MODEL), lambda j, i: (i, 0)),
                  pl.BlockSpec((1, tn, D_MODEL), lambda j, i: (layer, j, 0)),
                  tab_spec, tab_spec, tab_spec],
        out_specs=pl.BlockSpec((tm, tn), lambda j, i: (i, j)),
        compiler_params=_cparams(("arbitrary", "arbitrary")),
        name="inproj",
    )(xb, w_in_s, *tabs)


def _rms(xf, g):
    return xf * lax.rsqrt(jnp.mean(jnp.square(xf), axis=-1, keepdims=True) + RMS_EPS) * g


def _mlaprep_kernel(ql_ref, kvl_ref, kr_ref, gq_ref, gkv_ref, wq_ref, wkn_ref, wv_ref,
                    c_ref, s1_ref, s2_ref, q_ref, k_ref, v_ref, *, scale):
    c, s1, s2 = c_ref[...], s1_ref[...], s2_ref[...]
    half = MLA_ROPE_DIM // 2
    qn = _rms(ql_ref[...].astype(F32), gq_ref[...]).astype(BF16)
    q = jnp.dot(qn, wq_ref[...], preferred_element_type=F32)
    for h in range(MLA_HEADS):
        lo = h * MLA_QK_PAD
        q_ref[:, lo:lo + LANES] = (q[:, lo:lo + LANES] * scale).astype(q_ref.dtype)
        a = q[:, lo + LANES:lo + 2 * LANES]
        q_ref[:, lo + LANES:lo + 2 * LANES] = (_rope_lanes(a, c, s1, s2, half) * scale).astype(q_ref.dtype)
    kvn = _rms(kvl_ref[...].astype(F32), gkv_ref[...]).astype(BF16)
    kn = jnp.dot(kvn, wkn_ref[...], preferred_element_type=F32)
    v_ref[...] = jnp.dot(kvn, wv_ref[...], preferred_element_type=F32).astype(v_ref.dtype)
    kpe = _rope_lanes(kr_ref[...].astype(F32), c, s1, s2, half).astype(k_ref.dtype)
    for h in range(MLA_HEADS):
        lo = h * MLA_QK_PAD
        k_ref[:, lo:lo + LANES] = kn[:, h * LANES:(h + 1) * LANES].astype(k_ref.dtype)
        k_ref[:, lo + LANES:lo + 2 * LANES] = kpe


def _mlaprep(hproj, gq, gkv, wq_p, wkn, wv, tabs):
    t = hproj.shape[0]
    tm = MLAPREP_TM
    nrow = SEQ // tm
    tab_spec = pl.BlockSpec((tm, LANES), lambda i: (i % nrow, 0))
    full = lambda shape: pl.BlockSpec(shape, lambda i: (0,) * len(shape))
    return pl.pallas_call(
        functools.partial(_mlaprep_kernel, scale=(MLA_NOPE_DIM + MLA_ROPE_DIM) ** -0.5),
        out_shape=(jax.ShapeDtypeStruct((t, MLA_HEADS * MLA_QK_PAD), BF16),
                   jax.ShapeDtypeStruct((t, MLA_HEADS * MLA_QK_PAD), BF16),
                   jax.ShapeDtypeStruct((t, MLA_HEADS * HEAD_DIM), BF16)),
        grid=(t // tm,),
        in_specs=[pl.BlockSpec((tm, MLA_Q_RANK), lambda i: (i, BLK_QLAT * LANES // MLA_Q_RANK)),
                  pl.BlockSpec((tm, LANES), lambda i: (i, BLK_KVLAT)),
                  pl.BlockSpec((tm, LANES), lambda i: (i, BLK_KROPE)),
                  full((1, MLA_Q_RANK)), full((1, MLA_KV_RANK)),
                  full((MLA_Q_RANK, MLA_HEADS * MLA_QK_PAD)),
                  full((MLA_KV_RANK, MLA_HEADS * MLA_NOPE_DIM)),
                  full((MLA_KV_RANK, MLA_HEADS * HEAD_DIM)),
                  tab_spec, tab_spec, tab_spec],
        out_specs=(pl.BlockSpec((tm, MLA_HEADS * MLA_QK_PAD), lambda i: (i, 0)),
                   pl.BlockSpec((tm, MLA_HEADS * MLA_QK_PAD), lambda i: (i, 0)),
                   pl.BlockSpec((tm, MLA_HEADS * HEAD_DIM), lambda i: (i, 0))),
        compiler_params=_cparams(("arbitrary",)),
        name="mlaprep",
    )(hproj, hproj, hproj, gq, gkv, wq_p, wkn, wv, *tabs)


def _gelu_tanh(x):
    return 0.5 * x * (1.0 + jnp.tanh(0.7978845608028654 * (x + 0.044715 * (x * x * x))))


def _compress_kernel(tk_ref, tv_ref, pek_ref, pev_ref, w1k_ref, w2k_ref, w1v_ref, w2v_ref,
                     c_ref, s1_ref, s2_ref, kc_ref, vc_ref):
    half_in = NSA_CMP_STRIDE * HEAD_DIM

    def comp(t_ref, pe_ref, w1_ref, w2_ref):
        t = t_ref[0].astype(F32)
        a = (t + pe_ref[0:1, :]).astype(BF16)
        b = (t + pe_ref[1:2, :]).astype(BF16)
        y0 = jnp.dot(a, w1_ref[0:half_in, :], preferred_element_type=F32)
        y1 = jnp.dot(b, w1_ref[half_in:2 * half_in, :], preferred_element_type=F32)
        hid = _gelu_tanh(y0 + pltpu.roll(y1, N_CMP_PAD - 1, 0))
        return jnp.dot(hid.astype(BF16), w2_ref[...], preferred_element_type=F32)

    kc = comp(tk_ref, pek_ref, w1k_ref, w2k_ref)
    kc_ref[0] = _rope_lanes(kc, c_ref[...], s1_ref[...], s2_ref[...], ROT_DIM // 2).astype(kc_ref.dtype)
    vc_ref[0] = comp(tv_ref, pev_ref, w1v_ref, w2v_ref).astype(vc_ref.dtype)


def _compress(tk2, tv2, pek, pev, w1k, w2k, w1v, w2v, tabs):
    nb = tk2.shape[0]
    wide = NSA_CMP_STRIDE * HEAD_DIM
    full = lambda shape: pl.BlockSpec(shape, lambda b: (0,) * len(shape))
    bspec = pl.BlockSpec((1, N_CMP_PAD, wide), lambda b: (b, 0, 0))
    ospec = pl.BlockSpec((1, N_CMP_PAD, HEAD_DIM), lambda b: (b, 0, 0))
    return pl.pallas_call(
        _compress_kernel,
        out_shape=(jax.ShapeDtypeStruct((nb, N_CMP_PAD, HEAD_DIM), BF16),) * 2,
        grid=(nb,),
        in_specs=[bspec, bspec, full((2, wide)), full((2, wide)),
                  full((2 * wide, NSA_CMP_HIDDEN)), full((NSA_CMP_HIDDEN, HEAD_DIM)),
                  full((2 * wide, NSA_CMP_HIDDEN)), full((NSA_CMP_HIDDEN, HEAD_DIM)),
                  full((N_CMP_PAD, LANES)), full((N_CMP_PAD, LANES)), full((N_CMP_PAD, LANES))],
        out_specs=(ospec, ospec),
        compiler_params=_cparams(("arbitrary",)),
        name="compress",
    )(tk2, tv2, pek, pev, w1k, w2k, w1v, w2v, *tabs)


def _qk(q, k):
    return lax.dot_general(q, k, (((1,), (1,)), ((), ())), preferred_element_type=F32)


def _lane_fold(x, op):
    out = x[:, 0:LANES]
    for u in range(1, x.shape[1] // LANES):
        out = op(out, x[:, u * LANES:(u + 1) * LANES])
    return out


def _split3(x):
    hi = x.astype(BF16)
    r1 = x - hi.astype(F32)
    mid = r1.astype(BF16)
    lo = (r1 - mid.astype(F32)).astype(BF16)
    return hi, mid, lo


def _attend(q, k_ref, v_ref, bias_fn, j_lo, n_tiles, s_scr, *, tk, groups=1):
    rows = q.shape[0]
    dv = v_ref.shape[-1]

    def pass1(t, mx):
        off = pl.multiple_of((j_lo + t) * tk, tk)
        s = _qk(q, k_ref[pl.ds(off, tk), :])
        bias = bias_fn(j_lo + t)
        if groups > 1:
            s = (s.reshape(groups, rows // groups, tk) + bias[None]).reshape(rows, tk)
        else:
            s = s + bias
        s_scr[t] = s
        return jnp.maximum(mx, _lane_fold(s, jnp.maximum))

    mx = lax.fori_loop(0, n_tiles, pass1, jnp.full((rows, LANES), NEG_INF, F32))
    m = jnp.broadcast_to(jnp.max(mx, axis=-1, keepdims=True), (rows, LANES))

    def pass2(t, carry):
        lsum, acc = carry
        off = pl.multiple_of((j_lo + t) * tk, tk)
        s = s_scr[t]
        parts = [jnp.exp(s[:, u * LANES:(u + 1) * LANES] - m) for u in range(tk // LANES)]
        for p in parts:
            lsum = lsum + p
        pb = jnp.concatenate([p.astype(BF16) for p in parts], axis=1)
        acc = acc + jnp.dot(pb, v_ref[pl.ds(off, tk), :], preferred_element_type=F32)
        return lsum, acc

    lsum, acc = lax.fori_loop(0, n_tiles, pass2, (jnp.zeros((rows, LANES), F32), jnp.zeros((rows, dv), F32)))
    return acc / jnp.sum(lsum, axis=-1, keepdims=True)


def _flash_kernel(q_ref, k_ref, v_ref, bias_ref, o_ref, s_scr):
    i = pl.program_id(2)
    last = bias_ref.shape[0] - 1
    out = _attend(q_ref[...], k_ref, v_ref, lambda j: bias_ref[jnp.minimum(i - j, last)], 0, i + 1, s_scr,
                  tk=FLASH_T)
    o_ref[...] = out.astype(o_ref.dtype)


def _flash(q_arr, k_arr, v_arr, bias, *, nb, heads, dqk, dv, q_blk0, k_blk0, v_blk0):
    t = FLASH_T
    nq = SEQ // t
    return pl.pallas_call(
        _flash_kernel,
        out_shape=jax.ShapeDtypeStruct((nb * SEQ, heads * dv), BF16),
        grid=(nb, heads, nq),
        in_specs=[pl.BlockSpec((t, dqk), lambda b, h, i: (b * nq + i, q_blk0 + h)),
                  pl.BlockSpec((SEQ, dqk), lambda b, h, i: (b, k_blk0 + h)),
                  pl.BlockSpec((SEQ, dv), lambda b, h, i: (b, v_blk0 + h)),
                  pl.BlockSpec(bias.shape, lambda b, h, i: (0, 0, 0))],
        out_specs=pl.BlockSpec((t, dv), lambda b, h, i: (b * nq + i, h)),
        scratch_shapes=[pltpu.VMEM((nq, t, t), F32)],
        compiler_params=_cparams(("arbitrary", "arbitrary", "arbitrary")),
        name="flash",
    )(q_arr, k_arr, v_arr, bias)


def _nsa_kernel(q0_ref, q1_ref, q2_ref, q3_ref, q4_ref, q5_ref, kc_ref, vc_ref, ks_ref, vs_ref, kw_ref, vw_ref,
                gate_ref, cover_ref, expand_ref, wbias_ref, o_ref, bias_scr, s_scr):
    i = pl.program_id(1)
    tq, tk = NSA_TQ, NSA_T
    per = tq // tk
    nh = NSA_HEADS
    rows = nh * tq
    pos = i * tq + lax.broadcasted_iota(jnp.int32, (tq, 1), 0)
    q = jnp.concatenate([r[...] for r in (q0_ref, q1_ref, q2_ref, q3_ref, q4_ref, q5_ref)], axis=0)

    cidx = lax.broadcasted_iota(jnp.int32, (tq, N_CMP_PAD), 1)
    valid_c = (cidx * NSA_CMP_STRIDE + (NSA_CMP_LEN - 1) <= pos) & (cidx < N_CMP_PAD - 1)
    valid_cf = valid_c.astype(F32)
    cbias = jnp.where(valid_c, 0.0, NEG_INF).astype(F32)
    s = (_qk(q, kc_ref[0]).reshape(nh, tq, N_CMP_PAD) + cbias[None]).reshape(rows, N_CMP_PAD)
    p = (jnp.exp(s - jnp.max(s, axis=-1, keepdims=True)).reshape(nh, tq, N_CMP_PAD) * valid_cf[None])
    p = p.reshape(rows, N_CMP_PAD)
    l = jnp.sum(p, axis=-1, keepdims=True)
    p = p / jnp.where(l > 0.0, l, 1.0)
    o_cmp = jnp.dot(p.astype(BF16), vc_ref[0], preferred_element_type=F32)
    psum = jnp.sum(p.reshape(nh, tq, N_CMP_PAD), axis=0)

    cover_t = cover_ref[...]
    imp = sum(jnp.dot(cover_t, part, preferred_element_type=F32) for part in _split3(psum.T))

    pos_t = i * tq + lax.broadcasted_iota(jnp.int32, (N_SEL_BLOCKS, tq), 1)
    jidx = lax.broadcasted_iota(jnp.int32, (N_SEL_BLOCKS, tq), 0)
    qblk = pos_t // NSA_SEL_BLOCK
    valid_s = jidx * NSA_SEL_BLOCK <= pos_t
    forced = (jidx == 0) | (jidx == qblk) | (jidx == qblk - 1)
    score = jnp.where(valid_s, jnp.where(forced, NSA_FORCE_SCORE, imp), -1.0)
    rank = jnp.zeros((N_SEL_BLOCKS, tq), jnp.int32)
    for ii in range(N_SEL_BLOCKS):
        si = score[ii:ii + 1, :]
        rank = rank + ((si > score) | ((si == score) & (ii < jidx))).astype(jnp.int32)
    sel_t = (rank < NSA_TOP_N).astype(F32)
    sel = jnp.concatenate([sel_t, jnp.zeros((LANES - N_SEL_BLOCKS, tq), F32)], axis=0).T.astype(BF16)

    n_causal = per * (i + 1)

    def fill(j, carry):
        hit = jnp.dot(sel, expand_ref[j], preferred_element_type=F32)
        kpos = j * tk + lax.broadcasted_iota(jnp.int32, (tq, tk), 1)
        bias_scr[j] = jnp.where((hit > 0.5) & (kpos <= pos), 0.0, NEG_INF).astype(F32)
        return carry

    lax.fori_loop(0, n_causal, fill, 0)

    o_slc = _attend(q, ks_ref, vs_ref, lambda j: bias_scr[j], 0, n_causal, s_scr, tk=tk, groups=nh)
    far = wbias_ref.shape[0] - per
    w_lo = jnp.maximum(per * i - far, 0)
    o_win = _attend(q, kw_ref, vw_ref, lambda j: wbias_ref[per * i - j + per - 1], w_lo, n_causal - w_lo, s_scr,
                    tk=tk, groups=nh)

    gates = jax.nn.sigmoid(gate_ref[...].astype(F32))
    for h in range(nh):
        sl = slice(h * tq, (h + 1) * tq)
        g0 = NSA_BRANCHES * h
        out = (gates[:, g0:g0 + 1] * o_cmp[sl] + gates[:, g0 + 1:g0 + 2] * o_slc[sl]
               + gates[:, g0 + 2:g0 + 3] * o_win[sl])
        o_ref[:, h * LANES:(h + 1) * LANES] = out.astype(o_ref.dtype)


def _nsa(hproj, kc, vc, cover, expand, wbias, nb):
    t = NSA_TQ
    nq = SEQ // t
    nk = SEQ // NSA_T
    wide = NSA_HEADS * HEAD_DIM
    q_spec = lambda h: pl.BlockSpec((t, LANES), lambda b, i: (b * nq + i, BLK_NQ + h))
    kv_spec = lambda blk: pl.BlockSpec((SEQ, LANES), lambda b, i: (b, blk))
    cspec = pl.BlockSpec((1, N_CMP_PAD, HEAD_DIM), lambda b, i: (b, 0, 0))
    full = lambda shape: pl.BlockSpec(shape, lambda b, i: (0,) * len(shape))
    return pl.pallas_call(
        _nsa_kernel,
        out_shape=jax.ShapeDtypeStruct((nb * SEQ, wide), BF16),
        grid=(nb, nq),
        in_specs=[q_spec(h) for h in range(NSA_HEADS)]
        + [cspec, cspec, kv_spec(BLK_NKS), kv_spec(BLK_NVS), kv_spec(BLK_NKW), kv_spec(BLK_NVW),
           pl.BlockSpec((t, LANES), lambda b, i: (b * nq + i, BLK_GATE)),
           full(cover.shape), full(expand.shape), full(wbias.shape)],
        out_specs=pl.BlockSpec((t, wide), lambda b, i: (b * nq + i, 0)),
        scratch_shapes=[pltpu.VMEM((nk, t, NSA_T), F32), pltpu.VMEM((nk, NSA_HEADS * t, NSA_T), F32)],
        compiler_params=_cparams(("arbitrary", "arbitrary")),
        name="nsa",
    )(*([hproj] * NSA_HEADS), kc, vc, hproj, hproj, hproj, hproj, hproj, cover, expand, wbias)


def _layer_norm(y, g, b):
    mu = jnp.mean(y, axis=-1, keepdims=True)
    var = jnp.mean(jnp.square(y - mu), axis=-1, keepdims=True)
    return (y - mu) * lax.rsqrt(var + LN_EPS) * g + b


def _lane_min(x):
    return jnp.min(x, axis=-1, keepdims=True)


def _lane_max(x):
    return jnp.max(x, axis=-1, keepdims=True)


def _route(logits):
    lane = lax.broadcasted_iota(jnp.int32, logits.shape, 1)
    lane_f = lane.astype(F32)
    far = float(LANES)
    is_grp = lane < N_GROUPS
    lg = jnp.where(is_grp, logits, NEG_INF)
    eg = jnp.where(is_grp, jnp.exp(lg - _lane_max(lg)), 0.0)
    prob = eg / jnp.sum(eg, axis=-1, keepdims=True)
    p_g = _lane_max(prob)
    g_idx = _lane_min(jnp.where(is_grp & (prob == p_g), lane_f, far))
    e_lo = N_GROUPS + EXPERTS_PER_GROUP * g_idx
    in_grp = (lane_f >= e_lo) & (lane_f < e_lo + EXPERTS_PER_GROUP)
    le = jnp.where(in_grp, logits, NEG_INF)
    v1 = _lane_max(le)
    i1 = _lane_min(jnp.where(in_grp & (le == v1), lane_f, far))
    rest = in_grp & (lane_f != i1)
    le2 = jnp.where(rest, logits, NEG_INF)
    v2 = _lane_max(le2)
    i2 = _lane_min(jnp.where(rest & (le2 == v2), lane_f, far))
    e21 = jnp.exp(v2 - v1)
    den = 1.0 + e21
    gate1 = p_g * (1.0 / den)
    gate2 = p_g * (e21 / den)
    out = jnp.where(lane == 0, i1 - N_GROUPS, 0.0)
    out = jnp.where(lane == 1, i2 - N_GROUPS, out)
    out = jnp.where(lane == 2, gate1, out)
    out = jnp.where(lane == 3, gate2, out)
    return out


def _outproj_kernel(om_ref, od_ref, on_ref, w_ref, x_ref, g_ref, b_ref, wr_ref, x1_ref, route_ref):
    n_mla = MLA_HEADS * HEAD_DIM
    n_dil = DIL_HEADS * HEAD_DIM
    for hh in range(OUTPROJ_TM // ROW_TM):
        rows = slice(hh * ROW_TM, (hh + 1) * ROW_TM)
        mix = jnp.dot(om_ref[rows, :], w_ref[0, 0:n_mla, :], preferred_element_type=F32)
        mix = mix + jnp.dot(od_ref[rows, :], w_ref[0, n_mla:n_mla + n_dil, :], preferred_element_type=F32)
        mix = mix + jnp.dot(on_ref[rows, :], w_ref[0, n_mla + n_dil:, :], preferred_element_type=F32)
        x1 = _layer_norm(DN_ALPHA * x_ref[rows, :] + mix, g_ref[...], b_ref[...])
        x1_ref[rows, :] = x1
        xh = x1.astype(BF16)
        xm = (x1 - xh.astype(F32)).astype(BF16)
        both = jnp.dot(xh, wr_ref[...], preferred_element_type=F32)
        logits = both[:, 0:LANES] + (both[:, LANES:2 * LANES]
                                     + jnp.dot(xm, wr_ref[:, 0:LANES], preferred_element_type=F32))
        route_ref[rows, :] = _route(logits)


def _outproj(o_mla, o_dil, o_nsa, w_out_s, layer, x, g, b, wr3):
    t = x.shape[0]
    tm = OUTPROJ_TM
    full = lambda shape: pl.BlockSpec(shape, lambda i: (0,) * len(shape))
    row = lambda w: pl.BlockSpec((tm, w), lambda i: (i, 0))
    return pl.pallas_call(
        _outproj_kernel,
        out_shape=(jax.ShapeDtypeStruct((t, D_MODEL), F32), jax.ShapeDtypeStruct((t, LANES), F32)),
        grid=(t // tm,),
        in_specs=[row(o_mla.shape[1]), row(o_dil.shape[1]), row(o_nsa.shape[1]),
                  pl.BlockSpec((1,) + w_out_s.shape[1:], lambda i: (layer, 0, 0)),
                  row(D_MODEL), full((1, D_MODEL)), full((1, D_MODEL)), full(wr3.shape)],
        out_specs=(row(D_MODEL), row(LANES)),
        compiler_params=_cparams(("arbitrary",)),
        name="outproj",
    )(o_mla, o_dil, o_nsa, w_out_s, x, g, b, wr3)


def _dispatch_kernel(fill_ref, pos_ref, x1_ref, xb_hbm, stage, zbuf, sem, zsem):
    i = pl.program_id(0)
    n = pl.num_programs(0)
    tm = ROW_TM
    slot = i % 2

    def drain(s):
        for _ in range(MOE_TOP_K):
            pltpu.make_async_copy(stage.at[s], xb_hbm.at[pl.ds(0, tm)], sem.at[s]).wait()

    def zero_copy(e):
        start = pl.multiple_of(fill_ref[e], MOE_ROWS)
        return pltpu.make_async_copy(zbuf, xb_hbm.at[pl.ds(start, MOE_ROWS)], zsem)

    @pl.when(i == 0)
    def _():
        zbuf[...] = jnp.zeros_like(zbuf)
        for e in range(N_EXPERTS):
            @pl.when(fill_ref[e] >= 0)
            def _():
                zero_copy(e).start()
        for e in range(N_EXPERTS):
            @pl.when(fill_ref[e] >= 0)
            def _():
                zero_copy(e).wait()

        def tail_copy(b):
            start = pl.multiple_of(b * MOE_ROWS, MOE_ROWS)
            return pltpu.make_async_copy(zbuf, xb_hbm.at[pl.ds(start, MOE_ROWS)], zsem)

        n_blocks = xb_hbm.shape[0] // MOE_ROWS
        lax.fori_loop(fill_ref[N_EXPERTS], n_blocks, lambda b, c: (tail_copy(b).start(), c)[1], 0)
        lax.fori_loop(fill_ref[N_EXPERTS], n_blocks, lambda b, c: (tail_copy(b).wait(), c)[1], 0)

    @pl.when(i >= 2)
    def _():
        drain(slot)

    stage[slot] = x1_ref[...]

    for r in range(tm):
        for k in range(MOE_TOP_K):
            p = pos_ref[0, 0, MOE_TOP_K * r + k]
            pltpu.make_async_copy(stage.at[slot, pl.ds(r, 1)], xb_hbm.at[pl.ds(p, 1)],
                                  sem.at[slot]).start(priority=k % 2)

    @pl.when(i == n - 1)
    def _():
        drain(1 - slot)
        drain(slot)


def _dispatch(fill, pos3, x1, n_rows):
    t = x1.shape[0]
    tm = ROW_TM
    grid_spec = pltpu.PrefetchScalarGridSpec(
        num_scalar_prefetch=1,
        grid=(t // tm,),
        in_specs=[pl.BlockSpec((1, 1, MOE_TOP_K * tm), lambda i, fl: (i, 0, 0), memory_space=pltpu.SMEM),
                  pl.BlockSpec((tm, D_MODEL), lambda i, fl: (i, 0))],
        out_specs=pl.BlockSpec(memory_space=pl.ANY),
        scratch_shapes=[pltpu.VMEM((2, tm, D_MODEL), F32), pltpu.VMEM((MOE_ROWS, D_MODEL), F32),
                        pltpu.SemaphoreType.DMA((2,)), pltpu.SemaphoreType.DMA(())],
    )
    return pl.pallas_call(
        _dispatch_kernel,
        out_shape=jax.ShapeDtypeStruct((n_rows, D_MODEL), F32),
        grid_spec=grid_spec,
        compiler_params=_cparams(("arbitrary",)),
        name="dispatch",
    )(fill, pos3, x1)


def _experts_kernel(blk_ex_ref, first_ref, nxt_ref, nused_ref, x_ref, wg_hbm, wu_hbm, wd_hbm, y_ref,
                    wg_f, wu_f, wd_f, wg_b, wu_b, wd_b, sem, *, layer):
    b = pl.program_id(0)
    nused = nused_ref[0]

    def weight_copies(e, s):
        return (pltpu.make_async_copy(wg_hbm.at[layer, e], wg_f.at[s], sem.at[s]),
                pltpu.make_async_copy(wu_hbm.at[layer, e], wu_f.at[s], sem.at[s]),
                pltpu.make_async_copy(wd_hbm.at[layer, e], wd_f.at[s], sem.at[s]))

    @pl.when(b == 0)
    def _():
        for cp in weight_copies(blk_ex_ref[0], 0):
            cp.start()

    @pl.when(b < nused)
    def _():
        first = first_ref[b]

        @pl.when(first > 0)
        def _():
            s = first - 1
            for cp in weight_copies(blk_ex_ref[b], s):
                cp.wait()

            @pl.when(nxt_ref[b] >= 0)
            def _():
                for cp in weight_copies(nxt_ref[b], 1 - s):
                    cp.start()

            wg_b[...] = wg_f[s].astype(BF16)
            wu_b[...] = wu_f[s].astype(BF16)
            wd_b[...] = wd_f[s].astype(BF16)

        rows = x_ref[...].astype(BF16)
        gate = jnp.dot(rows, wg_b[...], preferred_element_type=F32)
        up = jnp.dot(rows, wu_b[...], preferred_element_type=F32)
        hid = (gate * jax.nn.sigmoid(gate) * up).astype(BF16)
        y_ref[...] = jnp.dot(hid, wd_b[...], preferred_element_type=F32)

    @pl.when(b >= nused)
    def _():
        y_ref[...] = jnp.zeros_like(y_ref)


def _experts(blk_ex, first, nxt, nused, xb_rows, w_gate, w_up, w_down, layer):
    n_blocks = xb_rows.shape[0] // MOE_ROWS
    hbm = pl.BlockSpec(memory_space=pl.ANY)
    grid_spec = pltpu.PrefetchScalarGridSpec(
        num_scalar_prefetch=4,
        grid=(n_blocks,),
        in_specs=[pl.BlockSpec((MOE_ROWS, D_MODEL), lambda b, ex, fi, nx, nu: (jnp.minimum(b, nu[0] - 1), 0)),
                  hbm, hbm, hbm],
        out_specs=pl.BlockSpec((MOE_ROWS, D_MODEL), lambda b, ex, fi, nx, nu: (b, 0)),
        scratch_shapes=[pltpu.VMEM((2, D_MODEL, EXPERT_FF), F32), pltpu.VMEM((2, D_MODEL, EXPERT_FF), F32),
                        pltpu.VMEM((2, EXPERT_FF, D_MODEL), F32),
                        pltpu.VMEM((D_MODEL, EXPERT_FF), BF16), pltpu.VMEM((D_MODEL, EXPERT_FF), BF16),
                        pltpu.VMEM((EXPERT_FF, D_MODEL), BF16), pltpu.SemaphoreType.DMA((2,))],
    )
    return pl.pallas_call(
        functools.partial(_experts_kernel, layer=layer),
        out_shape=jax.ShapeDtypeStruct((n_blocks * MOE_ROWS, D_MODEL), F32),
        grid_spec=grid_spec,
        compiler_params=_cparams(("arbitrary",)),
        name="experts",
    )(blk_ex, first, nxt, nused, xb_rows, w_gate, w_up, w_down)


def _combine_kernel(pos_cur_ref, pos_nxt_ref, y_hbm, x1_ref, route_ref, g_ref, b_ref, o_ref, ob_ref, ybuf, sem):
    i = pl.program_id(0)
    n = pl.num_programs(0)
    tm = ROW_TM

    def issue(pos_ref, slot):
        for r in range(tm):
            for k in range(MOE_TOP_K):
                p = pos_ref[0, 0, MOE_TOP_K * r + k]
                pltpu.make_async_copy(y_hbm.at[pl.ds(p, 1)], ybuf.at[slot, k, pl.ds(r, 1)], sem.at[slot]).start()

    def wait(slot):
        for k in range(MOE_TOP_K):
            pltpu.make_async_copy(y_hbm.at[pl.ds(0, tm)], ybuf.at[slot, k], sem.at[slot]).wait()

    @pl.when(i == 0)
    def _():
        issue(pos_cur_ref, 0)

    for cur in range(2):
        @pl.when(i % 2 == cur)
        def _():
            wait(cur)
            issue(pos_nxt_ref, 1 - cur)
            route = route_ref[...]
            ffn = route[:, 2:3] * ybuf[cur, 0] + route[:, 3:4] * ybuf[cur, 1]
            x2 = _layer_norm(DN_ALPHA * x1_ref[...] + ffn, g_ref[...], b_ref[...])
            o_ref[...] = x2
            ob_ref[...] = x2.astype(ob_ref.dtype)

    @pl.when(i == n - 1)
    def _():
        wait(1 - i % 2)


def _combine(pos3, yb, x1, route, g, b):
    t = x1.shape[0]
    tm = ROW_TM
    nt = t // tm
    pos_spec = lambda shift: pl.BlockSpec((1, 1, MOE_TOP_K * tm), lambda i: (jnp.minimum(i + shift, nt - 1), 0, 0),
                                          memory_space=pltpu.SMEM)
    row = lambda w: pl.BlockSpec((tm, w), lambda i: (i, 0))
    full = lambda shape: pl.BlockSpec(shape, lambda i: (0,) * len(shape))
    return pl.pallas_call(
        _combine_kernel,
        out_shape=(jax.ShapeDtypeStruct((t, D_MODEL), F32), jax.ShapeDtypeStruct((t, D_MODEL), BF16)),
        grid=(nt,),
        in_specs=[pos_spec(0), pos_spec(1), pl.BlockSpec(memory_space=pl.ANY), row(D_MODEL), row(LANES),
                  full((1, D_MODEL)), full((1, D_MODEL))],
        out_specs=(row(D_MODEL), row(D_MODEL)),
        scratch_shapes=[pltpu.VMEM((2, MOE_TOP_K, tm, D_MODEL), F32), pltpu.SemaphoreType.DMA((2,))],
        compiler_params=_cparams(("arbitrary",)),
        name="combine",
    )(pos3, pos3, yb, x1, route, g, b)


def _dispatch_plan(route, t):
    m = t * MOE_TOP_K
    n_blocks = -(-m // MOE_ROWS) + N_EXPERTS
    chunk = MOE_ROWS
    ex = route[:, 0:MOE_TOP_K].reshape(m // chunk, chunk, 1)
    onehot = (ex == jnp.arange(N_EXPERTS, dtype=F32)[None, None, :]).astype(F32)
    tri = jnp.asarray(np.tril(np.ones((chunk, chunk), np.float32)))
    within = jnp.einsum('ij,cje->cie', tri, onehot)
    totals = jnp.sum(onehot, axis=1)
    before = jnp.cumsum(totals, axis=0) - totals
    counts = jnp.sum(totals, axis=0)
    padded = jnp.ceil(counts / MOE_ROWS) * MOE_ROWS
    pad_end = jnp.cumsum(padded)
    pad_start = pad_end - padded
    row = jnp.sum(onehot * (within - 1.0 + before[:, None, :] + pad_start[None, None, :]), axis=-1)
    pos = row.astype(jnp.int32).reshape(m)
    blk_start = jnp.arange(n_blocks, dtype=F32) * MOE_ROWS
    blk_ex = jnp.minimum(jnp.sum((pad_end[None, :] <= blk_start[:, None]).astype(jnp.int32), axis=1),
                         N_EXPERTS - 1).astype(jnp.int32)
    nused = (pad_end[-1] / MOE_ROWS).astype(jnp.int32).reshape(1)
    fill = jnp.concatenate([jnp.where(padded > 0, pad_end - MOE_ROWS, -1.0).astype(jnp.int32), nused])
    nonempty = padded > 0
    slot_e = (jnp.cumsum(nonempty.astype(jnp.int32)) - 1) % 2
    expert_ids = jnp.arange(N_EXPERTS, dtype=jnp.int32)
    later = lax.cummin(jnp.where(nonempty, expert_ids, N_EXPERTS), axis=0, reverse=True)
    nxt_e = jnp.concatenate([later[1:], jnp.full((1,), N_EXPERTS, jnp.int32)])
    nxt_e = jnp.where(nxt_e >= N_EXPERTS, -1, nxt_e)
    is_first = (blk_start == pad_start[blk_ex]) & (jnp.arange(n_blocks) < nused[0])
    first = jnp.where(is_first, 1 + slot_e[blk_ex], 0).astype(jnp.int32)
    return blk_ex, first, nxt_e[blk_ex].astype(jnp.int32), nused, fill, pos


W_HALF = LANES // 2


def _shift_kernel(a_ref, b_ref, tail_ref, o_ref):
    m = pl.program_id(0)
    n_src = IN_COLS // LANES
    nl = o_ref.shape[0]
    d = o_ref.shape[2]
    nc = d // LANES
    a2, b2, t2 = a_ref, b_ref, tail_ref

    def slab(ref4, col0, c, l):
        return ref4[pl.ds(col0, W_HALF), c, l, :].astype(o_ref.dtype)

    def copy(ref_top, top0, ref_bot, bot0):
        for l in range(nl):
            for c in range(nc):
                o_ref[l, 0:W_HALF, c * LANES:(c + 1) * LANES] = slab(ref_top, top0, c, l)
                if ref_bot is None:
                    o_ref[l, W_HALF:LANES, c * LANES:(c + 1) * LANES] = jnp.zeros((W_HALF, LANES), o_ref.dtype)
                else:
                    o_ref[l, W_HALF:LANES, c * LANES:(c + 1) * LANES] = slab(ref_bot, bot0, c, l)

    @pl.when(m < n_src)
    def _():
        copy(a2, 0, b2, 0)

    @pl.when(m == n_src)
    def _():
        copy(a2, 0, t2, 0)

    @pl.when(m == n_src + 1)
    def _():
        copy(t2, W_HALF, None, 0)


def _shift_w_in(w_in):
    nl, d, _ = w_in.shape
    n_src = IN_COLS // LANES
    n_keep = (IN_SHIFT_FROM + MLA_ROPE_DIM) // LANES
    last = n_src * (LANES // W_HALF) - 1
    nc = d // LANES
    view = lambda w: jnp.transpose(w.reshape(nl, nc, LANES, w.shape[-1]), (3, 1, 0, 2))
    w_t = view(w_in)
    tail_t = view(jnp.pad(w_in[:, :, n_src * LANES:], ((0, 0), (0, 0), (0, (n_src + 1) * LANES - IN_COLS))))
    first = lambda m: jnp.where(m < n_keep, 2 * m, 2 * m - 1)
    return pl.pallas_call(
        _shift_kernel,
        out_shape=jax.ShapeDtypeStruct((nl, IN_COLS_PAD, d), BF16),
        grid=(N_IN_BLOCKS,),
        in_specs=[pl.BlockSpec((W_HALF, nc, nl, LANES), lambda m: (jnp.minimum(first(m), last), 0, 0, 0)),
                  pl.BlockSpec((W_HALF, nc, nl, LANES), lambda m: (jnp.minimum(first(m) + 1, last), 0, 0, 0)),
                  pl.BlockSpec((LANES, nc, nl, LANES), lambda m: (0, 0, 0, 0))],
        out_specs=pl.BlockSpec((nl, LANES, d), lambda m: (0, m, 0)),
        compiler_params=_cparams(("arbitrary",)),
        name="shiftw",
    )(w_t, w_t, tail_t)


def _permute_w_q_up(w_q_up):
    w = w_q_up.reshape(MLA_Q_RANK, MLA_HEADS, MLA_NOPE_DIM + MLA_ROPE_DIM)
    zero = jnp.zeros((MLA_Q_RANK, MLA_HEADS, MLA_QK_PAD - MLA_NOPE_DIM - MLA_ROPE_DIM), w.dtype)
    w = jnp.concatenate([w[:, :, :MLA_NOPE_DIM], zero, w[:, :, MLA_NOPE_DIM:]], axis=-1)
    return w.reshape(MLA_Q_RANK, MLA_HEADS * MLA_QK_PAD).astype(BF16)


def _split_w_kv_up(w_kv_up):
    w = w_kv_up.reshape(MLA_KV_RANK, MLA_HEADS, MLA_NOPE_DIM + HEAD_DIM)
    wkn = w[:, :, :MLA_NOPE_DIM].reshape(MLA_KV_RANK, MLA_HEADS * MLA_NOPE_DIM)
    wv = w[:, :, MLA_NOPE_DIM:].reshape(MLA_KV_RANK, MLA_HEADS * HEAD_DIM)
    return wkn.astype(BF16), wv.astype(BF16)


def _router_slices(w_grp, w_exp):
    w = jnp.concatenate([w_grp, w_exp, jnp.zeros((D_MODEL, LANES - N_GROUPS - N_EXPERTS), F32)], axis=1)
    hi = w.astype(BF16)
    mid = (w - hi.astype(F32)).astype(BF16)
    return jnp.concatenate([hi, mid], axis=1)


def _mixer(xb, nb, layer, w_in_s, q_lat_norm, w_q_up, kv_lat_norm, w_kv_up,
           cmp_pos_k, cmp_w1_k, cmp_w2_k, cmp_pos_v, cmp_w1_v, cmp_w2_v, consts):
    hproj = _inproj(xb, w_in_s, layer, consts["rope32"])
    wkn, wv = _split_w_kv_up(w_kv_up)
    q_mla, k_mla, v_mla = _mlaprep(hproj, q_lat_norm.reshape(1, -1), kv_lat_norm.reshape(1, -1),
                                   _permute_w_q_up(w_q_up), wkn, wv, consts["rope64"])
    wide = NSA_CMP_STRIDE * HEAD_DIM
    tk2 = hproj[:, BLK_NKC * LANES:(BLK_NKC + 1) * LANES].reshape(nb, N_CMP_PAD, wide)
    tv2 = hproj[:, BLK_NVC * LANES:(BLK_NVC + 1) * LANES].reshape(nb, N_CMP_PAD, wide)
    kc, vc = _compress(tk2, tv2, cmp_pos_k.reshape(2, wide), cmp_pos_v.reshape(2, wide),
                       cmp_w1_k.astype(BF16), cmp_w2_k.astype(BF16), cmp_w1_v.astype(BF16), cmp_w2_v.astype(BF16),
                       consts["rope_cmp"])
    o_mla = _flash(q_mla, k_mla, v_mla, consts["bias_causal"], nb=nb, heads=MLA_HEADS, dqk=MLA_QK_PAD,
                   dv=HEAD_DIM, q_blk0=0, k_blk0=0, v_blk0=0)
    o_dil = _flash(hproj, hproj, hproj, consts["bias_dil"], nb=nb, heads=DIL_HEADS, dqk=HEAD_DIM, dv=HEAD_DIM,
                   q_blk0=BLK_DQ, k_blk0=BLK_DK, v_blk0=BLK_DV)
    o_nsa = _nsa(hproj, kc, vc, consts["cover"], consts["expand"], consts["bias_win"], nb)
    return o_mla, o_dil, o_nsa


def _make_consts():
    pos = np.arange(SEQ)
    slot = np.arange(N_CMP_PAD)
    cmp_start = slot * NSA_CMP_STRIDE
    cmp_end = cmp_start + NSA_CMP_LEN - 1
    sel_start = np.arange(N_SEL_BLOCKS) * NSA_SEL_BLOCK
    cover = ((cmp_start[:, None] < sel_start[None, :] + NSA_SEL_BLOCK)
             & (cmp_start[:, None] + NSA_CMP_LEN > sel_start[None, :])
             & (slot[:, None] < N_CMP_PAD - 1)).astype(np.float32)
    expand = (np.arange(LANES)[:, None] == (pos // NSA_SEL_BLOCK)[None, :]).astype(np.float32)
    expand = expand.reshape(LANES, SEQ // NSA_T, NSA_T).transpose(1, 0, 2)
    win_diags = range(-(NSA_TQ // NSA_T - 1), -(-NSA_WINDOW // NSA_T) + 1)
    return {
        "rope32": _rope_tables(pos, ROT_DIM, True),
        "rope64": _rope_tables(pos, MLA_ROPE_DIM, False, KROPE_LANE0),
        "rope_cmp": _rope_tables(cmp_end, ROT_DIM, True),
        "bias_causal": _distance_bias(_causal_mult, (0, 1), FLASH_T),
        "bias_dil": _distance_bias(_dilated_mult, range(SEQ // FLASH_T), FLASH_T),
        "bias_win": _distance_bias(_window_mult, win_diags, NSA_T, NSA_TQ),
        "cover": jnp.asarray(cover.T, BF16),
        "expand": jnp.asarray(expand, BF16),
    }


def kernel(x, w_in, q_lat_norm, w_q_up, kv_lat_norm, w_kv_up, cmp_pos_k, cmp_w1_k, cmp_w2_k, cmp_pos_v, cmp_w1_v,
           cmp_w2_v, w_out, ln1_g, ln1_b, w_grp, w_exp, w_gate, w_up, w_down, ln2_g, ln2_b):
    nb, s, d = x.shape
    assert s == SEQ and d == D_MODEL
    t = nb * s
    consts = _make_consts()
    w_in_s = _shift_w_in(w_in)
    w_out_s = w_out.astype(BF16)
    xf = x.reshape(t, d)
    xb = xf.astype(BF16)
    for l in range(DEPTH):
        o_mla, o_dil, o_nsa = _mixer(xb, nb, l, w_in_s, q_lat_norm[l], w_q_up[l], kv_lat_norm[l], w_kv_up[l],
                                     cmp_pos_k[l], cmp_w1_k[l], cmp_w2_k[l], cmp_pos_v[l], cmp_w1_v[l],
                                     cmp_w2_v[l], consts)
        x1, route = _outproj(o_mla, o_dil, o_nsa, w_out_s, l, xf, ln1_g[l].reshape(1, d),
                             ln1_b[l].reshape(1, d), _router_slices(w_grp[l], w_exp[l]))
        blk_ex, first, nxt, nused, fill, pos = _dispatch_plan(route, t)
        pos3 = pos.reshape(t // ROW_TM, 1, MOE_TOP_K * ROW_TM)
        n_rows = (-(-t * MOE_TOP_K // MOE_ROWS) + N_EXPERTS) * MOE_ROWS
        yb = _experts(blk_ex, first, nxt, nused, _dispatch(fill, pos3, x1, n_rows), w_gate, w_up, w_down, l)
        xf, xb = _combine(pos3, yb, x1, route,
                          ln2_g[l].reshape(1, d), ln2_b[l].reshape(1, d))
    return xf.reshape(nb, s, d)
```

```python
import functools

import numpy as np
import jax
import jax.numpy as jnp
from jax import lax
from jax.experimental import pallas as pl
from jax.experimental.pallas import tpu as pltpu

F32 = jnp.float32
BF16 = jnp.bfloat16

D_MODEL = 2048
SEQ = 2048
DEPTH = 2
HEAD_DIM = 128
LANES = 128
MLA_HEADS = 4
DIL_HEADS = 6
NSA_HEADS = 6
MLA_Q_RANK = 384
MLA_KV_RANK = 128
MLA_NOPE_DIM = 128
MLA_ROPE_DIM = 64
MLA_QK_PAD = 256
DIL_PAIRS = ((128, 1), (512, 4), (2048, 16))
NSA_CMP_LEN = 32
NSA_CMP_STRIDE = 16
NSA_CMP_HIDDEN = 256
NSA_SEL_BLOCK = 64
NSA_TOP_N = 16
NSA_WINDOW = 512
NSA_BRANCHES = 3
NSA_FORCE_SCORE = 1e4
ROPE_THETA = 500000.0
ROT_DIM = HEAD_DIM // 4
N_GROUPS = 4
EXPERTS_PER_GROUP = 8
N_EXPERTS = N_GROUPS * EXPERTS_PER_GROUP
EXPERT_FF = 512
MOE_TOP_K = 2
MOE_ROWS = 256
DN_ALPHA = (2 * DEPTH) ** 0.25
LN_EPS = 1e-5
RMS_EPS = 1e-6
NEG_INF = -1e30

N_CMP_PAD = SEQ // NSA_CMP_STRIDE
N_SEL_BLOCKS = SEQ // NSA_SEL_BLOCK

BLK_QLAT = 0
BLK_KVLAT = 3
BLK_KROPE = 4
BLK_DQ = 5
BLK_DK = 11
BLK_DV = 17
BLK_NQ = 23
BLK_NKC = 29
BLK_NVC = 30
BLK_NKS = 31
BLK_NVS = 32
BLK_NKW = 33
BLK_NVW = 34
BLK_GATE = 35
N_IN_BLOCKS = 36
IN_COLS = 4434
IN_SHIFT_FROM = MLA_Q_RANK + MLA_KV_RANK - MLA_ROPE_DIM
IN_COLS_PAD = N_IN_BLOCKS * LANES
KROPE_LANE0 = LANES - MLA_ROPE_DIM

INPROJ_TM = 2048
INPROJ_SUB = 512
INPROJ_TN = 512
FLASH_T = 512
NSA_T = 256
NSA_TQ = 256
ROW_TM = 256
MLAPREP_TM = 512
OUTPROJ_TM = 512

VMEM_LIMIT = 56 * 1024 * 1024


def _cparams(sem):
    return pltpu.CompilerParams(dimension_semantics=sem, vmem_limit_bytes=VMEM_LIMIT)


def _rope_tables(pos, rot_dim, keep_rest, lane0=0):
    half = rot_dim // 2
    inv_freq = np.power(np.float32(ROPE_THETA), -np.arange(half, dtype=np.float32) / np.float32(half))
    ang = (pos.astype(np.float32)[:, None] * inv_freq[None, :]).astype(np.float32)
    cos, sin = np.cos(ang).astype(np.float32), np.sin(ang).astype(np.float32)
    n = pos.shape[0]
    c = np.full((n, LANES), 1.0 if keep_rest else 0.0, np.float32)
    s1 = np.zeros((n, LANES), np.float32)
    s2 = np.zeros((n, LANES), np.float32)
    c[:, lane0:lane0 + half] = cos
    c[:, lane0 + half:lane0 + rot_dim] = cos
    s1[:, lane0 + half:lane0 + rot_dim] = sin
    s2[:, lane0:lane0 + half] = -sin
    return jnp.asarray(c), jnp.asarray(s1), jnp.asarray(s2)


def _rope_lanes(a, c, s1, s2, half):
    return a * c + pltpu.roll(a, half, 1) * s1 + pltpu.roll(a, LANES - half, 1) * s2


def _distance_bias(mult_fn, diags, t, rows=None):
    d = np.asarray(diags, np.int64)[:, None, None] * t
    r = np.arange(t if rows is None else rows, dtype=np.int64)[None, :, None]
    c = np.arange(t, dtype=np.int64)[None, None, :]
    mult = mult_fn(d + r - c)
    return jnp.asarray(np.where(mult > 0, np.log(np.maximum(mult, 1).astype(np.float32)), NEG_INF).astype(np.float32))


def _causal_mult(dist):
    return (dist >= 0).astype(np.int64)


def _dilated_mult(dist):
    m = np.zeros_like(dist)
    for window, dil in DIL_PAIRS:
        m = m + ((dist >= 0) & (dist % dil == 0) & (dist <= (window // dil) * dil)).astype(np.int64)
    return m


def _window_mult(dist):
    return ((dist >= 0) & (dist <= NSA_WINDOW - 1)).astype(np.int64)


def _in_block(blk, lo, n):
    return jnp.logical_and(blk >= lo, blk < lo + n)


def _inproj_kernel(x_ref, w_ref, c_ref, s1_ref, s2_ref, o_ref, *, scale):
    j = pl.program_id(0)
    for hh in range(INPROJ_TM // INPROJ_SUB):
        rows = slice(hh * INPROJ_SUB, (hh + 1) * INPROJ_SUB)
        acc = _qk(x_ref[rows, :], w_ref[0])
        c, s1, s2 = c_ref[rows, :], s1_ref[rows, :], s2_ref[rows, :]
        for u in range(INPROJ_TN // LANES):
            blk = j * (INPROJ_TN // LANES) + u
            is_q = jnp.logical_or(_in_block(blk, BLK_DQ, DIL_HEADS), _in_block(blk, BLK_NQ, NSA_HEADS))
            is_rope = is_q | _in_block(blk, BLK_DK, DIL_HEADS) | (blk == BLK_NKS) | (blk == BLK_NKW)
            sc = jnp.where(is_q, scale, 1.0).astype(F32)
            a = acc[:, u * LANES:(u + 1) * LANES]
            r = _rope_lanes(a, jnp.where(is_rope, c, 1.0), jnp.where(is_rope, s1, 0.0),
                            jnp.where(is_rope, s2, 0.0), ROT_DIM // 2)
            o_ref[rows, u * LANES:(u + 1) * LANES] = (r * sc).astype(o_ref.dtype)


def _inproj(xb, w_in_s, layer, tabs):
    t = xb.shape[0]
    tm, tn = INPROJ_TM, INPROJ_TN
    nrow = SEQ // tm
    tab_spec = pl.BlockSpec((tm, LANES), lambda j, i: (i % nrow, 0))
    return pl.pallas_call(
        functools.partial(_inproj_kernel, scale=HEAD_DIM ** -0.5),
        out_shape=jax.ShapeDtypeStruct((t, IN_COLS_PAD), BF16),
        grid=(IN_COLS_PAD // tn, t // tm),
        in_specs=[pl.BlockSpec((tm, D_MODEL), lambda j, i: (i, 0)),
                  pl.BlockSpec((1, tn, D_MODEL), lambda j, i: (layer, j, 0)),
                  tab_spec, tab_spec, tab_spec],
        out_specs=pl.BlockSpec((tm, tn), lambda j, i: (i, j)),
        compiler_params=_cparams(("arbitrary", "arbitrary")),
        name="inproj",
    )(xb, w_in_s, *tabs)


def _rms(xf, g):
    return xf * lax.rsqrt(jnp.mean(jnp.square(xf), axis=-1, keepdims=True) + RMS_EPS) * g


def _mlaprep_kernel(ql_ref, kvl_ref, kr_ref, gq_ref, gkv_ref, wq_ref, wkn_ref, wv_ref,
                    c_ref, s1_ref, s2_ref, q_ref, k_ref, v_ref, *, scale):
    c, s1, s2 = c_ref[...], s1_ref[...], s2_ref[...]
    half = MLA_ROPE_DIM // 2
    qn = _rms(ql_ref[...].astype(F32), gq_ref[...]).astype(BF16)
    q = jnp.dot(qn, wq_ref[...], preferred_element_type=F32)
    for h in range(MLA_HEADS):
        lo = h * MLA_QK_PAD
        q_ref[:, lo:lo + LANES] = (q[:, lo:lo + LANES] * scale).astype(q_ref.dtype)
        a = q[:, lo + LANES:lo + 2 * LANES]
        q_ref[:, lo + LANES:lo + 2 * LANES] = (_rope_lanes(a, c, s1, s2, half) * scale).astype(q_ref.dtype)
    kvn = _rms(kvl_ref[...].astype(F32), gkv_ref[...]).astype(BF16)
    kn = jnp.dot(kvn, wkn_ref[...], preferred_element_type=F32)
    v_ref[...] = jnp.dot(kvn, wv_ref[...], preferred_element_type=F32).astype(v_ref.dtype)
    kpe = _rope_lanes(kr_ref[...].astype(F32), c, s1, s2, half).astype(k_ref.dtype)
    for h in range(MLA_HEADS):
        lo = h * MLA_QK_PAD
        k_ref[:, lo:lo + LANES] = kn[:, h * LANES:(h + 1) * LANES].astype(k_ref.dtype)
        k_ref[:, lo + LANES:lo + 2 * LANES] = kpe


def _mlaprep(hproj, gq, gkv, wq_p, wkn, wv, tabs):
    t = hproj.shape[0]
    tm = MLAPREP_TM
    nrow = SEQ // tm
    tab_spec = pl.BlockSpec((tm, LANES), lambda i: (i % nrow, 0))
    full = lambda shape: pl.BlockSpec(shape, lambda i: (0,) * len(shape))
    return pl.pallas_call(
        functools.partial(_mlaprep_kernel, scale=(MLA_NOPE_DIM + MLA_ROPE_DIM) ** -0.5),
        out_shape=(jax.ShapeDtypeStruct((t, MLA_HEADS * MLA_QK_PAD), BF16),
                   jax.ShapeDtypeStruct((t, MLA_HEADS * MLA_QK_PAD), BF16),
                   jax.ShapeDtypeStruct((t, MLA_HEADS * HEAD_DIM), BF16)),
        grid=(t // tm,),
        in_specs=[pl.BlockSpec((tm, MLA_Q_RANK), lambda i: (i, BLK_QLAT * LANES // MLA_Q_RANK)),
                  pl.BlockSpec((tm, LANES), lambda i: (i, BLK_KVLAT)),
                  pl.BlockSpec((tm, LANES), lambda i: (i, BLK_KROPE)),
                  full((1, MLA_Q_RANK)), full((1, MLA_KV_RANK)),
                  full((MLA_Q_RANK, MLA_HEADS * MLA_QK_PAD)),
                  full((MLA_KV_RANK, MLA_HEADS * MLA_NOPE_DIM)),
                  full((MLA_KV_RANK, MLA_HEADS * HEAD_DIM)),
                  tab_spec, tab_spec, tab_spec],
        out_specs=(pl.BlockSpec((tm, MLA_HEADS * MLA_QK_PAD), lambda i: (i, 0)),
                   pl.BlockSpec((tm, MLA_HEADS * MLA_QK_PAD), lambda i: (i, 0)),
                   pl.BlockSpec((tm, MLA_HEADS * HEAD_DIM), lambda i: (i, 0))),
        compiler_params=_cparams(("arbitrary",)),
        name="mlaprep",
    )(hproj, hproj, hproj, gq, gkv, wq_p, wkn, wv, *tabs)


def _gelu_tanh(x):
    return 0.5 * x * (1.0 + jnp.tanh(0.7978845608028654 * (x + 0.044715 * (x * x * x))))


def _compress_kernel(tk_ref, tv_ref, pek_ref, pev_ref, w1k_ref, w2k_ref, w1v_ref, w2v_ref,
                     c_ref, s1_ref, s2_ref, kc_ref, vc_ref):
    half_in = NSA_CMP_STRIDE * HEAD_DIM

    def comp(t_ref, pe_ref, w1_ref, w2_ref):
        t = t_ref[0].astype(F32)
        a = (t + pe_ref[0:1, :]).astype(BF16)
        b = (t + pe_ref[1:2, :]).astype(BF16)
        y0 = jnp.dot(a, w1_ref[0:half_in, :], preferred_element_type=F32)
        y1 = jnp.dot(b, w1_ref[half_in:2 * half_in, :], preferred_element_type=F32)
        hid = _gelu_tanh(y0 + pltpu.roll(y1, N_CMP_PAD - 1, 0))
        return jnp.dot(hid.astype(BF16), w2_ref[...], preferred_element_type=F32)

    kc = comp(tk_ref, pek_ref, w1k_ref, w2k_ref)
    kc_ref[0] = _rope_lanes(kc, c_ref[...], s1_ref[...], s2_ref[...], ROT_DIM // 2).astype(kc_ref.dtype)
    vc_ref[0] = comp(tv_ref, pev_ref, w1v_ref, w2v_ref).astype(vc_ref.dtype)


def _compress(tk2, tv2, pek, pev, w1k, w2k, w1v, w2v, tabs):
    nb = tk2.shape[0]
    wide = NSA_CMP_STRIDE * HEAD_DIM
    full = lambda shape: pl.BlockSpec(shape, lambda b: (0,) * len(shape))
    bspec = pl.BlockSpec((1, N_CMP_PAD, wide), lambda b: (b, 0, 0))
    ospec = pl.BlockSpec((1, N_CMP_PAD, HEAD_DIM), lambda b: (b, 0, 0))
    return pl.pallas_call(
        _compress_kernel,
        out_shape=(jax.ShapeDtypeStruct((nb, N_CMP_PAD, HEAD_DIM), BF16),) * 2,
        grid=(nb,),
        in_specs=[bspec, bspec, full((2, wide)), full((2, wide)),
                  full((2 * wide, NSA_CMP_HIDDEN)), full((NSA_CMP_HIDDEN, HEAD_DIM)),
                  full((2 * wide, NSA_CMP_HIDDEN)), full((NSA_CMP_HIDDEN, HEAD_DIM)),
                  full((N_CMP_PAD, LANES)), full((N_CMP_PAD, LANES)), full((N_CMP_PAD, LANES))],
        out_specs=(ospec, ospec),
        compiler_params=_cparams(("arbitrary",)),
        name="compress",
    )(tk2, tv2, pek, pev, w1k, w2k, w1v, w2v, *tabs)


def _qk(q, k):
    return lax.dot_general(q, k, (((1,), (1,)), ((), ())), preferred_element_type=F32)


def _lane_fold(x, op):
    out = x[:, 0:LANES]
    for u in range(1, x.shape[1] // LANES):
        out = op(out, x[:, u * LANES:(u + 1) * LANES])
    return out


def _split3(x):
    hi = x.astype(BF16)
    r1 = x - hi.astype(F32)
    mid = r1.astype(BF16)
    lo = (r1 - mid.astype(F32)).astype(BF16)
    return hi, mid, lo


def _attend(q, k_ref, v_ref, bias_fn, j_lo, n_tiles, s_scr, *, tk, groups=1):
    rows = q.shape[0]
    dv = v_ref.shape[-1]

    def pass1(t, mx):
        off = pl.multiple_of((j_lo + t) * tk, tk)
        s = _qk(q, k_ref[pl.ds(off, tk), :])
        bias = bias_fn(j_lo + t)
        if groups > 1:
            s = (s.reshape(groups, rows // groups, tk) + bias[None]).reshape(rows, tk)
        else:
            s = s + bias
        s_scr[t] = s
        return jnp.maximum(mx, _lane_fold(s, jnp.maximum))

    mx = lax.fori_loop(0, n_tiles, pass1, jnp.full((rows, LANES), NEG_INF, F32))
    m = jnp.broadcast_to(jnp.max(mx, axis=-1, keepdims=True), (rows, LANES))

    def pass2(t, carry):
        lsum, acc = carry
        off = pl.multiple_of((j_lo + t) * tk, tk)
        s = s_scr[t]
        parts = [jnp.exp(s[:, u * LANES:(u + 1) * LANES] - m) for u in range(tk // LANES)]
        for p in parts:
            lsum = lsum + p
        pb = jnp.concatenate([p.astype(BF16) for p in parts], axis=1)
        acc = acc + jnp.dot(pb, v_ref[pl.ds(off, tk), :], preferred_element_type=F32)
        return lsum, acc

    lsum, acc = lax.fori_loop(0, n_tiles, pass2, (jnp.zeros((rows, LANES), F32), jnp.zeros((rows, dv), F32)))
    return acc / jnp.sum(lsum, axis=-1, keepdims=True)


def _flash_kernel(q_ref, k_ref, v_ref, bias_ref, o_ref, s_scr):
    i = pl.program_id(2)
    last = bias_ref.shape[0] - 1
    out = _attend(q_ref[...], k_ref, v_ref, lambda j: bias_ref[jnp.minimum(i - j, last)], 0, i + 1, s_scr,
                  tk=FLASH_T)
    o_ref[...] = out.astype(o_ref.dtype)


def _flash(q_arr, k_arr, v_arr, bias, *, nb, heads, dqk, dv, q_blk0, k_blk0, v_blk0):
    t = FLASH_T
    nq = SEQ // t
    return pl.pallas_call(
        _flash_kernel,
        out_shape=jax.ShapeDtypeStruct((nb * SEQ, heads * dv), BF16),
        grid=(nb, heads, nq),
        in_specs=[pl.BlockSpec((t, dqk), lambda b, h, i: (b * nq + i, q_blk0 + h)),
                  pl.BlockSpec((SEQ, dqk), lambda b, h, i: (b, k_blk0 + h)),
                  pl.BlockSpec((SEQ, dv), lambda b, h, i: (b, v_blk0 + h)),
                  pl.BlockSpec(bias.shape, lambda b, h, i: (0, 0, 0))],
        out_specs=pl.BlockSpec((t, dv), lambda b, h, i: (b * nq + i, h)),
        scratch_shapes=[pltpu.VMEM((nq, t, t), F32)],
        compiler_params=_cparams(("arbitrary", "arbitrary", "arbitrary")),
        name="flash",
    )(q_arr, k_arr, v_arr, bias)


def _nsa_kernel(q0_ref, q1_ref, q2_ref, q3_ref, q4_ref, q5_ref, kc_ref, vc_ref, ks_ref, vs_ref, kw_ref, vw_ref,
                gate_ref, cover_ref, expand_ref, wbias_ref, o_ref, bias_scr, s_scr):
    i = pl.program_id(1)
    tq, tk = NSA_TQ, NSA_T
    per = tq // tk
    nh = NSA_HEADS
    rows = nh * tq
    pos = i * tq + lax.broadcasted_iota(jnp.int32, (tq, 1), 0)
    q = jnp.concatenate([r[...] for r in (q0_ref, q1_ref, q2_ref, q3_ref, q4_ref, q5_ref)], axis=0)

    cidx = lax.broadcasted_iota(jnp.int32, (tq, N_CMP_PAD), 1)
    valid_c = (cidx * NSA_CMP_STRIDE + (NSA_CMP_LEN - 1) <= pos) & (cidx < N_CMP_PAD - 1)
    valid_cf = valid_c.astype(F32)
    cbias = jnp.where(valid_c, 0.0, NEG_INF).astype(F32)
    s = (_qk(q, kc_ref[0]).reshape(nh, tq, N_CMP_PAD) + cbias[None]).reshape(rows, N_CMP_PAD)
    p = (jnp.exp(s - jnp.max(s, axis=-1, keepdims=True)).reshape(nh, tq, N_CMP_PAD) * valid_cf[None])
    p = p.reshape(rows, N_CMP_PAD)
    l = jnp.sum(p, axis=-1, keepdims=True)
    p = p / jnp.where(l > 0.0, l, 1.0)
    o_cmp = jnp.dot(p.astype(BF16), vc_ref[0], preferred_element_type=F32)
    psum = jnp.sum(p.reshape(nh, tq, N_CMP_PAD), axis=0)

    cover_t = cover_ref[...]
    imp = sum(jnp.dot(cover_t, part, preferred_element_type=F32) for part in _split3(psum.T))

    pos_t = i * tq + lax.broadcasted_iota(jnp.int32, (N_SEL_BLOCKS, tq), 1)
    jidx = lax.broadcasted_iota(jnp.int32, (N_SEL_BLOCKS, tq), 0)
    qblk = pos_t // NSA_SEL_BLOCK
    valid_s = jidx * NSA_SEL_BLOCK <= pos_t
    forced = (jidx == 0) | (jidx == qblk) | (jidx == qblk - 1)
    score = jnp.where(valid_s, jnp.where(forced, NSA_FORCE_SCORE, imp), -1.0)
    rank = jnp.zeros((N_SEL_BLOCKS, tq), jnp.int32)
    for ii in range(N_SEL_BLOCKS):
        si = score[ii:ii + 1, :]
        rank = rank + ((si > score) | ((si == score) & (ii < jidx))).astype(jnp.int32)
    sel_t = (rank < NSA_TOP_N).astype(F32)
    sel = jnp.concatenate([sel_t, jnp.zeros((LANES - N_SEL_BLOCKS, tq), F32)], axis=0).T.astype(BF16)

    n_causal = per * (i + 1)

    def fill(j, carry):
        hit = jnp.dot(sel, expand_ref[j], preferred_element_type=F32)
        kpos = j * tk + lax.broadcasted_iota(jnp.int32, (tq, tk), 1)
        bias_scr[j] = jnp.where((hit > 0.5) & (kpos <= pos), 0.0, NEG_INF).astype(F32)
        return carry

    lax.fori_loop(0, n_causal, fill, 0)

    o_slc = _attend(q, ks_ref, vs_ref, lambda j: bias_scr[j], 0, n_causal, s_scr, tk=tk, groups=nh)
    far = wbias_ref.shape[0] - per
    w_lo = jnp.maximum(per * i - far, 0)
    o_win = _attend(q, kw_ref, vw_ref, lambda j: wbias_ref[per * i - j + per - 1], w_lo, n_causal - w_lo, s_scr,
                    tk=tk, groups=nh)

    gates = jax.nn.sigmoid(gate_ref[...].astype(F32))
    for h in range(nh):
        sl = slice(h * tq, (h + 1) * tq)
        g0 = NSA_BRANCHES * h
        out = (gates[:, g0:g0 + 1] * o_cmp[sl] + gates[:, g0 + 1:g0 + 2] * o_slc[sl]
               + gates[:, g0 + 2:g0 + 3] * o_win[sl])
        o_ref[:, h * LANES:(h + 1) * LANES] = out.astype(o_ref.dtype)


def _nsa(hproj, kc, vc, cover, expand, wbias, nb):
    t = NSA_TQ
    nq = SEQ // t
    nk = SEQ // NSA_T
    wide = NSA_HEADS * HEAD_DIM
    q_spec = lambda h: pl.BlockSpec((t, LANES), lambda b, i: (b * nq + i, BLK_NQ + h))
    kv_spec = lambda blk: pl.BlockSpec((SEQ, LANES), lambda b, i: (b, blk))
    cspec = pl.BlockSpec((1, N_CMP_PAD, HEAD_DIM), lambda b, i: (b, 0, 0))
    full = lambda shape: pl.BlockSpec(shape, lambda b, i: (0,) * len(shape))
    return pl.pallas_call(
        _nsa_kernel,
        out_shape=jax.ShapeDtypeStruct((nb * SEQ, wide), BF16),
        grid=(nb, nq),
        in_specs=[q_spec(h) for h in range(NSA_HEADS)]
        + [cspec, cspec, kv_spec(BLK_NKS), kv_spec(BLK_NVS), kv_spec(BLK_NKW), kv_spec(BLK_NVW),
           pl.BlockSpec((t, LANES), lambda b, i: (b * nq + i, BLK_GATE)),
           full(cover.shape), full(expand.shape), full(wbias.shape)],
        out_specs=pl.BlockSpec((t, wide), lambda b, i: (b * nq + i, 0)),
        scratch_shapes=[pltpu.VMEM((nk, t, NSA_T), F32), pltpu.VMEM((nk, NSA_HEADS * t, NSA_T), F32)],
        compiler_params=_cparams(("arbitrary", "arbitrary")),
        name="nsa",
    )(*([hproj] * NSA_HEADS), kc, vc, hproj, hproj, hproj, hproj, hproj, cover, expand, wbias)


def _layer_norm(y, g, b):
    mu = jnp.mean(y, axis=-1, keepdims=True)
    var = jnp.mean(jnp.square(y - mu), axis=-1, keepdims=True)
    return (y - mu) * lax.rsqrt(var + LN_EPS) * g + b


def _lane_min(x):
    return jnp.min(x, axis=-1, keepdims=True)


def _lane_max(x):
    return jnp.max(x, axis=-1, keepdims=True)


def _route(logits):
    lane = lax.broadcasted_iota(jnp.int32, logits.shape, 1)
    lane_f = lane.astype(F32)
    far = float(LANES)
    is_grp = lane < N_GROUPS
    lg = jnp.where(is_grp, logits, NEG_INF)
    eg = jnp.where(is_grp, jnp.exp(lg - _lane_max(lg)), 0.0)
    prob = eg / jnp.sum(eg, axis=-1, keepdims=True)
    p_g = _lane_max(prob)
    g_idx = _lane_min(jnp.where(is_grp & (prob == p_g), lane_f, far))
    e_lo = N_GROUPS + EXPERTS_PER_GROUP * g_idx
    in_grp = (lane_f >= e_lo) & (lane_f < e_lo + EXPERTS_PER_GROUP)
    le = jnp.where(in_grp, logits, NEG_INF)
    v1 = _lane_max(le)
    i1 = _lane_min(jnp.where(in_grp & (le == v1), lane_f, far))
    rest = in_grp & (lane_f != i1)
    le2 = jnp.where(rest, logits, NEG_INF)
    v2 = _lane_max(le2)
    i2 = _lane_min(jnp.where(rest & (le2 == v2), lane_f, far))
    e21 = jnp.exp(v2 - v1)
    den = 1.0 + e21
    gate1 = p_g * (1.0 / den)
    gate2 = p_g * (e21 / den)
    out = jnp.where(lane == 0, i1 - N_GROUPS, 0.0)
    out = jnp.where(lane == 1, i2 - N_GROUPS, out)
    out = jnp.where(lane == 2, gate1, out)
    out = jnp.where(lane == 3, gate2, out)
    return out


def _outproj_kernel(om_ref, od_ref, on_ref, w_ref, x_ref, g_ref, b_ref, wr_ref, x1_ref, route_ref):
    n_mla = MLA_HEADS * HEAD_DIM
    n_dil = DIL_HEADS * HEAD_DIM
    for hh in range(OUTPROJ_TM // ROW_TM):
        rows = slice(hh * ROW_TM, (hh + 1) * ROW_TM)
        mix = jnp.dot(om_ref[rows, :], w_ref[0, 0:n_mla, :], preferred_element_type=F32)
        mix = mix + jnp.dot(od_ref[rows, :], w_ref[0, n_mla:n_mla + n_dil, :], preferred_element_type=F32)
        mix = mix + jnp.dot(on_ref[rows, :], w_ref[0, n_mla + n_dil:, :], preferred_element_type=F32)
        x1 = _layer_norm(DN_ALPHA * x_ref[rows, :] + mix, g_ref[...], b_ref[...])
        x1_ref[rows, :] = x1
        xh = x1.astype(BF16)
        xm = (x1 - xh.astype(F32)).astype(BF16)
        both = jnp.dot(xh, wr_ref[...], preferred_element_type=F32)
        logits = both[:, 0:LANES] + (both[:, LANES:2 * LANES]
                                     + jnp.dot(xm, wr_ref[:, 0:LANES], preferred_element_type=F32))
        route_ref[rows, :] = _route(logits)


def _outproj(o_mla, o_dil, o_nsa, w_out_s, layer, x, g, b, wr3):
    t = x.shape[0]
    tm = OUTPROJ_TM
    full = lambda shape: pl.BlockSpec(shape, lambda i: (0,) * len(shape))
    row = lambda w: pl.BlockSpec((tm, w), lambda i: (i, 0))
    return pl.pallas_call(
        _outproj_kernel,
        out_shape=(jax.ShapeDtypeStruct((t, D_MODEL), F32), jax.ShapeDtypeStruct((t, LANES), F32)),
        grid=(t // tm,),
        in_specs=[row(o_mla.shape[1]), row(o_dil.shape[1]), row(o_nsa.shape[1]),
                  pl.BlockSpec((1,) + w_out_s.shape[1:], lambda i: (layer, 0, 0)),
                  row(D_MODEL), full((1, D_MODEL)), full((1, D_MODEL)), full(wr3.shape)],
        out_specs=(row(D_MODEL), row(LANES)),
        compiler_params=_cparams(("arbitrary",)),
        name="outproj",
    )(o_mla, o_dil, o_nsa, w_out_s, x, g, b, wr3)


def _dispatch_kernel(fill_ref, pos_ref, x1_ref, xb_hbm, stage, zbuf, sem, zsem):
    i = pl.program_id(0)
    n = pl.num_programs(0)
    tm = ROW_TM
    slot = i % 2

    def drain(s):
        for _ in range(MOE_TOP_K):
            pltpu.make_async_copy(stage.at[s], xb_hbm.at[pl.ds(0, tm)], sem.at[s]).wait()

    def zero_copy(e):
        start = pl.multiple_of(fill_ref[e], MOE_ROWS)
        return pltpu.make_async_copy(zbuf, xb_hbm.at[pl.ds(start, MOE_ROWS)], zsem)

    @pl.when(i == 0)
    def _():
        zbuf[...] = jnp.zeros_like(zbuf)
        for e in range(N_EXPERTS):
            @pl.when(fill_ref[e] >= 0)
            def _():
                zero_copy(e).start()
        for e in range(N_EXPERTS):
            @pl.when(fill_ref[e] >= 0)
            def _():
                zero_copy(e).wait()

        def tail_copy(b):
            start = pl.multiple_of(b * MOE_ROWS, MOE_ROWS)
            return pltpu.make_async_copy(zbuf, xb_hbm.at[pl.ds(start, MOE_ROWS)], zsem)

        n_blocks = xb_hbm.shape[0] // MOE_ROWS
        lax.fori_loop(fill_ref[N_EXPERTS], n_blocks, lambda b, c: (tail_copy(b).start(), c)[1], 0)
        lax.fori_loop(fill_ref[N_EXPERTS], n_blocks, lambda b, c: (tail_copy(b).wait(), c)[1], 0)

    @pl.when(i >= 2)
    def _():
        drain(slot)

    for c in range(D_MODEL // LANES):
        stage[slot, :, c, :] = x1_ref[:, c * LANES:(c + 1) * LANES]

    for r in range(tm):
        for k in range(MOE_TOP_K):
            p = pos_ref[0, 0, MOE_TOP_K * r + k]
            pltpu.make_async_copy(stage.at[slot, r], xb_hbm.at[p], sem.at[slot]).start(priority=k % 2)

    @pl.when(i == n - 1)
    def _():
        drain(1 - slot)
        drain(slot)


def _dispatch(fill, pos3, x1, n_rows):
    t = x1.shape[0]
    tm = ROW_TM
    grid_spec = pltpu.PrefetchScalarGridSpec(
        num_scalar_prefetch=1,
        grid=(t // tm,),
        in_specs=[pl.BlockSpec((1, 1, MOE_TOP_K * tm), lambda i, fl: (i, 0, 0), memory_space=pltpu.SMEM),
                  pl.BlockSpec((tm, D_MODEL), lambda i, fl: (i, 0))],
        out_specs=pl.BlockSpec(memory_space=pl.ANY),
        scratch_shapes=[pltpu.VMEM((2, tm, D_MODEL // LANES, LANES), F32),
                        pltpu.VMEM((MOE_ROWS, D_MODEL // LANES, LANES), F32),
                        pltpu.SemaphoreType.DMA((2,)), pltpu.SemaphoreType.DMA(())],
    )
    return pl.pallas_call(
        _dispatch_kernel,
        out_shape=jax.ShapeDtypeStruct((n_rows, D_MODEL // LANES, LANES), F32),
        grid_spec=grid_spec,
        compiler_params=_cparams(("arbitrary",)),
        name="dispatch",
    )(fill, pos3, x1)


def _experts_kernel(blk_ex_ref, first_ref, nxt_ref, nused_ref, x_ref, wg_hbm, wu_hbm, wd_hbm, y_ref,
                    wg_f, wu_f, wd_f, wg_b, wu_b, wd_b, sem, *, layer):
    b = pl.program_id(0)
    nused = nused_ref[0]

    def weight_copies(e, s):
        return (pltpu.make_async_copy(wg_hbm.at[layer, e], wg_f.at[s], sem.at[s]),
                pltpu.make_async_copy(wu_hbm.at[layer, e], wu_f.at[s], sem.at[s]),
                pltpu.make_async_copy(wd_hbm.at[layer, e], wd_f.at[s], sem.at[s]))

    @pl.when(b == 0)
    def _():
        for cp in weight_copies(blk_ex_ref[0], 0):
            cp.start()

    @pl.when(b < nused)
    def _():
        first = first_ref[b]

        @pl.when(first > 0)
        def _():
            s = first - 1
            for cp in weight_copies(blk_ex_ref[b], s):
                cp.wait()

            @pl.when(nxt_ref[b] >= 0)
            def _():
                for cp in weight_copies(nxt_ref[b], 1 - s):
                    cp.start()

            wg_b[...] = wg_f[s].astype(BF16)
            wu_b[...] = wu_f[s].astype(BF16)
            wd_b[...] = wd_f[s].astype(BF16)

        rows = jnp.concatenate([x_ref[:, c, :] for c in range(D_MODEL // LANES)], axis=1).astype(BF16)
        gate = jnp.dot(rows, wg_b[...], preferred_element_type=F32)
        up = jnp.dot(rows, wu_b[...], preferred_element_type=F32)
        hid = (gate * jax.nn.sigmoid(gate) * up).astype(BF16)
        y_ref[...] = jnp.dot(hid, wd_b[...], preferred_element_type=F32)

    @pl.when(b >= nused)
    def _():
        y_ref[...] = jnp.zeros_like(y_ref)


def _experts(blk_ex, first, nxt, nused, xb_rows, w_gate, w_up, w_down, layer):
    n_blocks = xb_rows.shape[0] // MOE_ROWS
    hbm = pl.BlockSpec(memory_space=pl.ANY)
    grid_spec = pltpu.PrefetchScalarGridSpec(
        num_scalar_prefetch=4,
        grid=(n_blocks,),
        in_specs=[pl.BlockSpec((MOE_ROWS, D_MODEL // LANES, LANES),
                               lambda b, ex, fi, nx, nu: (jnp.minimum(b, nu[0] - 1), 0, 0)),
                  hbm, hbm, hbm],
        out_specs=pl.BlockSpec((MOE_ROWS, D_MODEL), lambda b, ex, fi, nx, nu: (b, 0)),
        scratch_shapes=[pltpu.VMEM((2, D_MODEL, EXPERT_FF), F32), pltpu.VMEM((2, D_MODEL, EXPERT_FF), F32),
                        pltpu.VMEM((2, EXPERT_FF, D_MODEL), F32),
                        pltpu.VMEM((D_MODEL, EXPERT_FF), BF16), pltpu.VMEM((D_MODEL, EXPERT_FF), BF16),
                        pltpu.VMEM((EXPERT_FF, D_MODEL), BF16), pltpu.SemaphoreType.DMA((2,))],
    )
    return pl.pallas_call(
        functools.partial(_experts_kernel, layer=layer),
        out_shape=jax.ShapeDtypeStruct((n_blocks * MOE_ROWS, D_MODEL), F32),
        grid_spec=grid_spec,
        compiler_params=_cparams(("arbitrary",)),
        name="experts",
    )(blk_ex, first, nxt, nused, xb_rows, w_gate, w_up, w_down)


def _combine_kernel(pos_cur_ref, pos_nxt_ref, y_hbm, x1_ref, route_ref, g_ref, b_ref, o_ref, ob_ref, ybuf, sem):
    i = pl.program_id(0)
    n = pl.num_programs(0)
    tm = ROW_TM

    def issue(pos_ref, slot):
        for r in range(tm):
            for k in range(MOE_TOP_K):
                p = pos_ref[0, 0, MOE_TOP_K * r + k]
                pltpu.make_async_copy(y_hbm.at[pl.ds(p, 1)], ybuf.at[slot, k, pl.ds(r, 1)], sem.at[slot]).start()

    @pl.when(i == 0)
    def _():
        issue(pos_cur_ref, 0)

    @pl.when(i + 1 < n)
    def _():
        issue(pos_nxt_ref, (i + 1) % 2)

    slot = i % 2
    for k in range(MOE_TOP_K):
        pltpu.make_async_copy(y_hbm.at[pl.ds(0, tm)], ybuf.at[slot, k], sem.at[slot]).wait()
    route = route_ref[...]
    ffn = route[:, 2:3] * ybuf[slot, 0] + route[:, 3:4] * ybuf[slot, 1]
    x2 = _layer_norm(DN_ALPHA * x1_ref[...] + ffn, g_ref[...], b_ref[...])
    o_ref[...] = x2
    ob_ref[...] = x2.astype(ob_ref.dtype)


def _combine(pos3, yb, x1, route, g, b):
    t = x1.shape[0]
    tm = ROW_TM
    nt = t // tm
    pos_spec = lambda shift: pl.BlockSpec((1, 1, MOE_TOP_K * tm), lambda i: (jnp.minimum(i + shift, nt - 1), 0, 0),
                                          memory_space=pltpu.SMEM)
    row = lambda w: pl.BlockSpec((tm, w), lambda i: (i, 0))
    full = lambda shape: pl.BlockSpec(shape, lambda i: (0,) * len(shape))
    return pl.pallas_call(
        _combine_kernel,
        out_shape=(jax.ShapeDtypeStruct((t, D_MODEL), F32), jax.ShapeDtypeStruct((t, D_MODEL), BF16)),
        grid=(nt,),
        in_specs=[pos_spec(0), pos_spec(1), pl.BlockSpec(memory_space=pl.ANY), row(D_MODEL), row(LANES),
                  full((1, D_MODEL)), full((1, D_MODEL))],
        out_specs=(row(D_MODEL), row(D_MODEL)),
        scratch_shapes=[pltpu.VMEM((2, MOE_TOP_K, tm, D_MODEL), F32), pltpu.SemaphoreType.DMA((2,))],
        compiler_params=_cparams(("arbitrary",)),
        name="combine",
    )(pos3, pos3, yb, x1, route, g, b)


def _dispatch_plan(route, t):
    m = t * MOE_TOP_K
    n_blocks = -(-m // MOE_ROWS) + N_EXPERTS
    chunk = MOE_ROWS
    ex = route[:, 0:MOE_TOP_K].reshape(m // chunk, chunk, 1)
    onehot = (ex == jnp.arange(N_EXPERTS, dtype=F32)[None, None, :]).astype(F32)
    tri = jnp.asarray(np.tril(np.ones((chunk, chunk), np.float32)))
    within = jnp.einsum('ij,cje->cie', tri, onehot)
    totals = jnp.sum(onehot, axis=1)
    before = jnp.cumsum(totals, axis=0) - totals
    counts = jnp.sum(totals, axis=0)
    padded = jnp.ceil(counts / MOE_ROWS) * MOE_ROWS
    pad_end = jnp.cumsum(padded)
    pad_start = pad_end - padded
    row = jnp.sum(onehot * (within - 1.0 + before[:, None, :] + pad_start[None, None, :]), axis=-1)
    pos = row.astype(jnp.int32).reshape(m)
    blk_start = jnp.arange(n_blocks, dtype=F32) * MOE_ROWS
    blk_ex = jnp.minimum(jnp.sum((pad_end[None, :] <= blk_start[:, None]).astype(jnp.int32), axis=1),
                         N_EXPERTS - 1).astype(jnp.int32)
    nused = (pad_end[-1] / MOE_ROWS).astype(jnp.int32).reshape(1)
    fill = jnp.concatenate([jnp.where(padded > 0, pad_end - MOE_ROWS, -1.0).astype(jnp.int32), nused])
    nonempty = padded > 0
    slot_e = (jnp.cumsum(nonempty.astype(jnp.int32)) - 1) % 2
    expert_ids = jnp.arange(N_EXPERTS, dtype=jnp.int32)
    later = lax.cummin(jnp.where(nonempty, expert_ids, N_EXPERTS), axis=0, reverse=True)
    nxt_e = jnp.concatenate([later[1:], jnp.full((1,), N_EXPERTS, jnp.int32)])
    nxt_e = jnp.where(nxt_e >= N_EXPERTS, -1, nxt_e)
    is_first = (blk_start == pad_start[blk_ex]) & (jnp.arange(n_blocks) < nused[0])
    first = jnp.where(is_first, 1 + slot_e[blk_ex], 0).astype(jnp.int32)
    return blk_ex, first, nxt_e[blk_ex].astype(jnp.int32), nused, fill, pos


W_HALF = LANES // 2


def _shift_kernel(a_ref, b_ref, tail_ref, o_ref):
    m = pl.program_id(0)
    n_src = IN_COLS // LANES
    nl = o_ref.shape[0]
    d = o_ref.shape[2]
    nc = d // LANES
    a2, b2, t2 = a_ref, b_ref, tail_ref

    def slab(ref4, col0, c, l):
        return ref4[pl.ds(col0, W_HALF), c, l, :].astype(o_ref.dtype)

    def copy(ref_top, top0, ref_bot, bot0):
        for l in range(nl):
            for c in range(nc):
                o_ref[l, 0:W_HALF, c * LANES:(c + 1) * LANES] = slab(ref_top, top0, c, l)
                if ref_bot is None:
                    o_ref[l, W_HALF:LANES, c * LANES:(c + 1) * LANES] = jnp.zeros((W_HALF, LANES), o_ref.dtype)
                else:
                    o_ref[l, W_HALF:LANES, c * LANES:(c + 1) * LANES] = slab(ref_bot, bot0, c, l)

    @pl.when(m < n_src)
    def _():
        copy(a2, 0, b2, 0)

    @pl.when(m == n_src)
    def _():
        copy(a2, 0, t2, 0)

    @pl.when(m == n_src + 1)
    def _():
        copy(t2, W_HALF, None, 0)


def _shift_w_in(w_in):
    nl, d, _ = w_in.shape
    n_src = IN_COLS // LANES
    n_keep = (IN_SHIFT_FROM + MLA_ROPE_DIM) // LANES
    last = n_src * (LANES // W_HALF) - 1
    nc = d // LANES
    view = lambda w: jnp.transpose(w.reshape(nl, nc, LANES, w.shape[-1]), (3, 1, 0, 2))
    w_t = view(w_in)
    tail_t = view(jnp.pad(w_in[:, :, n_src * LANES:], ((0, 0), (0, 0), (0, (n_src + 1) * LANES - IN_COLS))))
    first = lambda m: jnp.where(m < n_keep, 2 * m, 2 * m - 1)
    return pl.pallas_call(
        _shift_kernel,
        out_shape=jax.ShapeDtypeStruct((nl, IN_COLS_PAD, d), BF16),
        grid=(N_IN_BLOCKS,),
        in_specs=[pl.BlockSpec((W_HALF, nc, nl, LANES), lambda m: (jnp.minimum(first(m), last), 0, 0, 0)),
                  pl.BlockSpec((W_HALF, nc, nl, LANES), lambda m: (jnp.minimum(first(m) + 1, last), 0, 0, 0)),
                  pl.BlockSpec((LANES, nc, nl, LANES), lambda m: (0, 0, 0, 0))],
        out_specs=pl.BlockSpec((nl, LANES, d), lambda m: (0, m, 0)),
        compiler_params=_cparams(("arbitrary",)),
        name="shiftw",
    )(w_t, w_t, tail_t)


def _permute_w_q_up(w_q_up):
    w = w_q_up.reshape(MLA_Q_RANK, MLA_HEADS, MLA_NOPE_DIM + MLA_ROPE_DIM)
    zero = jnp.zeros((MLA_Q_RANK, MLA_HEADS, MLA_QK_PAD - MLA_NOPE_DIM - MLA_ROPE_DIM), w.dtype)
    w = jnp.concatenate([w[:, :, :MLA_NOPE_DIM], zero, w[:, :, MLA_NOPE_DIM:]], axis=-1)
    return w.reshape(MLA_Q_RANK, MLA_HEADS * MLA_QK_PAD).astype(BF16)


def _split_w_kv_up(w_kv_up):
    w = w_kv_up.reshape(MLA_KV_RANK, MLA_HEADS, MLA_NOPE_DIM + HEAD_DIM)
    wkn = w[:, :, :MLA_NOPE_DIM].reshape(MLA_KV_RANK, MLA_HEADS * MLA_NOPE_DIM)
    wv = w[:, :, MLA_NOPE_DIM:].reshape(MLA_KV_RANK, MLA_HEADS * HEAD_DIM)
    return wkn.astype(BF16), wv.astype(BF16)


def _router_slices(w_grp, w_exp):
    w = jnp.concatenate([w_grp, w_exp, jnp.zeros((D_MODEL, LANES - N_GROUPS - N_EXPERTS), F32)], axis=1)
    hi = w.astype(BF16)
    mid = (w - hi.astype(F32)).astype(BF16)
    return jnp.concatenate([hi, mid], axis=1)


def _mixer(xb, nb, layer, w_in_s, q_lat_norm, w_q_up, kv_lat_norm, w_kv_up,
           cmp_pos_k, cmp_w1_k, cmp_w2_k, cmp_pos_v, cmp_w1_v, cmp_w2_v, consts):
    hproj = _inproj(xb, w_in_s, layer, consts["rope32"])
    wkn, wv = _split_w_kv_up(w_kv_up)
    q_mla, k_mla, v_mla = _mlaprep(hproj, q_lat_norm.reshape(1, -1), kv_lat_norm.reshape(1, -1),
                                   _permute_w_q_up(w_q_up), wkn, wv, consts["rope64"])
    wide = NSA_CMP_STRIDE * HEAD_DIM
    tk2 = hproj[:, BLK_NKC * LANES:(BLK_NKC + 1) * LANES].reshape(nb, N_CMP_PAD, wide)
    tv2 = hproj[:, BLK_NVC * LANES:(BLK_NVC + 1) * LANES].reshape(nb, N_CMP_PAD, wide)
    kc, vc = _compress(tk2, tv2, cmp_pos_k.reshape(2, wide), cmp_pos_v.reshape(2, wide),
                       cmp_w1_k.astype(BF16), cmp_w2_k.astype(BF16), cmp_w1_v.astype(BF16), cmp_w2_v.astype(BF16),
                       consts["rope_cmp"])
    o_mla = _flash(q_mla, k_mla, v_mla, consts["bias_causal"], nb=nb, heads=MLA_HEADS, dqk=MLA_QK_PAD,
                   dv=HEAD_DIM, q_blk0=0, k_blk0=0, v_blk0=0)
    o_dil = _flash(hproj, hproj, hproj, consts["bias_dil"], nb=nb, heads=DIL_HEADS, dqk=HEAD_DIM, dv=HEAD_DIM,
                   q_blk0=BLK_DQ, k_blk0=BLK_DK, v_blk0=BLK_DV)
    o_nsa = _nsa(hproj, kc, vc, consts["cover"], consts["expand"], consts["bias_win"], nb)
    return o_mla, o_dil, o_nsa


def _make_consts():
    pos = np.arange(SEQ)
    slot = np.arange(N_CMP_PAD)
    cmp_start = slot * NSA_CMP_STRIDE
    cmp_end = cmp_start + NSA_CMP_LEN - 1
    sel_start = np.arange(N_SEL_BLOCKS) * NSA_SEL_BLOCK
    cover = ((cmp_start[:, None] < sel_start[None, :] + NSA_SEL_BLOCK)
             & (cmp_start[:, None] + NSA_CMP_LEN > sel_start[None, :])
             & (slot[:, None] < N_CMP_PAD - 1)).astype(np.float32)
    expand = (np.arange(LANES)[:, None] == (pos // NSA_SEL_BLOCK)[None, :]).astype(np.float32)
    expand = expand.reshape(LANES, SEQ // NSA_T, NSA_T).transpose(1, 0, 2)
    win_diags = range(-(NSA_TQ // NSA_T - 1), -(-NSA_WINDOW // NSA_T) + 1)
    return {
        "rope32": _rope_tables(pos, ROT_DIM, True),
        "rope64": _rope_tables(pos, MLA_ROPE_DIM, False, KROPE_LANE0),
        "rope_cmp": _rope_tables(cmp_end, ROT_DIM, True),
        "bias_causal": _distance_bias(_causal_mult, (0, 1), FLASH_T),
        "bias_dil": _distance_bias(_dilated_mult, range(SEQ // FLASH_T), FLASH_T),
        "bias_win": _distance_bias(_window_mult, win_diags, NSA_T, NSA_TQ),
        "cover": jnp.asarray(cover.T, BF16),
        "expand": jnp.asarray(expand, BF16),
    }


def kernel(x, w_in, q_lat_norm, w_q_up, kv_lat_norm, w_kv_up, cmp_pos_k, cmp_w1_k, cmp_w2_k, cmp_pos_v, cmp_w1_v,
           cmp_w2_v, w_out, ln1_g, ln1_b, w_grp, w_exp, w_gate, w_up, w_down, ln2_g, ln2_b):
    nb, s, d = x.shape
    assert s == SEQ and d == D_MODEL
    t = nb * s
    consts = _make_consts()
    w_in_s = _shift_w_in(w_in)
    w_out_s = w_out.astype(BF16)
    xf = x.reshape(t, d)
    xb = xf.astype(BF16)
    for l in range(DEPTH):
        o_mla, o_dil, o_nsa = _mixer(xb, nb, l, w_in_s, q_lat_norm[l], w_q_up[l], kv_lat_norm[l], w_kv_up[l],
                                     cmp_pos_k[l], cmp_w1_k[l], cmp_w2_k[l], cmp_pos_v[l], cmp_w1_v[l],
                                     cmp_w2_v[l], consts)
        x1, route = _outproj(o_mla, o_dil, o_nsa, w_out_s, l, xf, ln1_g[l].reshape(1, d),
                             ln1_b[l].reshape(1, d), _router_slices(w_grp[l], w_exp[l]))
        blk_ex, first, nxt, nused, fill, pos = _dispatch_plan(route, t)
        pos3 = pos.reshape(t // ROW_TM, 1, MOE_TOP_K * ROW_TM)
        n_rows = (-(-t * MOE_TOP_K // MOE_ROWS) + N_EXPERTS) * MOE_ROWS
        yb = _experts(blk_ex, first, nxt, nused, _dispatch(fill, pos3, x1, n_rows), w_gate, w_up, w_down, l)
        xf, xb = _combine(pos3, yb, x1, route,
                          ln2_g[l].reshape(1, d), ln2_b[l].reshape(1, d))
    return xf.reshape(nb, s, d)
```

```python
import functools

import numpy as np
import jax
import jax.numpy as jnp
from jax import lax
from jax.experimental import pallas as pl
from jax.experimental.pallas import tpu as pltpu

F32 = jnp.float32
BF16 = jnp.bfloat16

D_MODEL = 2048
SEQ = 2048
DEPTH = 2
HEAD_DIM = 128
LANES = 128
MLA_HEADS = 4
DIL_HEADS = 6
NSA_HEADS = 6
MLA_Q_RANK = 384
MLA_KV_RANK = 128
MLA_NOPE_DIM = 128
MLA_ROPE_DIM = 64
MLA_QK_PAD = 256
DIL_PAIRS = ((128, 1), (512, 4), (2048, 16))
NSA_CMP_LEN = 32
NSA_CMP_STRIDE = 16
NSA_CMP_HIDDEN = 256
NSA_SEL_BLOCK = 64
NSA_TOP_N = 16
NSA_WINDOW = 512
NSA_BRANCHES = 3
NSA_FORCE_SCORE = 1e4
ROPE_THETA = 500000.0
ROT_DIM = HEAD_DIM // 4
N_GROUPS = 4
EXPERTS_PER_GROUP = 8
N_EXPERTS = N_GROUPS * EXPERTS_PER_GROUP
EXPERT_FF = 512
MOE_TOP_K = 2
MOE_ROWS = 256
DN_ALPHA = (2 * DEPTH) ** 0.25
LN_EPS = 1e-5
RMS_EPS = 1e-6
NEG_INF = -1e30

N_CMP_PAD = SEQ // NSA_CMP_STRIDE
N_SEL_BLOCKS = SEQ // NSA_SEL_BLOCK

BLK_QLAT = 0
BLK_KVLAT = 3
BLK_KROPE = 4
BLK_DQ = 5
BLK_DK = 11
BLK_DV = 17
BLK_NQ = 23
BLK_NKC = 29
BLK_NVC = 30
BLK_NKS = 31
BLK_NVS = 32
BLK_NKW = 33
BLK_NVW = 34
BLK_GATE = 35
N_IN_BLOCKS = 36
IN_COLS = 4434
IN_SHIFT_FROM = MLA_Q_RANK + MLA_KV_RANK - MLA_ROPE_DIM
IN_COLS_PAD = N_IN_BLOCKS * LANES
KROPE_LANE0 = LANES - MLA_ROPE_DIM

INPROJ_TM = 2048
INPROJ_SUB = 512
INPROJ_TN = 512
FLASH_T = 1024
NSA_T = 256
NSA_TQ = 256
ROW_TM = 256
MLAPREP_TM = 512
OUTPROJ_TM = 512

VMEM_LIMIT = 56 * 1024 * 1024


def _cparams(sem):
    return pltpu.CompilerParams(dimension_semantics=sem, vmem_limit_bytes=VMEM_LIMIT)


def _rope_tables(pos, rot_dim, keep_rest, lane0=0):
    half = rot_dim // 2
    inv_freq = np.power(np.float32(ROPE_THETA), -np.arange(half, dtype=np.float32) / np.float32(half))
    ang = (pos.astype(np.float32)[:, None] * inv_freq[None, :]).astype(np.float32)
    cos, sin = np.cos(ang).astype(np.float32), np.sin(ang).astype(np.float32)
    n = pos.shape[0]
    c = np.full((n, LANES), 1.0 if keep_rest else 0.0, np.float32)
    s1 = np.zeros((n, LANES), np.float32)
    s2 = np.zeros((n, LANES), np.float32)
    c[:, lane0:lane0 + half] = cos
    c[:, lane0 + half:lane0 + rot_dim] = cos
    s1[:, lane0 + half:lane0 + rot_dim] = sin
    s2[:, lane0:lane0 + half] = -sin
    return jnp.asarray(c), jnp.asarray(s1), jnp.asarray(s2)


def _rope_lanes(a, c, s1, s2, half):
    return a * c + pltpu.roll(a, half, 1) * s1 + pltpu.roll(a, LANES - half, 1) * s2


def _distance_bias(mult_fn, diags, t, rows=None):
    d = np.asarray(diags, np.int64)[:, None, None] * t
    r = np.arange(t if rows is None else rows, dtype=np.int64)[None, :, None]
    c = np.arange(t, dtype=np.int64)[None, None, :]
    mult = mult_fn(d + r - c)
    return jnp.asarray(np.where(mult > 0, np.log(np.maximum(mult, 1).astype(np.float32)), NEG_INF).astype(np.float32))


def _causal_mult(dist):
    return (dist >= 0).astype(np.int64)


def _dilated_mult(dist):
    m = np.zeros_like(dist)
    for window, dil in DIL_PAIRS:
        m = m + ((dist >= 0) & (dist % dil == 0) & (dist <= (window // dil) * dil)).astype(np.int64)
    return m


def _window_mult(dist):
    return ((dist >= 0) & (dist <= NSA_WINDOW - 1)).astype(np.int64)


def _in_block(blk, lo, n):
    return jnp.logical_and(blk >= lo, blk < lo + n)


def _inproj_kernel(x_ref, w_ref, c_ref, s1_ref, s2_ref, o_ref, *, scale):
    j = pl.program_id(0)
    for hh in range(INPROJ_TM // INPROJ_SUB):
        rows = slice(hh * INPROJ_SUB, (hh + 1) * INPROJ_SUB)
        acc = _qk(x_ref[rows, :], w_ref[0])
        c, s1, s2 = c_ref[rows, :], s1_ref[rows, :], s2_ref[rows, :]
        for u in range(INPROJ_TN // LANES):
            blk = j * (INPROJ_TN // LANES) + u
            is_q = jnp.logical_or(_in_block(blk, BLK_DQ, DIL_HEADS), _in_block(blk, BLK_NQ, NSA_HEADS))
            is_rope = is_q | _in_block(blk, BLK_DK, DIL_HEADS) | (blk == BLK_NKS) | (blk == BLK_NKW)
            sc = jnp.where(is_q, scale, 1.0).astype(F32)
            a = acc[:, u * LANES:(u + 1) * LANES]
            r = _rope_lanes(a, jnp.where(is_rope, c, 1.0), jnp.where(is_rope, s1, 0.0),
                            jnp.where(is_rope, s2, 0.0), ROT_DIM // 2)
            o_ref[rows, u * LANES:(u + 1) * LANES] = (r * sc).astype(o_ref.dtype)


def _inproj(xb, w_in_s, layer, tabs):
    t = xb.shape[0]
    tm, tn = INPROJ_TM, INPROJ_TN
    nrow = SEQ // tm
    tab_spec = pl.BlockSpec((tm, LANES), lambda j, i: (i % nrow, 0))
    return pl.pallas_call(
        functools.partial(_inproj_kernel, scale=HEAD_DIM ** -0.5),
        out_shape=jax.ShapeDtypeStruct((t, IN_COLS_PAD), BF16),
        grid=(IN_COLS_PAD // tn, t // tm),
        in_specs=[pl.BlockSpec((tm, D_MODEL), lambda j, i: (i, 0)),
                  pl.BlockSpec((1, tn, D_MODEL), lambda j, i: (layer, j, 0)),
                  tab_spec, tab_spec, tab_spec],
        out_specs=pl.BlockSpec((tm, tn), lambda j, i: (i, j)),
        compiler_params=_cparams(("arbitrary", "arbitrary")),
        name="inproj",
    )(xb, w_in_s, *tabs)


def _rms(xf, g):
    return xf * lax.rsqrt(jnp.mean(jnp.square(xf), axis=-1, keepdims=True) + RMS_EPS) * g


def _mlaprep_kernel(ql_ref, kvl_ref, kr_ref, gq_ref, gkv_ref, wq_ref, wkn_ref, wv_ref,
                    c_ref, s1_ref, s2_ref, q_ref, k_ref, v_ref, *, scale):
    c, s1, s2 = c_ref[...], s1_ref[...], s2_ref[...]
    half = MLA_ROPE_DIM // 2
    qn = _rms(ql_ref[...].astype(F32), gq_ref[...]).astype(BF16)
    q = jnp.dot(qn, wq_ref[...], preferred_element_type=F32)
    for h in range(MLA_HEADS):
        lo = h * MLA_QK_PAD
        q_ref[:, lo:lo + LANES] = (q[:, lo:lo + LANES] * scale).astype(q_ref.dtype)
        a = q[:, lo + LANES:lo + 2 * LANES]
        q_ref[:, lo + LANES:lo + 2 * LANES] = (_rope_lanes(a, c, s1, s2, half) * scale).astype(q_ref.dtype)
    kvn = _rms(kvl_ref[...].astype(F32), gkv_ref[...]).astype(BF16)
    kn = jnp.dot(kvn, wkn_ref[...], preferred_element_type=F32)
    v_ref[...] = jnp.dot(kvn, wv_ref[...], preferred_element_type=F32).astype(v_ref.dtype)
    kpe = _rope_lanes(kr_ref[...].astype(F32), c, s1, s2, half).astype(k_ref.dtype)
    for h in range(MLA_HEADS):
        lo = h * MLA_QK_PAD
        k_ref[:, lo:lo + LANES] = kn[:, h * LANES:(h + 1) * LANES].astype(k_ref.dtype)
        k_ref[:, lo + LANES:lo + 2 * LANES] = kpe


def _mlaprep(hproj, gq, gkv, wq_p, wkn, wv, tabs):
    t = hproj.shape[0]
    tm = MLAPREP_TM
    nrow = SEQ // tm
    tab_spec = pl.BlockSpec((tm, LANES), lambda i: (i % nrow, 0))
    full = lambda shape: pl.BlockSpec(shape, lambda i: (0,) * len(shape))
    return pl.pallas_call(
        functools.partial(_mlaprep_kernel, scale=(MLA_NOPE_DIM + MLA_ROPE_DIM) ** -0.5),
        out_shape=(jax.ShapeDtypeStruct((t, MLA_HEADS * MLA_QK_PAD), BF16),
                   jax.ShapeDtypeStruct((t, MLA_HEADS * MLA_QK_PAD), BF16),
                   jax.ShapeDtypeStruct((t, MLA_HEADS * HEAD_DIM), BF16)),
        grid=(t // tm,),
        in_specs=[pl.BlockSpec((tm, MLA_Q_RANK), lambda i: (i, BLK_QLAT * LANES // MLA_Q_RANK)),
                  pl.BlockSpec((tm, LANES), lambda i: (i, BLK_KVLAT)),
                  pl.BlockSpec((tm, LANES), lambda i: (i, BLK_KROPE)),
                  full((1, MLA_Q_RANK)), full((1, MLA_KV_RANK)),
                  full((MLA_Q_RANK, MLA_HEADS * MLA_QK_PAD)),
                  full((MLA_KV_RANK, MLA_HEADS * MLA_NOPE_DIM)),
                  full((MLA_KV_RANK, MLA_HEADS * HEAD_DIM)),
                  tab_spec, tab_spec, tab_spec],
        out_specs=(pl.BlockSpec((tm, MLA_HEADS * MLA_QK_PAD), lambda i: (i, 0)),
                   pl.BlockSpec((tm, MLA_HEADS * MLA_QK_PAD), lambda i: (i, 0)),
                   pl.BlockSpec((tm, MLA_HEADS * HEAD_DIM), lambda i: (i, 0))),
        compiler_params=_cparams(("arbitrary",)),
        name="mlaprep",
    )(hproj, hproj, hproj, gq, gkv, wq_p, wkn, wv, *tabs)


def _gelu_tanh(x):
    return 0.5 * x * (1.0 + jnp.tanh(0.7978845608028654 * (x + 0.044715 * (x * x * x))))


def _compress_kernel(tk_ref, tv_ref, pek_ref, pev_ref, w1k_ref, w2k_ref, w1v_ref, w2v_ref,
                     c_ref, s1_ref, s2_ref, kc_ref, vc_ref):
    half_in = NSA_CMP_STRIDE * HEAD_DIM

    def comp(t_ref, pe_ref, w1_ref, w2_ref):
        t = t_ref[0].astype(F32)
        a = (t + pe_ref[0:1, :]).astype(BF16)
        b = (t + pe_ref[1:2, :]).astype(BF16)
        y0 = jnp.dot(a, w1_ref[0:half_in, :], preferred_element_type=F32)
        y1 = jnp.dot(b, w1_ref[half_in:2 * half_in, :], preferred_element_type=F32)
        hid = _gelu_tanh(y0 + pltpu.roll(y1, N_CMP_PAD - 1, 0))
        return jnp.dot(hid.astype(BF16), w2_ref[...], preferred_element_type=F32)

    kc = comp(tk_ref, pek_ref, w1k_ref, w2k_ref)
    kc_ref[0] = _rope_lanes(kc, c_ref[...], s1_ref[...], s2_ref[...], ROT_DIM // 2).astype(kc_ref.dtype)
    vc_ref[0] = comp(tv_ref, pev_ref, w1v_ref, w2v_ref).astype(vc_ref.dtype)


def _compress(tk2, tv2, pek, pev, w1k, w2k, w1v, w2v, tabs):
    nb = tk2.shape[0]
    wide = NSA_CMP_STRIDE * HEAD_DIM
    full = lambda shape: pl.BlockSpec(shape, lambda b: (0,) * len(shape))
    bspec = pl.BlockSpec((1, N_CMP_PAD, wide), lambda b: (b, 0, 0))
    ospec = pl.BlockSpec((1, N_CMP_PAD, HEAD_DIM), lambda b: (b, 0, 0))
    return pl.pallas_call(
        _compress_kernel,
        out_shape=(jax.ShapeDtypeStruct((nb, N_CMP_PAD, HEAD_DIM), BF16),) * 2,
        grid=(nb,),
        in_specs=[bspec, bspec, full((2, wide)), full((2, wide)),
                  full((2 * wide, NSA_CMP_HIDDEN)), full((NSA_CMP_HIDDEN, HEAD_DIM)),
                  full((2 * wide, NSA_CMP_HIDDEN)), full((NSA_CMP_HIDDEN, HEAD_DIM)),
                  full((N_CMP_PAD, LANES)), full((N_CMP_PAD, LANES)), full((N_CMP_PAD, LANES))],
        out_specs=(ospec, ospec),
        compiler_params=_cparams(("arbitrary",)),
        name="compress",
    )(tk2, tv2, pek, pev, w1k, w2k, w1v, w2v, *tabs)


def _qk(q, k):
    return lax.dot_general(q, k, (((1,), (1,)), ((), ())), preferred_element_type=F32)


def _lane_fold(x, op):
    out = x[:, 0:LANES]
    for u in range(1, x.shape[1] // LANES):
        out = op(out, x[:, u * LANES:(u + 1) * LANES])
    return out


def _split3(x):
    hi = x.astype(BF16)
    r1 = x - hi.astype(F32)
    mid = r1.astype(BF16)
    lo = (r1 - mid.astype(F32)).astype(BF16)
    return hi, mid, lo


def _attend(q, k_ref, v_ref, bias_fn, j_lo, n_tiles, s_scr, *, tk, groups=1):
    rows = q.shape[0]
    dv = v_ref.shape[-1]

    def pass1(t, mx):
        off = pl.multiple_of((j_lo + t) * tk, tk)
        s = _qk(q, k_ref[pl.ds(off, tk), :])
        bias = bias_fn(j_lo + t)
        if groups > 1:
            s = (s.reshape(groups, rows // groups, tk) + bias[None]).reshape(rows, tk)
        else:
            s = s + bias
        s_scr[t] = s
        return jnp.maximum(mx, _lane_fold(s, jnp.maximum))

    mx = lax.fori_loop(0, n_tiles, pass1, jnp.full((rows, LANES), NEG_INF, F32))
    m = jnp.broadcast_to(jnp.max(mx, axis=-1, keepdims=True), (rows, LANES))

    def pass2(t, carry):
        lsum, acc = carry
        off = pl.multiple_of((j_lo + t) * tk, tk)
        s = s_scr[t]
        parts = [jnp.exp(s[:, u * LANES:(u + 1) * LANES] - m) for u in range(tk // LANES)]
        for p in parts:
            lsum = lsum + p
        pb = jnp.concatenate([p.astype(BF16) for p in parts], axis=1)
        acc = acc + jnp.dot(pb, v_ref[pl.ds(off, tk), :], preferred_element_type=F32)
        return lsum, acc

    lsum, acc = lax.fori_loop(0, n_tiles, pass2, (jnp.zeros((rows, LANES), F32), jnp.zeros((rows, dv), F32)))
    return acc / jnp.sum(lsum, axis=-1, keepdims=True)


def _flash_kernel(q_ref, k_ref, v_ref, bias_ref, o_ref, s_scr):
    i = pl.program_id(2)
    last = bias_ref.shape[0] - 1
    out = _attend(q_ref[...], k_ref, v_ref, lambda j: bias_ref[jnp.minimum(i - j, last)], 0, i + 1, s_scr,
                  tk=FLASH_T)
    o_ref[...] = out.astype(o_ref.dtype)


def _flash(q_arr, k_arr, v_arr, bias, *, nb, heads, dqk, dv, q_blk0, k_blk0, v_blk0):
    t = FLASH_T
    nq = SEQ // t
    return pl.pallas_call(
        _flash_kernel,
        out_shape=jax.ShapeDtypeStruct((nb * SEQ, heads * dv), BF16),
        grid=(nb, heads, nq),
        in_specs=[pl.BlockSpec((t, dqk), lambda b, h, i: (b * nq + i, q_blk0 + h)),
                  pl.BlockSpec((SEQ, dqk), lambda b, h, i: (b, k_blk0 + h)),
                  pl.BlockSpec((SEQ, dv), lambda b, h, i: (b, v_blk0 + h)),
                  pl.BlockSpec(bias.shape, lambda b, h, i: (0, 0, 0))],
        out_specs=pl.BlockSpec((t, dv), lambda b, h, i: (b * nq + i, h)),
        scratch_shapes=[pltpu.VMEM((nq, t, t), F32)],
        compiler_params=_cparams(("arbitrary", "arbitrary", "arbitrary")),
        name="flash",
    )(q_arr, k_arr, v_arr, bias)


def _nsa_kernel(q0_ref, q1_ref, q2_ref, q3_ref, q4_ref, q5_ref, kc_ref, vc_ref, ks_ref, vs_ref, kw_ref, vw_ref,
                gate_ref, cover_ref, expand_ref, wbias_ref, o_ref, bias_scr, s_scr):
    i = pl.program_id(1)
    tq, tk = NSA_TQ, NSA_T
    per = tq // tk
    nh = NSA_HEADS
    rows = nh * tq
    pos = i * tq + lax.broadcasted_iota(jnp.int32, (tq, 1), 0)
    q = jnp.concatenate([r[...] for r in (q0_ref, q1_ref, q2_ref, q3_ref, q4_ref, q5_ref)], axis=0)

    cidx = lax.broadcasted_iota(jnp.int32, (tq, N_CMP_PAD), 1)
    valid_c = (cidx * NSA_CMP_STRIDE + (NSA_CMP_LEN - 1) <= pos) & (cidx < N_CMP_PAD - 1)
    valid_cf = valid_c.astype(F32)
    cbias = jnp.where(valid_c, 0.0, NEG_INF).astype(F32)
    s = (_qk(q, kc_ref[0]).reshape(nh, tq, N_CMP_PAD) + cbias[None]).reshape(rows, N_CMP_PAD)
    p = (jnp.exp(s - jnp.max(s, axis=-1, keepdims=True)).reshape(nh, tq, N_CMP_PAD) * valid_cf[None])
    p = p.reshape(rows, N_CMP_PAD)
    l = jnp.sum(p, axis=-1, keepdims=True)
    p = p / jnp.where(l > 0.0, l, 1.0)
    o_cmp = jnp.dot(p.astype(BF16), vc_ref[0], preferred_element_type=F32)
    psum = jnp.sum(p.reshape(nh, tq, N_CMP_PAD), axis=0)

    cover_t = cover_ref[...]
    imp = sum(jnp.dot(cover_t, part, preferred_element_type=F32) for part in _split3(psum.T))

    pos_t = i * tq + lax.broadcasted_iota(jnp.int32, (N_SEL_BLOCKS, tq), 1)
    jidx = lax.broadcasted_iota(jnp.int32, (N_SEL_BLOCKS, tq), 0)
    qblk = pos_t // NSA_SEL_BLOCK
    valid_s = jidx * NSA_SEL_BLOCK <= pos_t
    forced = (jidx == 0) | (jidx == qblk) | (jidx == qblk - 1)
    score = jnp.where(valid_s, jnp.where(forced, NSA_FORCE_SCORE, imp), -1.0)
    rank = jnp.zeros((N_SEL_BLOCKS, tq), jnp.int32)
    for ii in range(N_SEL_BLOCKS):
        si = score[ii:ii + 1, :]
        rank = rank + ((si > score) | ((si == score) & (ii < jidx))).astype(jnp.int32)
    sel_t = (rank < NSA_TOP_N).astype(F32)
    sel = jnp.concatenate([sel_t, jnp.zeros((LANES - N_SEL_BLOCKS, tq), F32)], axis=0).T.astype(BF16)

    n_causal = per * (i + 1)

    def fill(j, carry):
        hit = jnp.dot(sel, expand_ref[j], preferred_element_type=F32)
        kpos = j * tk + lax.broadcasted_iota(jnp.int32, (tq, tk), 1)
        bias_scr[j] = jnp.where((hit > 0.5) & (kpos <= pos), 0.0, NEG_INF).astype(F32)
        return carry

    lax.fori_loop(0, n_causal, fill, 0)

    o_slc = _attend(q, ks_ref, vs_ref, lambda j: bias_scr[j], 0, n_causal, s_scr, tk=tk, groups=nh)
    far = wbias_ref.shape[0] - per
    w_lo = jnp.maximum(per * i - far, 0)
    o_win = _attend(q, kw_ref, vw_ref, lambda j: wbias_ref[per * i - j + per - 1], w_lo, n_causal - w_lo, s_scr,
                    tk=tk, groups=nh)

    gates = jax.nn.sigmoid(gate_ref[...].astype(F32))
    for h in range(nh):
        sl = slice(h * tq, (h + 1) * tq)
        g0 = NSA_BRANCHES * h
        out = (gates[:, g0:g0 + 1] * o_cmp[sl] + gates[:, g0 + 1:g0 + 2] * o_slc[sl]
               + gates[:, g0 + 2:g0 + 3] * o_win[sl])
        o_ref[:, h * LANES:(h + 1) * LANES] = out.astype(o_ref.dtype)


def _nsa(hproj, kc, vc, cover, expand, wbias, nb):
    t = NSA_TQ
    nq = SEQ // t
    nk = SEQ // NSA_T
    wide = NSA_HEADS * HEAD_DIM
    q_spec = lambda h: pl.BlockSpec((t, LANES), lambda b, i: (b * nq + i, BLK_NQ + h))
    kv_spec = lambda blk: pl.BlockSpec((SEQ, LANES), lambda b, i: (b, blk))
    cspec = pl.BlockSpec((1, N_CMP_PAD, HEAD_DIM), lambda b, i: (b, 0, 0))
    full = lambda shape: pl.BlockSpec(shape, lambda b, i: (0,) * len(shape))
    return pl.pallas_call(
        _nsa_kernel,
        out_shape=jax.ShapeDtypeStruct((nb * SEQ, wide), BF16),
        grid=(nb, nq),
        in_specs=[q_spec(h) for h in range(NSA_HEADS)]
        + [cspec, cspec, kv_spec(BLK_NKS), kv_spec(BLK_NVS), kv_spec(BLK_NKW), kv_spec(BLK_NVW),
           pl.BlockSpec((t, LANES), lambda b, i: (b * nq + i, BLK_GATE)),
           full(cover.shape), full(expand.shape), full(wbias.shape)],
        out_specs=pl.BlockSpec((t, wide), lambda b, i: (b * nq + i, 0)),
        scratch_shapes=[pltpu.VMEM((nk, t, NSA_T), F32), pltpu.VMEM((nk, NSA_HEADS * t, NSA_T), F32)],
        compiler_params=_cparams(("arbitrary", "arbitrary")),
        name="nsa",
    )(*([hproj] * NSA_HEADS), kc, vc, hproj, hproj, hproj, hproj, hproj, cover, expand, wbias)


def _layer_norm(y, g, b):
    mu = jnp.mean(y, axis=-1, keepdims=True)
    var = jnp.mean(jnp.square(y - mu), axis=-1, keepdims=True)
    return (y - mu) * lax.rsqrt(var + LN_EPS) * g + b


def _lane_min(x):
    return jnp.min(x, axis=-1, keepdims=True)


def _lane_max(x):
    return jnp.max(x, axis=-1, keepdims=True)


def _route(logits):
    lane = lax.broadcasted_iota(jnp.int32, logits.shape, 1)
    lane_f = lane.astype(F32)
    far = float(LANES)
    is_grp = lane < N_GROUPS
    lg = jnp.where(is_grp, logits, NEG_INF)
    eg = jnp.where(is_grp, jnp.exp(lg - _lane_max(lg)), 0.0)
    prob = eg / jnp.sum(eg, axis=-1, keepdims=True)
    p_g = _lane_max(prob)
    g_idx = _lane_min(jnp.where(is_grp & (prob == p_g), lane_f, far))
    e_lo = N_GROUPS + EXPERTS_PER_GROUP * g_idx
    in_grp = (lane_f >= e_lo) & (lane_f < e_lo + EXPERTS_PER_GROUP)
    le = jnp.where(in_grp, logits, NEG_INF)
    v1 = _lane_max(le)
    i1 = _lane_min(jnp.where(in_grp & (le == v1), lane_f, far))
    rest = in_grp & (lane_f != i1)
    le2 = jnp.where(rest, logits, NEG_INF)
    v2 = _lane_max(le2)
    i2 = _lane_min(jnp.where(rest & (le2 == v2), lane_f, far))
    e21 = jnp.exp(v2 - v1)
    den = 1.0 + e21
    gate1 = p_g * (1.0 / den)
    gate2 = p_g * (e21 / den)
    out = jnp.where(lane == 0, i1 - N_GROUPS, 0.0)
    out = jnp.where(lane == 1, i2 - N_GROUPS, out)
    out = jnp.where(lane == 2, gate1, out)
    out = jnp.where(lane == 3, gate2, out)
    return out


def _outproj_kernel(om_ref, od_ref, on_ref, w_ref, x_ref, g_ref, b_ref, wr_ref, x1_ref, route_ref):
    n_mla = MLA_HEADS * HEAD_DIM
    n_dil = DIL_HEADS * HEAD_DIM
    for hh in range(OUTPROJ_TM // ROW_TM):
        rows = slice(hh * ROW_TM, (hh + 1) * ROW_TM)
        mix = jnp.dot(om_ref[rows, :], w_ref[0, 0:n_mla, :], preferred_element_type=F32)
        mix = mix + jnp.dot(od_ref[rows, :], w_ref[0, n_mla:n_mla + n_dil, :], preferred_element_type=F32)
        mix = mix + jnp.dot(on_ref[rows, :], w_ref[0, n_mla + n_dil:, :], preferred_element_type=F32)
        x1 = _layer_norm(DN_ALPHA * x_ref[rows, :] + mix, g_ref[...], b_ref[...])
        x1_ref[rows, :] = x1
        xh = x1.astype(BF16)
        xm = (x1 - xh.astype(F32)).astype(BF16)
        both = jnp.dot(xh, wr_ref[...], preferred_element_type=F32)
        logits = both[:, 0:LANES] + (both[:, LANES:2 * LANES]
                                     + jnp.dot(xm, wr_ref[:, 0:LANES], preferred_element_type=F32))
        route_ref[rows, :] = _route(logits)


def _outproj(o_mla, o_dil, o_nsa, w_out_s, layer, x, g, b, wr3):
    t = x.shape[0]
    tm = OUTPROJ_TM
    full = lambda shape: pl.BlockSpec(shape, lambda i: (0,) * len(shape))
    row = lambda w: pl.BlockSpec((tm, w), lambda i: (i, 0))
    return pl.pallas_call(
        _outproj_kernel,
        out_shape=(jax.ShapeDtypeStruct((t, D_MODEL), F32), jax.ShapeDtypeStruct((t, LANES), F32)),
        grid=(t // tm,),
        in_specs=[row(o_mla.shape[1]), row(o_dil.shape[1]), row(o_nsa.shape[1]),
                  pl.BlockSpec((1,) + w_out_s.shape[1:], lambda i: (layer, 0, 0)),
                  row(D_MODEL), full((1, D_MODEL)), full((1, D_MODEL)), full(wr3.shape)],
        out_specs=(row(D_MODEL), row(LANES)),
        compiler_params=_cparams(("arbitrary",)),
        name="outproj",
    )(o_mla, o_dil, o_nsa, w_out_s, x, g, b, wr3)


def _dispatch_kernel(fill_ref, pos_ref, x1_ref, xb_hbm, stage, zbuf, sem, zsem):
    i = pl.program_id(0)
    n = pl.num_programs(0)
    tm = ROW_TM
    slot = i % 2

    def drain(s):
        for _ in range(MOE_TOP_K):
            pltpu.make_async_copy(stage.at[s], xb_hbm.at[pl.ds(0, tm)], sem.at[s]).wait()

    def zero_copy(e):
        start = pl.multiple_of(fill_ref[e], MOE_ROWS)
        return pltpu.make_async_copy(zbuf, xb_hbm.at[pl.ds(start, MOE_ROWS)], zsem)

    @pl.when(i == 0)
    def _():
        zbuf[...] = jnp.zeros_like(zbuf)
        for e in range(N_EXPERTS):
            @pl.when(fill_ref[e] >= 0)
            def _():
                zero_copy(e).start()
        for e in range(N_EXPERTS):
            @pl.when(fill_ref[e] >= 0)
            def _():
                zero_copy(e).wait()

        def tail_copy(b):
            start = pl.multiple_of(b * MOE_ROWS, MOE_ROWS)
            return pltpu.make_async_copy(zbuf, xb_hbm.at[pl.ds(start, MOE_ROWS)], zsem)

        n_blocks = xb_hbm.shape[0] // MOE_ROWS
        lax.fori_loop(fill_ref[N_EXPERTS], n_blocks, lambda b, c: (tail_copy(b).start(), c)[1], 0)
        lax.fori_loop(fill_ref[N_EXPERTS], n_blocks, lambda b, c: (tail_copy(b).wait(), c)[1], 0)

    @pl.when(i >= 2)
    def _():
        drain(slot)

    stage[slot] = x1_ref[...]

    for r in range(tm):
        for k in range(MOE_TOP_K):
            p = pos_ref[0, 0, MOE_TOP_K * r + k]
            pltpu.make_async_copy(stage.at[slot, pl.ds(r, 1)], xb_hbm.at[pl.ds(p, 1)],
                                  sem.at[slot]).start(priority=k % 2)

    @pl.when(i == n - 1)
    def _():
        drain(1 - slot)
        drain(slot)


def _dispatch(fill, pos3, x1, n_rows):
    t = x1.shape[0]
    tm = ROW_TM
    grid_spec = pltpu.PrefetchScalarGridSpec(
        num_scalar_prefetch=1,
        grid=(t // tm,),
        in_specs=[pl.BlockSpec((1, 1, MOE_TOP_K * tm), lambda i, fl: (i, 0, 0), memory_space=pltpu.SMEM),
                  pl.BlockSpec((tm, D_MODEL), lambda i, fl: (i, 0))],
        out_specs=pl.BlockSpec(memory_space=pl.ANY),
        scratch_shapes=[pltpu.VMEM((2, tm, D_MODEL), F32), pltpu.VMEM((MOE_ROWS, D_MODEL), F32),
                        pltpu.SemaphoreType.DMA((2,)), pltpu.SemaphoreType.DMA(())],
    )
    return pl.pallas_call(
        _dispatch_kernel,
        out_shape=jax.ShapeDtypeStruct((n_rows, D_MODEL), F32),
        grid_spec=grid_spec,
        compiler_params=_cparams(("arbitrary",)),
        name="dispatch",
    )(fill, pos3, x1)


def _experts_kernel(blk_ex_ref, first_ref, nxt_ref, nused_ref, x_ref, wg_hbm, wu_hbm, wd_hbm, y_ref,
                    wg_f, wu_f, wd_f, wg_b, wu_b, wd_b, sem, *, layer):
    b = pl.program_id(0)
    nused = nused_ref[0]

    def weight_copies(e, s):
        return (pltpu.make_async_copy(wg_hbm.at[layer, e], wg_f.at[s], sem.at[s]),
                pltpu.make_async_copy(wu_hbm.at[layer, e], wu_f.at[s], sem.at[s]),
                pltpu.make_async_copy(wd_hbm.at[layer, e], wd_f.at[s], sem.at[s]))

    @pl.when(b == 0)
    def _():
        for cp in weight_copies(blk_ex_ref[0], 0):
            cp.start()

    @pl.when(b < nused)
    def _():
        first = first_ref[b]

        @pl.when(first > 0)
        def _():
            s = first - 1
            for cp in weight_copies(blk_ex_ref[b], s):
                cp.wait()

            @pl.when(nxt_ref[b] >= 0)
            def _():
                for cp in weight_copies(nxt_ref[b], 1 - s):
                    cp.start()

            wg_b[...] = wg_f[s].astype(BF16)
            wu_b[...] = wu_f[s].astype(BF16)
            wd_b[...] = wd_f[s].astype(BF16)

        rows = x_ref[...].astype(BF16)
        gate = jnp.dot(rows, wg_b[...], preferred_element_type=F32)
        up = jnp.dot(rows, wu_b[...], preferred_element_type=F32)
        hid = (gate * jax.nn.sigmoid(gate) * up).astype(BF16)
        y_ref[...] = jnp.dot(hid, wd_b[...], preferred_element_type=F32)

    @pl.when(b >= nused)
    def _():
        y_ref[...] = jnp.zeros_like(y_ref)


def _experts(blk_ex, first, nxt, nused, xb_rows, w_gate, w_up, w_down, layer):
    n_blocks = xb_rows.shape[0] // MOE_ROWS
    hbm = pl.BlockSpec(memory_space=pl.ANY)
    grid_spec = pltpu.PrefetchScalarGridSpec(
        num_scalar_prefetch=4,
        grid=(n_blocks,),
        in_specs=[pl.BlockSpec((MOE_ROWS, D_MODEL), lambda b, ex, fi, nx, nu: (jnp.minimum(b, nu[0] - 1), 0)),
                  hbm, hbm, hbm],
        out_specs=pl.BlockSpec((MOE_ROWS, D_MODEL), lambda b, ex, fi, nx, nu: (b, 0)),
        scratch_shapes=[pltpu.VMEM((2, D_MODEL, EXPERT_FF), F32), pltpu.VMEM((2, D_MODEL, EXPERT_FF), F32),
                        pltpu.VMEM((2, EXPERT_FF, D_MODEL), F32),
                        pltpu.VMEM((D_MODEL, EXPERT_FF), BF16), pltpu.VMEM((D_MODEL, EXPERT_FF), BF16),
                        pltpu.VMEM((EXPERT_FF, D_MODEL), BF16), pltpu.SemaphoreType.DMA((2,))],
    )
    return pl.pallas_call(
        functools.partial(_experts_kernel, layer=layer),
        out_shape=jax.ShapeDtypeStruct((n_blocks * MOE_ROWS, D_MODEL), F32),
        grid_spec=grid_spec,
        compiler_params=_cparams(("arbitrary",)),
        name="experts",
    )(blk_ex, first, nxt, nused, xb_rows, w_gate, w_up, w_down)


def _combine_kernel(pos_cur_ref, pos_nxt_ref, y_hbm, x1_ref, route_ref, g_ref, b_ref, o_ref, ob_ref, ybuf, sem):
    i = pl.program_id(0)
    n = pl.num_programs(0)
    tm = ROW_TM

    def issue(pos_ref, slot):
        for r in range(tm):
            for k in range(MOE_TOP_K):
                p = pos_ref[0, 0, MOE_TOP_K * r + k]
                pltpu.make_async_copy(y_hbm.at[pl.ds(p, 1)], ybuf.at[slot, k, pl.ds(r, 1)], sem.at[slot]).start()

    @pl.when(i == 0)
    def _():
        issue(pos_cur_ref, 0)

    @pl.when(i + 1 < n)
    def _():
        issue(pos_nxt_ref, (i + 1) % 2)

    slot = i % 2
    for k in range(MOE_TOP_K):
        pltpu.make_async_copy(y_hbm.at[pl.ds(0, tm)], ybuf.at[slot, k], sem.at[slot]).wait()
    route = route_ref[...]
    ffn = route[:, 2:3] * ybuf[slot, 0] + route[:, 3:4] * ybuf[slot, 1]
    x2 = _layer_norm(DN_ALPHA * x1_ref[...] + ffn, g_ref[...], b_ref[...])
    o_ref[...] = x2
    ob_ref[...] = x2.astype(ob_ref.dtype)


def _combine(pos3, yb, x1, route, g, b):
    t = x1.shape[0]
    tm = ROW_TM
    nt = t // tm
    pos_spec = lambda shift: pl.BlockSpec((1, 1, MOE_TOP_K * tm), lambda i: (jnp.minimum(i + shift, nt - 1), 0, 0),
                                          memory_space=pltpu.SMEM)
    row = lambda w: pl.BlockSpec((tm, w), lambda i: (i, 0))
    full = lambda shape: pl.BlockSpec(shape, lambda i: (0,) * len(shape))
    return pl.pallas_call(
        _combine_kernel,
        out_shape=(jax.ShapeDtypeStruct((t, D_MODEL), F32), jax.ShapeDtypeStruct((t, D_MODEL), BF16)),
        grid=(nt,),
        in_specs=[pos_spec(0), pos_spec(1), pl.BlockSpec(memory_space=pl.ANY), row(D_MODEL), row(LANES),
                  full((1, D_MODEL)), full((1, D_MODEL))],
        out_specs=(row(D_MODEL), row(D_MODEL)),
        scratch_shapes=[pltpu.VMEM((2, MOE_TOP_K, tm, D_MODEL), F32), pltpu.SemaphoreType.DMA((2,))],
        compiler_params=_cparams(("arbitrary",)),
        name="combine",
    )(pos3, pos3, yb, x1, route, g, b)


def _dispatch_plan(route, t):
    m = t * MOE_TOP_K
    n_blocks = -(-m // MOE_ROWS) + N_EXPERTS
    chunk = MOE_ROWS
    ex = route[:, 0:MOE_TOP_K].reshape(m // chunk, chunk, 1)
    onehot = (ex == jnp.arange(N_EXPERTS, dtype=F32)[None, None, :]).astype(F32)
    tri = jnp.asarray(np.tril(np.ones((chunk, chunk), np.float32)))
    within = jnp.einsum('ij,cje->cie', tri, onehot)
    totals = jnp.sum(onehot, axis=1)
    before = jnp.cumsum(totals, axis=0) - totals
    counts = jnp.sum(totals, axis=0)
    padded = jnp.ceil(counts / MOE_ROWS) * MOE_ROWS
    pad_end = jnp.cumsum(padded)
    pad_start = pad_end - padded
    row = jnp.sum(onehot * (within - 1.0 + before[:, None, :] + pad_start[None, None, :]), axis=-1)
    pos = row.astype(jnp.int32).reshape(m)
    blk_start = jnp.arange(n_blocks, dtype=F32) * MOE_ROWS
    blk_ex = jnp.minimum(jnp.sum((pad_end[None, :] <= blk_start[:, None]).astype(jnp.int32), axis=1),
                         N_EXPERTS - 1).astype(jnp.int32)
    nused = (pad_end[-1] / MOE_ROWS).astype(jnp.int32).reshape(1)
    fill = jnp.concatenate([jnp.where(padded > 0, pad_end - MOE_ROWS, -1.0).astype(jnp.int32), nused])
    nonempty = padded > 0
    slot_e = (jnp.cumsum(nonempty.astype(jnp.int32)) - 1) % 2
    expert_ids = jnp.arange(N_EXPERTS, dtype=jnp.int32)
    later = lax.cummin(jnp.where(nonempty, expert_ids, N_EXPERTS), axis=0, reverse=True)
    nxt_e = jnp.concatenate([later[1:], jnp.full((1,), N_EXPERTS, jnp.int32)])
    nxt_e = jnp.where(nxt_e >= N_EXPERTS, -1, nxt_e)
    is_first = (blk_start == pad_start[blk_ex]) & (jnp.arange(n_blocks) < nused[0])
    first = jnp.where(is_first, 1 + slot_e[blk_ex], 0).astype(jnp.int32)
    return blk_ex, first, nxt_e[blk_ex].astype(jnp.int32), nused, fill, pos


W_HALF = LANES // 2


def _shift_kernel(a_ref, b_ref, tail_ref, o_ref):
    m = pl.program_id(0)
    n_src = IN_COLS // LANES
    nl = o_ref.shape[0]
    d = o_ref.shape[2]
    nc = d // LANES
    a2, b2, t2 = a_ref, b_ref, tail_ref

    def slab(ref4, col0, c, l):
        return ref4[pl.ds(col0, W_HALF), c, l, :].astype(o_ref.dtype)

    def copy(ref_top, top0, ref_bot, bot0):
        for l in range(nl):
            for c in range(nc):
                o_ref[l, 0:W_HALF, c * LANES:(c + 1) * LANES] = slab(ref_top, top0, c, l)
                if ref_bot is None:
                    o_ref[l, W_HALF:LANES, c * LANES:(c + 1) * LANES] = jnp.zeros((W_HALF, LANES), o_ref.dtype)
                else:
                    o_ref[l, W_HALF:LANES, c * LANES:(c + 1) * LANES] = slab(ref_bot, bot0, c, l)

    @pl.when(m < n_src)
    def _():
        copy(a2, 0, b2, 0)

    @pl.when(m == n_src)
    def _():
        copy(a2, 0, t2, 0)

    @pl.when(m == n_src + 1)
    def _():
        copy(t2, W_HALF, None, 0)


def _shift_w_in(w_in):
    nl, d, _ = w_in.shape
    n_src = IN_COLS // LANES
    n_keep = (IN_SHIFT_FROM + MLA_ROPE_DIM) // LANES
    last = n_src * (LANES // W_HALF) - 1
    nc = d // LANES
    view = lambda w: jnp.transpose(w.reshape(nl, nc, LANES, w.shape[-1]), (3, 1, 0, 2))
    w_t = view(w_in)
    tail_t = view(jnp.pad(w_in[:, :, n_src * LANES:], ((0, 0), (0, 0), (0, (n_src + 1) * LANES - IN_COLS))))
    first = lambda m: jnp.where(m < n_keep, 2 * m, 2 * m - 1)
    return pl.pallas_call(
        _shift_kernel,
        out_shape=jax.ShapeDtypeStruct((nl, IN_COLS_PAD, d), BF16),
        grid=(N_IN_BLOCKS,),
        in_specs=[pl.BlockSpec((W_HALF, nc, nl, LANES), lambda m: (jnp.minimum(first(m), last), 0, 0, 0)),
                  pl.BlockSpec((W_HALF, nc, nl, LANES), lambda m: (jnp.minimum(first(m) + 1, last), 0, 0, 0)),
                  pl.BlockSpec((LANES, nc, nl, LANES), lambda m: (0, 0, 0, 0))],
        out_specs=pl.BlockSpec((nl, LANES, d), lambda m: (0, m, 0)),
        compiler_params=_cparams(("arbitrary",)),
        name="shiftw",
    )(w_t, w_t, tail_t)


def _permute_w_q_up(w_q_up):
    w = w_q_up.reshape(MLA_Q_RANK, MLA_HEADS, MLA_NOPE_DIM + MLA_ROPE_DIM)
    zero = jnp.zeros((MLA_Q_RANK, MLA_HEADS, MLA_QK_PAD - MLA_NOPE_DIM - MLA_ROPE_DIM), w.dtype)
    w = jnp.concatenate([w[:, :, :MLA_NOPE_DIM], zero, w[:, :, MLA_NOPE_DIM:]], axis=-1)
    return w.reshape(MLA_Q_RANK, MLA_HEADS * MLA_QK_PAD).astype(BF16)


def _split_w_kv_up(w_kv_up):
    w = w_kv_up.reshape(MLA_KV_RANK, MLA_HEADS, MLA_NOPE_DIM + HEAD_DIM)
    wkn = w[:, :, :MLA_NOPE_DIM].reshape(MLA_KV_RANK, MLA_HEADS * MLA_NOPE_DIM)
    wv = w[:, :, MLA_NOPE_DIM:].reshape(MLA_KV_RANK, MLA_HEADS * HEAD_DIM)
    return wkn.astype(BF16), wv.astype(BF16)


def _router_slices(w_grp, w_exp):
    w = jnp.concatenate([w_grp, w_exp, jnp.zeros((D_MODEL, LANES - N_GROUPS - N_EXPERTS), F32)], axis=1)
    hi = w.astype(BF16)
    mid = (w - hi.astype(F32)).astype(BF16)
    return jnp.concatenate([hi, mid], axis=1)


def _mixer(xb, nb, layer, w_in_s, q_lat_norm, w_q_up, kv_lat_norm, w_kv_up,
           cmp_pos_k, cmp_w1_k, cmp_w2_k, cmp_pos_v, cmp_w1_v, cmp_w2_v, consts):
    hproj = _inproj(xb, w_in_s, layer, consts["rope32"])
    wkn, wv = _split_w_kv_up(w_kv_up)
    q_mla, k_mla, v_mla = _mlaprep(hproj, q_lat_norm.reshape(1, -1), kv_lat_norm.reshape(1, -1),
                                   _permute_w_q_up(w_q_up), wkn, wv, consts["rope64"])
    wide = NSA_CMP_STRIDE * HEAD_DIM
    tk2 = hproj[:, BLK_NKC * LANES:(BLK_NKC + 1) * LANES].reshape(nb, N_CMP_PAD, wide)
    tv2 = hproj[:, BLK_NVC * LANES:(BLK_NVC + 1) * LANES].reshape(nb, N_CMP_PAD, wide)
    kc, vc = _compress(tk2, tv2, cmp_pos_k.reshape(2, wide), cmp_pos_v.reshape(2, wide),
                       cmp_w1_k.astype(BF16), cmp_w2_k.astype(BF16), cmp_w1_v.astype(BF16), cmp_w2_v.astype(BF16),
                       consts["rope_cmp"])
    o_mla = _flash(q_mla, k_mla, v_mla, consts["bias_causal"], nb=nb, heads=MLA_HEADS, dqk=MLA_QK_PAD,
                   dv=HEAD_DIM, q_blk0=0, k_blk0=0, v_blk0=0)
    o_dil = _flash(hproj, hproj, hproj, consts["bias_dil"], nb=nb, heads=DIL_HEADS, dqk=HEAD_DIM, dv=HEAD_DIM,
                   q_blk0=BLK_DQ, k_blk0=BLK_DK, v_blk0=BLK_DV)
    o_nsa = _nsa(hproj, kc, vc, consts["cover"], consts["expand"], consts["bias_win"], nb)
    return o_mla, o_dil, o_nsa


def _make_consts():
    pos = np.arange(SEQ)
    slot = np.arange(N_CMP_PAD)
    cmp_start = slot * NSA_CMP_STRIDE
    cmp_end = cmp_start + NSA_CMP_LEN - 1
    sel_start = np.arange(N_SEL_BLOCKS) * NSA_SEL_BLOCK
    cover = ((cmp_start[:, None] < sel_start[None, :] + NSA_SEL_BLOCK)
             & (cmp_start[:, None] + NSA_CMP_LEN > sel_start[None, :])
             & (slot[:, None] < N_CMP_PAD - 1)).astype(np.float32)
    expand = (np.arange(LANES)[:, None] == (pos // NSA_SEL_BLOCK)[None, :]).astype(np.float32)
    expand = expand.reshape(LANES, SEQ // NSA_T, NSA_T).transpose(1, 0, 2)
    win_diags = range(-(NSA_TQ // NSA_T - 1), -(-NSA_WINDOW // NSA_T) + 1)
    return {
        "rope32": _rope_tables(pos, ROT_DIM, True),
        "rope64": _rope_tables(pos, MLA_ROPE_DIM, False, KROPE_LANE0),
        "rope_cmp": _rope_tables(cmp_end, ROT_DIM, True),
        "bias_causal": _distance_bias(_causal_mult, (0, 1), FLASH_T),
        "bias_dil": _distance_bias(_dilated_mult, range(SEQ // FLASH_T), FLASH_T),
        "bias_win": _distance_bias(_window_mult, win_diags, NSA_T, NSA_TQ),
        "cover": jnp.asarray(cover.T, BF16),
        "expand": jnp.asarray(expand, BF16),
    }


def kernel(x, w_in, q_lat_norm, w_q_up, kv_lat_norm, w_kv_up, cmp_pos_k, cmp_w1_k, cmp_w2_k, cmp_pos_v, cmp_w1_v,
           cmp_w2_v, w_out, ln1_g, ln1_b, w_grp, w_exp, w_gate, w_up, w_down, ln2_g, ln2_b):
    nb, s, d = x.shape
    assert s == SEQ and d == D_MODEL
    t = nb * s
    consts = _make_consts()
    w_in_s = _shift_w_in(w_in)
    w_out_s = w_out.astype(BF16)
    xf = x.reshape(t, d)
    xb = xf.astype(BF16)
    for l in range(DEPTH):
        o_mla, o_dil, o_nsa = _mixer(xb, nb, l, w_in_s, q_lat_norm[l], w_q_up[l], kv_lat_norm[l], w_kv_up[l],
                                     cmp_pos_k[l], cmp_w1_k[l], cmp_w2_k[l], cmp_pos_v[l], cmp_w1_v[l],
                                     cmp_w2_v[l], consts)
        x1, route = _outproj(o_mla, o_dil, o_nsa, w_out_s, l, xf, ln1_g[l].reshape(1, d),
                             ln1_b[l].reshape(1, d), _router_slices(w_grp[l], w_exp[l]))
        blk_ex, first, nxt, nused, fill, pos = _dispatch_plan(route, t)
        pos3 = pos.reshape(t // ROW_TM, 1, MOE_TOP_K * ROW_TM)
        n_rows = (-(-t * MOE_TOP_K // MOE_ROWS) + N_EXPERTS) * MOE_ROWS
        yb = _experts(blk_ex, first, nxt, nused, _dispatch(fill, pos3, x1, n_rows), w_gate, w_up, w_down, l)
        xf, xb = _combine(pos3, yb, x1, route,
                          ln2_g[l].reshape(1, d), ln2_b[l].reshape(1, d))
    return xf.reshape(nb, s, d)
```

```python
import functools

import numpy as np
import jax
import jax.numpy as jnp
from jax import lax
from jax.experimental import pallas as pl
from jax.experimental.pallas import tpu as pltpu

F32 = jnp.float32
BF16 = jnp.bfloat16

D_MODEL = 2048
SEQ = 2048
DEPTH = 2
HEAD_DIM = 128
LANES = 128
MLA_HEADS = 4
DIL_HEADS = 6
NSA_HEADS = 6
MLA_Q_RANK = 384
MLA_KV_RANK = 128
MLA_NOPE_DIM = 128
MLA_ROPE_DIM = 64
MLA_QK_PAD = 256
DIL_PAIRS = ((128, 1), (512, 4), (2048, 16))
NSA_CMP_LEN = 32
NSA_CMP_STRIDE = 16
NSA_CMP_HIDDEN = 256
NSA_SEL_BLOCK = 64
NSA_TOP_N = 16
NSA_WINDOW = 512
NSA_BRANCHES = 3
NSA_FORCE_SCORE = 1e4
ROPE_THETA = 500000.0
ROT_DIM = HEAD_DIM // 4
N_GROUPS = 4
EXPERTS_PER_GROUP = 8
N_EXPERTS = N_GROUPS * EXPERTS_PER_GROUP
EXPERT_FF = 512
MOE_TOP_K = 2
MOE_ROWS = 256
DN_ALPHA = (2 * DEPTH) ** 0.25
LN_EPS = 1e-5
RMS_EPS = 1e-6
NEG_INF = -1e30

N_CMP_PAD = SEQ // NSA_CMP_STRIDE
N_SEL_BLOCKS = SEQ // NSA_SEL_BLOCK

BLK_QLAT = 0
BLK_KVLAT = 3
BLK_KROPE = 4
BLK_DQ = 5
BLK_DK = 11
BLK_DV = 17
BLK_NQ = 23
BLK_NKC = 29
BLK_NVC = 30
BLK_NKS = 31
BLK_NVS = 32
BLK_NKW = 33
BLK_NVW = 34
BLK_GATE = 35
N_IN_BLOCKS = 36
IN_COLS = 4434
IN_SHIFT_FROM = MLA_Q_RANK + MLA_KV_RANK - MLA_ROPE_DIM
IN_COLS_PAD = N_IN_BLOCKS * LANES
KROPE_LANE0 = LANES - MLA_ROPE_DIM

INPROJ_TM = 2048
INPROJ_SUB = 512
INPROJ_TN = 512
FLASH_T = 1024
NSA_T = 512
NSA_TQ = 512
ROW_TM = 256
MLAPREP_TM = 512
OUTPROJ_TM = 512

VMEM_LIMIT = 56 * 1024 * 1024


def _cparams(sem):
    return pltpu.CompilerParams(dimension_semantics=sem, vmem_limit_bytes=VMEM_LIMIT)


def _rope_tables(pos, rot_dim, keep_rest, lane0=0):
    half = rot_dim // 2
    inv_freq = np.power(np.float32(ROPE_THETA), -np.arange(half, dtype=np.float32) / np.float32(half))
    ang = (pos.astype(np.float32)[:, None] * inv_freq[None, :]).astype(np.float32)
    cos, sin = np.cos(ang).astype(np.float32), np.sin(ang).astype(np.float32)
    n = pos.shape[0]
    c = np.full((n, LANES), 1.0 if keep_rest else 0.0, np.float32)
    s1 = np.zeros((n, LANES), np.float32)
    s2 = np.zeros((n, LANES), np.float32)
    c[:, lane0:lane0 + half] = cos
    c[:, lane0 + half:lane0 + rot_dim] = cos
    s1[:, lane0 + half:lane0 + rot_dim] = sin
    s2[:, lane0:lane0 + half] = -sin
    return jnp.asarray(c), jnp.asarray(s1), jnp.asarray(s2)


def _rope_lanes(a, c, s1, s2, half):
    return a * c + pltpu.roll(a, half, 1) * s1 + pltpu.roll(a, LANES - half, 1) * s2


def _distance_bias(mult_fn, diags, t, rows=None):
    d = np.asarray(diags, np.int64)[:, None, None] * t
    r = np.arange(t if rows is None else rows, dtype=np.int64)[None, :, None]
    c = np.arange(t, dtype=np.int64)[None, None, :]
    mult = mult_fn(d + r - c)
    return jnp.asarray(np.where(mult > 0, np.log(np.maximum(mult, 1).astype(np.float32)), NEG_INF).astype(np.float32))


def _causal_mult(dist):
    return (dist >= 0).astype(np.int64)


def _dilated_mult(dist):
    m = np.zeros_like(dist)
    for window, dil in DIL_PAIRS:
        m = m + ((dist >= 0) & (dist % dil == 0) & (dist <= (window // dil) * dil)).astype(np.int64)
    return m


def _window_mult(dist):
    return ((dist >= 0) & (dist <= NSA_WINDOW - 1)).astype(np.int64)


def _in_block(blk, lo, n):
    return jnp.logical_and(blk >= lo, blk < lo + n)


def _inproj_kernel(x_ref, w_ref, c_ref, s1_ref, s2_ref, o_ref, *, scale):
    j = pl.program_id(0)
    for hh in range(INPROJ_TM // INPROJ_SUB):
        rows = slice(hh * INPROJ_SUB, (hh + 1) * INPROJ_SUB)
        acc = _qk(x_ref[rows, :], w_ref[0])
        c, s1, s2 = c_ref[rows, :], s1_ref[rows, :], s2_ref[rows, :]
        for u in range(INPROJ_TN // LANES):
            blk = j * (INPROJ_TN // LANES) + u
            is_q = jnp.logical_or(_in_block(blk, BLK_DQ, DIL_HEADS), _in_block(blk, BLK_NQ, NSA_HEADS))
            is_rope = is_q | _in_block(blk, BLK_DK, DIL_HEADS) | (blk == BLK_NKS) | (blk == BLK_NKW)
            sc = jnp.where(is_q, scale, 1.0).astype(F32)
            a = acc[:, u * LANES:(u + 1) * LANES]
            r = _rope_lanes(a, jnp.where(is_rope, c, 1.0), jnp.where(is_rope, s1, 0.0),
                            jnp.where(is_rope, s2, 0.0), ROT_DIM // 2)
            o_ref[rows, u * LANES:(u + 1) * LANES] = (r * sc).astype(o_ref.dtype)


def _inproj(xb, w_in_s, layer, tabs):
    t = xb.shape[0]
    tm, tn = INPROJ_TM, INPROJ_TN
    nrow = SEQ // tm
    tab_spec = pl.BlockSpec((tm, LANES), lambda j, i: (i % nrow, 0))
    return pl.pallas_call(
        functools.partial(_inproj_kernel, scale=HEAD_DIM ** -0.5),
        out_shape=jax.ShapeDtypeStruct((t, IN_COLS_PAD), BF16),
        grid=(IN_COLS_PAD // tn, t // tm),
        in_specs=[pl.BlockSpec((tm, D_MODEL), lambda j, i: (i, 0)),
                  pl.BlockSpec((1, tn, D_MODEL), lambda j, i: (layer, j, 0)),
                  tab_spec, tab_spec, tab_spec],
        out_specs=pl.BlockSpec((tm, tn), lambda j, i: (i, j)),
        compiler_params=_cparams(("arbitrary", "arbitrary")),
        name="inproj",
    )(xb, w_in_s, *tabs)


def _rms(xf, g):
    return xf * lax.rsqrt(jnp.mean(jnp.square(xf), axis=-1, keepdims=True) + RMS_EPS) * g


def _mlaprep_kernel(ql_ref, kvl_ref, kr_ref, gq_ref, gkv_ref, wq_ref, wkn_ref, wv_ref,
                    c_ref, s1_ref, s2_ref, q_ref, k_ref, v_ref, *, scale):
    c, s1, s2 = c_ref[...], s1_ref[...], s2_ref[...]
    half = MLA_ROPE_DIM // 2
    qn = _rms(ql_ref[...].astype(F32), gq_ref[...]).astype(BF16)
    q = jnp.dot(qn, wq_ref[...], preferred_element_type=F32)
    for h in range(MLA_HEADS):
        lo = h * MLA_QK_PAD
        q_ref[:, lo:lo + LANES] = (q[:, lo:lo + LANES] * scale).astype(q_ref.dtype)
        a = q[:, lo + LANES:lo + 2 * LANES]
        q_ref[:, lo + LANES:lo + 2 * LANES] = (_rope_lanes(a, c, s1, s2, half) * scale).astype(q_ref.dtype)
    kvn = _rms(kvl_ref[...].astype(F32), gkv_ref[...]).astype(BF16)
    kn = jnp.dot(kvn, wkn_ref[...], preferred_element_type=F32)
    v_ref[...] = jnp.dot(kvn, wv_ref[...], preferred_element_type=F32).astype(v_ref.dtype)
    kpe = _rope_lanes(kr_ref[...].astype(F32), c, s1, s2, half).astype(k_ref.dtype)
    for h in range(MLA_HEADS):
        lo = h * MLA_QK_PAD
        k_ref[:, lo:lo + LANES] = kn[:, h * LANES:(h + 1) * LANES].astype(k_ref.dtype)
        k_ref[:, lo + LANES:lo + 2 * LANES] = kpe


def _mlaprep(hproj, gq, gkv, wq_p, wkn, wv, tabs):
    t = hproj.shape[0]
    tm = MLAPREP_TM
    nrow = SEQ // tm
    tab_spec = pl.BlockSpec((tm, LANES), lambda i: (i % nrow, 0))
    full = lambda shape: pl.BlockSpec(shape, lambda i: (0,) * len(shape))
    return pl.pallas_call(
        functools.partial(_mlaprep_kernel, scale=(MLA_NOPE_DIM + MLA_ROPE_DIM) ** -0.5),
        out_shape=(jax.ShapeDtypeStruct((t, MLA_HEADS * MLA_QK_PAD), BF16),
                   jax.ShapeDtypeStruct((t, MLA_HEADS * MLA_QK_PAD), BF16),
                   jax.ShapeDtypeStruct((t, MLA_HEADS * HEAD_DIM), BF16)),
        grid=(t // tm,),
        in_specs=[pl.BlockSpec((tm, MLA_Q_RANK), lambda i: (i, BLK_QLAT * LANES // MLA_Q_RANK)),
                  pl.BlockSpec((tm, LANES), lambda i: (i, BLK_KVLAT)),
                  pl.BlockSpec((tm, LANES), lambda i: (i, BLK_KROPE)),
                  full((1, MLA_Q_RANK)), full((1, MLA_KV_RANK)),
                  full((MLA_Q_RANK, MLA_HEADS * MLA_QK_PAD)),
                  full((MLA_KV_RANK, MLA_HEADS * MLA_NOPE_DIM)),
                  full((MLA_KV_RANK, MLA_HEADS * HEAD_DIM)),
                  tab_spec, tab_spec, tab_spec],
        out_specs=(pl.BlockSpec((tm, MLA_HEADS * MLA_QK_PAD), lambda i: (i, 0)),
                   pl.BlockSpec((tm, MLA_HEADS * MLA_QK_PAD), lambda i: (i, 0)),
                   pl.BlockSpec((tm, MLA_HEADS * HEAD_DIM), lambda i: (i, 0))),
        compiler_params=_cparams(("arbitrary",)),
        name="mlaprep",
    )(hproj, hproj, hproj, gq, gkv, wq_p, wkn, wv, *tabs)


def _gelu_tanh(x):
    return 0.5 * x * (1.0 + jnp.tanh(0.7978845608028654 * (x + 0.044715 * (x * x * x))))


def _compress_kernel(tk_ref, tv_ref, pek_ref, pev_ref, w1k_ref, w2k_ref, w1v_ref, w2v_ref,
                     c_ref, s1_ref, s2_ref, kc_ref, vc_ref):
    half_in = NSA_CMP_STRIDE * HEAD_DIM

    def comp(t_ref, pe_ref, w1_ref, w2_ref):
        t = t_ref[0].astype(F32)
        a = (t + pe_ref[0:1, :]).astype(BF16)
        b = (t + pe_ref[1:2, :]).astype(BF16)
        y0 = jnp.dot(a, w1_ref[0:half_in, :], preferred_element_type=F32)
        y1 = jnp.dot(b, w1_ref[half_in:2 * half_in, :], preferred_element_type=F32)
        hid = _gelu_tanh(y0 + pltpu.roll(y1, N_CMP_PAD - 1, 0))
        return jnp.dot(hid.astype(BF16), w2_ref[...], preferred_element_type=F32)

    kc = comp(tk_ref, pek_ref, w1k_ref, w2k_ref)
    kc_ref[0] = _rope_lanes(kc, c_ref[...], s1_ref[...], s2_ref[...], ROT_DIM // 2).astype(kc_ref.dtype)
    vc_ref[0] = comp(tv_ref, pev_ref, w1v_ref, w2v_ref).astype(vc_ref.dtype)


def _compress(tk2, tv2, pek, pev, w1k, w2k, w1v, w2v, tabs):
    nb = tk2.shape[0]
    wide = NSA_CMP_STRIDE * HEAD_DIM
    full = lambda shape: pl.BlockSpec(shape, lambda b: (0,) * len(shape))
    bspec = pl.BlockSpec((1, N_CMP_PAD, wide), lambda b: (b, 0, 0))
    ospec = pl.BlockSpec((1, N_CMP_PAD, HEAD_DIM), lambda b: (b, 0, 0))
    return pl.pallas_call(
        _compress_kernel,
        out_shape=(jax.ShapeDtypeStruct((nb, N_CMP_PAD, HEAD_DIM), BF16),) * 2,
        grid=(nb,),
        in_specs=[bspec, bspec, full((2, wide)), full((2, wide)),
                  full((2 * wide, NSA_CMP_HIDDEN)), full((NSA_CMP_HIDDEN, HEAD_DIM)),
                  full((2 * wide, NSA_CMP_HIDDEN)), full((NSA_CMP_HIDDEN, HEAD_DIM)),
                  full((N_CMP_PAD, LANES)), full((N_CMP_PAD, LANES)), full((N_CMP_PAD, LANES))],
        out_specs=(ospec, ospec),
        compiler_params=_cparams(("arbitrary",)),
        name="compress",
    )(tk2, tv2, pek, pev, w1k, w2k, w1v, w2v, *tabs)


def _qk(q, k):
    return lax.dot_general(q, k, (((1,), (1,)), ((), ())), preferred_element_type=F32)


def _lane_fold(x, op):
    out = x[:, 0:LANES]
    for u in range(1, x.shape[1] // LANES):
        out = op(out, x[:, u * LANES:(u + 1) * LANES])
    return out


def _split3(x):
    hi = x.astype(BF16)
    r1 = x - hi.astype(F32)
    mid = r1.astype(BF16)
    lo = (r1 - mid.astype(F32)).astype(BF16)
    return hi, mid, lo


def _attend(q, k_ref, v_ref, bias_fn, j_lo, n_tiles, s_scr, *, tk, groups=1):
    rows = q.shape[0]
    dv = v_ref.shape[-1]

    def pass1(t, mx):
        off = pl.multiple_of((j_lo + t) * tk, tk)
        s = _qk(q, k_ref[pl.ds(off, tk), :])
        bias = bias_fn(j_lo + t)
        if groups > 1:
            s = (s.reshape(groups, rows // groups, tk) + bias[None]).reshape(rows, tk)
        else:
            s = s + bias
        s_scr[t] = s
        return jnp.maximum(mx, _lane_fold(s, jnp.maximum))

    mx = lax.fori_loop(0, n_tiles, pass1, jnp.full((rows, LANES), NEG_INF, F32))
    m = jnp.broadcast_to(jnp.max(mx, axis=-1, keepdims=True), (rows, LANES))

    def pass2(t, carry):
        lsum, acc = carry
        off = pl.multiple_of((j_lo + t) * tk, tk)
        s = s_scr[t]
        parts = [jnp.exp(s[:, u * LANES:(u + 1) * LANES] - m) for u in range(tk // LANES)]
        for p in parts:
            lsum = lsum + p
        pb = jnp.concatenate([p.astype(BF16) for p in parts], axis=1)
        acc = acc + jnp.dot(pb, v_ref[pl.ds(off, tk), :], preferred_element_type=F32)
        return lsum, acc

    lsum, acc = lax.fori_loop(0, n_tiles, pass2, (jnp.zeros((rows, LANES), F32), jnp.zeros((rows, dv), F32)))
    return acc / jnp.sum(lsum, axis=-1, keepdims=True)


def _flash_kernel(q_ref, k_ref, v_ref, bias_ref, o_ref, s_scr):
    i = pl.program_id(2)
    last = bias_ref.shape[0] - 1
    out = _attend(q_ref[...], k_ref, v_ref, lambda j: bias_ref[jnp.minimum(i - j, last)], 0, i + 1, s_scr,
                  tk=FLASH_T)
    o_ref[...] = out.astype(o_ref.dtype)


def _flash(q_arr, k_arr, v_arr, bias, *, nb, heads, dqk, dv, q_blk0, k_blk0, v_blk0):
    t = FLASH_T
    nq = SEQ // t
    return pl.pallas_call(
        _flash_kernel,
        out_shape=jax.ShapeDtypeStruct((nb * SEQ, heads * dv), BF16),
        grid=(nb, heads, nq),
        in_specs=[pl.BlockSpec((t, dqk), lambda b, h, i: (b * nq + i, q_blk0 + h)),
                  pl.BlockSpec((SEQ, dqk), lambda b, h, i: (b, k_blk0 + h)),
                  pl.BlockSpec((SEQ, dv), lambda b, h, i: (b, v_blk0 + h)),
                  pl.BlockSpec(bias.shape, lambda b, h, i: (0, 0, 0))],
        out_specs=pl.BlockSpec((t, dv), lambda b, h, i: (b * nq + i, h)),
        scratch_shapes=[pltpu.VMEM((nq, t, t), F32)],
        compiler_params=_cparams(("arbitrary", "arbitrary", "arbitrary")),
        name="flash",
    )(q_arr, k_arr, v_arr, bias)


def _nsa_kernel(q0_ref, q1_ref, q2_ref, q3_ref, q4_ref, q5_ref, kc_ref, vc_ref, ks_ref, vs_ref, kw_ref, vw_ref,
                gate_ref, cover_ref, expand_ref, wbias_ref, o_ref, bias_scr, s_scr):
    i = pl.program_id(1)
    tq, tk = NSA_TQ, NSA_T
    per = tq // tk
    nh = NSA_HEADS
    rows = nh * tq
    pos = i * tq + lax.broadcasted_iota(jnp.int32, (tq, 1), 0)
    q = jnp.concatenate([r[...] for r in (q0_ref, q1_ref, q2_ref, q3_ref, q4_ref, q5_ref)], axis=0)

    cidx = lax.broadcasted_iota(jnp.int32, (tq, N_CMP_PAD), 1)
    valid_c = (cidx * NSA_CMP_STRIDE + (NSA_CMP_LEN - 1) <= pos) & (cidx < N_CMP_PAD - 1)
    valid_cf = valid_c.astype(F32)
    cbias = jnp.where(valid_c, 0.0, NEG_INF).astype(F32)
    s = (_qk(q, kc_ref[0]).reshape(nh, tq, N_CMP_PAD) + cbias[None]).reshape(rows, N_CMP_PAD)
    p = (jnp.exp(s - jnp.max(s, axis=-1, keepdims=True)).reshape(nh, tq, N_CMP_PAD) * valid_cf[None])
    p = p.reshape(rows, N_CMP_PAD)
    l = jnp.sum(p, axis=-1, keepdims=True)
    p = p / jnp.where(l > 0.0, l, 1.0)
    o_cmp = jnp.dot(p.astype(BF16), vc_ref[0], preferred_element_type=F32)
    psum = jnp.sum(p.reshape(nh, tq, N_CMP_PAD), axis=0)

    cover_t = cover_ref[...]
    imp = sum(jnp.dot(cover_t, part, preferred_element_type=F32) for part in _split3(psum.T))

    pos_t = i * tq + lax.broadcasted_iota(jnp.int32, (N_SEL_BLOCKS, tq), 1)
    jidx = lax.broadcasted_iota(jnp.int32, (N_SEL_BLOCKS, tq), 0)
    qblk = pos_t // NSA_SEL_BLOCK
    valid_s = jidx * NSA_SEL_BLOCK <= pos_t
    forced = (jidx == 0) | (jidx == qblk) | (jidx == qblk - 1)
    score = jnp.where(valid_s, jnp.where(forced, NSA_FORCE_SCORE, imp), -1.0)
    rank = jnp.zeros((N_SEL_BLOCKS, tq), jnp.int32)
    for ii in range(N_SEL_BLOCKS):
        si = score[ii:ii + 1, :]
        rank = rank + ((si > score) | ((si == score) & (ii < jidx))).astype(jnp.int32)
    sel_t = (rank < NSA_TOP_N).astype(F32)
    sel = jnp.concatenate([sel_t, jnp.zeros((LANES - N_SEL_BLOCKS, tq), F32)], axis=0).T.astype(BF16)

    n_causal = per * (i + 1)

    def fill(j, carry):
        hit = jnp.dot(sel, expand_ref[j], preferred_element_type=F32)
        kpos = j * tk + lax.broadcasted_iota(jnp.int32, (tq, tk), 1)
        bias_scr[j] = jnp.where((hit > 0.5) & (kpos <= pos), 0.0, NEG_INF).astype(F32)
        return carry

    lax.fori_loop(0, n_causal, fill, 0)

    o_slc = _attend(q, ks_ref, vs_ref, lambda j: bias_scr[j], 0, n_causal, s_scr, tk=tk, groups=nh)
    far = wbias_ref.shape[0] - per
    w_lo = jnp.maximum(per * i - far, 0)
    o_win = _attend(q, kw_ref, vw_ref, lambda j: wbias_ref[per * i - j + per - 1], w_lo, n_causal - w_lo, s_scr,
                    tk=tk, groups=nh)

    gates = jax.nn.sigmoid(gate_ref[...].astype(F32))
    for h in range(nh):
        sl = slice(h * tq, (h + 1) * tq)
        g0 = NSA_BRANCHES * h
        out = (gates[:, g0:g0 + 1] * o_cmp[sl] + gates[:, g0 + 1:g0 + 2] * o_slc[sl]
               + gates[:, g0 + 2:g0 + 3] * o_win[sl])
        o_ref[:, h * LANES:(h + 1) * LANES] = out.astype(o_ref.dtype)


def _nsa(hproj, kc, vc, cover, expand, wbias, nb):
    t = NSA_TQ
    nq = SEQ // t
    nk = SEQ // NSA_T
    wide = NSA_HEADS * HEAD_DIM
    q_spec = lambda h: pl.BlockSpec((t, LANES), lambda b, i: (b * nq + i, BLK_NQ + h))
    kv_spec = lambda blk: pl.BlockSpec((SEQ, LANES), lambda b, i: (b, blk))
    cspec = pl.BlockSpec((1, N_CMP_PAD, HEAD_DIM), lambda b, i: (b, 0, 0))
    full = lambda shape: pl.BlockSpec(shape, lambda b, i: (0,) * len(shape))
    return pl.pallas_call(
        _nsa_kernel,
        out_shape=jax.ShapeDtypeStruct((nb * SEQ, wide), BF16),
        grid=(nb, nq),
        in_specs=[q_spec(h) for h in range(NSA_HEADS)]
        + [cspec, cspec, kv_spec(BLK_NKS), kv_spec(BLK_NVS), kv_spec(BLK_NKW), kv_spec(BLK_NVW),
           pl.BlockSpec((t, LANES), lambda b, i: (b * nq + i, BLK_GATE)),
           full(cover.shape), full(expand.shape), full(wbias.shape)],
        out_specs=pl.BlockSpec((t, wide), lambda b, i: (b * nq + i, 0)),
        scratch_shapes=[pltpu.VMEM((nk, t, NSA_T), F32), pltpu.VMEM((nk, NSA_HEADS * t, NSA_T), F32)],
        compiler_params=_cparams(("arbitrary", "arbitrary")),
        name="nsa",
    )(*([hproj] * NSA_HEADS), kc, vc, hproj, hproj, hproj, hproj, hproj, cover, expand, wbias)


def _layer_norm(y, g, b):
    mu = jnp.mean(y, axis=-1, keepdims=True)
    var = jnp.mean(jnp.square(y - mu), axis=-1, keepdims=True)
    return (y - mu) * lax.rsqrt(var + LN_EPS) * g + b


def _lane_min(x):
    return jnp.min(x, axis=-1, keepdims=True)


def _lane_max(x):
    return jnp.max(x, axis=-1, keepdims=True)


def _route(logits):
    lane = lax.broadcasted_iota(jnp.int32, logits.shape, 1)
    lane_f = lane.astype(F32)
    far = float(LANES)
    is_grp = lane < N_GROUPS
    lg = jnp.where(is_grp, logits, NEG_INF)
    eg = jnp.where(is_grp, jnp.exp(lg - _lane_max(lg)), 0.0)
    prob = eg / jnp.sum(eg, axis=-1, keepdims=True)
    p_g = _lane_max(prob)
    g_idx = _lane_min(jnp.where(is_grp & (prob == p_g), lane_f, far))
    e_lo = N_GROUPS + EXPERTS_PER_GROUP * g_idx
    in_grp = (lane_f >= e_lo) & (lane_f < e_lo + EXPERTS_PER_GROUP)
    le = jnp.where(in_grp, logits, NEG_INF)
    v1 = _lane_max(le)
    i1 = _lane_min(jnp.where(in_grp & (le == v1), lane_f, far))
    rest = in_grp & (lane_f != i1)
    le2 = jnp.where(rest, logits, NEG_INF)
    v2 = _lane_max(le2)
    i2 = _lane_min(jnp.where(rest & (le2 == v2), lane_f, far))
    e21 = jnp.exp(v2 - v1)
    den = 1.0 + e21
    gate1 = p_g * (1.0 / den)
    gate2 = p_g * (e21 / den)
    out = jnp.where(lane == 0, i1 - N_GROUPS, 0.0)
    out = jnp.where(lane == 1, i2 - N_GROUPS, out)
    out = jnp.where(lane == 2, gate1, out)
    out = jnp.where(lane == 3, gate2, out)
    return out


def _outproj_kernel(om_ref, od_ref, on_ref, w_ref, x_ref, g_ref, b_ref, wr_ref, x1_ref, route_ref):
    n_mla = MLA_HEADS * HEAD_DIM
    n_dil = DIL_HEADS * HEAD_DIM
    for hh in range(OUTPROJ_TM // ROW_TM):
        rows = slice(hh * ROW_TM, (hh + 1) * ROW_TM)
        mix = jnp.dot(om_ref[rows, :], w_ref[0, 0:n_mla, :], preferred_element_type=F32)
        mix = mix + jnp.dot(od_ref[rows, :], w_ref[0, n_mla:n_mla + n_dil, :], preferred_element_type=F32)
        mix = mix + jnp.dot(on_ref[rows, :], w_ref[0, n_mla + n_dil:, :], preferred_element_type=F32)
        x1 = _layer_norm(DN_ALPHA * x_ref[rows, :] + mix, g_ref[...], b_ref[...])
        x1_ref[rows, :] = x1
        xh = x1.astype(BF16)
        xm = (x1 - xh.astype(F32)).astype(BF16)
        both = jnp.dot(xh, wr_ref[...], preferred_element_type=F32)
        logits = both[:, 0:LANES] + (both[:, LANES:2 * LANES]
                                     + jnp.dot(xm, wr_ref[:, 0:LANES], preferred_element_type=F32))
        route_ref[rows, :] = _route(logits)


def _outproj(o_mla, o_dil, o_nsa, w_out_s, layer, x, g, b, wr3):
    t = x.shape[0]
    tm = OUTPROJ_TM
    full = lambda shape: pl.BlockSpec(shape, lambda i: (0,) * len(shape))
    row = lambda w: pl.BlockSpec((tm, w), lambda i: (i, 0))
    return pl.pallas_call(
        _outproj_kernel,
        out_shape=(jax.ShapeDtypeStruct((t, D_MODEL), F32), jax.ShapeDtypeStruct((t, LANES), F32)),
        grid=(t // tm,),
        in_specs=[row(o_mla.shape[1]), row(o_dil.shape[1]), row(o_nsa.shape[1]),
                  pl.BlockSpec((1,) + w_out_s.shape[1:], lambda i: (layer, 0, 0)),
                  row(D_MODEL), full((1, D_MODEL)), full((1, D_MODEL)), full(wr3.shape)],
        out_specs=(row(D_MODEL), row(LANES)),
        compiler_params=_cparams(("arbitrary",)),
        name="outproj",
    )(o_mla, o_dil, o_nsa, w_out_s, x, g, b, wr3)


def _dispatch_kernel(fill_ref, pos_ref, x1_ref, xb_hbm, stage, zbuf, sem, zsem):
    i = pl.program_id(0)
    n = pl.num_programs(0)
    tm = ROW_TM
    slot = i % 2

    def drain(s):
        for _ in range(MOE_TOP_K):
            pltpu.make_async_copy(stage.at[s], xb_hbm.at[pl.ds(0, tm)], sem.at[s]).wait()

    def zero_copy(e):
        start = pl.multiple_of(fill_ref[e], MOE_ROWS)
        return pltpu.make_async_copy(zbuf, xb_hbm.at[pl.ds(start, MOE_ROWS)], zsem)

    @pl.when(i == 0)
    def _():
        zbuf[...] = jnp.zeros_like(zbuf)
        for e in range(N_EXPERTS):
            @pl.when(fill_ref[e] >= 0)
            def _():
                zero_copy(e).start()
        for e in range(N_EXPERTS):
            @pl.when(fill_ref[e] >= 0)
            def _():
                zero_copy(e).wait()

        def tail_copy(b):
            start = pl.multiple_of(b * MOE_ROWS, MOE_ROWS)
            return pltpu.make_async_copy(zbuf, xb_hbm.at[pl.ds(start, MOE_ROWS)], zsem)

        n_blocks = xb_hbm.shape[0] // MOE_ROWS
        lax.fori_loop(fill_ref[N_EXPERTS], n_blocks, lambda b, c: (tail_copy(b).start(), c)[1], 0)
        lax.fori_loop(fill_ref[N_EXPERTS], n_blocks, lambda b, c: (tail_copy(b).wait(), c)[1], 0)

    @pl.when(i >= 2)
    def _():
        drain(slot)

    stage[slot] = x1_ref[...]

    for r in range(tm):
        for k in range(MOE_TOP_K):
            p = pos_ref[0, 0, MOE_TOP_K * r + k]
            pltpu.make_async_copy(stage.at[slot, pl.ds(r, 1)], xb_hbm.at[pl.ds(p, 1)],
                                  sem.at[slot]).start(priority=k % 2)

    @pl.when(i == n - 1)
    def _():
        drain(1 - slot)
        drain(slot)


def _dispatch(fill, pos3, x1, n_rows):
    t = x1.shape[0]
    tm = ROW_TM
    grid_spec = pltpu.PrefetchScalarGridSpec(
        num_scalar_prefetch=1,
        grid=(t // tm,),
        in_specs=[pl.BlockSpec((1, 1, MOE_TOP_K * tm), lambda i, fl: (i, 0, 0), memory_space=pltpu.SMEM),
                  pl.BlockSpec((tm, D_MODEL), lambda i, fl: (i, 0))],
        out_specs=pl.BlockSpec(memory_space=pl.ANY),
        scratch_shapes=[pltpu.VMEM((2, tm, D_MODEL), F32), pltpu.VMEM((MOE_ROWS, D_MODEL), F32),
                        pltpu.SemaphoreType.DMA((2,)), pltpu.SemaphoreType.DMA(())],
    )
    return pl.pallas_call(
        _dispatch_kernel,
        out_shape=jax.ShapeDtypeStruct((n_rows, D_MODEL), F32),
        grid_spec=grid_spec,
        compiler_params=_cparams(("arbitrary",)),
        name="dispatch",
    )(fill, pos3, x1)


def _experts_kernel(blk_ex_ref, first_ref, nxt_ref, nused_ref, x_ref, wg_hbm, wu_hbm, wd_hbm, y_ref,
                    wg_f, wu_f, wd_f, wg_b, wu_b, wd_b, sem, *, layer):
    b = pl.program_id(0)
    nused = nused_ref[0]

    def weight_copies(e, s):
        return (pltpu.make_async_copy(wg_hbm.at[layer, e], wg_f.at[s], sem.at[s]),
                pltpu.make_async_copy(wu_hbm.at[layer, e], wu_f.at[s], sem.at[s]),
                pltpu.make_async_copy(wd_hbm.at[layer, e], wd_f.at[s], sem.at[s]))

    @pl.when(b == 0)
    def _():
        for cp in weight_copies(blk_ex_ref[0], 0):
            cp.start()

    @pl.when(b < nused)
    def _():
        first = first_ref[b]

        @pl.when(first > 0)
        def _():
            s = first - 1
            for cp in weight_copies(blk_ex_ref[b], s):
                cp.wait()

            @pl.when(nxt_ref[b] >= 0)
            def _():
                for cp in weight_copies(nxt_ref[b], 1 - s):
                    cp.start()

            wg_b[...] = wg_f[s].astype(BF16)
            wu_b[...] = wu_f[s].astype(BF16)
            wd_b[...] = wd_f[s].astype(BF16)

        rows = x_ref[...].astype(BF16)
        gate = jnp.dot(rows, wg_b[...], preferred_element_type=F32)
        up = jnp.dot(rows, wu_b[...], preferred_element_type=F32)
        hid = (gate * jax.nn.sigmoid(gate) * up).astype(BF16)
        y_ref[...] = jnp.dot(hid, wd_b[...], preferred_element_type=F32)

    @pl.when(b >= nused)
    def _():
        y_ref[...] = jnp.zeros_like(y_ref)


def _experts(blk_ex, first, nxt, nused, xb_rows, w_gate, w_up, w_down, layer):
    n_blocks = xb_rows.shape[0] // MOE_ROWS
    hbm = pl.BlockSpec(memory_space=pl.ANY)
    grid_spec = pltpu.PrefetchScalarGridSpec(
        num_scalar_prefetch=4,
        grid=(n_blocks,),
        in_specs=[pl.BlockSpec((MOE_ROWS, D_MODEL), lambda b, ex, fi, nx, nu: (jnp.minimum(b, nu[0] - 1), 0)),
                  hbm, hbm, hbm],
        out_specs=pl.BlockSpec((MOE_ROWS, D_MODEL), lambda b, ex, fi, nx, nu: (b, 0)),
        scratch_shapes=[pltpu.VMEM((2, D_MODEL, EXPERT_FF), F32), pltpu.VMEM((2, D_MODEL, EXPERT_FF), F32),
                        pltpu.VMEM((2, EXPERT_FF, D_MODEL), F32),
                        pltpu.VMEM((D_MODEL, EXPERT_FF), BF16), pltpu.VMEM((D_MODEL, EXPERT_FF), BF16),
                        pltpu.VMEM((EXPERT_FF, D_MODEL), BF16), pltpu.SemaphoreType.DMA((2,))],
    )
    return pl.pallas_call(
        functools.partial(_experts_kernel, layer=layer),
        out_shape=jax.ShapeDtypeStruct((n_blocks * MOE_ROWS, D_MODEL), F32),
        grid_spec=grid_spec,
        compiler_params=_cparams(("arbitrary",)),
        name="experts",
    )(blk_ex, first, nxt, nused, xb_rows, w_gate, w_up, w_down)


def _combine_kernel(pos_cur_ref, pos_nxt_ref, y_hbm, x1_ref, route_ref, g_ref, b_ref, o_ref, ob_ref, ybuf, sem):
    i = pl.program_id(0)
    n = pl.num_programs(0)
    tm = ROW_TM

    def issue(pos_ref, slot):
        for r in range(tm):
            for k in range(MOE_TOP_K):
                p = pos_ref[0, 0, MOE_TOP_K * r + k]
                pltpu.make_async_copy(y_hbm.at[pl.ds(p, 1)], ybuf.at[slot, k, pl.ds(r, 1)], sem.at[slot]).start()

    @pl.when(i == 0)
    def _():
        issue(pos_cur_ref, 0)

    @pl.when(i + 1 < n)
    def _():
        issue(pos_nxt_ref, (i + 1) % 2)

    slot = i % 2
    for k in range(MOE_TOP_K):
        pltpu.make_async_copy(y_hbm.at[pl.ds(0, tm)], ybuf.at[slot, k], sem.at[slot]).wait()
    route = route_ref[...]
    ffn = route[:, 2:3] * ybuf[slot, 0] + route[:, 3:4] * ybuf[slot, 1]
    x2 = _layer_norm(DN_ALPHA * x1_ref[...] + ffn, g_ref[...], b_ref[...])
    o_ref[...] = x2
    ob_ref[...] = x2.astype(ob_ref.dtype)


def _combine(pos3, yb, x1, route, g, b):
    t = x1.shape[0]
    tm = ROW_TM
    nt = t // tm
    pos_spec = lambda shift: pl.BlockSpec((1, 1, MOE_TOP_K * tm), lambda i: (jnp.minimum(i + shift, nt - 1), 0, 0),
                                          memory_space=pltpu.SMEM)
    row = lambda w: pl.BlockSpec((tm, w), lambda i: (i, 0))
    full = lambda shape: pl.BlockSpec(shape, lambda i: (0,) * len(shape))
    return pl.pallas_call(
        _combine_kernel,
        out_shape=(jax.ShapeDtypeStruct((t, D_MODEL), F32), jax.ShapeDtypeStruct((t, D_MODEL), BF16)),
        grid=(nt,),
        in_specs=[pos_spec(0), pos_spec(1), pl.BlockSpec(memory_space=pl.ANY), row(D_MODEL), row(LANES),
                  full((1, D_MODEL)), full((1, D_MODEL))],
        out_specs=(row(D_MODEL), row(D_MODEL)),
        scratch_shapes=[pltpu.VMEM((2, MOE_TOP_K, tm, D_MODEL), F32), pltpu.SemaphoreType.DMA((2,))],
        compiler_params=_cparams(("arbitrary",)),
        name="combine",
    )(pos3, pos3, yb, x1, route, g, b)


def _dispatch_plan(route, t):
    m = t * MOE_TOP_K
    n_blocks = -(-m // MOE_ROWS) + N_EXPERTS
    chunk = MOE_ROWS
    ex = route[:, 0:MOE_TOP_K].reshape(m // chunk, chunk, 1)
    onehot = (ex == jnp.arange(N_EXPERTS, dtype=F32)[None, None, :]).astype(F32)
    tri = jnp.asarray(np.tril(np.ones((chunk, chunk), np.float32)))
    within = jnp.einsum('ij,cje->cie', tri, onehot)
    totals = jnp.sum(onehot, axis=1)
    before = jnp.cumsum(totals, axis=0) - totals
    counts = jnp.sum(totals, axis=0)
    padded = jnp.ceil(counts / MOE_ROWS) * MOE_ROWS
    pad_end = jnp.cumsum(padded)
    pad_start = pad_end - padded
    row = jnp.sum(onehot * (within - 1.0 + before[:, None, :] + pad_start[None, None, :]), axis=-1)
    pos = row.astype(jnp.int32).reshape(m)
    blk_start = jnp.arange(n_blocks, dtype=F32) * MOE_ROWS
    blk_ex = jnp.minimum(jnp.sum((pad_end[None, :] <= blk_start[:, None]).astype(jnp.int32), axis=1),
                         N_EXPERTS - 1).astype(jnp.int32)
    nused = (pad_end[-1] / MOE_ROWS).astype(jnp.int32).reshape(1)
    fill = jnp.concatenate([jnp.where(padded > 0, pad_end - MOE_ROWS, -1.0).astype(jnp.int32), nused])
    nonempty = padded > 0
    slot_e = (jnp.cumsum(nonempty.astype(jnp.int32)) - 1) % 2
    expert_ids = jnp.arange(N_EXPERTS, dtype=jnp.int32)
    later = lax.cummin(jnp.where(nonempty, expert_ids, N_EXPERTS), axis=0, reverse=True)
    nxt_e = jnp.concatenate([later[1:], jnp.full((1,), N_EXPERTS, jnp.int32)])
    nxt_e = jnp.where(nxt_e >= N_EXPERTS, -1, nxt_e)
    is_first = (blk_start == pad_start[blk_ex]) & (jnp.arange(n_blocks) < nused[0])
    first = jnp.where(is_first, 1 + slot_e[blk_ex], 0).astype(jnp.int32)
    return blk_ex, first, nxt_e[blk_ex].astype(jnp.int32), nused, fill, pos


W_HALF = LANES // 2


def _shift_kernel(a_ref, b_ref, tail_ref, o_ref):
    m = pl.program_id(0)
    n_src = IN_COLS // LANES
    nl = o_ref.shape[0]
    d = o_ref.shape[2]
    nc = d // LANES
    a2, b2, t2 = a_ref, b_ref, tail_ref

    def slab(ref4, col0, c, l):
        return ref4[pl.ds(col0, W_HALF), c, l, :].astype(o_ref.dtype)

    def copy(ref_top, top0, ref_bot, bot0):
        for l in range(nl):
            for c in range(nc):
                o_ref[l, 0:W_HALF, c * LANES:(c + 1) * LANES] = slab(ref_top, top0, c, l)
                if ref_bot is None:
                    o_ref[l, W_HALF:LANES, c * LANES:(c + 1) * LANES] = jnp.zeros((W_HALF, LANES), o_ref.dtype)
                else:
                    o_ref[l, W_HALF:LANES, c * LANES:(c + 1) * LANES] = slab(ref_bot, bot0, c, l)

    @pl.when(m < n_src)
    def _():
        copy(a2, 0, b2, 0)

    @pl.when(m == n_src)
    def _():
        copy(a2, 0, t2, 0)

    @pl.when(m == n_src + 1)
    def _():
        copy(t2, W_HALF, None, 0)


def _shift_w_in(w_in):
    nl, d, _ = w_in.shape
    n_src = IN_COLS // LANES
    n_keep = (IN_SHIFT_FROM + MLA_ROPE_DIM) // LANES
    last = n_src * (LANES // W_HALF) - 1
    nc = d // LANES
    view = lambda w: jnp.transpose(w.reshape(nl, nc, LANES, w.shape[-1]), (3, 1, 0, 2))
    w_t = view(w_in)
    tail_t = view(jnp.pad(w_in[:, :, n_src * LANES:], ((0, 0), (0, 0), (0, (n_src + 1) * LANES - IN_COLS))))
    first = lambda m: jnp.where(m < n_keep, 2 * m, 2 * m - 1)
    return pl.pallas_call(
        _shift_kernel,
        out_shape=jax.ShapeDtypeStruct((nl, IN_COLS_PAD, d), BF16),
        grid=(N_IN_BLOCKS,),
        in_specs=[pl.BlockSpec((W_HALF, nc, nl, LANES), lambda m: (jnp.minimum(first(m), last), 0, 0, 0)),
                  pl.BlockSpec((W_HALF, nc, nl, LANES), lambda m: (jnp.minimum(first(m) + 1, last), 0, 0, 0)),
                  pl.BlockSpec((LANES, nc, nl, LANES), lambda m: (0, 0, 0, 0))],
        out_specs=pl.BlockSpec((nl, LANES, d), lambda m: (0, m, 0)),
        compiler_params=_cparams(("arbitrary",)),
        name="shiftw",
    )(w_t, w_t, tail_t)


def _permute_w_q_up(w_q_up):
    w = w_q_up.reshape(MLA_Q_RANK, MLA_HEADS, MLA_NOPE_DIM + MLA_ROPE_DIM)
    zero = jnp.zeros((MLA_Q_RANK, MLA_HEADS, MLA_QK_PAD - MLA_NOPE_DIM - MLA_ROPE_DIM), w.dtype)
    w = jnp.concatenate([w[:, :, :MLA_NOPE_DIM], zero, w[:, :, MLA_NOPE_DIM:]], axis=-1)
    return w.reshape(MLA_Q_RANK, MLA_HEADS * MLA_QK_PAD).astype(BF16)


def _split_w_kv_up(w_kv_up):
    w = w_kv_up.reshape(MLA_KV_RANK, MLA_HEADS, MLA_NOPE_DIM + HEAD_DIM)
    wkn = w[:, :, :MLA_NOPE_DIM].reshape(MLA_KV_RANK, MLA_HEADS * MLA_NOPE_DIM)
    wv = w[:, :, MLA_NOPE_DIM:].reshape(MLA_KV_RANK, MLA_HEADS * HEAD_DIM)
    return wkn.astype(BF16), wv.astype(BF16)


def _router_slices(w_grp, w_exp):
    w = jnp.concatenate([w_grp, w_exp, jnp.zeros((D_MODEL, LANES - N_GROUPS - N_EXPERTS), F32)], axis=1)
    hi = w.astype(BF16)
    mid = (w - hi.astype(F32)).astype(BF16)
    return jnp.concatenate([hi, mid], axis=1)


def _mixer(xb, nb, layer, w_in_s, q_lat_norm, w_q_up, kv_lat_norm, w_kv_up,
           cmp_pos_k, cmp_w1_k, cmp_w2_k, cmp_pos_v, cmp_w1_v, cmp_w2_v, consts):
    hproj = _inproj(xb, w_in_s, layer, consts["rope32"])
    wkn, wv = _split_w_kv_up(w_kv_up)
    q_mla, k_mla, v_mla = _mlaprep(hproj, q_lat_norm.reshape(1, -1), kv_lat_norm.reshape(1, -1),
                                   _permute_w_q_up(w_q_up), wkn, wv, consts["rope64"])
    wide = NSA_CMP_STRIDE * HEAD_DIM
    tk2 = hproj[:, BLK_NKC * LANES:(BLK_NKC + 1) * LANES].reshape(nb, N_CMP_PAD, wide)
    tv2 = hproj[:, BLK_NVC * LANES:(BLK_NVC + 1) * LANES].reshape(nb, N_CMP_PAD, wide)
    kc, vc = _compress(tk2, tv2, cmp_pos_k.reshape(2, wide), cmp_pos_v.reshape(2, wide),
                       cmp_w1_k.astype(BF16), cmp_w2_k.astype(BF16), cmp_w1_v.astype(BF16), cmp_w2_v.astype(BF16),
                       consts["rope_cmp"])
    o_mla = _flash(q_mla, k_mla, v_mla, consts["bias_causal"], nb=nb, heads=MLA_HEADS, dqk=MLA_QK_PAD,
                   dv=HEAD_DIM, q_blk0=0, k_blk0=0, v_blk0=0)
    o_dil = _flash(hproj, hproj, hproj, consts["bias_dil"], nb=nb, heads=DIL_HEADS, dqk=HEAD_DIM, dv=HEAD_DIM,
                   q_blk0=BLK_DQ, k_blk0=BLK_DK, v_blk0=BLK_DV)
    o_nsa = _nsa(hproj, kc, vc, consts["cover"], consts["expand"], consts["bias_win"], nb)
    return o_mla, o_dil, o_nsa


def _make_consts():
    pos = np.arange(SEQ)
    slot = np.arange(N_CMP_PAD)
    cmp_start = slot * NSA_CMP_STRIDE
    cmp_end = cmp_start + NSA_CMP_LEN - 1
    sel_start = np.arange(N_SEL_BLOCKS) * NSA_SEL_BLOCK
    cover = ((cmp_start[:, None] < sel_start[None, :] + NSA_SEL_BLOCK)
             & (cmp_start[:, None] + NSA_CMP_LEN > sel_start[None, :])
             & (slot[:, None] < N_CMP_PAD - 1)).astype(np.float32)
    expand = (np.arange(LANES)[:, None] == (pos // NSA_SEL_BLOCK)[None, :]).astype(np.float32)
    expand = expand.reshape(LANES, SEQ // NSA_T, NSA_T).transpose(1, 0, 2)
    win_diags = range(-(NSA_TQ // NSA_T - 1), -(-NSA_WINDOW // NSA_T) + 1)
    return {
        "rope32": _rope_tables(pos, ROT_DIM, True),
        "rope64": _rope_tables(pos, MLA_ROPE_DIM, False, KROPE_LANE0),
        "rope_cmp": _rope_tables(cmp_end, ROT_DIM, True),
        "bias_causal": _distance_bias(_causal_mult, (0, 1), FLASH_T),
        "bias_dil": _distance_bias(_dilated_mult, range(SEQ // FLASH_T), FLASH_T),
        "bias_win": _distance_bias(_window_mult, win_diags, NSA_T, NSA_TQ),
        "cover": jnp.asarray(cover.T, BF16),
        "expand": jnp.asarray(expand, BF16),
    }


def kernel(x, w_in, q_lat_norm, w_q_up, kv_lat_norm, w_kv_up, cmp_pos_k, cmp_w1_k, cmp_w2_k, cmp_pos_v, cmp_w1_v,
           cmp_w2_v, w_out, ln1_g, ln1_b, w_grp, w_exp, w_gate, w_up, w_down, ln2_g, ln2_b):
    nb, s, d = x.shape
    assert s == SEQ and d == D_MODEL
    t = nb * s
    consts = _make_consts()
    w_in_s = _shift_w_in(w_in)
    w_out_s = w_out.astype(BF16)
    xf = x.reshape(t, d)
    xb = xf.astype(BF16)
    for l in range(DEPTH):
        o_mla, o_dil, o_nsa = _mixer(xb, nb, l, w_in_s, q_lat_norm[l], w_q_up[l], kv_lat_norm[l], w_kv_up[l],
                                     cmp_pos_k[l], cmp_w1_k[l], cmp_w2_k[l], cmp_pos_v[l], cmp_w1_v[l],
                                     cmp_w2_v[l], consts)
        x1, route = _outproj(o_mla, o_dil, o_nsa, w_out_s, l, xf, ln1_g[l].reshape(1, d),
                             ln1_b[l].reshape(1, d), _router_slices(w_grp[l], w_exp[l]))
        blk_ex, first, nxt, nused, fill, pos = _dispatch_plan(route, t)
        pos3 = pos.reshape(t // ROW_TM, 1, MOE_TOP_K * ROW_TM)
        n_rows = (-(-t * MOE_TOP_K // MOE_ROWS) + N_EXPERTS) * MOE_ROWS
        yb = _experts(blk_ex, first, nxt, nused, _dispatch(fill, pos3, x1, n_rows), w_gate, w_up, w_down, l)
        xf, xb = _combine(pos3, yb, x1, route,
                          ln2_g[l].reshape(1, d), ln2_b[l].reshape(1, d))
    return xf.reshape(nb, s, d)
```

```python
import functools

import numpy as np
import jax
import jax.numpy as jnp
from jax import lax
from jax.experimental import pallas as pl
from jax.experimental.pallas import tpu as pltpu

F32 = jnp.float32
BF16 = jnp.bfloat16

D_MODEL = 2048
SEQ = 2048
DEPTH = 2
HEAD_DIM = 128
LANES = 128
MLA_HEADS = 4
DIL_HEADS = 6
NSA_HEADS = 6
MLA_Q_RANK = 384
MLA_KV_RANK = 128
MLA_NOPE_DIM = 128
MLA_ROPE_DIM = 64
MLA_QK_PAD = 256
DIL_PAIRS = ((128, 1), (512, 4), (2048, 16))
NSA_CMP_LEN = 32
NSA_CMP_STRIDE = 16
NSA_CMP_HIDDEN = 256
NSA_SEL_BLOCK = 64
NSA_TOP_N = 16
NSA_WINDOW = 512
NSA_BRANCHES = 3
NSA_FORCE_SCORE = 1e4
ROPE_THETA = 500000.0
ROT_DIM = HEAD_DIM // 4
N_GROUPS = 4
EXPERTS_PER_GROUP = 8
N_EXPERTS = N_GROUPS * EXPERTS_PER_GROUP
EXPERT_FF = 512
MOE_TOP_K = 2
MOE_ROWS = 256
DN_ALPHA = (2 * DEPTH) ** 0.25
LN_EPS = 1e-5
RMS_EPS = 1e-6
NEG_INF = -1e30
LOG2_E = 1.4426950408889634

N_CMP_PAD = SEQ // NSA_CMP_STRIDE
N_SEL_BLOCKS = SEQ // NSA_SEL_BLOCK

BLK_QLAT = 0
BLK_KVLAT = 3
BLK_KROPE = 4
BLK_DQ = 5
BLK_DK = 11
BLK_DV = 17
BLK_NQ = 23
BLK_NKC = 29
BLK_NVC = 30
BLK_NKS = 31
BLK_NVS = 32
BLK_NKW = 33
BLK_NVW = 34
BLK_GATE = 35
N_IN_BLOCKS = 36
IN_COLS = 4434
IN_SHIFT_FROM = MLA_Q_RANK + MLA_KV_RANK - MLA_ROPE_DIM
IN_COLS_PAD = N_IN_BLOCKS * LANES
KROPE_LANE0 = LANES - MLA_ROPE_DIM

INPROJ_TM = 2048
INPROJ_SUB = 512
INPROJ_TN = 512
FLASH_T = 1024
NSA_T = 512
NSA_TQ = 512
ROW_TM = 256
MLAPREP_TM = 512
OUTPROJ_TM = 512

VMEM_LIMIT = 56 * 1024 * 1024


def _cparams(sem):
    return pltpu.CompilerParams(dimension_semantics=sem, vmem_limit_bytes=VMEM_LIMIT)


def _rope_tables(pos, rot_dim, keep_rest, lane0=0):
    half = rot_dim // 2
    inv_freq = np.power(np.float32(ROPE_THETA), -np.arange(half, dtype=np.float32) / np.float32(half))
    ang = (pos.astype(np.float32)[:, None] * inv_freq[None, :]).astype(np.float32)
    cos, sin = np.cos(ang).astype(np.float32), np.sin(ang).astype(np.float32)
    n = pos.shape[0]
    c = np.full((n, LANES), 1.0 if keep_rest else 0.0, np.float32)
    s1 = np.zeros((n, LANES), np.float32)
    s2 = np.zeros((n, LANES), np.float32)
    c[:, lane0:lane0 + half] = cos
    c[:, lane0 + half:lane0 + rot_dim] = cos
    s1[:, lane0 + half:lane0 + rot_dim] = sin
    s2[:, lane0:lane0 + half] = -sin
    return jnp.asarray(c), jnp.asarray(s1), jnp.asarray(s2)


def _rope_lanes(a, c, s1, s2, half):
    return a * c + pltpu.roll(a, half, 1) * s1 + pltpu.roll(a, LANES - half, 1) * s2


def _distance_bias(mult_fn, diags, t, rows=None):
    d = np.asarray(diags, np.int64)[:, None, None] * t
    r = np.arange(t if rows is None else rows, dtype=np.int64)[None, :, None]
    c = np.arange(t, dtype=np.int64)[None, None, :]
    mult = mult_fn(d + r - c)
    return jnp.asarray(np.where(mult > 0, np.log2(np.maximum(mult, 1).astype(np.float32)), NEG_INF).astype(np.float32))


def _causal_mult(dist):
    return (dist >= 0).astype(np.int64)


def _dilated_mult(dist):
    m = np.zeros_like(dist)
    for window, dil in DIL_PAIRS:
        m = m + ((dist >= 0) & (dist % dil == 0) & (dist <= (window // dil) * dil)).astype(np.int64)
    return m


def _window_mult(dist):
    return ((dist >= 0) & (dist <= NSA_WINDOW - 1)).astype(np.int64)


def _in_block(blk, lo, n):
    return jnp.logical_and(blk >= lo, blk < lo + n)


def _inproj_kernel(x_ref, w_ref, c_ref, s1_ref, s2_ref, o_ref, *, scale):
    j = pl.program_id(0)
    for hh in range(INPROJ_TM // INPROJ_SUB):
        rows = slice(hh * INPROJ_SUB, (hh + 1) * INPROJ_SUB)
        acc = _qk(x_ref[rows, :], w_ref[0])
        c, s1, s2 = c_ref[rows, :], s1_ref[rows, :], s2_ref[rows, :]
        for u in range(INPROJ_TN // LANES):
            blk = j * (INPROJ_TN // LANES) + u
            is_q = jnp.logical_or(_in_block(blk, BLK_DQ, DIL_HEADS), _in_block(blk, BLK_NQ, NSA_HEADS))
            is_rope = is_q | _in_block(blk, BLK_DK, DIL_HEADS) | (blk == BLK_NKS) | (blk == BLK_NKW)
            sc = jnp.where(is_q, scale, 1.0).astype(F32)
            a = acc[:, u * LANES:(u + 1) * LANES]
            r = _rope_lanes(a, jnp.where(is_rope, c, 1.0), jnp.where(is_rope, s1, 0.0),
                            jnp.where(is_rope, s2, 0.0), ROT_DIM // 2)
            o_ref[rows, u * LANES:(u + 1) * LANES] = (r * sc).astype(o_ref.dtype)


def _inproj(xb, w_in_s, layer, tabs):
    t = xb.shape[0]
    tm, tn = INPROJ_TM, INPROJ_TN
    nrow = SEQ // tm
    tab_spec = pl.BlockSpec((tm, LANES), lambda j, i: (i % nrow, 0))
    return pl.pallas_call(
        functools.partial(_inproj_kernel, scale=HEAD_DIM ** -0.5 * LOG2_E),
        out_shape=jax.ShapeDtypeStruct((t, IN_COLS_PAD), BF16),
        grid=(IN_COLS_PAD // tn, t // tm),
        in_specs=[pl.BlockSpec((tm, D_MODEL), lambda j, i: (i, 0)),
                  pl.BlockSpec((1, tn, D_MODEL), lambda j, i: (layer, j, 0)),
                  tab_spec, tab_spec, tab_spec],
        out_specs=pl.BlockSpec((tm, tn), lambda j, i: (i, j)),
        compiler_params=_cparams(("arbitrary", "arbitrary")),
        name="inproj",
    )(xb, w_in_s, *tabs)


def _rms(xf, g):
    return xf * lax.rsqrt(jnp.mean(jnp.square(xf), axis=-1, keepdims=True) + RMS_EPS) * g


def _mlaprep_kernel(ql_ref, kvl_ref, kr_ref, gq_ref, gkv_ref, wq_ref, wkn_ref, wv_ref,
                    c_ref, s1_ref, s2_ref, q_ref, k_ref, v_ref, *, scale):
    c, s1, s2 = c_ref[...], s1_ref[...], s2_ref[...]
    half = MLA_ROPE_DIM // 2
    qn = _rms(ql_ref[...].astype(F32), gq_ref[...]).astype(BF16)
    q = jnp.dot(qn, wq_ref[...], preferred_element_type=F32)
    for h in range(MLA_HEADS):
        lo = h * MLA_QK_PAD
        q_ref[:, lo:lo + LANES] = (q[:, lo:lo + LANES] * scale).astype(q_ref.dtype)
        a = q[:, lo + LANES:lo + 2 * LANES]
        q_ref[:, lo + LANES:lo + 2 * LANES] = (_rope_lanes(a, c, s1, s2, half) * scale).astype(q_ref.dtype)
    kvn = _rms(kvl_ref[...].astype(F32), gkv_ref[...]).astype(BF16)
    kn = jnp.dot(kvn, wkn_ref[...], preferred_element_type=F32)
    v_ref[...] = jnp.dot(kvn, wv_ref[...], preferred_element_type=F32).astype(v_ref.dtype)
    kpe = _rope_lanes(kr_ref[...].astype(F32), c, s1, s2, half).astype(k_ref.dtype)
    for h in range(MLA_HEADS):
        lo = h * MLA_QK_PAD
        k_ref[:, lo:lo + LANES] = kn[:, h * LANES:(h + 1) * LANES].astype(k_ref.dtype)
        k_ref[:, lo + LANES:lo + 2 * LANES] = kpe


def _mlaprep(hproj, gq, gkv, wq_p, wkn, wv, tabs):
    t = hproj.shape[0]
    tm = MLAPREP_TM
    nrow = SEQ // tm
    tab_spec = pl.BlockSpec((tm, LANES), lambda i: (i % nrow, 0))
    full = lambda shape: pl.BlockSpec(shape, lambda i: (0,) * len(shape))
    return pl.pallas_call(
        functools.partial(_mlaprep_kernel, scale=(MLA_NOPE_DIM + MLA_ROPE_DIM) ** -0.5 * LOG2_E),
        out_shape=(jax.ShapeDtypeStruct((t, MLA_HEADS * MLA_QK_PAD), BF16),
                   jax.ShapeDtypeStruct((t, MLA_HEADS * MLA_QK_PAD), BF16),
                   jax.ShapeDtypeStruct((t, MLA_HEADS * HEAD_DIM), BF16)),
        grid=(t // tm,),
        in_specs=[pl.BlockSpec((tm, MLA_Q_RANK), lambda i: (i, BLK_QLAT * LANES // MLA_Q_RANK)),
                  pl.BlockSpec((tm, LANES), lambda i: (i, BLK_KVLAT)),
                  pl.BlockSpec((tm, LANES), lambda i: (i, BLK_KROPE)),
                  full((1, MLA_Q_RANK)), full((1, MLA_KV_RANK)),
                  full((MLA_Q_RANK, MLA_HEADS * MLA_QK_PAD)),
                  full((MLA_KV_RANK, MLA_HEADS * MLA_NOPE_DIM)),
                  full((MLA_KV_RANK, MLA_HEADS * HEAD_DIM)),
                  tab_spec, tab_spec, tab_spec],
        out_specs=(pl.BlockSpec((tm, MLA_HEADS * MLA_QK_PAD), lambda i: (i, 0)),
                   pl.BlockSpec((tm, MLA_HEADS * MLA_QK_PAD), lambda i: (i, 0)),
                   pl.BlockSpec((tm, MLA_HEADS * HEAD_DIM), lambda i: (i, 0))),
        compiler_params=_cparams(("arbitrary",)),
        name="mlaprep",
    )(hproj, hproj, hproj, gq, gkv, wq_p, wkn, wv, *tabs)


def _gelu_tanh(x):
    return 0.5 * x * (1.0 + jnp.tanh(0.7978845608028654 * (x + 0.044715 * (x * x * x))))


def _compress_kernel(tk_ref, tv_ref, pek_ref, pev_ref, w1k_ref, w2k_ref, w1v_ref, w2v_ref,
                     c_ref, s1_ref, s2_ref, kc_ref, vc_ref):
    half_in = NSA_CMP_STRIDE * HEAD_DIM

    def comp(t_ref, pe_ref, w1_ref, w2_ref):
        t = t_ref[0].astype(F32)
        a = (t + pe_ref[0:1, :]).astype(BF16)
        b = (t + pe_ref[1:2, :]).astype(BF16)
        y0 = jnp.dot(a, w1_ref[0:half_in, :], preferred_element_type=F32)
        y1 = jnp.dot(b, w1_ref[half_in:2 * half_in, :], preferred_element_type=F32)
        hid = _gelu_tanh(y0 + pltpu.roll(y1, N_CMP_PAD - 1, 0))
        return jnp.dot(hid.astype(BF16), w2_ref[...], preferred_element_type=F32)

    kc = comp(tk_ref, pek_ref, w1k_ref, w2k_ref)
    kc_ref[0] = _rope_lanes(kc, c_ref[...], s1_ref[...], s2_ref[...], ROT_DIM // 2).astype(kc_ref.dtype)
    vc_ref[0] = comp(tv_ref, pev_ref, w1v_ref, w2v_ref).astype(vc_ref.dtype)


def _compress(tk2, tv2, pek, pev, w1k, w2k, w1v, w2v, tabs):
    nb = tk2.shape[0]
    wide = NSA_CMP_STRIDE * HEAD_DIM
    full = lambda shape: pl.BlockSpec(shape, lambda b: (0,) * len(shape))
    bspec = pl.BlockSpec((1, N_CMP_PAD, wide), lambda b: (b, 0, 0))
    ospec = pl.BlockSpec((1, N_CMP_PAD, HEAD_DIM), lambda b: (b, 0, 0))
    return pl.pallas_call(
        _compress_kernel,
        out_shape=(jax.ShapeDtypeStruct((nb, N_CMP_PAD, HEAD_DIM), BF16),) * 2,
        grid=(nb,),
        in_specs=[bspec, bspec, full((2, wide)), full((2, wide)),
                  full((2 * wide, NSA_CMP_HIDDEN)), full((NSA_CMP_HIDDEN, HEAD_DIM)),
                  full((2 * wide, NSA_CMP_HIDDEN)), full((NSA_CMP_HIDDEN, HEAD_DIM)),
                  full((N_CMP_PAD, LANES)), full((N_CMP_PAD, LANES)), full((N_CMP_PAD, LANES))],
        out_specs=(ospec, ospec),
        compiler_params=_cparams(("arbitrary",)),
        name="compress",
    )(tk2, tv2, pek, pev, w1k, w2k, w1v, w2v, *tabs)


def _qk(q, k):
    return lax.dot_general(q, k, (((1,), (1,)), ((), ())), preferred_element_type=F32)


def _lane_fold(x, op):
    out = x[:, 0:LANES]
    for u in range(1, x.shape[1] // LANES):
        out = op(out, x[:, u * LANES:(u + 1) * LANES])
    return out


def _split3(x):
    hi = x.astype(BF16)
    r1 = x - hi.astype(F32)
    mid = r1.astype(BF16)
    lo = (r1 - mid.astype(F32)).astype(BF16)
    return hi, mid, lo


def _attend(q, k_ref, v_ref, bias_fn, j_lo, n_tiles, s_scr, *, tk, groups=1):
    rows = q.shape[0]
    dv = v_ref.shape[-1]

    def pass1(t, mx):
        off = pl.multiple_of((j_lo + t) * tk, tk)
        s = _qk(q, k_ref[pl.ds(off, tk), :])
        bias = bias_fn(j_lo + t)
        if groups > 1:
            s = (s.reshape(groups, rows // groups, tk) + bias[None]).reshape(rows, tk)
        else:
            s = s + bias
        s_scr[t] = s
        return jnp.maximum(mx, _lane_fold(s, jnp.maximum))

    mx = lax.fori_loop(0, n_tiles, pass1, jnp.full((rows, LANES), NEG_INF, F32))
    m = jnp.broadcast_to(jnp.max(mx, axis=-1, keepdims=True), (rows, LANES))

    def pass2(t, carry):
        lsum, acc = carry
        off = pl.multiple_of((j_lo + t) * tk, tk)
        s = s_scr[t]
        parts = [jnp.exp2(s[:, u * LANES:(u + 1) * LANES] - m) for u in range(tk // LANES)]
        for p in parts:
            lsum = lsum + p
        pb = jnp.concatenate([p.astype(BF16) for p in parts], axis=1)
        acc = acc + jnp.dot(pb, v_ref[pl.ds(off, tk), :], preferred_element_type=F32)
        return lsum, acc

    lsum, acc = lax.fori_loop(0, n_tiles, pass2, (jnp.zeros((rows, LANES), F32), jnp.zeros((rows, dv), F32)))
    return acc / jnp.sum(lsum, axis=-1, keepdims=True)


def _flash_kernel(q_ref, k_ref, v_ref, bias_ref, o_ref, s_scr):
    i = pl.program_id(2)
    last = bias_ref.shape[0] - 1
    out = _attend(q_ref[...], k_ref, v_ref, lambda j: bias_ref[jnp.minimum(i - j, last)], 0, i + 1, s_scr,
                  tk=FLASH_T)
    o_ref[...] = out.astype(o_ref.dtype)


def _flash(q_arr, k_arr, v_arr, bias, *, nb, heads, dqk, dv, q_blk0, k_blk0, v_blk0):
    t = FLASH_T
    nq = SEQ // t
    return pl.pallas_call(
        _flash_kernel,
        out_shape=jax.ShapeDtypeStruct((nb * SEQ, heads * dv), BF16),
        grid=(nb, heads, nq),
        in_specs=[pl.BlockSpec((t, dqk), lambda b, h, i: (b * nq + i, q_blk0 + h)),
                  pl.BlockSpec((SEQ, dqk), lambda b, h, i: (b, k_blk0 + h)),
                  pl.BlockSpec((SEQ, dv), lambda b, h, i: (b, v_blk0 + h)),
                  pl.BlockSpec(bias.shape, lambda b, h, i: (0, 0, 0))],
        out_specs=pl.BlockSpec((t, dv), lambda b, h, i: (b * nq + i, h)),
        scratch_shapes=[pltpu.VMEM((nq, t, t), F32)],
        compiler_params=_cparams(("arbitrary", "arbitrary", "arbitrary")),
        name="flash",
    )(q_arr, k_arr, v_arr, bias)


def _nsa_kernel(q0_ref, q1_ref, q2_ref, q3_ref, q4_ref, q5_ref, kc_ref, vc_ref, ks_ref, vs_ref, kw_ref, vw_ref,
                gate_ref, cover_ref, expand_ref, wbias_ref, o_ref, bias_scr, s_scr):
    i = pl.program_id(1)
    tq, tk = NSA_TQ, NSA_T
    per = tq // tk
    nh = NSA_HEADS
    rows = nh * tq
    pos = i * tq + lax.broadcasted_iota(jnp.int32, (tq, 1), 0)
    q = jnp.concatenate([r[...] for r in (q0_ref, q1_ref, q2_ref, q3_ref, q4_ref, q5_ref)], axis=0)

    cidx = lax.broadcasted_iota(jnp.int32, (tq, N_CMP_PAD), 1)
    valid_c = (cidx * NSA_CMP_STRIDE + (NSA_CMP_LEN - 1) <= pos) & (cidx < N_CMP_PAD - 1)
    valid_cf = valid_c.astype(F32)
    cbias = jnp.where(valid_c, 0.0, NEG_INF).astype(F32)
    s = (_qk(q, kc_ref[0]).reshape(nh, tq, N_CMP_PAD) + cbias[None]).reshape(rows, N_CMP_PAD)
    p = (jnp.exp2(s - jnp.max(s, axis=-1, keepdims=True)).reshape(nh, tq, N_CMP_PAD) * valid_cf[None])
    p = p.reshape(rows, N_CMP_PAD)
    l = jnp.sum(p, axis=-1, keepdims=True)
    p = p / jnp.where(l > 0.0, l, 1.0)
    o_cmp = jnp.dot(p.astype(BF16), vc_ref[0], preferred_element_type=F32)
    psum = jnp.sum(p.reshape(nh, tq, N_CMP_PAD), axis=0)

    cover_t = cover_ref[...]
    imp = sum(jnp.dot(cover_t, part, preferred_element_type=F32) for part in _split3(psum.T))

    pos_t = i * tq + lax.broadcasted_iota(jnp.int32, (N_SEL_BLOCKS, tq), 1)
    jidx = lax.broadcasted_iota(jnp.int32, (N_SEL_BLOCKS, tq), 0)
    qblk = pos_t // NSA_SEL_BLOCK
    valid_s = jidx * NSA_SEL_BLOCK <= pos_t
    forced = (jidx == 0) | (jidx == qblk) | (jidx == qblk - 1)
    score = jnp.where(valid_s, jnp.where(forced, NSA_FORCE_SCORE, imp), -1.0)
    rank = jnp.zeros((N_SEL_BLOCKS, tq), jnp.int32)
    for ii in range(N_SEL_BLOCKS):
        si = score[ii:ii + 1, :]
        rank = rank + ((si > score) | ((si == score) & (ii < jidx))).astype(jnp.int32)
    sel_t = (rank < NSA_TOP_N).astype(F32)
    sel = jnp.concatenate([sel_t, jnp.zeros((LANES - N_SEL_BLOCKS, tq), F32)], axis=0).T.astype(BF16)

    n_causal = per * (i + 1)

    def fill(j, carry):
        hit = jnp.dot(sel, expand_ref[j], preferred_element_type=F32)
        kpos = j * tk + lax.broadcasted_iota(jnp.int32, (tq, tk), 1)
        bias_scr[j] = jnp.where((hit > 0.5) & (kpos <= pos), 0.0, NEG_INF).astype(F32)
        return carry

    lax.fori_loop(0, n_causal, fill, 0)

    o_slc = _attend(q, ks_ref, vs_ref, lambda j: bias_scr[j], 0, n_causal, s_scr, tk=tk, groups=nh)
    far = wbias_ref.shape[0] - per
    w_lo = jnp.maximum(per * i - far, 0)
    o_win = _attend(q, kw_ref, vw_ref, lambda j: wbias_ref[per * i - j + per - 1], w_lo, n_causal - w_lo, s_scr,
                    tk=tk, groups=nh)

    gates = jax.nn.sigmoid(gate_ref[...].astype(F32))
    for h in range(nh):
        sl = slice(h * tq, (h + 1) * tq)
        g0 = NSA_BRANCHES * h
        out = (gates[:, g0:g0 + 1] * o_cmp[sl] + gates[:, g0 + 1:g0 + 2] * o_slc[sl]
               + gates[:, g0 + 2:g0 + 3] * o_win[sl])
        o_ref[:, h * LANES:(h + 1) * LANES] = out.astype(o_ref.dtype)


def _nsa(hproj, kc, vc, cover, expand, wbias, nb):
    t = NSA_TQ
    nq = SEQ // t
    nk = SEQ // NSA_T
    wide = NSA_HEADS * HEAD_DIM
    q_spec = lambda h: pl.BlockSpec((t, LANES), lambda b, i: (b * nq + i, BLK_NQ + h))
    kv_spec = lambda blk: pl.BlockSpec((SEQ, LANES), lambda b, i: (b, blk))
    cspec = pl.BlockSpec((1, N_CMP_PAD, HEAD_DIM), lambda b, i: (b, 0, 0))
    full = lambda shape: pl.BlockSpec(shape, lambda b, i: (0,) * len(shape))
    return pl.pallas_call(
        _nsa_kernel,
        out_shape=jax.ShapeDtypeStruct((nb * SEQ, wide), BF16),
        grid=(nb, nq),
        in_specs=[q_spec(h) for h in range(NSA_HEADS)]
        + [cspec, cspec, kv_spec(BLK_NKS), kv_spec(BLK_NVS), kv_spec(BLK_NKW), kv_spec(BLK_NVW),
           pl.BlockSpec((t, LANES), lambda b, i: (b * nq + i, BLK_GATE)),
           full(cover.shape), full(expand.shape), full(wbias.shape)],
        out_specs=pl.BlockSpec((t, wide), lambda b, i: (b * nq + i, 0)),
        scratch_shapes=[pltpu.VMEM((nk, t, NSA_T), F32), pltpu.VMEM((nk, NSA_HEADS * t, NSA_T), F32)],
        compiler_params=_cparams(("arbitrary", "arbitrary")),
        name="nsa",
    )(*([hproj] * NSA_HEADS), kc, vc, hproj, hproj, hproj, hproj, hproj, cover, expand, wbias)


def _layer_norm(y, g, b):
    mu = jnp.mean(y, axis=-1, keepdims=True)
    var = jnp.mean(jnp.square(y - mu), axis=-1, keepdims=True)
    return (y - mu) * lax.rsqrt(var + LN_EPS) * g + b


def _lane_min(x):
    return jnp.min(x, axis=-1, keepdims=True)


def _lane_max(x):
    return jnp.max(x, axis=-1, keepdims=True)


def _route(logits):
    lane = lax.broadcasted_iota(jnp.int32, logits.shape, 1)
    lane_f = lane.astype(F32)
    far = float(LANES)
    is_grp = lane < N_GROUPS
    lg = jnp.where(is_grp, logits, NEG_INF)
    eg = jnp.where(is_grp, jnp.exp(lg - _lane_max(lg)), 0.0)
    prob = eg / jnp.sum(eg, axis=-1, keepdims=True)
    p_g = _lane_max(prob)
    g_idx = _lane_min(jnp.where(is_grp & (prob == p_g), lane_f, far))
    e_lo = N_GROUPS + EXPERTS_PER_GROUP * g_idx
    in_grp = (lane_f >= e_lo) & (lane_f < e_lo + EXPERTS_PER_GROUP)
    le = jnp.where(in_grp, logits, NEG_INF)
    v1 = _lane_max(le)
    i1 = _lane_min(jnp.where(in_grp & (le == v1), lane_f, far))
    rest = in_grp & (lane_f != i1)
    le2 = jnp.where(rest, logits, NEG_INF)
    v2 = _lane_max(le2)
    i2 = _lane_min(jnp.where(rest & (le2 == v2), lane_f, far))
    e21 = jnp.exp(v2 - v1)
    den = 1.0 + e21
    gate1 = p_g * (1.0 / den)
    gate2 = p_g * (e21 / den)
    out = jnp.where(lane == 0, i1 - N_GROUPS, 0.0)
    out = jnp.where(lane == 1, i2 - N_GROUPS, out)
    out = jnp.where(lane == 2, gate1, out)
    out = jnp.where(lane == 3, gate2, out)
    return out


def _outproj_kernel(om_ref, od_ref, on_ref, w_ref, x_ref, g_ref, b_ref, wr_ref, x1_ref, route_ref):
    n_mla = MLA_HEADS * HEAD_DIM
    n_dil = DIL_HEADS * HEAD_DIM
    for hh in range(OUTPROJ_TM // ROW_TM):
        rows = slice(hh * ROW_TM, (hh + 1) * ROW_TM)
        mix = jnp.dot(om_ref[rows, :], w_ref[0, 0:n_mla, :], preferred_element_type=F32)
        mix = mix + jnp.dot(od_ref[rows, :], w_ref[0, n_mla:n_mla + n_dil, :], preferred_element_type=F32)
        mix = mix + jnp.dot(on_ref[rows, :], w_ref[0, n_mla + n_dil:, :], preferred_element_type=F32)
        x1 = _layer_norm(DN_ALPHA * x_ref[rows, :] + mix, g_ref[...], b_ref[...])
        x1_ref[rows, :] = x1
        xh = x1.astype(BF16)
        xm = (x1 - xh.astype(F32)).astype(BF16)
        both = jnp.dot(xh, wr_ref[...], preferred_element_type=F32)
        logits = both[:, 0:LANES] + (both[:, LANES:2 * LANES]
                                     + jnp.dot(xm, wr_ref[:, 0:LANES], preferred_element_type=F32))
        route_ref[rows, :] = _route(logits)


def _outproj(o_mla, o_dil, o_nsa, w_out_s, layer, x, g, b, wr3):
    t = x.shape[0]
    tm = OUTPROJ_TM
    full = lambda shape: pl.BlockSpec(shape, lambda i: (0,) * len(shape))
    row = lambda w: pl.BlockSpec((tm, w), lambda i: (i, 0))
    return pl.pallas_call(
        _outproj_kernel,
        out_shape=(jax.ShapeDtypeStruct((t, D_MODEL), F32), jax.ShapeDtypeStruct((t, LANES), F32)),
        grid=(t // tm,),
        in_specs=[row(o_mla.shape[1]), row(o_dil.shape[1]), row(o_nsa.shape[1]),
                  pl.BlockSpec((1,) + w_out_s.shape[1:], lambda i: (layer, 0, 0)),
                  row(D_MODEL), full((1, D_MODEL)), full((1, D_MODEL)), full(wr3.shape)],
        out_specs=(row(D_MODEL), row(LANES)),
        compiler_params=_cparams(("arbitrary",)),
        name="outproj",
    )(o_mla, o_dil, o_nsa, w_out_s, x, g, b, wr3)


def _dispatch_kernel(fill_ref, pos_ref, x1_ref, xb_hbm, stage, zbuf, sem, zsem):
    i = pl.program_id(0)
    n = pl.num_programs(0)
    tm = ROW_TM
    slot = i % 2

    def drain(s):
        for _ in range(MOE_TOP_K):
            pltpu.make_async_copy(stage.at[s], xb_hbm.at[pl.ds(0, tm)], sem.at[s]).wait()

    def zero_copy(e):
        start = pl.multiple_of(fill_ref[e], MOE_ROWS)
        return pltpu.make_async_copy(zbuf, xb_hbm.at[pl.ds(start, MOE_ROWS)], zsem)

    @pl.when(i == 0)
    def _():
        zbuf[...] = jnp.zeros_like(zbuf)
        for e in range(N_EXPERTS):
            @pl.when(fill_ref[e] >= 0)
            def _():
                zero_copy(e).start()
        for e in range(N_EXPERTS):
            @pl.when(fill_ref[e] >= 0)
            def _():
                zero_copy(e).wait()

        def tail_copy(b):
            start = pl.multiple_of(b * MOE_ROWS, MOE_ROWS)
            return pltpu.make_async_copy(zbuf, xb_hbm.at[pl.ds(start, MOE_ROWS)], zsem)

        n_blocks = xb_hbm.shape[0] // MOE_ROWS
        lax.fori_loop(fill_ref[N_EXPERTS], n_blocks, lambda b, c: (tail_copy(b).start(), c)[1], 0)
        lax.fori_loop(fill_ref[N_EXPERTS], n_blocks, lambda b, c: (tail_copy(b).wait(), c)[1], 0)

    @pl.when(i >= 2)
    def _():
        drain(slot)

    stage[slot] = x1_ref[...]

    for r in range(tm):
        for k in range(MOE_TOP_K):
            p = pos_ref[0, 0, MOE_TOP_K * r + k]
            pltpu.make_async_copy(stage.at[slot, pl.ds(r, 1)], xb_hbm.at[pl.ds(p, 1)],
                                  sem.at[slot]).start(priority=k % 2)

    @pl.when(i == n - 1)
    def _():
        drain(1 - slot)
        drain(slot)


def _dispatch(fill, pos3, x1, n_rows):
    t = x1.shape[0]
    tm = ROW_TM
    grid_spec = pltpu.PrefetchScalarGridSpec(
        num_scalar_prefetch=1,
        grid=(t // tm,),
        in_specs=[pl.BlockSpec((1, 1, MOE_TOP_K * tm), lambda i, fl: (i, 0, 0), memory_space=pltpu.SMEM),
                  pl.BlockSpec((tm, D_MODEL), lambda i, fl: (i, 0))],
        out_specs=pl.BlockSpec(memory_space=pl.ANY),
        scratch_shapes=[pltpu.VMEM((2, tm, D_MODEL), F32), pltpu.VMEM((MOE_ROWS, D_MODEL), F32),
                        pltpu.SemaphoreType.DMA((2,)), pltpu.SemaphoreType.DMA(())],
    )
    return pl.pallas_call(
        _dispatch_kernel,
        out_shape=jax.ShapeDtypeStruct((n_rows, D_MODEL), F32),
        grid_spec=grid_spec,
        compiler_params=_cparams(("arbitrary",)),
        name="dispatch",
    )(fill, pos3, x1)


def _experts_kernel(blk_ex_ref, first_ref, nxt_ref, nused_ref, x_ref, wg_hbm, wu_hbm, wd_hbm, y_ref,
                    wg_f, wu_f, wd_f, wg_b, wu_b, wd_b, sem, *, layer):
    b = pl.program_id(0)
    nused = nused_ref[0]

    def weight_copies(e, s):
        return (pltpu.make_async_copy(wg_hbm.at[layer, e], wg_f.at[s], sem.at[s]),
                pltpu.make_async_copy(wu_hbm.at[layer, e], wu_f.at[s], sem.at[s]),
                pltpu.make_async_copy(wd_hbm.at[layer, e], wd_f.at[s], sem.at[s]))

    @pl.when(b == 0)
    def _():
        for cp in weight_copies(blk_ex_ref[0], 0):
            cp.start()

    @pl.when(b < nused)
    def _():
        first = first_ref[b]

        @pl.when(first > 0)
        def _():
            s = first - 1
            for cp in weight_copies(blk_ex_ref[b], s):
                cp.wait()

            @pl.when(nxt_ref[b] >= 0)
            def _():
                for cp in weight_copies(nxt_ref[b], 1 - s):
                    cp.start()

            wg_b[...] = wg_f[s].astype(BF16)
            wu_b[...] = wu_f[s].astype(BF16)
            wd_b[...] = wd_f[s].astype(BF16)

        rows = x_ref[...].astype(BF16)
        gate = jnp.dot(rows, wg_b[...], preferred_element_type=F32)
        up = jnp.dot(rows, wu_b[...], preferred_element_type=F32)
        hid = (gate * jax.nn.sigmoid(gate) * up).astype(BF16)
        y_ref[...] = jnp.dot(hid, wd_b[...], preferred_element_type=F32)

    @pl.when(b >= nused)
    def _():
        y_ref[...] = jnp.zeros_like(y_ref)


def _experts(blk_ex, first, nxt, nused, xb_rows, w_gate, w_up, w_down, layer):
    n_blocks = xb_rows.shape[0] // MOE_ROWS
    hbm = pl.BlockSpec(memory_space=pl.ANY)
    grid_spec = pltpu.PrefetchScalarGridSpec(
        num_scalar_prefetch=4,
        grid=(n_blocks,),
        in_specs=[pl.BlockSpec((MOE_ROWS, D_MODEL), lambda b, ex, fi, nx, nu: (jnp.minimum(b, nu[0] - 1), 0)),
                  hbm, hbm, hbm],
        out_specs=pl.BlockSpec((MOE_ROWS, D_MODEL), lambda b, ex, fi, nx, nu: (b, 0)),
        scratch_shapes=[pltpu.VMEM((2, D_MODEL, EXPERT_FF), F32), pltpu.VMEM((2, D_MODEL, EXPERT_FF), F32),
                        pltpu.VMEM((2, EXPERT_FF, D_MODEL), F32),
                        pltpu.VMEM((D_MODEL, EXPERT_FF), BF16), pltpu.VMEM((D_MODEL, EXPERT_FF), BF16),
                        pltpu.VMEM((EXPERT_FF, D_MODEL), BF16), pltpu.SemaphoreType.DMA((2,))],
    )
    return pl.pallas_call(
        functools.partial(_experts_kernel, layer=layer),
        out_shape=jax.ShapeDtypeStruct((n_blocks * MOE_ROWS, D_MODEL), F32),
        grid_spec=grid_spec,
        compiler_params=_cparams(("arbitrary",)),
        name="experts",
    )(blk_ex, first, nxt, nused, xb_rows, w_gate, w_up, w_down)


def _combine_kernel(pos_cur_ref, pos_nxt_ref, y_hbm, x1_ref, route_ref, g_ref, b_ref, o_ref, ob_ref, ybuf, sem):
    i = pl.program_id(0)
    n = pl.num_programs(0)
    tm = ROW_TM

    def issue(pos_ref, slot):
        for r in range(tm):
            for k in range(MOE_TOP_K):
                p = pos_ref[0, 0, MOE_TOP_K * r + k]
                pltpu.make_async_copy(y_hbm.at[pl.ds(p, 1)], ybuf.at[slot, k, pl.ds(r, 1)], sem.at[slot]).start()

    @pl.when(i == 0)
    def _():
        issue(pos_cur_ref, 0)

    @pl.when(i + 1 < n)
    def _():
        issue(pos_nxt_ref, (i + 1) % 2)

    slot = i % 2
    for k in range(MOE_TOP_K):
        pltpu.make_async_copy(y_hbm.at[pl.ds(0, tm)], ybuf.at[slot, k], sem.at[slot]).wait()
    route = route_ref[...]
    ffn = route[:, 2:3] * ybuf[slot, 0] + route[:, 3:4] * ybuf[slot, 1]
    x2 = _layer_norm(DN_ALPHA * x1_ref[...] + ffn, g_ref[...], b_ref[...])
    o_ref[...] = x2
    ob_ref[...] = x2.astype(ob_ref.dtype)


def _combine(pos3, yb, x1, route, g, b):
    t = x1.shape[0]
    tm = ROW_TM
    nt = t // tm
    pos_spec = lambda shift: pl.BlockSpec((1, 1, MOE_TOP_K * tm), lambda i: (jnp.minimum(i + shift, nt - 1), 0, 0),
                                          memory_space=pltpu.SMEM)
    row = lambda w: pl.BlockSpec((tm, w), lambda i: (i, 0))
    full = lambda shape: pl.BlockSpec(shape, lambda i: (0,) * len(shape))
    return pl.pallas_call(
        _combine_kernel,
        out_shape=(jax.ShapeDtypeStruct((t, D_MODEL), F32), jax.ShapeDtypeStruct((t, D_MODEL), BF16)),
        grid=(nt,),
        in_specs=[pos_spec(0), pos_spec(1), pl.BlockSpec(memory_space=pl.ANY), row(D_MODEL), row(LANES),
                  full((1, D_MODEL)), full((1, D_MODEL))],
        out_specs=(row(D_MODEL), row(D_MODEL)),
        scratch_shapes=[pltpu.VMEM((2, MOE_TOP_K, tm, D_MODEL), F32), pltpu.SemaphoreType.DMA((2,))],
        compiler_params=_cparams(("arbitrary",)),
        name="combine",
    )(pos3, pos3, yb, x1, route, g, b)


def _dispatch_plan(route, t):
    m = t * MOE_TOP_K
    n_blocks = -(-m // MOE_ROWS) + N_EXPERTS
    chunk = MOE_ROWS
    ex = route[:, 0:MOE_TOP_K].reshape(m // chunk, chunk, 1)
    onehot = (ex == jnp.arange(N_EXPERTS, dtype=F32)[None, None, :]).astype(F32)
    tri = jnp.asarray(np.tril(np.ones((chunk, chunk), np.float32)))
    within = jnp.einsum('ij,cje->cie', tri, onehot)
    totals = jnp.sum(onehot, axis=1)
    before = jnp.cumsum(totals, axis=0) - totals
    counts = jnp.sum(totals, axis=0)
    padded = jnp.ceil(counts / MOE_ROWS) * MOE_ROWS
    pad_end = jnp.cumsum(padded)
    pad_start = pad_end - padded
    row = jnp.sum(onehot * (within - 1.0 + before[:, None, :] + pad_start[None, None, :]), axis=-1)
    pos = row.astype(jnp.int32).reshape(m)
    blk_start = jnp.arange(n_blocks, dtype=F32) * MOE_ROWS
    blk_ex = jnp.minimum(jnp.sum((pad_end[None, :] <= blk_start[:, None]).astype(jnp.int32), axis=1),
                         N_EXPERTS - 1).astype(jnp.int32)
    nused = (pad_end[-1] / MOE_ROWS).astype(jnp.int32).reshape(1)
    fill = jnp.concatenate([jnp.where(padded > 0, pad_end - MOE_ROWS, -1.0).astype(jnp.int32), nused])
    nonempty = padded > 0
    slot_e = (jnp.cumsum(nonempty.astype(jnp.int32)) - 1) % 2
    expert_ids = jnp.arange(N_EXPERTS, dtype=jnp.int32)
    later = lax.cummin(jnp.where(nonempty, expert_ids, N_EXPERTS), axis=0, reverse=True)
    nxt_e = jnp.concatenate([later[1:], jnp.full((1,), N_EXPERTS, jnp.int32)])
    nxt_e = jnp.where(nxt_e >= N_EXPERTS, -1, nxt_e)
    is_first = (blk_start == pad_start[blk_ex]) & (jnp.arange(n_blocks) < nused[0])
    first = jnp.where(is_first, 1 + slot_e[blk_ex], 0).astype(jnp.int32)
    return blk_ex, first, nxt_e[blk_ex].astype(jnp.int32), nused, fill, pos


W_HALF = LANES // 2


def _shift_kernel(a_ref, b_ref, tail_ref, o_ref):
    m = pl.program_id(0)
    n_src = IN_COLS // LANES
    nl = o_ref.shape[0]
    d = o_ref.shape[2]
    nc = d // LANES
    a2, b2, t2 = a_ref, b_ref, tail_ref

    def slab(ref4, col0, c, l):
        return ref4[pl.ds(col0, W_HALF), c, l, :].astype(o_ref.dtype)

    def copy(ref_top, top0, ref_bot, bot0):
        for l in range(nl):
            for c in range(nc):
                o_ref[l, 0:W_HALF, c * LANES:(c + 1) * LANES] = slab(ref_top, top0, c, l)
                if ref_bot is None:
                    o_ref[l, W_HALF:LANES, c * LANES:(c + 1) * LANES] = jnp.zeros((W_HALF, LANES), o_ref.dtype)
                else:
                    o_ref[l, W_HALF:LANES, c * LANES:(c + 1) * LANES] = slab(ref_bot, bot0, c, l)

    @pl.when(m < n_src)
    def _():
        copy(a2, 0, b2, 0)

    @pl.when(m == n_src)
    def _():
        copy(a2, 0, t2, 0)

    @pl.when(m == n_src + 1)
    def _():
        copy(t2, W_HALF, None, 0)


def _shift_w_in(w_in):
    nl, d, _ = w_in.shape
    n_src = IN_COLS // LANES
    n_keep = (IN_SHIFT_FROM + MLA_ROPE_DIM) // LANES
    last = n_src * (LANES // W_HALF) - 1
    nc = d // LANES
    view = lambda w: jnp.transpose(w.reshape(nl, nc, LANES, w.shape[-1]), (3, 1, 0, 2))
    w_t = view(w_in)
    tail_t = view(jnp.pad(w_in[:, :, n_src * LANES:], ((0, 0), (0, 0), (0, (n_src + 1) * LANES - IN_COLS))))
    first = lambda m: jnp.where(m < n_keep, 2 * m, 2 * m - 1)
    return pl.pallas_call(
        _shift_kernel,
        out_shape=jax.ShapeDtypeStruct((nl, IN_COLS_PAD, d), BF16),
        grid=(N_IN_BLOCKS,),
        in_specs=[pl.BlockSpec((W_HALF, nc, nl, LANES), lambda m: (jnp.minimum(first(m), last), 0, 0, 0)),
                  pl.BlockSpec((W_HALF, nc, nl, LANES), lambda m: (jnp.minimum(first(m) + 1, last), 0, 0, 0)),
                  pl.BlockSpec((LANES, nc, nl, LANES), lambda m: (0, 0, 0, 0))],
        out_specs=pl.BlockSpec((nl, LANES, d), lambda m: (0, m, 0)),
        compiler_params=_cparams(("arbitrary",)),
        name="shiftw",
    )(w_t, w_t, tail_t)


def _permute_w_q_up(w_q_up):
    w = w_q_up.reshape(MLA_Q_RANK, MLA_HEADS, MLA_NOPE_DIM + MLA_ROPE_DIM)
    zero = jnp.zeros((MLA_Q_RANK, MLA_HEADS, MLA_QK_PAD - MLA_NOPE_DIM - MLA_ROPE_DIM), w.dtype)
    w = jnp.concatenate([w[:, :, :MLA_NOPE_DIM], zero, w[:, :, MLA_NOPE_DIM:]], axis=-1)
    return w.reshape(MLA_Q_RANK, MLA_HEADS * MLA_QK_PAD).astype(BF16)


def _split_w_kv_up(w_kv_up):
    w = w_kv_up.reshape(MLA_KV_RANK, MLA_HEADS, MLA_NOPE_DIM + HEAD_DIM)
    wkn = w[:, :, :MLA_NOPE_DIM].reshape(MLA_KV_RANK, MLA_HEADS * MLA_NOPE_DIM)
    wv = w[:, :, MLA_NOPE_DIM:].reshape(MLA_KV_RANK, MLA_HEADS * HEAD_DIM)
    return wkn.astype(BF16), wv.astype(BF16)


def _router_slices(w_grp, w_exp):
    w = jnp.concatenate([w_grp, w_exp, jnp.zeros((D_MODEL, LANES - N_GROUPS - N_EXPERTS), F32)], axis=1)
    hi = w.astype(BF16)
    mid = (w - hi.astype(F32)).astype(BF16)
    return jnp.concatenate([hi, mid], axis=1)


def _mixer(xb, nb, layer, w_in_s, q_lat_norm, w_q_up, kv_lat_norm, w_kv_up,
           cmp_pos_k, cmp_w1_k, cmp_w2_k, cmp_pos_v, cmp_w1_v, cmp_w2_v, consts):
    hproj = _inproj(xb, w_in_s, layer, consts["rope32"])
    wkn, wv = _split_w_kv_up(w_kv_up)
    q_mla, k_mla, v_mla = _mlaprep(hproj, q_lat_norm.reshape(1, -1), kv_lat_norm.reshape(1, -1),
                                   _permute_w_q_up(w_q_up), wkn, wv, consts["rope64"])
    wide = NSA_CMP_STRIDE * HEAD_DIM
    tk2 = hproj[:, BLK_NKC * LANES:(BLK_NKC + 1) * LANES].reshape(nb, N_CMP_PAD, wide)
    tv2 = hproj[:, BLK_NVC * LANES:(BLK_NVC + 1) * LANES].reshape(nb, N_CMP_PAD, wide)
    kc, vc = _compress(tk2, tv2, cmp_pos_k.reshape(2, wide), cmp_pos_v.reshape(2, wide),
                       cmp_w1_k.astype(BF16), cmp_w2_k.astype(BF16), cmp_w1_v.astype(BF16), cmp_w2_v.astype(BF16),
                       consts["rope_cmp"])
    o_mla = _flash(q_mla, k_mla, v_mla, consts["bias_causal"], nb=nb, heads=MLA_HEADS, dqk=MLA_QK_PAD,
                   dv=HEAD_DIM, q_blk0=0, k_blk0=0, v_blk0=0)
    o_dil = _flash(hproj, hproj, hproj, consts["bias_dil"], nb=nb, heads=DIL_HEADS, dqk=HEAD_DIM, dv=HEAD_DIM,
                   q_blk0=BLK_DQ, k_blk0=BLK_DK, v_blk0=BLK_DV)
    o_nsa = _nsa(hproj, kc, vc, consts["cover"], consts["expand"], consts["bias_win"], nb)
    return o_mla, o_dil, o_nsa


def _make_consts():
    pos = np.arange(SEQ)
    slot = np.arange(N_CMP_PAD)
    cmp_start = slot * NSA_CMP_STRIDE
    cmp_end = cmp_start + NSA_CMP_LEN - 1
    sel_start = np.arange(N_SEL_BLOCKS) * NSA_SEL_BLOCK
    cover = ((cmp_start[:, None] < sel_start[None, :] + NSA_SEL_BLOCK)
             & (cmp_start[:, None] + NSA_CMP_LEN > sel_start[None, :])
             & (slot[:, None] < N_CMP_PAD - 1)).astype(np.float32)
    expand = (np.arange(LANES)[:, None] == (pos // NSA_SEL_BLOCK)[None, :]).astype(np.float32)
    expand = expand.reshape(LANES, SEQ // NSA_T, NSA_T).transpose(1, 0, 2)
    win_diags = range(-(NSA_TQ // NSA_T - 1), -(-NSA_WINDOW // NSA_T) + 1)
    return {
        "rope32": _rope_tables(pos, ROT_DIM, True),
        "rope64": _rope_tables(pos, MLA_ROPE_DIM, False, KROPE_LANE0),
        "rope_cmp": _rope_tables(cmp_end, ROT_DIM, True),
        "bias_causal": _distance_bias(_causal_mult, (0, 1), FLASH_T),
        "bias_dil": _distance_bias(_dilated_mult, range(SEQ // FLASH_T), FLASH_T),
        "bias_win": _distance_bias(_window_mult, win_diags, NSA_T, NSA_TQ),
        "cover": jnp.asarray(cover.T, BF16),
        "expand": jnp.asarray(expand, BF16),
    }


def kernel(x, w_in, q_lat_norm, w_q_up, kv_lat_norm, w_kv_up, cmp_pos_k, cmp_w1_k, cmp_w2_k, cmp_pos_v, cmp_w1_v,
           cmp_w2_v, w_out, ln1_g, ln1_b, w_grp, w_exp, w_gate, w_up, w_down, ln2_g, ln2_b):
    nb, s, d = x.shape
    assert s == SEQ and d == D_MODEL
    t = nb * s
    consts = _make_consts()
    w_in_s = _shift_w_in(w_in)
    w_out_s = w_out.astype(BF16)
    xf = x.reshape(t, d)
    xb = xf.astype(BF16)
    for l in range(DEPTH):
        o_mla, o_dil, o_nsa = _mixer(xb, nb, l, w_in_s, q_lat_norm[l], w_q_up[l], kv_lat_norm[l], w_kv_up[l],
                                     cmp_pos_k[l], cmp_w1_k[l], cmp_w2_k[l], cmp_pos_v[l], cmp_w1_v[l],
                                     cmp_w2_v[l], consts)
        x1, route = _outproj(o_mla, o_dil, o_nsa, w_out_s, l, xf, ln1_g[l].reshape(1, d),
                             ln1_b[l].reshape(1, d), _router_slices(w_grp[l], w_exp[l]))
        blk_ex, first, nxt, nused, fill, pos = _dispatch_plan(route, t)
        pos3 = pos.reshape(t // ROW_TM, 1, MOE_TOP_K * ROW_TM)
        n_rows = (-(-t * MOE_TOP_K // MOE_ROWS) + N_EXPERTS) * MOE_ROWS
        yb = _experts(blk_ex, first, nxt, nused, _dispatch(fill, pos3, x1, n_rows), w_gate, w_up, w_down, l)
        xf, xb = _combine(pos3, yb, x1, route,
                          ln2_g[l].reshape(1, d), ln2_b[l].reshape(1, d))
    return xf.reshape(nb, s, d)
```

```python
import functools

import numpy as np
import jax
import jax.numpy as jnp
from jax import lax
from jax.experimental import pallas as pl
from jax.experimental.pallas import tpu as pltpu

F32 = jnp.float32
BF16 = jnp.bfloat16

D_MODEL = 2048
SEQ = 2048
DEPTH = 2
HEAD_DIM = 128
LANES = 128
MLA_HEADS = 4
DIL_HEADS = 6
NSA_HEADS = 6
MLA_Q_RANK = 384
MLA_KV_RANK = 128
MLA_NOPE_DIM = 128
MLA_ROPE_DIM = 64
MLA_QK_PAD = 256
DIL_PAIRS = ((128, 1), (512, 4), (2048, 16))
NSA_CMP_LEN = 32
NSA_CMP_STRIDE = 16
NSA_CMP_HIDDEN = 256
NSA_SEL_BLOCK = 64
NSA_TOP_N = 16
NSA_WINDOW = 512
NSA_BRANCHES = 3
NSA_FORCE_SCORE = 1e4
ROPE_THETA = 500000.0
ROT_DIM = HEAD_DIM // 4
N_GROUPS = 4
EXPERTS_PER_GROUP = 8
N_EXPERTS = N_GROUPS * EXPERTS_PER_GROUP
EXPERT_FF = 512
MOE_TOP_K = 2
MOE_ROWS = 256
DN_ALPHA = (2 * DEPTH) ** 0.25
LN_EPS = 1e-5
RMS_EPS = 1e-6
NEG_INF = -1e30
LOG2_E = 1.4426950408889634

N_CMP_PAD = SEQ // NSA_CMP_STRIDE
N_SEL_BLOCKS = SEQ // NSA_SEL_BLOCK

BLK_QLAT = 0
BLK_KVLAT = 3
BLK_KROPE = 4
BLK_DQ = 5
BLK_DK = 11
BLK_DV = 17
BLK_NQ = 23
BLK_NKC = 29
BLK_NVC = 30
BLK_NKS = 31
BLK_NVS = 32
BLK_NKW = 33
BLK_NVW = 34
BLK_GATE = 35
N_IN_BLOCKS = 36
IN_COLS = 4434
IN_SHIFT_FROM = MLA_Q_RANK + MLA_KV_RANK - MLA_ROPE_DIM
IN_COLS_PAD = N_IN_BLOCKS * LANES
KROPE_LANE0 = LANES - MLA_ROPE_DIM

INPROJ_TM = 2048
INPROJ_SUB = 512
INPROJ_TN = 768
FLASH_T = 1024
NSA_T = 512
NSA_TQ = 512
ROW_TM = 256
MLAPREP_TM = 512
OUTPROJ_TM = 512

VMEM_LIMIT = 56 * 1024 * 1024


def _cparams(sem):
    return pltpu.CompilerParams(dimension_semantics=sem, vmem_limit_bytes=VMEM_LIMIT)


def _rope_tables(pos, rot_dim, keep_rest, lane0=0):
    half = rot_dim // 2
    inv_freq = np.power(np.float32(ROPE_THETA), -np.arange(half, dtype=np.float32) / np.float32(half))
    ang = (pos.astype(np.float32)[:, None] * inv_freq[None, :]).astype(np.float32)
    cos, sin = np.cos(ang).astype(np.float32), np.sin(ang).astype(np.float32)
    n = pos.shape[0]
    c = np.full((n, LANES), 1.0 if keep_rest else 0.0, np.float32)
    s1 = np.zeros((n, LANES), np.float32)
    s2 = np.zeros((n, LANES), np.float32)
    c[:, lane0:lane0 + half] = cos
    c[:, lane0 + half:lane0 + rot_dim] = cos
    s1[:, lane0 + half:lane0 + rot_dim] = sin
    s2[:, lane0:lane0 + half] = -sin
    return jnp.asarray(c), jnp.asarray(s1), jnp.asarray(s2)


def _rope_lanes(a, c, s1, s2, half):
    return a * c + pltpu.roll(a, half, 1) * s1 + pltpu.roll(a, LANES - half, 1) * s2


def _distance_bias(mult_fn, diags, t, rows=None):
    d = np.asarray(diags, np.int64)[:, None, None] * t
    r = np.arange(t if rows is None else rows, dtype=np.int64)[None, :, None]
    c = np.arange(t, dtype=np.int64)[None, None, :]
    mult = mult_fn(d + r - c)
    return jnp.asarray(np.where(mult > 0, np.log2(np.maximum(mult, 1).astype(np.float32)), NEG_INF).astype(np.float32))


def _causal_mult(dist):
    return (dist >= 0).astype(np.int64)


def _dilated_mult(dist):
    m = np.zeros_like(dist)
    for window, dil in DIL_PAIRS:
        m = m + ((dist >= 0) & (dist % dil == 0) & (dist <= (window // dil) * dil)).astype(np.int64)
    return m


def _window_mult(dist):
    return ((dist >= 0) & (dist <= NSA_WINDOW - 1)).astype(np.int64)


def _in_block(blk, lo, n):
    return jnp.logical_and(blk >= lo, blk < lo + n)


def _inproj_kernel(x_ref, w_ref, c_ref, s1_ref, s2_ref, o_ref, *, scale):
    j = pl.program_id(0)
    for hh in range(INPROJ_TM // INPROJ_SUB):
        rows = slice(hh * INPROJ_SUB, (hh + 1) * INPROJ_SUB)
        acc = _qk(x_ref[rows, :], w_ref[0])
        c, s1, s2 = c_ref[rows, :], s1_ref[rows, :], s2_ref[rows, :]
        for u in range(INPROJ_TN // LANES):
            blk = j * (INPROJ_TN // LANES) + u
            is_q = jnp.logical_or(_in_block(blk, BLK_DQ, DIL_HEADS), _in_block(blk, BLK_NQ, NSA_HEADS))
            is_rope = is_q | _in_block(blk, BLK_DK, DIL_HEADS) | (blk == BLK_NKS) | (blk == BLK_NKW)
            sc = jnp.where(is_q, scale, 1.0).astype(F32)
            a = acc[:, u * LANES:(u + 1) * LANES]
            r = _rope_lanes(a, jnp.where(is_rope, c, 1.0), jnp.where(is_rope, s1, 0.0),
                            jnp.where(is_rope, s2, 0.0), ROT_DIM // 2)
            o_ref[rows, u * LANES:(u + 1) * LANES] = (r * sc).astype(o_ref.dtype)


def _inproj(xb, w_in_s, layer, tabs):
    t = xb.shape[0]
    tm, tn = INPROJ_TM, INPROJ_TN
    nrow = SEQ // tm
    tab_spec = pl.BlockSpec((tm, LANES), lambda j, i: (i % nrow, 0))
    return pl.pallas_call(
        functools.partial(_inproj_kernel, scale=HEAD_DIM ** -0.5 * LOG2_E),
        out_shape=jax.ShapeDtypeStruct((t, IN_COLS_PAD), BF16),
        grid=(IN_COLS_PAD // tn, t // tm),
        in_specs=[pl.BlockSpec((tm, D_MODEL), lambda j, i: (i, 0)),
                  pl.BlockSpec((1, tn, D_MODEL), lambda j, i: (layer, j, 0)),
                  tab_spec, tab_spec, tab_spec],
        out_specs=pl.BlockSpec((tm, tn), lambda j, i: (i, j)),
        compiler_params=_cparams(("arbitrary", "arbitrary")),
        name="inproj",
    )(xb, w_in_s, *tabs)


def _rms(xf, g):
    return xf * lax.rsqrt(jnp.mean(jnp.square(xf), axis=-1, keepdims=True) + RMS_EPS) * g


def _mlaprep_kernel(ql_ref, kvl_ref, kr_ref, gq_ref, gkv_ref, wq_ref, wkn_ref, wv_ref,
                    c_ref, s1_ref, s2_ref, q_ref, k_ref, v_ref, *, scale):
    c, s1, s2 = c_ref[...], s1_ref[...], s2_ref[...]
    half = MLA_ROPE_DIM // 2
    qn = _rms(ql_ref[...].astype(F32), gq_ref[...]).astype(BF16)
    q = jnp.dot(qn, wq_ref[...], preferred_element_type=F32)
    for h in range(MLA_HEADS):
        lo = h * MLA_QK_PAD
        q_ref[:, lo:lo + LANES] = (q[:, lo:lo + LANES] * scale).astype(q_ref.dtype)
        a = q[:, lo + LANES:lo + 2 * LANES]
        q_ref[:, lo + LANES:lo + 2 * LANES] = (_rope_lanes(a, c, s1, s2, half) * scale).astype(q_ref.dtype)
    kvn = _rms(kvl_ref[...].astype(F32), gkv_ref[...]).astype(BF16)
    kn = jnp.dot(kvn, wkn_ref[...], preferred_element_type=F32)
    v_ref[...] = jnp.dot(kvn, wv_ref[...], preferred_element_type=F32).astype(v_ref.dtype)
    kpe = _rope_lanes(kr_ref[...].astype(F32), c, s1, s2, half).astype(k_ref.dtype)
    for h in range(MLA_HEADS):
        lo = h * MLA_QK_PAD
        k_ref[:, lo:lo + LANES] = kn[:, h * LANES:(h + 1) * LANES].astype(k_ref.dtype)
        k_ref[:, lo + LANES:lo + 2 * LANES] = kpe


def _mlaprep(hproj, gq, gkv, wq_p, wkn, wv, tabs):
    t = hproj.shape[0]
    tm = MLAPREP_TM
    nrow = SEQ // tm
    tab_spec = pl.BlockSpec((tm, LANES), lambda i: (i % nrow, 0))
    full = lambda shape: pl.BlockSpec(shape, lambda i: (0,) * len(shape))
    return pl.pallas_call(
        functools.partial(_mlaprep_kernel, scale=(MLA_NOPE_DIM + MLA_ROPE_DIM) ** -0.5 * LOG2_E),
        out_shape=(jax.ShapeDtypeStruct((t, MLA_HEADS * MLA_QK_PAD), BF16),
                   jax.ShapeDtypeStruct((t, MLA_HEADS * MLA_QK_PAD), BF16),
                   jax.ShapeDtypeStruct((t, MLA_HEADS * HEAD_DIM), BF16)),
        grid=(t // tm,),
        in_specs=[pl.BlockSpec((tm, MLA_Q_RANK), lambda i: (i, BLK_QLAT * LANES // MLA_Q_RANK)),
                  pl.BlockSpec((tm, LANES), lambda i: (i, BLK_KVLAT)),
                  pl.BlockSpec((tm, LANES), lambda i: (i, BLK_KROPE)),
                  full((1, MLA_Q_RANK)), full((1, MLA_KV_RANK)),
                  full((MLA_Q_RANK, MLA_HEADS * MLA_QK_PAD)),
                  full((MLA_KV_RANK, MLA_HEADS * MLA_NOPE_DIM)),
                  full((MLA_KV_RANK, MLA_HEADS * HEAD_DIM)),
                  tab_spec, tab_spec, tab_spec],
        out_specs=(pl.BlockSpec((tm, MLA_HEADS * MLA_QK_PAD), lambda i: (i, 0)),
                   pl.BlockSpec((tm, MLA_HEADS * MLA_QK_PAD), lambda i: (i, 0)),
                   pl.BlockSpec((tm, MLA_HEADS * HEAD_DIM), lambda i: (i, 0))),
        compiler_params=_cparams(("arbitrary",)),
        name="mlaprep",
    )(hproj, hproj, hproj, gq, gkv, wq_p, wkn, wv, *tabs)


def _gelu_tanh(x):
    return 0.5 * x * (1.0 + jnp.tanh(0.7978845608028654 * (x + 0.044715 * (x * x * x))))


def _compress_kernel(tk_ref, tv_ref, pek_ref, pev_ref, w1k_ref, w2k_ref, w1v_ref, w2v_ref,
                     c_ref, s1_ref, s2_ref, kc_ref, vc_ref):
    half_in = NSA_CMP_STRIDE * HEAD_DIM

    def comp(t_ref, pe_ref, w1_ref, w2_ref):
        t = t_ref[0].astype(F32)
        a = (t + pe_ref[0:1, :]).astype(BF16)
        b = (t + pe_ref[1:2, :]).astype(BF16)
        y0 = jnp.dot(a, w1_ref[0:half_in, :], preferred_element_type=F32)
        y1 = jnp.dot(b, w1_ref[half_in:2 * half_in, :], preferred_element_type=F32)
        hid = _gelu_tanh(y0 + pltpu.roll(y1, N_CMP_PAD - 1, 0))
        return jnp.dot(hid.astype(BF16), w2_ref[...], preferred_element_type=F32)

    kc = comp(tk_ref, pek_ref, w1k_ref, w2k_ref)
    kc_ref[0] = _rope_lanes(kc, c_ref[...], s1_ref[...], s2_ref[...], ROT_DIM // 2).astype(kc_ref.dtype)
    vc_ref[0] = comp(tv_ref, pev_ref, w1v_ref, w2v_ref).astype(vc_ref.dtype)


def _compress(tk2, tv2, pek, pev, w1k, w2k, w1v, w2v, tabs):
    nb = tk2.shape[0]
    wide = NSA_CMP_STRIDE * HEAD_DIM
    full = lambda shape: pl.BlockSpec(shape, lambda b: (0,) * len(shape))
    bspec = pl.BlockSpec((1, N_CMP_PAD, wide), lambda b: (b, 0, 0))
    ospec = pl.BlockSpec((1, N_CMP_PAD, HEAD_DIM), lambda b: (b, 0, 0))
    return pl.pallas_call(
        _compress_kernel,
        out_shape=(jax.ShapeDtypeStruct((nb, N_CMP_PAD, HEAD_DIM), BF16),) * 2,
        grid=(nb,),
        in_specs=[bspec, bspec, full((2, wide)), full((2, wide)),
                  full((2 * wide, NSA_CMP_HIDDEN)), full((NSA_CMP_HIDDEN, HEAD_DIM)),
                  full((2 * wide, NSA_CMP_HIDDEN)), full((NSA_CMP_HIDDEN, HEAD_DIM)),
                  full((N_CMP_PAD, LANES)), full((N_CMP_PAD, LANES)), full((N_CMP_PAD, LANES))],
        out_specs=(ospec, ospec),
        compiler_params=_cparams(("arbitrary",)),
        name="compress",
    )(tk2, tv2, pek, pev, w1k, w2k, w1v, w2v, *tabs)


def _qk(q, k):
    return lax.dot_general(q, k, (((1,), (1,)), ((), ())), preferred_element_type=F32)


def _lane_fold(x, op):
    out = x[:, 0:LANES]
    for u in range(1, x.shape[1] // LANES):
        out = op(out, x[:, u * LANES:(u + 1) * LANES])
    return out


def _split3(x):
    hi = x.astype(BF16)
    r1 = x - hi.astype(F32)
    mid = r1.astype(BF16)
    lo = (r1 - mid.astype(F32)).astype(BF16)
    return hi, mid, lo


def _attend(q, k_ref, v_ref, bias_fn, j_lo, n_tiles, s_scr, *, tk, groups=1):
    rows = q.shape[0]
    dv = v_ref.shape[-1]

    def pass1(t, mx):
        off = pl.multiple_of((j_lo + t) * tk, tk)
        s = _qk(q, k_ref[pl.ds(off, tk), :])
        bias = bias_fn(j_lo + t)
        if groups > 1:
            s = (s.reshape(groups, rows // groups, tk) + bias[None]).reshape(rows, tk)
        else:
            s = s + bias
        s_scr[t] = s
        return jnp.maximum(mx, _lane_fold(s, jnp.maximum))

    mx = lax.fori_loop(0, n_tiles, pass1, jnp.full((rows, LANES), NEG_INF, F32))
    m = jnp.broadcast_to(jnp.max(mx, axis=-1, keepdims=True), (rows, LANES))

    def pass2(t, carry):
        lsum, acc = carry
        off = pl.multiple_of((j_lo + t) * tk, tk)
        s = s_scr[t]
        parts = [jnp.exp2(s[:, u * LANES:(u + 1) * LANES] - m) for u in range(tk // LANES)]
        for p in parts:
            lsum = lsum + p
        pb = jnp.concatenate([p.astype(BF16) for p in parts], axis=1)
        acc = acc + jnp.dot(pb, v_ref[pl.ds(off, tk), :], preferred_element_type=F32)
        return lsum, acc

    lsum, acc = lax.fori_loop(0, n_tiles, pass2, (jnp.zeros((rows, LANES), F32), jnp.zeros((rows, dv), F32)))
    return acc / jnp.sum(lsum, axis=-1, keepdims=True)


def _flash_kernel(q_ref, k_ref, v_ref, bias_ref, o_ref, s_scr):
    i = pl.program_id(2)
    last = bias_ref.shape[0] - 1
    out = _attend(q_ref[...], k_ref, v_ref, lambda j: bias_ref[jnp.minimum(i - j, last)], 0, i + 1, s_scr,
                  tk=FLASH_T)
    o_ref[...] = out.astype(o_ref.dtype)


def _flash(q_arr, k_arr, v_arr, bias, *, nb, heads, dqk, dv, q_blk0, k_blk0, v_blk0):
    t = FLASH_T
    nq = SEQ // t
    return pl.pallas_call(
        _flash_kernel,
        out_shape=jax.ShapeDtypeStruct((nb * SEQ, heads * dv), BF16),
        grid=(nb, heads, nq),
        in_specs=[pl.BlockSpec((t, dqk), lambda b, h, i: (b * nq + i, q_blk0 + h)),
                  pl.BlockSpec((SEQ, dqk), lambda b, h, i: (b, k_blk0 + h)),
                  pl.BlockSpec((SEQ, dv), lambda b, h, i: (b, v_blk0 + h)),
                  pl.BlockSpec(bias.shape, lambda b, h, i: (0, 0, 0))],
        out_specs=pl.BlockSpec((t, dv), lambda b, h, i: (b * nq + i, h)),
        scratch_shapes=[pltpu.VMEM((nq, t, t), F32)],
        compiler_params=_cparams(("arbitrary", "arbitrary", "arbitrary")),
        name="flash",
    )(q_arr, k_arr, v_arr, bias)


def _nsa_kernel(q0_ref, q1_ref, q2_ref, q3_ref, q4_ref, q5_ref, kc_ref, vc_ref, ks_ref, vs_ref, kw_ref, vw_ref,
                gate_ref, cover_ref, expand_ref, wbias_ref, o_ref, bias_scr, s_scr):
    i = pl.program_id(1)
    tq, tk = NSA_TQ, NSA_T
    per = tq // tk
    nh = NSA_HEADS
    rows = nh * tq
    pos = i * tq + lax.broadcasted_iota(jnp.int32, (tq, 1), 0)
    q = jnp.concatenate([r[...] for r in (q0_ref, q1_ref, q2_ref, q3_ref, q4_ref, q5_ref)], axis=0)

    cidx = lax.broadcasted_iota(jnp.int32, (tq, N_CMP_PAD), 1)
    valid_c = (cidx * NSA_CMP_STRIDE + (NSA_CMP_LEN - 1) <= pos) & (cidx < N_CMP_PAD - 1)
    valid_cf = valid_c.astype(F32)
    cbias = jnp.where(valid_c, 0.0, NEG_INF).astype(F32)
    s = (_qk(q, kc_ref[0]).reshape(nh, tq, N_CMP_PAD) + cbias[None]).reshape(rows, N_CMP_PAD)
    p = (jnp.exp2(s - jnp.max(s, axis=-1, keepdims=True)).reshape(nh, tq, N_CMP_PAD) * valid_cf[None])
    p = p.reshape(rows, N_CMP_PAD)
    l = jnp.sum(p, axis=-1, keepdims=True)
    p = p / jnp.where(l > 0.0, l, 1.0)
    o_cmp = jnp.dot(p.astype(BF16), vc_ref[0], preferred_element_type=F32)
    psum = jnp.sum(p.reshape(nh, tq, N_CMP_PAD), axis=0)

    cover_t = cover_ref[...]
    imp = sum(jnp.dot(cover_t, part, preferred_element_type=F32) for part in _split3(psum.T))

    pos_t = i * tq + lax.broadcasted_iota(jnp.int32, (N_SEL_BLOCKS, tq), 1)
    jidx = lax.broadcasted_iota(jnp.int32, (N_SEL_BLOCKS, tq), 0)
    qblk = pos_t // NSA_SEL_BLOCK
    valid_s = jidx * NSA_SEL_BLOCK <= pos_t
    forced = (jidx == 0) | (jidx == qblk) | (jidx == qblk - 1)
    score = jnp.where(valid_s, jnp.where(forced, NSA_FORCE_SCORE, imp), -1.0)
    rank = jnp.zeros((N_SEL_BLOCKS, tq), jnp.int32)
    for ii in range(N_SEL_BLOCKS):
        si = score[ii:ii + 1, :]
        rank = rank + ((si > score) | ((si == score) & (ii < jidx))).astype(jnp.int32)
    sel_t = (rank < NSA_TOP_N).astype(F32)
    sel = jnp.concatenate([sel_t, jnp.zeros((LANES - N_SEL_BLOCKS, tq), F32)], axis=0).T.astype(BF16)

    n_causal = per * (i + 1)

    def fill(j, carry):
        hit = jnp.dot(sel, expand_ref[j], preferred_element_type=F32)
        kpos = j * tk + lax.broadcasted_iota(jnp.int32, (tq, tk), 1)
        bias_scr[j] = jnp.where((hit > 0.5) & (kpos <= pos), 0.0, NEG_INF).astype(F32)
        return carry

    lax.fori_loop(0, n_causal, fill, 0)

    o_slc = _attend(q, ks_ref, vs_ref, lambda j: bias_scr[j], 0, n_causal, s_scr, tk=tk, groups=nh)
    far = wbias_ref.shape[0] - per
    w_lo = jnp.maximum(per * i - far, 0)
    o_win = _attend(q, kw_ref, vw_ref, lambda j: wbias_ref[per * i - j + per - 1], w_lo, n_causal - w_lo, s_scr,
                    tk=tk, groups=nh)

    gates = jax.nn.sigmoid(gate_ref[...].astype(F32))
    for h in range(nh):
        sl = slice(h * tq, (h + 1) * tq)
        g0 = NSA_BRANCHES * h
        out = (gates[:, g0:g0 + 1] * o_cmp[sl] + gates[:, g0 + 1:g0 + 2] * o_slc[sl]
               + gates[:, g0 + 2:g0 + 3] * o_win[sl])
        o_ref[:, h * LANES:(h + 1) * LANES] = out.astype(o_ref.dtype)


def _nsa(hproj, kc, vc, cover, expand, wbias, nb):
    t = NSA_TQ
    nq = SEQ // t
    nk = SEQ // NSA_T
    wide = NSA_HEADS * HEAD_DIM
    q_spec = lambda h: pl.BlockSpec((t, LANES), lambda b, i: (b * nq + i, BLK_NQ + h))
    kv_spec = lambda blk: pl.BlockSpec((SEQ, LANES), lambda b, i: (b, blk))
    cspec = pl.BlockSpec((1, N_CMP_PAD, HEAD_DIM), lambda b, i: (b, 0, 0))
    full = lambda shape: pl.BlockSpec(shape, lambda b, i: (0,) * len(shape))
    return pl.pallas_call(
        _nsa_kernel,
        out_shape=jax.ShapeDtypeStruct((nb * SEQ, wide), BF16),
        grid=(nb, nq),
        in_specs=[q_spec(h) for h in range(NSA_HEADS)]
        + [cspec, cspec, kv_spec(BLK_NKS), kv_spec(BLK_NVS), kv_spec(BLK_NKW), kv_spec(BLK_NVW),
           pl.BlockSpec((t, LANES), lambda b, i: (b * nq + i, BLK_GATE)),
           full(cover.shape), full(expand.shape), full(wbias.shape)],
        out_specs=pl.BlockSpec((t, wide), lambda b, i: (b * nq + i, 0)),
        scratch_shapes=[pltpu.VMEM((nk, t, NSA_T), F32), pltpu.VMEM((nk, NSA_HEADS * t, NSA_T), F32)],
        compiler_params=_cparams(("arbitrary", "arbitrary")),
        name="nsa",
    )(*([hproj] * NSA_HEADS), kc, vc, hproj, hproj, hproj, hproj, hproj, cover, expand, wbias)


def _layer_norm(y, g, b):
    mu = jnp.mean(y, axis=-1, keepdims=True)
    var = jnp.mean(jnp.square(y - mu), axis=-1, keepdims=True)
    return (y - mu) * lax.rsqrt(var + LN_EPS) * g + b


def _lane_min(x):
    return jnp.min(x, axis=-1, keepdims=True)


def _lane_max(x):
    return jnp.max(x, axis=-1, keepdims=True)


def _route(logits):
    lane = lax.broadcasted_iota(jnp.int32, logits.shape, 1)
    lane_f = lane.astype(F32)
    far = float(LANES)
    is_grp = lane < N_GROUPS
    lg = jnp.where(is_grp, logits, NEG_INF)
    eg = jnp.where(is_grp, jnp.exp(lg - _lane_max(lg)), 0.0)
    prob = eg / jnp.sum(eg, axis=-1, keepdims=True)
    p_g = _lane_max(prob)
    g_idx = _lane_min(jnp.where(is_grp & (prob == p_g), lane_f, far))
    e_lo = N_GROUPS + EXPERTS_PER_GROUP * g_idx
    in_grp = (lane_f >= e_lo) & (lane_f < e_lo + EXPERTS_PER_GROUP)
    le = jnp.where(in_grp, logits, NEG_INF)
    v1 = _lane_max(le)
    i1 = _lane_min(jnp.where(in_grp & (le == v1), lane_f, far))
    rest = in_grp & (lane_f != i1)
    le2 = jnp.where(rest, logits, NEG_INF)
    v2 = _lane_max(le2)
    i2 = _lane_min(jnp.where(rest & (le2 == v2), lane_f, far))
    e21 = jnp.exp(v2 - v1)
    den = 1.0 + e21
    gate1 = p_g * (1.0 / den)
    gate2 = p_g * (e21 / den)
    out = jnp.where(lane == 0, i1 - N_GROUPS, 0.0)
    out = jnp.where(lane == 1, i2 - N_GROUPS, out)
    out = jnp.where(lane == 2, gate1, out)
    out = jnp.where(lane == 3, gate2, out)
    return out


def _outproj_kernel(om_ref, od_ref, on_ref, w_ref, x_ref, g_ref, b_ref, wr_ref, x1_ref, route_ref):
    n_mla = MLA_HEADS * HEAD_DIM
    n_dil = DIL_HEADS * HEAD_DIM
    for hh in range(OUTPROJ_TM // ROW_TM):
        rows = slice(hh * ROW_TM, (hh + 1) * ROW_TM)
        mix = jnp.dot(om_ref[rows, :], w_ref[0, 0:n_mla, :], preferred_element_type=F32)
        mix = mix + jnp.dot(od_ref[rows, :], w_ref[0, n_mla:n_mla + n_dil, :], preferred_element_type=F32)
        mix = mix + jnp.dot(on_ref[rows, :], w_ref[0, n_mla + n_dil:, :], preferred_element_type=F32)
        x1 = _layer_norm(DN_ALPHA * x_ref[rows, :] + mix, g_ref[...], b_ref[...])
        x1_ref[rows, :] = x1
        xh = x1.astype(BF16)
        xm = (x1 - xh.astype(F32)).astype(BF16)
        both = jnp.dot(xh, wr_ref[...], preferred_element_type=F32)
        logits = both[:, 0:LANES] + (both[:, LANES:2 * LANES]
                                     + jnp.dot(xm, wr_ref[:, 0:LANES], preferred_element_type=F32))
        route_ref[rows, :] = _route(logits)


def _outproj(o_mla, o_dil, o_nsa, w_out_s, layer, x, g, b, wr3):
    t = x.shape[0]
    tm = OUTPROJ_TM
    full = lambda shape: pl.BlockSpec(shape, lambda i: (0,) * len(shape))
    row = lambda w: pl.BlockSpec((tm, w), lambda i: (i, 0))
    return pl.pallas_call(
        _outproj_kernel,
        out_shape=(jax.ShapeDtypeStruct((t, D_MODEL), F32), jax.ShapeDtypeStruct((t, LANES), F32)),
        grid=(t // tm,),
        in_specs=[row(o_mla.shape[1]), row(o_dil.shape[1]), row(o_nsa.shape[1]),
                  pl.BlockSpec((1,) + w_out_s.shape[1:], lambda i: (layer, 0, 0)),
                  row(D_MODEL), full((1, D_MODEL)), full((1, D_MODEL)), full(wr3.shape)],
        out_specs=(row(D_MODEL), row(LANES)),
        compiler_params=_cparams(("arbitrary",)),
        name="outproj",
    )(o_mla, o_dil, o_nsa, w_out_s, x, g, b, wr3)


def _dispatch_kernel(fill_ref, pos_ref, x1_ref, xb_hbm, stage, zbuf, sem, zsem):
    i = pl.program_id(0)
    n = pl.num_programs(0)
    tm = ROW_TM
    slot = i % 2

    def drain(s):
        for _ in range(MOE_TOP_K):
            pltpu.make_async_copy(stage.at[s], xb_hbm.at[pl.ds(0, tm)], sem.at[s]).wait()

    def zero_copy(e):
        start = pl.multiple_of(fill_ref[e], MOE_ROWS)
        return pltpu.make_async_copy(zbuf, xb_hbm.at[pl.ds(start, MOE_ROWS)], zsem)

    @pl.when(i == 0)
    def _():
        zbuf[...] = jnp.zeros_like(zbuf)
        for e in range(N_EXPERTS):
            @pl.when(fill_ref[e] >= 0)
            def _():
                zero_copy(e).start()
        for e in range(N_EXPERTS):
            @pl.when(fill_ref[e] >= 0)
            def _():
                zero_copy(e).wait()

        def tail_copy(b):
            start = pl.multiple_of(b * MOE_ROWS, MOE_ROWS)
            return pltpu.make_async_copy(zbuf, xb_hbm.at[pl.ds(start, MOE_ROWS)], zsem)

        n_blocks = xb_hbm.shape[0] // MOE_ROWS
        lax.fori_loop(fill_ref[N_EXPERTS], n_blocks, lambda b, c: (tail_copy(b).start(), c)[1], 0)
        lax.fori_loop(fill_ref[N_EXPERTS], n_blocks, lambda b, c: (tail_copy(b).wait(), c)[1], 0)

    @pl.when(i >= 2)
    def _():
        drain(slot)

    stage[slot] = x1_ref[...]

    for r in range(tm):
        for k in range(MOE_TOP_K):
            p = pos_ref[0, 0, MOE_TOP_K * r + k]
            pltpu.make_async_copy(stage.at[slot, pl.ds(r, 1)], xb_hbm.at[pl.ds(p, 1)],
                                  sem.at[slot]).start(priority=k % 2)

    @pl.when(i == n - 1)
    def _():
        drain(1 - slot)
        drain(slot)


def _dispatch(fill, pos3, x1, n_rows):
    t = x1.shape[0]
    tm = ROW_TM
    grid_spec = pltpu.PrefetchScalarGridSpec(
        num_scalar_prefetch=1,
        grid=(t // tm,),
        in_specs=[pl.BlockSpec((1, 1, MOE_TOP_K * tm), lambda i, fl: (i, 0, 0), memory_space=pltpu.SMEM),
                  pl.BlockSpec((tm, D_MODEL), lambda i, fl: (i, 0))],
        out_specs=pl.BlockSpec(memory_space=pl.ANY),
        scratch_shapes=[pltpu.VMEM((2, tm, D_MODEL), F32), pltpu.VMEM((MOE_ROWS, D_MODEL), F32),
                        pltpu.SemaphoreType.DMA((2,)), pltpu.SemaphoreType.DMA(())],
    )
    return pl.pallas_call(
        _dispatch_kernel,
        out_shape=jax.ShapeDtypeStruct((n_rows, D_MODEL), F32),
        grid_spec=grid_spec,
        compiler_params=_cparams(("arbitrary",)),
        name="dispatch",
    )(fill, pos3, x1)


def _experts_kernel(blk_ex_ref, first_ref, nxt_ref, nused_ref, x_ref, wg_hbm, wu_hbm, wd_hbm, y_ref,
                    wg_f, wu_f, wd_f, wg_b, wu_b, wd_b, sem, *, layer):
    b = pl.program_id(0)
    nused = nused_ref[0]

    def weight_copies(e, s):
        return (pltpu.make_async_copy(wg_hbm.at[layer, e], wg_f.at[s], sem.at[s]),
                pltpu.make_async_copy(wu_hbm.at[layer, e], wu_f.at[s], sem.at[s]),
                pltpu.make_async_copy(wd_hbm.at[layer, e], wd_f.at[s], sem.at[s]))

    @pl.when(b == 0)
    def _():
        for cp in weight_copies(blk_ex_ref[0], 0):
            cp.start()

    @pl.when(b < nused)
    def _():
        first = first_ref[b]

        @pl.when(first > 0)
        def _():
            s = first - 1
            for cp in weight_copies(blk_ex_ref[b], s):
                cp.wait()

            @pl.when(nxt_ref[b] >= 0)
            def _():
                for cp in weight_copies(nxt_ref[b], 1 - s):
                    cp.start()

            wg_b[...] = wg_f[s].astype(BF16)
            wu_b[...] = wu_f[s].astype(BF16)
            wd_b[...] = wd_f[s].astype(BF16)

        rows = x_ref[...].astype(BF16)
        gate = jnp.dot(rows, wg_b[...], preferred_element_type=F32)
        up = jnp.dot(rows, wu_b[...], preferred_element_type=F32)
        hid = (gate * jax.nn.sigmoid(gate) * up).astype(BF16)
        y_ref[...] = jnp.dot(hid, wd_b[...], preferred_element_type=F32)

    @pl.when(b >= nused)
    def _():
        y_ref[...] = jnp.zeros_like(y_ref)


def _experts(blk_ex, first, nxt, nused, xb_rows, w_gate, w_up, w_down, layer):
    n_blocks = xb_rows.shape[0] // MOE_ROWS
    hbm = pl.BlockSpec(memory_space=pl.ANY)
    grid_spec = pltpu.PrefetchScalarGridSpec(
        num_scalar_prefetch=4,
        grid=(n_blocks,),
        in_specs=[pl.BlockSpec((MOE_ROWS, D_MODEL), lambda b, ex, fi, nx, nu: (jnp.minimum(b, nu[0] - 1), 0)),
                  hbm, hbm, hbm],
        out_specs=pl.BlockSpec((MOE_ROWS, D_MODEL), lambda b, ex, fi, nx, nu: (b, 0)),
        scratch_shapes=[pltpu.VMEM((2, D_MODEL, EXPERT_FF), F32), pltpu.VMEM((2, D_MODEL, EXPERT_FF), F32),
                        pltpu.VMEM((2, EXPERT_FF, D_MODEL), F32),
                        pltpu.VMEM((D_MODEL, EXPERT_FF), BF16), pltpu.VMEM((D_MODEL, EXPERT_FF), BF16),
                        pltpu.VMEM((EXPERT_FF, D_MODEL), BF16), pltpu.SemaphoreType.DMA((2,))],
    )
    return pl.pallas_call(
        functools.partial(_experts_kernel, layer=layer),
        out_shape=jax.ShapeDtypeStruct((n_blocks * MOE_ROWS, D_MODEL), F32),
        grid_spec=grid_spec,
        compiler_params=_cparams(("arbitrary",)),
        name="experts",
    )(blk_ex, first, nxt, nused, xb_rows, w_gate, w_up, w_down)


def _combine_kernel(pos_cur_ref, pos_nxt_ref, y_hbm, x1_ref, route_ref, g_ref, b_ref, o_ref, ob_ref, ybuf, sem):
    i = pl.program_id(0)
    n = pl.num_programs(0)
    tm = ROW_TM

    def issue(pos_ref, slot):
        for r in range(tm):
            for k in range(MOE_TOP_K):
                p = pos_ref[0, 0, MOE_TOP_K * r + k]
                pltpu.make_async_copy(y_hbm.at[pl.ds(p, 1)], ybuf.at[slot, k, pl.ds(r, 1)], sem.at[slot]).start()

    @pl.when(i == 0)
    def _():
        issue(pos_cur_ref, 0)

    @pl.when(i + 1 < n)
    def _():
        issue(pos_nxt_ref, (i + 1) % 2)

    slot = i % 2
    for k in range(MOE_TOP_K):
        pltpu.make_async_copy(y_hbm.at[pl.ds(0, tm)], ybuf.at[slot, k], sem.at[slot]).wait()
    route = route_ref[...]
    ffn = route[:, 2:3] * ybuf[slot, 0] + route[:, 3:4] * ybuf[slot, 1]
    x2 = _layer_norm(DN_ALPHA * x1_ref[...] + ffn, g_ref[...], b_ref[...])
    o_ref[...] = x2
    ob_ref[...] = x2.astype(ob_ref.dtype)


def _combine(pos3, yb, x1, route, g, b):
    t = x1.shape[0]
    tm = ROW_TM
    nt = t // tm
    pos_spec = lambda shift: pl.BlockSpec((1, 1, MOE_TOP_K * tm), lambda i: (jnp.minimum(i + shift, nt - 1), 0, 0),
                                          memory_space=pltpu.SMEM)
    row = lambda w: pl.BlockSpec((tm, w), lambda i: (i, 0))
    full = lambda shape: pl.BlockSpec(shape, lambda i: (0,) * len(shape))
    return pl.pallas_call(
        _combine_kernel,
        out_shape=(jax.ShapeDtypeStruct((t, D_MODEL), F32), jax.ShapeDtypeStruct((t, D_MODEL), BF16)),
        grid=(nt,),
        in_specs=[pos_spec(0), pos_spec(1), pl.BlockSpec(memory_space=pl.ANY), row(D_MODEL), row(LANES),
                  full((1, D_MODEL)), full((1, D_MODEL))],
        out_specs=(row(D_MODEL), row(D_MODEL)),
        scratch_shapes=[pltpu.VMEM((2, MOE_TOP_K, tm, D_MODEL), F32), pltpu.SemaphoreType.DMA((2,))],
        compiler_params=_cparams(("arbitrary",)),
        name="combine",
    )(pos3, pos3, yb, x1, route, g, b)


def _dispatch_plan(route, t):
    m = t * MOE_TOP_K
    n_blocks = -(-m // MOE_ROWS) + N_EXPERTS
    chunk = MOE_ROWS
    ex = route[:, 0:MOE_TOP_K].reshape(m // chunk, chunk, 1)
    onehot = (ex == jnp.arange(N_EXPERTS, dtype=F32)[None, None, :]).astype(F32)
    tri = jnp.asarray(np.tril(np.ones((chunk, chunk), np.float32)))
    within = jnp.einsum('ij,cje->cie', tri, onehot)
    totals = jnp.sum(onehot, axis=1)
    before = jnp.cumsum(totals, axis=0) - totals
    counts = jnp.sum(totals, axis=0)
    padded = jnp.ceil(counts / MOE_ROWS) * MOE_ROWS
    pad_end = jnp.cumsum(padded)
    pad_start = pad_end - padded
    row = jnp.sum(onehot * (within - 1.0 + before[:, None, :] + pad_start[None, None, :]), axis=-1)
    pos = row.astype(jnp.int32).reshape(m)
    blk_start = jnp.arange(n_blocks, dtype=F32) * MOE_ROWS
    blk_ex = jnp.minimum(jnp.sum((pad_end[None, :] <= blk_start[:, None]).astype(jnp.int32), axis=1),
                         N_EXPERTS - 1).astype(jnp.int32)
    nused = (pad_end[-1] / MOE_ROWS).astype(jnp.int32).reshape(1)
    fill = jnp.concatenate([jnp.where(padded > 0, pad_end - MOE_ROWS, -1.0).astype(jnp.int32), nused])
    nonempty = padded > 0
    slot_e = (jnp.cumsum(nonempty.astype(jnp.int32)) - 1) % 2
    expert_ids = jnp.arange(N_EXPERTS, dtype=jnp.int32)
    later = lax.cummin(jnp.where(nonempty, expert_ids, N_EXPERTS), axis=0, reverse=True)
    nxt_e = jnp.concatenate([later[1:], jnp.full((1,), N_EXPERTS, jnp.int32)])
    nxt_e = jnp.where(nxt_e >= N_EXPERTS, -1, nxt_e)
    is_first = (blk_start == pad_start[blk_ex]) & (jnp.arange(n_blocks) < nused[0])
    first = jnp.where(is_first, 1 + slot_e[blk_ex], 0).astype(jnp.int32)
    return blk_ex, first, nxt_e[blk_ex].astype(jnp.int32), nused, fill, pos


W_HALF = LANES // 2


def _shift_kernel(a_ref, b_ref, tail_ref, o_ref):
    m = pl.program_id(0)
    n_src = IN_COLS // LANES
    nl = o_ref.shape[0]
    d = o_ref.shape[2]
    nc = d // LANES
    a2, b2, t2 = a_ref, b_ref, tail_ref

    def slab(ref4, col0, c, l):
        return ref4[pl.ds(col0, W_HALF), c, l, :].astype(o_ref.dtype)

    def copy(ref_top, top0, ref_bot, bot0):
        for l in range(nl):
            for c in range(nc):
                o_ref[l, 0:W_HALF, c * LANES:(c + 1) * LANES] = slab(ref_top, top0, c, l)
                if ref_bot is None:
                    o_ref[l, W_HALF:LANES, c * LANES:(c + 1) * LANES] = jnp.zeros((W_HALF, LANES), o_ref.dtype)
                else:
                    o_ref[l, W_HALF:LANES, c * LANES:(c + 1) * LANES] = slab(ref_bot, bot0, c, l)

    @pl.when(m < n_src)
    def _():
        copy(a2, 0, b2, 0)

    @pl.when(m == n_src)
    def _():
        copy(a2, 0, t2, 0)

    @pl.when(m == n_src + 1)
    def _():
        copy(t2, W_HALF, None, 0)


def _shift_w_in(w_in):
    nl, d, _ = w_in.shape
    n_src = IN_COLS // LANES
    n_keep = (IN_SHIFT_FROM + MLA_ROPE_DIM) // LANES
    last = n_src * (LANES // W_HALF) - 1
    nc = d // LANES
    view = lambda w: jnp.transpose(w.reshape(nl, nc, LANES, w.shape[-1]), (3, 1, 0, 2))
    w_t = view(w_in)
    tail_t = view(jnp.pad(w_in[:, :, n_src * LANES:], ((0, 0), (0, 0), (0, (n_src + 1) * LANES - IN_COLS))))
    first = lambda m: jnp.where(m < n_keep, 2 * m, 2 * m - 1)
    return pl.pallas_call(
        _shift_kernel,
        out_shape=jax.ShapeDtypeStruct((nl, IN_COLS_PAD, d), BF16),
        grid=(N_IN_BLOCKS,),
        in_specs=[pl.BlockSpec((W_HALF, nc, nl, LANES), lambda m: (jnp.minimum(first(m), last), 0, 0, 0)),
                  pl.BlockSpec((W_HALF, nc, nl, LANES), lambda m: (jnp.minimum(first(m) + 1, last), 0, 0, 0)),
                  pl.BlockSpec((LANES, nc, nl, LANES), lambda m: (0, 0, 0, 0))],
        out_specs=pl.BlockSpec((nl, LANES, d), lambda m: (0, m, 0)),
        compiler_params=_cparams(("arbitrary",)),
        name="shiftw",
    )(w_t, w_t, tail_t)


def _permute_w_q_up(w_q_up):
    w = w_q_up.reshape(MLA_Q_RANK, MLA_HEADS, MLA_NOPE_DIM + MLA_ROPE_DIM)
    zero = jnp.zeros((MLA_Q_RANK, MLA_HEADS, MLA_QK_PAD - MLA_NOPE_DIM - MLA_ROPE_DIM), w.dtype)
    w = jnp.concatenate([w[:, :, :MLA_NOPE_DIM], zero, w[:, :, MLA_NOPE_DIM:]], axis=-1)
    return w.reshape(MLA_Q_RANK, MLA_HEADS * MLA_QK_PAD).astype(BF16)


def _split_w_kv_up(w_kv_up):
    w = w_kv_up.reshape(MLA_KV_RANK, MLA_HEADS, MLA_NOPE_DIM + HEAD_DIM)
    wkn = w[:, :, :MLA_NOPE_DIM].reshape(MLA_KV_RANK, MLA_HEADS * MLA_NOPE_DIM)
    wv = w[:, :, MLA_NOPE_DIM:].reshape(MLA_KV_RANK, MLA_HEADS * HEAD_DIM)
    return wkn.astype(BF16), wv.astype(BF16)


def _router_slices(w_grp, w_exp):
    w = jnp.concatenate([w_grp, w_exp, jnp.zeros((D_MODEL, LANES - N_GROUPS - N_EXPERTS), F32)], axis=1)
    hi = w.astype(BF16)
    mid = (w - hi.astype(F32)).astype(BF16)
    return jnp.concatenate([hi, mid], axis=1)


def _mixer(xb, nb, layer, w_in_s, q_lat_norm, w_q_up, kv_lat_norm, w_kv_up,
           cmp_pos_k, cmp_w1_k, cmp_w2_k, cmp_pos_v, cmp_w1_v, cmp_w2_v, consts):
    hproj = _inproj(xb, w_in_s, layer, consts["rope32"])
    wkn, wv = _split_w_kv_up(w_kv_up)
    q_mla, k_mla, v_mla = _mlaprep(hproj, q_lat_norm.reshape(1, -1), kv_lat_norm.reshape(1, -1),
                                   _permute_w_q_up(w_q_up), wkn, wv, consts["rope64"])
    wide = NSA_CMP_STRIDE * HEAD_DIM
    tk2 = hproj[:, BLK_NKC * LANES:(BLK_NKC + 1) * LANES].reshape(nb, N_CMP_PAD, wide)
    tv2 = hproj[:, BLK_NVC * LANES:(BLK_NVC + 1) * LANES].reshape(nb, N_CMP_PAD, wide)
    kc, vc = _compress(tk2, tv2, cmp_pos_k.reshape(2, wide), cmp_pos_v.reshape(2, wide),
                       cmp_w1_k.astype(BF16), cmp_w2_k.astype(BF16), cmp_w1_v.astype(BF16), cmp_w2_v.astype(BF16),
                       consts["rope_cmp"])
    o_mla = _flash(q_mla, k_mla, v_mla, consts["bias_causal"], nb=nb, heads=MLA_HEADS, dqk=MLA_QK_PAD,
                   dv=HEAD_DIM, q_blk0=0, k_blk0=0, v_blk0=0)
    o_dil = _flash(hproj, hproj, hproj, consts["bias_dil"], nb=nb, heads=DIL_HEADS, dqk=HEAD_DIM, dv=HEAD_DIM,
                   q_blk0=BLK_DQ, k_blk0=BLK_DK, v_blk0=BLK_DV)
    o_nsa = _nsa(hproj, kc, vc, consts["cover"], consts["expand"], consts["bias_win"], nb)
    return o_mla, o_dil, o_nsa


def _make_consts():
    pos = np.arange(SEQ)
    slot = np.arange(N_CMP_PAD)
    cmp_start = slot * NSA_CMP_STRIDE
    cmp_end = cmp_start + NSA_CMP_LEN - 1
    sel_start = np.arange(N_SEL_BLOCKS) * NSA_SEL_BLOCK
    cover = ((cmp_start[:, None] < sel_start[None, :] + NSA_SEL_BLOCK)
             & (cmp_start[:, None] + NSA_CMP_LEN > sel_start[None, :])
             & (slot[:, None] < N_CMP_PAD - 1)).astype(np.float32)
    expand = (np.arange(LANES)[:, None] == (pos // NSA_SEL_BLOCK)[None, :]).astype(np.float32)
    expand = expand.reshape(LANES, SEQ // NSA_T, NSA_T).transpose(1, 0, 2)
    win_diags = range(-(NSA_TQ // NSA_T - 1), -(-NSA_WINDOW // NSA_T) + 1)
    return {
        "rope32": _rope_tables(pos, ROT_DIM, True),
        "rope64": _rope_tables(pos, MLA_ROPE_DIM, False, KROPE_LANE0),
        "rope_cmp": _rope_tables(cmp_end, ROT_DIM, True),
        "bias_causal": _distance_bias(_causal_mult, (0, 1), FLASH_T),
        "bias_dil": _distance_bias(_dilated_mult, range(SEQ // FLASH_T), FLASH_T),
        "bias_win": _distance_bias(_window_mult, win_diags, NSA_T, NSA_TQ),
        "cover": jnp.asarray(cover.T, BF16),
        "expand": jnp.asarray(expand, BF16),
    }


def kernel(x, w_in, q_lat_norm, w_q_up, kv_lat_norm, w_kv_up, cmp_pos_k, cmp_w1_k, cmp_w2_k, cmp_pos_v, cmp_w1_v,
           cmp_w2_v, w_out, ln1_g, ln1_b, w_grp, w_exp, w_gate, w_up, w_down, ln2_g, ln2_b):
    nb, s, d = x.shape
    assert s == SEQ and d == D_MODEL
    t = nb * s
    consts = _make_consts()
    w_in_s = _shift_w_in(w_in)
    w_out_s = w_out.astype(BF16)
    xf = x.reshape(t, d)
    xb = xf.astype(BF16)
    for l in range(DEPTH):
        o_mla, o_dil, o_nsa = _mixer(xb, nb, l, w_in_s, q_lat_norm[l], w_q_up[l], kv_lat_norm[l], w_kv_up[l],
                                     cmp_pos_k[l], cmp_w1_k[l], cmp_w2_k[l], cmp_pos_v[l], cmp_w1_v[l],
                                     cmp_w2_v[l], consts)
        x1, route = _outproj(o_mla, o_dil, o_nsa, w_out_s, l, xf, ln1_g[l].reshape(1, d),
                             ln1_b[l].reshape(1, d), _router_slices(w_grp[l], w_exp[l]))
        blk_ex, first, nxt, nused, fill, pos = _dispatch_plan(route, t)
        pos3 = pos.reshape(t // ROW_TM, 1, MOE_TOP_K * ROW_TM)
        n_rows = (-(-t * MOE_TOP_K // MOE_ROWS) + N_EXPERTS) * MOE_ROWS
        yb = _experts(blk_ex, first, nxt, nused, _dispatch(fill, pos3, x1, n_rows), w_gate, w_up, w_down, l)
        xf, xb = _combine(pos3, yb, x1, route,
                          ln2_g[l].reshape(1, d), ln2_b[l].reshape(1, d))
    return xf.reshape(nb, s, d)
```

```python
import functools

import numpy as np
import jax
import jax.numpy as jnp
from jax import lax
from jax.experimental import pallas as pl
from jax.experimental.pallas import tpu as pltpu

F32 = jnp.float32
BF16 = jnp.bfloat16

D_MODEL = 2048
SEQ = 2048
DEPTH = 2
HEAD_DIM = 128
LANES = 128
MLA_HEADS = 4
DIL_HEADS = 6
NSA_HEADS = 6
MLA_Q_RANK = 384
MLA_KV_RANK = 128
MLA_NOPE_DIM = 128
MLA_ROPE_DIM = 64
MLA_QK_PAD = 256
DIL_PAIRS = ((128, 1), (512, 4), (2048, 16))
NSA_CMP_LEN = 32
NSA_CMP_STRIDE = 16
NSA_CMP_HIDDEN = 256
NSA_SEL_BLOCK = 64
NSA_TOP_N = 16
NSA_WINDOW = 512
NSA_BRANCHES = 3
NSA_FORCE_SCORE = 1e4
ROPE_THETA = 500000.0
ROT_DIM = HEAD_DIM // 4
N_GROUPS = 4
EXPERTS_PER_GROUP = 8
N_EXPERTS = N_GROUPS * EXPERTS_PER_GROUP
EXPERT_FF = 512
MOE_TOP_K = 2
MOE_ROWS = 256
DN_ALPHA = (2 * DEPTH) ** 0.25
LN_EPS = 1e-5
RMS_EPS = 1e-6
NEG_INF = -1e30
LOG2_E = 1.4426950408889634

N_CMP_PAD = SEQ // NSA_CMP_STRIDE
N_SEL_BLOCKS = SEQ // NSA_SEL_BLOCK

BLK_QLAT = 0
BLK_KVLAT = 3
BLK_KROPE = 4
BLK_DQ = 5
BLK_DK = 11
BLK_DV = 17
BLK_NQ = 23
BLK_NKC = 29
BLK_NVC = 30
BLK_NKS = 31
BLK_NVS = 32
BLK_NKW = 33
BLK_NVW = 34
BLK_GATE = 35
N_IN_BLOCKS = 36
IN_COLS = 4434
IN_SHIFT_FROM = MLA_Q_RANK + MLA_KV_RANK - MLA_ROPE_DIM
IN_COLS_PAD = N_IN_BLOCKS * LANES
KROPE_LANE0 = LANES - MLA_ROPE_DIM

INPROJ_TM = 2048
INPROJ_SUB = 512
INPROJ_TN = 1536
FLASH_T = 1024
NSA_T = 512
NSA_TQ = 512
ROW_TM = 256
MLAPREP_TM = 512
OUTPROJ_TM = 512

VMEM_LIMIT = 56 * 1024 * 1024


def _cparams(sem):
    return pltpu.CompilerParams(dimension_semantics=sem, vmem_limit_bytes=VMEM_LIMIT)


def _rope_tables(pos, rot_dim, keep_rest, lane0=0):
    half = rot_dim // 2
    inv_freq = np.power(np.float32(ROPE_THETA), -np.arange(half, dtype=np.float32) / np.float32(half))
    ang = (pos.astype(np.float32)[:, None] * inv_freq[None, :]).astype(np.float32)
    cos, sin = np.cos(ang).astype(np.float32), np.sin(ang).astype(np.float32)
    n = pos.shape[0]
    c = np.full((n, LANES), 1.0 if keep_rest else 0.0, np.float32)
    s1 = np.zeros((n, LANES), np.float32)
    s2 = np.zeros((n, LANES), np.float32)
    c[:, lane0:lane0 + half] = cos
    c[:, lane0 + half:lane0 + rot_dim] = cos
    s1[:, lane0 + half:lane0 + rot_dim] = sin
    s2[:, lane0:lane0 + half] = -sin
    return jnp.asarray(c), jnp.asarray(s1), jnp.asarray(s2)


def _rope_lanes(a, c, s1, s2, half):
    return a * c + pltpu.roll(a, half, 1) * s1 + pltpu.roll(a, LANES - half, 1) * s2


def _distance_bias(mult_fn, diags, t, rows=None):
    d = np.asarray(diags, np.int64)[:, None, None] * t
    r = np.arange(t if rows is None else rows, dtype=np.int64)[None, :, None]
    c = np.arange(t, dtype=np.int64)[None, None, :]
    mult = mult_fn(d + r - c)
    return jnp.asarray(np.where(mult > 0, np.log2(np.maximum(mult, 1).astype(np.float32)), NEG_INF).astype(np.float32))


def _causal_mult(dist):
    return (dist >= 0).astype(np.int64)


def _dilated_mult(dist):
    m = np.zeros_like(dist)
    for window, dil in DIL_PAIRS:
        m = m + ((dist >= 0) & (dist % dil == 0) & (dist <= (window // dil) * dil)).astype(np.int64)
    return m


def _window_mult(dist):
    return ((dist >= 0) & (dist <= NSA_WINDOW - 1)).astype(np.int64)


def _in_block(blk, lo, n):
    return jnp.logical_and(blk >= lo, blk < lo + n)


def _inproj_kernel(x_ref, w_ref, c_ref, s1_ref, s2_ref, o_ref, *, scale):
    j = pl.program_id(0)
    for hh in range(INPROJ_TM // INPROJ_SUB):
        rows = slice(hh * INPROJ_SUB, (hh + 1) * INPROJ_SUB)
        acc = _qk(x_ref[rows, :], w_ref[0])
        c, s1, s2 = c_ref[rows, :], s1_ref[rows, :], s2_ref[rows, :]
        for u in range(INPROJ_TN // LANES):
            blk = j * (INPROJ_TN // LANES) + u
            is_q = jnp.logical_or(_in_block(blk, BLK_DQ, DIL_HEADS), _in_block(blk, BLK_NQ, NSA_HEADS))
            is_rope = is_q | _in_block(blk, BLK_DK, DIL_HEADS) | (blk == BLK_NKS) | (blk == BLK_NKW)
            sc = jnp.where(is_q, scale, 1.0).astype(F32)
            a = acc[:, u * LANES:(u + 1) * LANES]
            r = _rope_lanes(a, jnp.where(is_rope, c, 1.0), jnp.where(is_rope, s1, 0.0),
                            jnp.where(is_rope, s2, 0.0), ROT_DIM // 2)
            o_ref[rows, u * LANES:(u + 1) * LANES] = (r * sc).astype(o_ref.dtype)


def _inproj(xb, w_in_s, layer, tabs):
    t = xb.shape[0]
    tm, tn = INPROJ_TM, INPROJ_TN
    nrow = SEQ // tm
    tab_spec = pl.BlockSpec((tm, LANES), lambda j, i: (i % nrow, 0))
    return pl.pallas_call(
        functools.partial(_inproj_kernel, scale=HEAD_DIM ** -0.5 * LOG2_E),
        out_shape=jax.ShapeDtypeStruct((t, IN_COLS_PAD), BF16),
        grid=(IN_COLS_PAD // tn, t // tm),
        in_specs=[pl.BlockSpec((tm, D_MODEL), lambda j, i: (i, 0)),
                  pl.BlockSpec((1, tn, D_MODEL), lambda j, i: (layer, j, 0)),
                  tab_spec, tab_spec, tab_spec],
        out_specs=pl.BlockSpec((tm, tn), lambda j, i: (i, j)),
        compiler_params=_cparams(("arbitrary", "arbitrary")),
        name="inproj",
    )(xb, w_in_s, *tabs)


def _rms(xf, g):
    return xf * lax.rsqrt(jnp.mean(jnp.square(xf), axis=-1, keepdims=True) + RMS_EPS) * g


def _mlaprep_kernel(ql_ref, kvl_ref, kr_ref, gq_ref, gkv_ref, wq_ref, wkn_ref, wv_ref,
                    c_ref, s1_ref, s2_ref, q_ref, k_ref, v_ref, *, scale):
    c, s1, s2 = c_ref[...], s1_ref[...], s2_ref[...]
    half = MLA_ROPE_DIM // 2
    qn = _rms(ql_ref[...].astype(F32), gq_ref[...]).astype(BF16)
    q = jnp.dot(qn, wq_ref[...], preferred_element_type=F32)
    for h in range(MLA_HEADS):
        lo = h * MLA_QK_PAD
        q_ref[:, lo:lo + LANES] = (q[:, lo:lo + LANES] * scale).astype(q_ref.dtype)
        a = q[:, lo + LANES:lo + 2 * LANES]
        q_ref[:, lo + LANES:lo + 2 * LANES] = (_rope_lanes(a, c, s1, s2, half) * scale).astype(q_ref.dtype)
    kvn = _rms(kvl_ref[...].astype(F32), gkv_ref[...]).astype(BF16)
    kn = jnp.dot(kvn, wkn_ref[...], preferred_element_type=F32)
    v_ref[...] = jnp.dot(kvn, wv_ref[...], preferred_element_type=F32).astype(v_ref.dtype)
    kpe = _rope_lanes(kr_ref[...].astype(F32), c, s1, s2, half).astype(k_ref.dtype)
    for h in range(MLA_HEADS):
        lo = h * MLA_QK_PAD
        k_ref[:, lo:lo + LANES] = kn[:, h * LANES:(h + 1) * LANES].astype(k_ref.dtype)
        k_ref[:, lo + LANES:lo + 2 * LANES] = kpe


def _mlaprep(hproj, gq, gkv, wq_p, wkn, wv, tabs):
    t = hproj.shape[0]
    tm = MLAPREP_TM
    nrow = SEQ // tm
    tab_spec = pl.BlockSpec((tm, LANES), lambda i: (i % nrow, 0))
    full = lambda shape: pl.BlockSpec(shape, lambda i: (0,) * len(shape))
    return pl.pallas_call(
        functools.partial(_mlaprep_kernel, scale=(MLA_NOPE_DIM + MLA_ROPE_DIM) ** -0.5 * LOG2_E),
        out_shape=(jax.ShapeDtypeStruct((t, MLA_HEADS * MLA_QK_PAD), BF16),
                   jax.ShapeDtypeStruct((t, MLA_HEADS * MLA_QK_PAD), BF16),
                   jax.ShapeDtypeStruct((t, MLA_HEADS * HEAD_DIM), BF16)),
        grid=(t // tm,),
        in_specs=[pl.BlockSpec((tm, MLA_Q_RANK), lambda i: (i, BLK_QLAT * LANES // MLA_Q_RANK)),
                  pl.BlockSpec((tm, LANES), lambda i: (i, BLK_KVLAT)),
                  pl.BlockSpec((tm, LANES), lambda i: (i, BLK_KROPE)),
                  full((1, MLA_Q_RANK)), full((1, MLA_KV_RANK)),
                  full((MLA_Q_RANK, MLA_HEADS * MLA_QK_PAD)),
                  full((MLA_KV_RANK, MLA_HEADS * MLA_NOPE_DIM)),
                  full((MLA_KV_RANK, MLA_HEADS * HEAD_DIM)),
                  tab_spec, tab_spec, tab_spec],
        out_specs=(pl.BlockSpec((tm, MLA_HEADS * MLA_QK_PAD), lambda i: (i, 0)),
                   pl.BlockSpec((tm, MLA_HEADS * MLA_QK_PAD), lambda i: (i, 0)),
                   pl.BlockSpec((tm, MLA_HEADS * HEAD_DIM), lambda i: (i, 0))),
        compiler_params=_cparams(("arbitrary",)),
        name="mlaprep",
    )(hproj, hproj, hproj, gq, gkv, wq_p, wkn, wv, *tabs)


def _gelu_tanh(x):
    return 0.5 * x * (1.0 + jnp.tanh(0.7978845608028654 * (x + 0.044715 * (x * x * x))))


def _compress_kernel(tk_ref, tv_ref, pek_ref, pev_ref, w1k_ref, w2k_ref, w1v_ref, w2v_ref,
                     c_ref, s1_ref, s2_ref, kc_ref, vc_ref):
    half_in = NSA_CMP_STRIDE * HEAD_DIM

    def comp(t_ref, pe_ref, w1_ref, w2_ref):
        t = t_ref[0].astype(F32)
        a = (t + pe_ref[0:1, :]).astype(BF16)
        b = (t + pe_ref[1:2, :]).astype(BF16)
        y0 = jnp.dot(a, w1_ref[0:half_in, :], preferred_element_type=F32)
        y1 = jnp.dot(b, w1_ref[half_in:2 * half_in, :], preferred_element_type=F32)
        hid = _gelu_tanh(y0 + pltpu.roll(y1, N_CMP_PAD - 1, 0))
        return jnp.dot(hid.astype(BF16), w2_ref[...], preferred_element_type=F32)

    kc = comp(tk_ref, pek_ref, w1k_ref, w2k_ref)
    kc_ref[0] = _rope_lanes(kc, c_ref[...], s1_ref[...], s2_ref[...], ROT_DIM // 2).astype(kc_ref.dtype)
    vc_ref[0] = comp(tv_ref, pev_ref, w1v_ref, w2v_ref).astype(vc_ref.dtype)


def _compress(tk2, tv2, pek, pev, w1k, w2k, w1v, w2v, tabs):
    nb = tk2.shape[0]
    wide = NSA_CMP_STRIDE * HEAD_DIM
    full = lambda shape: pl.BlockSpec(shape, lambda b: (0,) * len(shape))
    bspec = pl.BlockSpec((1, N_CMP_PAD, wide), lambda b: (b, 0, 0))
    ospec = pl.BlockSpec((1, N_CMP_PAD, HEAD_DIM), lambda b: (b, 0, 0))
    return pl.pallas_call(
        _compress_kernel,
        out_shape=(jax.ShapeDtypeStruct((nb, N_CMP_PAD, HEAD_DIM), BF16),) * 2,
        grid=(nb,),
        in_specs=[bspec, bspec, full((2, wide)), full((2, wide)),
                  full((2 * wide, NSA_CMP_HIDDEN)), full((NSA_CMP_HIDDEN, HEAD_DIM)),
                  full((2 * wide, NSA_CMP_HIDDEN)), full((NSA_CMP_HIDDEN, HEAD_DIM)),
                  full((N_CMP_PAD, LANES)), full((N_CMP_PAD, LANES)), full((N_CMP_PAD, LANES))],
        out_specs=(ospec, ospec),
        compiler_params=_cparams(("arbitrary",)),
        name="compress",
    )(tk2, tv2, pek, pev, w1k, w2k, w1v, w2v, *tabs)


def _qk(q, k):
    return lax.dot_general(q, k, (((1,), (1,)), ((), ())), preferred_element_type=F32)


def _lane_fold(x, op):
    out = x[:, 0:LANES]
    for u in range(1, x.shape[1] // LANES):
        out = op(out, x[:, u * LANES:(u + 1) * LANES])
    return out


def _split3(x):
    hi = x.astype(BF16)
    r1 = x - hi.astype(F32)
    mid = r1.astype(BF16)
    lo = (r1 - mid.astype(F32)).astype(BF16)
    return hi, mid, lo


def _attend(q, k_ref, v_ref, bias_fn, j_lo, n_tiles, s_scr, *, tk, groups=1):
    rows = q.shape[0]
    dv = v_ref.shape[-1]

    def pass1(t, mx):
        off = pl.multiple_of((j_lo + t) * tk, tk)
        s = _qk(q, k_ref[pl.ds(off, tk), :])
        bias = bias_fn(j_lo + t)
        if groups > 1:
            s = (s.reshape(groups, rows // groups, tk) + bias[None]).reshape(rows, tk)
        else:
            s = s + bias
        s_scr[t] = s
        return jnp.maximum(mx, _lane_fold(s, jnp.maximum))

    mx = lax.fori_loop(0, n_tiles, pass1, jnp.full((rows, LANES), NEG_INF, F32))
    m = jnp.broadcast_to(jnp.max(mx, axis=-1, keepdims=True), (rows, LANES))

    def pass2(t, carry):
        lsum, acc = carry
        off = pl.multiple_of((j_lo + t) * tk, tk)
        s = s_scr[t]
        parts = [jnp.exp2(s[:, u * LANES:(u + 1) * LANES] - m) for u in range(tk // LANES)]
        for p in parts:
            lsum = lsum + p
        pb = jnp.concatenate([p.astype(BF16) for p in parts], axis=1)
        acc = acc + jnp.dot(pb, v_ref[pl.ds(off, tk), :], preferred_element_type=F32)
        return lsum, acc

    lsum, acc = lax.fori_loop(0, n_tiles, pass2, (jnp.zeros((rows, LANES), F32), jnp.zeros((rows, dv), F32)))
    return acc / jnp.sum(lsum, axis=-1, keepdims=True)


def _flash_kernel(q_ref, k_ref, v_ref, bias_ref, o_ref, s_scr):
    i = pl.program_id(2)
    last = bias_ref.shape[0] - 1
    out = _attend(q_ref[...], k_ref, v_ref, lambda j: bias_ref[jnp.minimum(i - j, last)], 0, i + 1, s_scr,
                  tk=FLASH_T)
    o_ref[...] = out.astype(o_ref.dtype)


def _flash(q_arr, k_arr, v_arr, bias, *, nb, heads, dqk, dv, q_blk0, k_blk0, v_blk0):
    t = FLASH_T
    nq = SEQ // t
    return pl.pallas_call(
        _flash_kernel,
        out_shape=jax.ShapeDtypeStruct((nb * SEQ, heads * dv), BF16),
        grid=(nb, heads, nq),
        in_specs=[pl.BlockSpec((t, dqk), lambda b, h, i: (b * nq + i, q_blk0 + h)),
                  pl.BlockSpec((SEQ, dqk), lambda b, h, i: (b, k_blk0 + h)),
                  pl.BlockSpec((SEQ, dv), lambda b, h, i: (b, v_blk0 + h)),
                  pl.BlockSpec(bias.shape, lambda b, h, i: (0, 0, 0))],
        out_specs=pl.BlockSpec((t, dv), lambda b, h, i: (b * nq + i, h)),
        scratch_shapes=[pltpu.VMEM((nq, t, t), F32)],
        compiler_params=_cparams(("arbitrary", "arbitrary", "arbitrary")),
        name="flash",
    )(q_arr, k_arr, v_arr, bias)


def _nsa_kernel(q0_ref, q1_ref, q2_ref, q3_ref, q4_ref, q5_ref, kc_ref, vc_ref, ks_ref, vs_ref, kw_ref, vw_ref,
                gate_ref, cover_ref, expand_ref, wbias_ref, o_ref, bias_scr, s_scr):
    i = pl.program_id(1)
    tq, tk = NSA_TQ, NSA_T
    per = tq // tk
    nh = NSA_HEADS
    rows = nh * tq
    pos = i * tq + lax.broadcasted_iota(jnp.int32, (tq, 1), 0)
    q = jnp.concatenate([r[...] for r in (q0_ref, q1_ref, q2_ref, q3_ref, q4_ref, q5_ref)], axis=0)

    cidx = lax.broadcasted_iota(jnp.int32, (tq, N_CMP_PAD), 1)
    valid_c = (cidx * NSA_CMP_STRIDE + (NSA_CMP_LEN - 1) <= pos) & (cidx < N_CMP_PAD - 1)
    valid_cf = valid_c.astype(F32)
    cbias = jnp.where(valid_c, 0.0, NEG_INF).astype(F32)
    s = (_qk(q, kc_ref[0]).reshape(nh, tq, N_CMP_PAD) + cbias[None]).reshape(rows, N_CMP_PAD)
    p = (jnp.exp2(s - jnp.max(s, axis=-1, keepdims=True)).reshape(nh, tq, N_CMP_PAD) * valid_cf[None])
    p = p.reshape(rows, N_CMP_PAD)
    l = jnp.sum(p, axis=-1, keepdims=True)
    p = p / jnp.where(l > 0.0, l, 1.0)
    o_cmp = jnp.dot(p.astype(BF16), vc_ref[0], preferred_element_type=F32)
    psum = jnp.sum(p.reshape(nh, tq, N_CMP_PAD), axis=0)

    cover_t = cover_ref[...]
    imp = sum(jnp.dot(cover_t, part, preferred_element_type=F32) for part in _split3(psum.T))

    pos_t = i * tq + lax.broadcasted_iota(jnp.int32, (N_SEL_BLOCKS, tq), 1)
    jidx = lax.broadcasted_iota(jnp.int32, (N_SEL_BLOCKS, tq), 0)
    qblk = pos_t // NSA_SEL_BLOCK
    valid_s = jidx * NSA_SEL_BLOCK <= pos_t
    forced = (jidx == 0) | (jidx == qblk) | (jidx == qblk - 1)
    score = jnp.where(valid_s, jnp.where(forced, NSA_FORCE_SCORE, imp), -1.0)
    rank = jnp.zeros((N_SEL_BLOCKS, tq), jnp.int32)
    for ii in range(N_SEL_BLOCKS):
        si = score[ii:ii + 1, :]
        rank = rank + ((si > score) | ((si == score) & (ii < jidx))).astype(jnp.int32)
    sel_t = (rank < NSA_TOP_N).astype(F32)
    sel = jnp.concatenate([sel_t, jnp.zeros((LANES - N_SEL_BLOCKS, tq), F32)], axis=0).T.astype(BF16)

    n_causal = per * (i + 1)

    def fill(j, carry):
        hit = jnp.dot(sel, expand_ref[j], preferred_element_type=F32)
        kpos = j * tk + lax.broadcasted_iota(jnp.int32, (tq, tk), 1)
        bias_scr[j] = jnp.where((hit > 0.5) & (kpos <= pos), 0.0, NEG_INF).astype(F32)
        return carry

    lax.fori_loop(0, n_causal, fill, 0)

    o_slc = _attend(q, ks_ref, vs_ref, lambda j: bias_scr[j], 0, n_causal, s_scr, tk=tk, groups=nh)
    far = wbias_ref.shape[0] - per
    w_lo = jnp.maximum(per * i - far, 0)
    o_win = _attend(q, kw_ref, vw_ref, lambda j: wbias_ref[per * i - j + per - 1], w_lo, n_causal - w_lo, s_scr,
                    tk=tk, groups=nh)

    gates = jax.nn.sigmoid(gate_ref[...].astype(F32))
    for h in range(nh):
        sl = slice(h * tq, (h + 1) * tq)
        g0 = NSA_BRANCHES * h
        out = (gates[:, g0:g0 + 1] * o_cmp[sl] + gates[:, g0 + 1:g0 + 2] * o_slc[sl]
               + gates[:, g0 + 2:g0 + 3] * o_win[sl])
        o_ref[:, h * LANES:(h + 1) * LANES] = out.astype(o_ref.dtype)


def _nsa(hproj, kc, vc, cover, expand, wbias, nb):
    t = NSA_TQ
    nq = SEQ // t
    nk = SEQ // NSA_T
    wide = NSA_HEADS * HEAD_DIM
    q_spec = lambda h: pl.BlockSpec((t, LANES), lambda b, i: (b * nq + i, BLK_NQ + h))
    kv_spec = lambda blk: pl.BlockSpec((SEQ, LANES), lambda b, i: (b, blk))
    cspec = pl.BlockSpec((1, N_CMP_PAD, HEAD_DIM), lambda b, i: (b, 0, 0))
    full = lambda shape: pl.BlockSpec(shape, lambda b, i: (0,) * len(shape))
    return pl.pallas_call(
        _nsa_kernel,
        out_shape=jax.ShapeDtypeStruct((nb * SEQ, wide), BF16),
        grid=(nb, nq),
        in_specs=[q_spec(h) for h in range(NSA_HEADS)]
        + [cspec, cspec, kv_spec(BLK_NKS), kv_spec(BLK_NVS), kv_spec(BLK_NKW), kv_spec(BLK_NVW),
           pl.BlockSpec((t, LANES), lambda b, i: (b * nq + i, BLK_GATE)),
           full(cover.shape), full(expand.shape), full(wbias.shape)],
        out_specs=pl.BlockSpec((t, wide), lambda b, i: (b * nq + i, 0)),
        scratch_shapes=[pltpu.VMEM((nk, t, NSA_T), F32), pltpu.VMEM((nk, NSA_HEADS * t, NSA_T), F32)],
        compiler_params=_cparams(("arbitrary", "arbitrary")),
        name="nsa",
    )(*([hproj] * NSA_HEADS), kc, vc, hproj, hproj, hproj, hproj, hproj, cover, expand, wbias)


def _layer_norm(y, g, b):
    mu = jnp.mean(y, axis=-1, keepdims=True)
    var = jnp.mean(jnp.square(y - mu), axis=-1, keepdims=True)
    return (y - mu) * lax.rsqrt(var + LN_EPS) * g + b


def _lane_min(x):
    return jnp.min(x, axis=-1, keepdims=True)


def _lane_max(x):
    return jnp.max(x, axis=-1, keepdims=True)


def _route(logits):
    lane = lax.broadcasted_iota(jnp.int32, logits.shape, 1)
    lane_f = lane.astype(F32)
    far = float(LANES)
    is_grp = lane < N_GROUPS
    lg = jnp.where(is_grp, logits, NEG_INF)
    eg = jnp.where(is_grp, jnp.exp(lg - _lane_max(lg)), 0.0)
    prob = eg / jnp.sum(eg, axis=-1, keepdims=True)
    p_g = _lane_max(prob)
    g_idx = _lane_min(jnp.where(is_grp & (prob == p_g), lane_f, far))
    e_lo = N_GROUPS + EXPERTS_PER_GROUP * g_idx
    in_grp = (lane_f >= e_lo) & (lane_f < e_lo + EXPERTS_PER_GROUP)
    le = jnp.where(in_grp, logits, NEG_INF)
    v1 = _lane_max(le)
    i1 = _lane_min(jnp.where(in_grp & (le == v1), lane_f, far))
    rest = in_grp & (lane_f != i1)
    le2 = jnp.where(rest, logits, NEG_INF)
    v2 = _lane_max(le2)
    i2 = _lane_min(jnp.where(rest & (le2 == v2), lane_f, far))
    e21 = jnp.exp(v2 - v1)
    den = 1.0 + e21
    gate1 = p_g * (1.0 / den)
    gate2 = p_g * (e21 / den)
    out = jnp.where(lane == 0, i1 - N_GROUPS, 0.0)
    out = jnp.where(lane == 1, i2 - N_GROUPS, out)
    out = jnp.where(lane == 2, gate1, out)
    out = jnp.where(lane == 3, gate2, out)
    return out


def _outproj_kernel(om_ref, od_ref, on_ref, w_ref, x_ref, g_ref, b_ref, wr_ref, x1_ref, route_ref):
    n_mla = MLA_HEADS * HEAD_DIM
    n_dil = DIL_HEADS * HEAD_DIM
    for hh in range(OUTPROJ_TM // ROW_TM):
        rows = slice(hh * ROW_TM, (hh + 1) * ROW_TM)
        mix = jnp.dot(om_ref[rows, :], w_ref[0, 0:n_mla, :], preferred_element_type=F32)
        mix = mix + jnp.dot(od_ref[rows, :], w_ref[0, n_mla:n_mla + n_dil, :], preferred_element_type=F32)
        mix = mix + jnp.dot(on_ref[rows, :], w_ref[0, n_mla + n_dil:, :], preferred_element_type=F32)
        x1 = _layer_norm(DN_ALPHA * x_ref[rows, :] + mix, g_ref[...], b_ref[...])
        x1_ref[rows, :] = x1
        xh = x1.astype(BF16)
        xm = (x1 - xh.astype(F32)).astype(BF16)
        both = jnp.dot(xh, wr_ref[...], preferred_element_type=F32)
        logits = both[:, 0:LANES] + (both[:, LANES:2 * LANES]
                                     + jnp.dot(xm, wr_ref[:, 0:LANES], preferred_element_type=F32))
        route_ref[rows, :] = _route(logits)


def _outproj(o_mla, o_dil, o_nsa, w_out_s, layer, x, g, b, wr3):
    t = x.shape[0]
    tm = OUTPROJ_TM
    full = lambda shape: pl.BlockSpec(shape, lambda i: (0,) * len(shape))
    row = lambda w: pl.BlockSpec((tm, w), lambda i: (i, 0))
    return pl.pallas_call(
        _outproj_kernel,
        out_shape=(jax.ShapeDtypeStruct((t, D_MODEL), F32), jax.ShapeDtypeStruct((t, LANES), F32)),
        grid=(t // tm,),
        in_specs=[row(o_mla.shape[1]), row(o_dil.shape[1]), row(o_nsa.shape[1]),
                  pl.BlockSpec((1,) + w_out_s.shape[1:], lambda i: (layer, 0, 0)),
                  row(D_MODEL), full((1, D_MODEL)), full((1, D_MODEL)), full(wr3.shape)],
        out_specs=(row(D_MODEL), row(LANES)),
        compiler_params=_cparams(("arbitrary",)),
        name="outproj",
    )(o_mla, o_dil, o_nsa, w_out_s, x, g, b, wr3)


def _dispatch_kernel(fill_ref, pos_ref, x1_ref, xb_hbm, stage, zbuf, sem, zsem):
    i = pl.program_id(0)
    n = pl.num_programs(0)
    tm = ROW_TM
    slot = i % 2

    def drain(s):
        for _ in range(MOE_TOP_K):
            pltpu.make_async_copy(stage.at[s], xb_hbm.at[pl.ds(0, tm)], sem.at[s]).wait()

    def zero_copy(e):
        start = pl.multiple_of(fill_ref[e], MOE_ROWS)
        return pltpu.make_async_copy(zbuf, xb_hbm.at[pl.ds(start, MOE_ROWS)], zsem)

    @pl.when(i == 0)
    def _():
        zbuf[...] = jnp.zeros_like(zbuf)
        for e in range(N_EXPERTS):
            @pl.when(fill_ref[e] >= 0)
            def _():
                zero_copy(e).start()
        for e in range(N_EXPERTS):
            @pl.when(fill_ref[e] >= 0)
            def _():
                zero_copy(e).wait()

        def tail_copy(b):
            start = pl.multiple_of(b * MOE_ROWS, MOE_ROWS)
            return pltpu.make_async_copy(zbuf, xb_hbm.at[pl.ds(start, MOE_ROWS)], zsem)

        n_blocks = xb_hbm.shape[0] // MOE_ROWS
        lax.fori_loop(fill_ref[N_EXPERTS], n_blocks, lambda b, c: (tail_copy(b).start(), c)[1], 0)
        lax.fori_loop(fill_ref[N_EXPERTS], n_blocks, lambda b, c: (tail_copy(b).wait(), c)[1], 0)

    @pl.when(i >= 2)
    def _():
        drain(slot)

    stage[slot] = x1_ref[...]

    for r in range(tm):
        for k in range(MOE_TOP_K):
            p = pos_ref[0, 0, MOE_TOP_K * r + k]
            pltpu.make_async_copy(stage.at[slot, pl.ds(r, 1)], xb_hbm.at[pl.ds(p, 1)],
                                  sem.at[slot]).start(priority=k % 2)

    @pl.when(i == n - 1)
    def _():
        drain(1 - slot)
        drain(slot)


def _dispatch(fill, pos3, x1, n_rows):
    t = x1.shape[0]
    tm = ROW_TM
    grid_spec = pltpu.PrefetchScalarGridSpec(
        num_scalar_prefetch=1,
        grid=(t // tm,),
        in_specs=[pl.BlockSpec((1, 1, MOE_TOP_K * tm), lambda i, fl: (i, 0, 0), memory_space=pltpu.SMEM),
                  pl.BlockSpec((tm, D_MODEL), lambda i, fl: (i, 0))],
        out_specs=pl.BlockSpec(memory_space=pl.ANY),
        scratch_shapes=[pltpu.VMEM((2, tm, D_MODEL), F32), pltpu.VMEM((MOE_ROWS, D_MODEL), F32),
                        pltpu.SemaphoreType.DMA((2,)), pltpu.SemaphoreType.DMA(())],
    )
    return pl.pallas_call(
        _dispatch_kernel,
        out_shape=jax.ShapeDtypeStruct((n_rows, D_MODEL), F32),
        grid_spec=grid_spec,
        compiler_params=_cparams(("arbitrary",)),
        name="dispatch",
    )(fill, pos3, x1)


def _experts_kernel(blk_ex_ref, first_ref, nxt_ref, nused_ref, x_ref, wg_hbm, wu_hbm, wd_hbm, y_ref,
                    wg_f, wu_f, wd_f, wg_b, wu_b, wd_b, sem, *, layer):
    b = pl.program_id(0)
    nused = nused_ref[0]

    def weight_copies(e, s):
        return (pltpu.make_async_copy(wg_hbm.at[layer, e], wg_f.at[s], sem.at[s]),
                pltpu.make_async_copy(wu_hbm.at[layer, e], wu_f.at[s], sem.at[s]),
                pltpu.make_async_copy(wd_hbm.at[layer, e], wd_f.at[s], sem.at[s]))

    @pl.when(b == 0)
    def _():
        for cp in weight_copies(blk_ex_ref[0], 0):
            cp.start()

    @pl.when(b < nused)
    def _():
        first = first_ref[b]

        @pl.when(first > 0)
        def _():
            s = first - 1
            for cp in weight_copies(blk_ex_ref[b], s):
                cp.wait()

            @pl.when(nxt_ref[b] >= 0)
            def _():
                for cp in weight_copies(nxt_ref[b], 1 - s):
                    cp.start()

            wg_b[...] = wg_f[s].astype(BF16)
            wu_b[...] = wu_f[s].astype(BF16)
            wd_b[...] = wd_f[s].astype(BF16)

        rows = x_ref[...].astype(BF16)
        gate = jnp.dot(rows, wg_b[...], preferred_element_type=F32)
        up = jnp.dot(rows, wu_b[...], preferred_element_type=F32)
        hid = (gate * jax.nn.sigmoid(gate) * up).astype(BF16)
        y_ref[...] = jnp.dot(hid, wd_b[...], preferred_element_type=F32)

    @pl.when(b >= nused)
    def _():
        y_ref[...] = jnp.zeros_like(y_ref)


def _experts(blk_ex, first, nxt, nused, xb_rows, w_gate, w_up, w_down, layer):
    n_blocks = xb_rows.shape[0] // MOE_ROWS
    hbm = pl.BlockSpec(memory_space=pl.ANY)
    grid_spec = pltpu.PrefetchScalarGridSpec(
        num_scalar_prefetch=4,
        grid=(n_blocks,),
        in_specs=[pl.BlockSpec((MOE_ROWS, D_MODEL), lambda b, ex, fi, nx, nu: (jnp.minimum(b, nu[0] - 1), 0)),
                  hbm, hbm, hbm],
        out_specs=pl.BlockSpec((MOE_ROWS, D_MODEL), lambda b, ex, fi, nx, nu: (b, 0)),
        scratch_shapes=[pltpu.VMEM((2, D_MODEL, EXPERT_FF), F32), pltpu.VMEM((2, D_MODEL, EXPERT_FF), F32),
                        pltpu.VMEM((2, EXPERT_FF, D_MODEL), F32),
                        pltpu.VMEM((D_MODEL, EXPERT_FF), BF16), pltpu.VMEM((D_MODEL, EXPERT_FF), BF16),
                        pltpu.VMEM((EXPERT_FF, D_MODEL), BF16), pltpu.SemaphoreType.DMA((2,))],
    )
    return pl.pallas_call(
        functools.partial(_experts_kernel, layer=layer),
        out_shape=jax.ShapeDtypeStruct((n_blocks * MOE_ROWS, D_MODEL), F32),
        grid_spec=grid_spec,
        compiler_params=_cparams(("arbitrary",)),
        name="experts",
    )(blk_ex, first, nxt, nused, xb_rows, w_gate, w_up, w_down)


def _combine_kernel(pos_cur_ref, pos_nxt_ref, y_hbm, x1_ref, route_ref, g_ref, b_ref, o_ref, ob_ref, ybuf, sem):
    i = pl.program_id(0)
    n = pl.num_programs(0)
    tm = ROW_TM

    def issue(pos_ref, slot):
        for r in range(tm):
            for k in range(MOE_TOP_K):
                p = pos_ref[0, 0, MOE_TOP_K * r + k]
                pltpu.make_async_copy(y_hbm.at[pl.ds(p, 1)], ybuf.at[slot, k, pl.ds(r, 1)], sem.at[slot]).start()

    @pl.when(i == 0)
    def _():
        issue(pos_cur_ref, 0)

    @pl.when(i + 1 < n)
    def _():
        issue(pos_nxt_ref, (i + 1) % 2)

    slot = i % 2
    for k in range(MOE_TOP_K):
        pltpu.make_async_copy(y_hbm.at[pl.ds(0, tm)], ybuf.at[slot, k], sem.at[slot]).wait()
    route = route_ref[...]
    ffn = route[:, 2:3] * ybuf[slot, 0] + route[:, 3:4] * ybuf[slot, 1]
    x2 = _layer_norm(DN_ALPHA * x1_ref[...] + ffn, g_ref[...], b_ref[...])
    o_ref[...] = x2
    ob_ref[...] = x2.astype(ob_ref.dtype)


def _combine(pos3, yb, x1, route, g, b):
    t = x1.shape[0]
    tm = ROW_TM
    nt = t // tm
    pos_spec = lambda shift: pl.BlockSpec((1, 1, MOE_TOP_K * tm), lambda i: (jnp.minimum(i + shift, nt - 1), 0, 0),
                                          memory_space=pltpu.SMEM)
    row = lambda w: pl.BlockSpec((tm, w), lambda i: (i, 0))
    full = lambda shape: pl.BlockSpec(shape, lambda i: (0,) * len(shape))
    return pl.pallas_call(
        _combine_kernel,
        out_shape=(jax.ShapeDtypeStruct((t, D_MODEL), F32), jax.ShapeDtypeStruct((t, D_MODEL), BF16)),
        grid=(nt,),
        in_specs=[pos_spec(0), pos_spec(1), pl.BlockSpec(memory_space=pl.ANY), row(D_MODEL), row(LANES),
                  full((1, D_MODEL)), full((1, D_MODEL))],
        out_specs=(row(D_MODEL), row(D_MODEL)),
        scratch_shapes=[pltpu.VMEM((2, MOE_TOP_K, tm, D_MODEL), F32), pltpu.SemaphoreType.DMA((2,))],
        compiler_params=_cparams(("arbitrary",)),
        name="combine",
    )(pos3, pos3, yb, x1, route, g, b)


def _dispatch_plan(route, t):
    m = t * MOE_TOP_K
    n_blocks = -(-m // MOE_ROWS) + N_EXPERTS
    chunk = MOE_ROWS
    ex = route[:, 0:MOE_TOP_K].reshape(m // chunk, chunk, 1)
    onehot = (ex == jnp.arange(N_EXPERTS, dtype=F32)[None, None, :]).astype(F32)
    tri = jnp.asarray(np.tril(np.ones((chunk, chunk), np.float32)))
    within = jnp.einsum('ij,cje->cie', tri, onehot)
    totals = jnp.sum(onehot, axis=1)
    before = jnp.cumsum(totals, axis=0) - totals
    counts = jnp.sum(totals, axis=0)
    padded = jnp.ceil(counts / MOE_ROWS) * MOE_ROWS
    pad_end = jnp.cumsum(padded)
    pad_start = pad_end - padded
    row = jnp.sum(onehot * (within - 1.0 + before[:, None, :] + pad_start[None, None, :]), axis=-1)
    pos = row.astype(jnp.int32).reshape(m)
    blk_start = jnp.arange(n_blocks, dtype=F32) * MOE_ROWS
    blk_ex = jnp.minimum(jnp.sum((pad_end[None, :] <= blk_start[:, None]).astype(jnp.int32), axis=1),
                         N_EXPERTS - 1).astype(jnp.int32)
    nused = (pad_end[-1] / MOE_ROWS).astype(jnp.int32).reshape(1)
    fill = jnp.concatenate([jnp.where(padded > 0, pad_end - MOE_ROWS, -1.0).astype(jnp.int32), nused])
    nonempty = padded > 0
    slot_e = (jnp.cumsum(nonempty.astype(jnp.int32)) - 1) % 2
    expert_ids = jnp.arange(N_EXPERTS, dtype=jnp.int32)
    later = lax.cummin(jnp.where(nonempty, expert_ids, N_EXPERTS), axis=0, reverse=True)
    nxt_e = jnp.concatenate([later[1:], jnp.full((1,), N_EXPERTS, jnp.int32)])
    nxt_e = jnp.where(nxt_e >= N_EXPERTS, -1, nxt_e)
    is_first = (blk_start == pad_start[blk_ex]) & (jnp.arange(n_blocks) < nused[0])
    first = jnp.where(is_first, 1 + slot_e[blk_ex], 0).astype(jnp.int32)
    return blk_ex, first, nxt_e[blk_ex].astype(jnp.int32), nused, fill, pos


W_HALF = LANES // 2


def _shift_kernel(a_ref, b_ref, tail_ref, o_ref):
    m = pl.program_id(0)
    n_src = IN_COLS // LANES
    nl = o_ref.shape[0]
    d = o_ref.shape[2]
    nc = d // LANES
    a2, b2, t2 = a_ref, b_ref, tail_ref

    def slab(ref4, col0, c, l):
        return ref4[pl.ds(col0, W_HALF), c, l, :].astype(o_ref.dtype)

    def copy(ref_top, top0, ref_bot, bot0):
        for l in range(nl):
            for c in range(nc):
                o_ref[l, 0:W_HALF, c * LANES:(c + 1) * LANES] = slab(ref_top, top0, c, l)
                if ref_bot is None:
                    o_ref[l, W_HALF:LANES, c * LANES:(c + 1) * LANES] = jnp.zeros((W_HALF, LANES), o_ref.dtype)
                else:
                    o_ref[l, W_HALF:LANES, c * LANES:(c + 1) * LANES] = slab(ref_bot, bot0, c, l)

    @pl.when(m < n_src)
    def _():
        copy(a2, 0, b2, 0)

    @pl.when(m == n_src)
    def _():
        copy(a2, 0, t2, 0)

    @pl.when(m == n_src + 1)
    def _():
        copy(t2, W_HALF, None, 0)


def _shift_w_in(w_in):
    nl, d, _ = w_in.shape
    n_src = IN_COLS // LANES
    n_keep = (IN_SHIFT_FROM + MLA_ROPE_DIM) // LANES
    last = n_src * (LANES // W_HALF) - 1
    nc = d // LANES
    view = lambda w: jnp.transpose(w.reshape(nl, nc, LANES, w.shape[-1]), (3, 1, 0, 2))
    w_t = view(w_in)
    tail_t = view(jnp.pad(w_in[:, :, n_src * LANES:], ((0, 0), (0, 0), (0, (n_src + 1) * LANES - IN_COLS))))
    first = lambda m: jnp.where(m < n_keep, 2 * m, 2 * m - 1)
    return pl.pallas_call(
        _shift_kernel,
        out_shape=jax.ShapeDtypeStruct((nl, IN_COLS_PAD, d), BF16),
        grid=(N_IN_BLOCKS,),
        in_specs=[pl.BlockSpec((W_HALF, nc, nl, LANES), lambda m: (jnp.minimum(first(m), last), 0, 0, 0)),
                  pl.BlockSpec((W_HALF, nc, nl, LANES), lambda m: (jnp.minimum(first(m) + 1, last), 0, 0, 0)),
                  pl.BlockSpec((LANES, nc, nl, LANES), lambda m: (0, 0, 0, 0))],
        out_specs=pl.BlockSpec((nl, LANES, d), lambda m: (0, m, 0)),
        compiler_params=_cparams(("arbitrary",)),
        name="shiftw",
    )(w_t, w_t, tail_t)


def _permute_w_q_up(w_q_up):
    w = w_q_up.reshape(MLA_Q_RANK, MLA_HEADS, MLA_NOPE_DIM + MLA_ROPE_DIM)
    zero = jnp.zeros((MLA_Q_RANK, MLA_HEADS, MLA_QK_PAD - MLA_NOPE_DIM - MLA_ROPE_DIM), w.dtype)
    w = jnp.concatenate([w[:, :, :MLA_NOPE_DIM], zero, w[:, :, MLA_NOPE_DIM:]], axis=-1)
    return w.reshape(MLA_Q_RANK, MLA_HEADS * MLA_QK_PAD).astype(BF16)


def _split_w_kv_up(w_kv_up):
    w = w_kv_up.reshape(MLA_KV_RANK, MLA_HEADS, MLA_NOPE_DIM + HEAD_DIM)
    wkn = w[:, :, :MLA_NOPE_DIM].reshape(MLA_KV_RANK, MLA_HEADS * MLA_NOPE_DIM)
    wv = w[:, :, MLA_NOPE_DIM:].reshape(MLA_KV_RANK, MLA_HEADS * HEAD_DIM)
    return wkn.astype(BF16), wv.astype(BF16)


def _router_slices(w_grp, w_exp):
    w = jnp.concatenate([w_grp, w_exp, jnp.zeros((D_MODEL, LANES - N_GROUPS - N_EXPERTS), F32)], axis=1)
    hi = w.astype(BF16)
    mid = (w - hi.astype(F32)).astype(BF16)
    return jnp.concatenate([hi, mid], axis=1)


def _mixer(xb, nb, layer, w_in_s, q_lat_norm, w_q_up, kv_lat_norm, w_kv_up,
           cmp_pos_k, cmp_w1_k, cmp_w2_k, cmp_pos_v, cmp_w1_v, cmp_w2_v, consts):
    hproj = _inproj(xb, w_in_s, layer, consts["rope32"])
    wkn, wv = _split_w_kv_up(w_kv_up)
    q_mla, k_mla, v_mla = _mlaprep(hproj, q_lat_norm.reshape(1, -1), kv_lat_norm.reshape(1, -1),
                                   _permute_w_q_up(w_q_up), wkn, wv, consts["rope64"])
    wide = NSA_CMP_STRIDE * HEAD_DIM
    tk2 = hproj[:, BLK_NKC * LANES:(BLK_NKC + 1) * LANES].reshape(nb, N_CMP_PAD, wide)
    tv2 = hproj[:, BLK_NVC * LANES:(BLK_NVC + 1) * LANES].reshape(nb, N_CMP_PAD, wide)
    kc, vc = _compress(tk2, tv2, cmp_pos_k.reshape(2, wide), cmp_pos_v.reshape(2, wide),
                       cmp_w1_k.astype(BF16), cmp_w2_k.astype(BF16), cmp_w1_v.astype(BF16), cmp_w2_v.astype(BF16),
                       consts["rope_cmp"])
    o_mla = _flash(q_mla, k_mla, v_mla, consts["bias_causal"], nb=nb, heads=MLA_HEADS, dqk=MLA_QK_PAD,
                   dv=HEAD_DIM, q_blk0=0, k_blk0=0, v_blk0=0)
    o_dil = _flash(hproj, hproj, hproj, consts["bias_dil"], nb=nb, heads=DIL_HEADS, dqk=HEAD_DIM, dv=HEAD_DIM,
                   q_blk0=BLK_DQ, k_blk0=BLK_DK, v_blk0=BLK_DV)
    o_nsa = _nsa(hproj, kc, vc, consts["cover"], consts["expand"], consts["bias_win"], nb)
    return o_mla, o_dil, o_nsa


def _make_consts():
    pos = np.arange(SEQ)
    slot = np.arange(N_CMP_PAD)
    cmp_start = slot * NSA_CMP_STRIDE
    cmp_end = cmp_start + NSA_CMP_LEN - 1
    sel_start = np.arange(N_SEL_BLOCKS) * NSA_SEL_BLOCK
    cover = ((cmp_start[:, None] < sel_start[None, :] + NSA_SEL_BLOCK)
             & (cmp_start[:, None] + NSA_CMP_LEN > sel_start[None, :])
             & (slot[:, None] < N_CMP_PAD - 1)).astype(np.float32)
    expand = (np.arange(LANES)[:, None] == (pos // NSA_SEL_BLOCK)[None, :]).astype(np.float32)
    expand = expand.reshape(LANES, SEQ // NSA_T, NSA_T).transpose(1, 0, 2)
    win_diags = range(-(NSA_TQ // NSA_T - 1), -(-NSA_WINDOW // NSA_T) + 1)
    return {
        "rope32": _rope_tables(pos, ROT_DIM, True),
        "rope64": _rope_tables(pos, MLA_ROPE_DIM, False, KROPE_LANE0),
        "rope_cmp": _rope_tables(cmp_end, ROT_DIM, True),
        "bias_causal": _distance_bias(_causal_mult, (0, 1), FLASH_T),
        "bias_dil": _distance_bias(_dilated_mult, range(SEQ // FLASH_T), FLASH_T),
        "bias_win": _distance_bias(_window_mult, win_diags, NSA_T, NSA_TQ),
        "cover": jnp.asarray(cover.T, BF16),
        "expand": jnp.asarray(expand, BF16),
    }


def kernel(x, w_in, q_lat_norm, w_q_up, kv_lat_norm, w_kv_up, cmp_pos_k, cmp_w1_k, cmp_w2_k, cmp_pos_v, cmp_w1_v,
           cmp_w2_v, w_out, ln1_g, ln1_b, w_grp, w_exp, w_gate, w_up, w_down, ln2_g, ln2_b):
    nb, s, d = x.shape
    assert s == SEQ and d == D_MODEL
    t = nb * s
    consts = _make_consts()
    w_in_s = _shift_w_in(w_in)
    w_out_s = w_out.astype(BF16)
    xf = x.reshape(t, d)
    xb = xf.astype(BF16)
    for l in range(DEPTH):
        o_mla, o_dil, o_nsa = _mixer(xb, nb, l, w_in_s, q_lat_norm[l], w_q_up[l], kv_lat_norm[l], w_kv_up[l],
                                     cmp_pos_k[l], cmp_w1_k[l], cmp_w2_k[l], cmp_pos_v[l], cmp_w1_v[l],
                                     cmp_w2_v[l], consts)
        x1, route = _outproj(o_mla, o_dil, o_nsa, w_out_s, l, xf, ln1_g[l].reshape(1, d),
                             ln1_b[l].reshape(1, d), _router_slices(w_grp[l], w_exp[l]))
        blk_ex, first, nxt, nused, fill, pos = _dispatch_plan(route, t)
        pos3 = pos.reshape(t // ROW_TM, 1, MOE_TOP_K * ROW_TM)
        n_rows = (-(-t * MOE_TOP_K // MOE_ROWS) + N_EXPERTS) * MOE_ROWS
        yb = _experts(blk_ex, first, nxt, nused, _dispatch(fill, pos3, x1, n_rows), w_gate, w_up, w_down, l)
        xf, xb = _combine(pos3, yb, x1, route,
                          ln2_g[l].reshape(1, d), ln2_b[l].reshape(1, d))
    return xf.reshape(nb, s, d)
```

```python
import functools

import numpy as np
import jax
import jax.numpy as jnp
from jax import lax
from jax.experimental import pallas as pl
from jax.experimental.pallas import tpu as pltpu

F32 = jnp.float32
BF16 = jnp.bfloat16

D_MODEL = 2048
SEQ = 2048
DEPTH = 2
HEAD_DIM = 128
LANES = 128
MLA_HEADS = 4
DIL_HEADS = 6
NSA_HEADS = 6
MLA_Q_RANK = 384
MLA_KV_RANK = 128
MLA_NOPE_DIM = 128
MLA_ROPE_DIM = 64
MLA_QK_PAD = 256
DIL_PAIRS = ((128, 1), (512, 4), (2048, 16))
NSA_CMP_LEN = 32
NSA_CMP_STRIDE = 16
NSA_CMP_HIDDEN = 256
NSA_SEL_BLOCK = 64
NSA_TOP_N = 16
NSA_WINDOW = 512
NSA_BRANCHES = 3
NSA_FORCE_SCORE = 1e4
ROPE_THETA = 500000.0
ROT_DIM = HEAD_DIM // 4
N_GROUPS = 4
EXPERTS_PER_GROUP = 8
N_EXPERTS = N_GROUPS * EXPERTS_PER_GROUP
EXPERT_FF = 512
MOE_TOP_K = 2
MOE_ROWS = 256
DN_ALPHA = (2 * DEPTH) ** 0.25
LN_EPS = 1e-5
RMS_EPS = 1e-6
NEG_INF = -1e30
LOG2_E = 1.4426950408889634

N_CMP_PAD = SEQ // NSA_CMP_STRIDE
N_SEL_BLOCKS = SEQ // NSA_SEL_BLOCK

BLK_QLAT = 0
BLK_KVLAT = 3
BLK_KROPE = 4
BLK_DQ = 5
BLK_DK = 11
BLK_DV = 17
BLK_NQ = 23
BLK_NKC = 29
BLK_NVC = 30
BLK_NKS = 31
BLK_NVS = 32
BLK_NKW = 33
BLK_NVW = 34
BLK_GATE = 35
N_IN_BLOCKS = 36
IN_COLS = 4434
IN_SHIFT_FROM = MLA_Q_RANK + MLA_KV_RANK - MLA_ROPE_DIM
IN_COLS_PAD = N_IN_BLOCKS * LANES
KROPE_LANE0 = LANES - MLA_ROPE_DIM

INPROJ_TM = 2048
INPROJ_SUB = 512
INPROJ_TN = 1536
FLASH_T = 1024
NSA_T = 512
NSA_TQ = 512
ROW_TM = 256
MLAPREP_TM = 512
OUTPROJ_TM = 512

VMEM_LIMIT = 56 * 1024 * 1024


def _cparams(sem):
    return pltpu.CompilerParams(dimension_semantics=sem, vmem_limit_bytes=VMEM_LIMIT)


def _rope_tables(pos, rot_dim, keep_rest, lane0=0):
    half = rot_dim // 2
    inv_freq = np.power(np.float32(ROPE_THETA), -np.arange(half, dtype=np.float32) / np.float32(half))
    ang = (pos.astype(np.float32)[:, None] * inv_freq[None, :]).astype(np.float32)
    cos, sin = np.cos(ang).astype(np.float32), np.sin(ang).astype(np.float32)
    n = pos.shape[0]
    c = np.full((n, LANES), 1.0 if keep_rest else 0.0, np.float32)
    s1 = np.zeros((n, LANES), np.float32)
    s2 = np.zeros((n, LANES), np.float32)
    c[:, lane0:lane0 + half] = cos
    c[:, lane0 + half:lane0 + rot_dim] = cos
    s1[:, lane0 + half:lane0 + rot_dim] = sin
    s2[:, lane0:lane0 + half] = -sin
    return jnp.asarray(c), jnp.asarray(s1), jnp.asarray(s2)


def _rope_lanes(a, c, s1, s2, half):
    return a * c + pltpu.roll(a, half, 1) * s1 + pltpu.roll(a, LANES - half, 1) * s2


def _distance_bias(mult_fn, diags, t, rows=None):
    d = np.asarray(diags, np.int64)[:, None, None] * t
    r = np.arange(t if rows is None else rows, dtype=np.int64)[None, :, None]
    c = np.arange(t, dtype=np.int64)[None, None, :]
    mult = mult_fn(d + r - c)
    return jnp.asarray(np.where(mult > 0, np.log2(np.maximum(mult, 1).astype(np.float32)), NEG_INF).astype(np.float32))


def _causal_mult(dist):
    return (dist >= 0).astype(np.int64)


def _dilated_mult(dist):
    m = np.zeros_like(dist)
    for window, dil in DIL_PAIRS:
        m = m + ((dist >= 0) & (dist % dil == 0) & (dist <= (window // dil) * dil)).astype(np.int64)
    return m


def _window_mult(dist):
    return ((dist >= 0) & (dist <= NSA_WINDOW - 1)).astype(np.int64)


def _in_block(blk, lo, n):
    return jnp.logical_and(blk >= lo, blk < lo + n)


def _inproj_kernel(x_ref, w_ref, c_ref, s1_ref, s2_ref, o_ref, *, scale):
    j = pl.program_id(0)
    for hh in range(INPROJ_TM // INPROJ_SUB):
        rows = slice(hh * INPROJ_SUB, (hh + 1) * INPROJ_SUB)
        acc = _qk(x_ref[rows, :], w_ref[0])
        c, s1, s2 = c_ref[rows, :], s1_ref[rows, :], s2_ref[rows, :]
        for u in range(INPROJ_TN // LANES):
            blk = j * (INPROJ_TN // LANES) + u
            is_q = jnp.logical_or(_in_block(blk, BLK_DQ, DIL_HEADS), _in_block(blk, BLK_NQ, NSA_HEADS))
            is_rope = is_q | _in_block(blk, BLK_DK, DIL_HEADS) | (blk == BLK_NKS) | (blk == BLK_NKW)
            sc = jnp.where(is_q, scale, 1.0).astype(F32)
            a = acc[:, u * LANES:(u + 1) * LANES]
            r = _rope_lanes(a, jnp.where(is_rope, c, 1.0), jnp.where(is_rope, s1, 0.0),
                            jnp.where(is_rope, s2, 0.0), ROT_DIM // 2)
            o_ref[rows, u * LANES:(u + 1) * LANES] = (r * sc).astype(o_ref.dtype)


def _inproj(xb, w_in_s, layer, tabs):
    t = xb.shape[0]
    tm, tn = INPROJ_TM, INPROJ_TN
    nrow = SEQ // tm
    tab_spec = pl.BlockSpec((tm, LANES), lambda j, i: (i % nrow, 0))
    return pl.pallas_call(
        functools.partial(_inproj_kernel, scale=HEAD_DIM ** -0.5 * LOG2_E),
        out_shape=jax.ShapeDtypeStruct((t, IN_COLS_PAD), BF16),
        grid=(IN_COLS_PAD // tn, t // tm),
        in_specs=[pl.BlockSpec((tm, D_MODEL), lambda j, i: (i, 0)),
                  pl.BlockSpec((1, tn, D_MODEL), lambda j, i: (layer, j, 0)),
                  tab_spec, tab_spec, tab_spec],
        out_specs=pl.BlockSpec((tm, tn), lambda j, i: (i, j)),
        compiler_params=_cparams(("arbitrary", "arbitrary")),
        name="inproj",
    )(xb, w_in_s, *tabs)


def _rms(xf, g):
    return xf * lax.rsqrt(jnp.mean(jnp.square(xf), axis=-1, keepdims=True) + RMS_EPS) * g


def _mlaprep_kernel(ql_ref, kvl_ref, kr_ref, gq_ref, gkv_ref, wq_ref, wkn_ref, wv_ref,
                    c_ref, s1_ref, s2_ref, q_ref, k_ref, v_ref, *, scale):
    c, s1, s2 = c_ref[...], s1_ref[...], s2_ref[...]
    half = MLA_ROPE_DIM // 2
    qn = _rms(ql_ref[...].astype(F32), gq_ref[...]).astype(BF16)
    q = jnp.dot(qn, wq_ref[...], preferred_element_type=F32)
    for h in range(MLA_HEADS):
        lo = h * MLA_QK_PAD
        q_ref[:, lo:lo + LANES] = (q[:, lo:lo + LANES] * scale).astype(q_ref.dtype)
        a = q[:, lo + LANES:lo + 2 * LANES]
        q_ref[:, lo + LANES:lo + 2 * LANES] = (_rope_lanes(a, c, s1, s2, half) * scale).astype(q_ref.dtype)
    kvn = _rms(kvl_ref[...].astype(F32), gkv_ref[...]).astype(BF16)
    kn = jnp.dot(kvn, wkn_ref[...], preferred_element_type=F32)
    v_ref[...] = jnp.dot(kvn, wv_ref[...], preferred_element_type=F32).astype(v_ref.dtype)
    kpe = _rope_lanes(kr_ref[...].astype(F32), c, s1, s2, half).astype(k_ref.dtype)
    for h in range(MLA_HEADS):
        lo = h * MLA_QK_PAD
        k_ref[:, lo:lo + LANES] = kn[:, h * LANES:(h + 1) * LANES].astype(k_ref.dtype)
        k_ref[:, lo + LANES:lo + 2 * LANES] = kpe


def _mlaprep(hproj, gq, gkv, wq_p, wkn, wv, tabs):
    t = hproj.shape[0]
    tm = MLAPREP_TM
    nrow = SEQ // tm
    tab_spec = pl.BlockSpec((tm, LANES), lambda i: (i % nrow, 0))
    full = lambda shape: pl.BlockSpec(shape, lambda i: (0,) * len(shape))
    return pl.pallas_call(
        functools.partial(_mlaprep_kernel, scale=(MLA_NOPE_DIM + MLA_ROPE_DIM) ** -0.5 * LOG2_E),
        out_shape=(jax.ShapeDtypeStruct((t, MLA_HEADS * MLA_QK_PAD), BF16),
                   jax.ShapeDtypeStruct((t, MLA_HEADS * MLA_QK_PAD), BF16),
                   jax.ShapeDtypeStruct((t, MLA_HEADS * HEAD_DIM), BF16)),
        grid=(t // tm,),
        in_specs=[pl.BlockSpec((tm, MLA_Q_RANK), lambda i: (i, BLK_QLAT * LANES // MLA_Q_RANK)),
                  pl.BlockSpec((tm, LANES), lambda i: (i, BLK_KVLAT)),
                  pl.BlockSpec((tm, LANES), lambda i: (i, BLK_KROPE)),
                  full((1, MLA_Q_RANK)), full((1, MLA_KV_RANK)),
                  full((MLA_Q_RANK, MLA_HEADS * MLA_QK_PAD)),
                  full((MLA_KV_RANK, MLA_HEADS * MLA_NOPE_DIM)),
                  full((MLA_KV_RANK, MLA_HEADS * HEAD_DIM)),
                  tab_spec, tab_spec, tab_spec],
        out_specs=(pl.BlockSpec((tm, MLA_HEADS * MLA_QK_PAD), lambda i: (i, 0)),
                   pl.BlockSpec((tm, MLA_HEADS * MLA_QK_PAD), lambda i: (i, 0)),
                   pl.BlockSpec((tm, MLA_HEADS * HEAD_DIM), lambda i: (i, 0))),
        compiler_params=_cparams(("arbitrary",)),
        name="mlaprep",
    )(hproj, hproj, hproj, gq, gkv, wq_p, wkn, wv, *tabs)


def _gelu_tanh(x):
    return 0.5 * x * (1.0 + jnp.tanh(0.7978845608028654 * (x + 0.044715 * (x * x * x))))


def _compress_kernel(tk_ref, tv_ref, pek_ref, pev_ref, w1k_ref, w2k_ref, w1v_ref, w2v_ref,
                     c_ref, s1_ref, s2_ref, kc_ref, vc_ref):
    half_in = NSA_CMP_STRIDE * HEAD_DIM

    def comp(t_ref, pe_ref, w1_ref, w2_ref):
        t = t_ref[0].astype(F32)
        a = (t + pe_ref[0:1, :]).astype(BF16)
        b = (t + pe_ref[1:2, :]).astype(BF16)
        y0 = jnp.dot(a, w1_ref[0:half_in, :], preferred_element_type=F32)
        y1 = jnp.dot(b, w1_ref[half_in:2 * half_in, :], preferred_element_type=F32)
        hid = _gelu_tanh(y0 + pltpu.roll(y1, N_CMP_PAD - 1, 0))
        return jnp.dot(hid.astype(BF16), w2_ref[...], preferred_element_type=F32)

    kc = comp(tk_ref, pek_ref, w1k_ref, w2k_ref)
    kc_ref[0] = _rope_lanes(kc, c_ref[...], s1_ref[...], s2_ref[...], ROT_DIM // 2).astype(kc_ref.dtype)
    vc_ref[0] = comp(tv_ref, pev_ref, w1v_ref, w2v_ref).astype(vc_ref.dtype)


def _compress(tk2, tv2, pek, pev, w1k, w2k, w1v, w2v, tabs):
    nb = tk2.shape[0]
    wide = NSA_CMP_STRIDE * HEAD_DIM
    full = lambda shape: pl.BlockSpec(shape, lambda b: (0,) * len(shape))
    bspec = pl.BlockSpec((1, N_CMP_PAD, wide), lambda b: (b, 0, 0))
    ospec = pl.BlockSpec((1, N_CMP_PAD, HEAD_DIM), lambda b: (b, 0, 0))
    return pl.pallas_call(
        _compress_kernel,
        out_shape=(jax.ShapeDtypeStruct((nb, N_CMP_PAD, HEAD_DIM), BF16),) * 2,
        grid=(nb,),
        in_specs=[bspec, bspec, full((2, wide)), full((2, wide)),
                  full((2 * wide, NSA_CMP_HIDDEN)), full((NSA_CMP_HIDDEN, HEAD_DIM)),
                  full((2 * wide, NSA_CMP_HIDDEN)), full((NSA_CMP_HIDDEN, HEAD_DIM)),
                  full((N_CMP_PAD, LANES)), full((N_CMP_PAD, LANES)), full((N_CMP_PAD, LANES))],
        out_specs=(ospec, ospec),
        compiler_params=_cparams(("arbitrary",)),
        name="compress",
    )(tk2, tv2, pek, pev, w1k, w2k, w1v, w2v, *tabs)


def _qk(q, k):
    return lax.dot_general(q, k, (((1,), (1,)), ((), ())), preferred_element_type=F32)


def _lane_fold(x, op):
    out = x[:, 0:LANES]
    for u in range(1, x.shape[1] // LANES):
        out = op(out, x[:, u * LANES:(u + 1) * LANES])
    return out


def _split3(x):
    hi = x.astype(BF16)
    r1 = x - hi.astype(F32)
    mid = r1.astype(BF16)
    lo = (r1 - mid.astype(F32)).astype(BF16)
    return hi, mid, lo


def _attend(q, k_ref, v_ref, bias_fn, j_lo, n_tiles, s_scr, *, tk, groups=1):
    rows = q.shape[0]
    dv = v_ref.shape[-1]

    def pass1(t, mx):
        off = pl.multiple_of((j_lo + t) * tk, tk)
        s = _qk(q, k_ref[pl.ds(off, tk), :])
        bias = bias_fn(j_lo + t)
        if groups > 1:
            s = (s.reshape(groups, rows // groups, tk) + bias[None]).reshape(rows, tk)
        else:
            s = s + bias
        s_scr[t] = s
        return jnp.maximum(mx, _lane_fold(s, jnp.maximum))

    mx = lax.fori_loop(0, n_tiles, pass1, jnp.full((rows, LANES), NEG_INF, F32))
    m = jnp.broadcast_to(jnp.max(mx, axis=-1, keepdims=True), (rows, LANES))

    def pass2(t, carry):
        lsum, acc = carry
        off = pl.multiple_of((j_lo + t) * tk, tk)
        s = s_scr[t]
        parts = [jnp.exp2(s[:, u * LANES:(u + 1) * LANES] - m) for u in range(tk // LANES)]
        for p in parts:
            lsum = lsum + p
        pb = jnp.concatenate([p.astype(BF16) for p in parts], axis=1)
        acc = acc + jnp.dot(pb, v_ref[pl.ds(off, tk), :], preferred_element_type=F32)
        return lsum, acc

    lsum, acc = lax.fori_loop(0, n_tiles, pass2, (jnp.zeros((rows, LANES), F32), jnp.zeros((rows, dv), F32)))
    return acc / jnp.sum(lsum, axis=-1, keepdims=True)


def _flash_kernel(q_ref, k_ref, v_ref, bias_ref, o_ref, s_scr):
    i = pl.program_id(2)
    last = bias_ref.shape[0] - 1
    out = _attend(q_ref[...], k_ref, v_ref, lambda j: bias_ref[jnp.minimum(i - j, last)], 0, i + 1, s_scr,
                  tk=FLASH_T)
    o_ref[...] = out.astype(o_ref.dtype)


def _flash(q_arr, k_arr, v_arr, bias, *, nb, heads, dqk, dv, q_blk0, k_blk0, v_blk0):
    t = FLASH_T
    nq = SEQ // t
    return pl.pallas_call(
        _flash_kernel,
        out_shape=jax.ShapeDtypeStruct((nb * SEQ, heads * dv), BF16),
        grid=(nb, heads, nq),
        in_specs=[pl.BlockSpec((t, dqk), lambda b, h, i: (b * nq + i, q_blk0 + h)),
                  pl.BlockSpec((SEQ, dqk), lambda b, h, i: (b, k_blk0 + h)),
                  pl.BlockSpec((SEQ, dv), lambda b, h, i: (b, v_blk0 + h)),
                  pl.BlockSpec(bias.shape, lambda b, h, i: (0, 0, 0))],
        out_specs=pl.BlockSpec((t, dv), lambda b, h, i: (b * nq + i, h)),
        scratch_shapes=[pltpu.VMEM((nq, t, t), F32)],
        compiler_params=_cparams(("arbitrary", "arbitrary", "arbitrary")),
        name="flash",
    )(q_arr, k_arr, v_arr, bias)


def _nsa_kernel(q0_ref, q1_ref, q2_ref, q3_ref, q4_ref, q5_ref, kc_ref, vc_ref, ks_ref, vs_ref, kw_ref, vw_ref,
                gate_ref, cover_ref, expand_ref, wbias_ref, o_ref, bias_scr, s_scr):
    i = pl.program_id(1)
    tq, tk = NSA_TQ, NSA_T
    per = tq // tk
    nh = NSA_HEADS
    rows = nh * tq
    pos = i * tq + lax.broadcasted_iota(jnp.int32, (tq, 1), 0)
    q = jnp.concatenate([r[...] for r in (q0_ref, q1_ref, q2_ref, q3_ref, q4_ref, q5_ref)], axis=0)

    cidx = lax.broadcasted_iota(jnp.int32, (tq, N_CMP_PAD), 1)
    valid_c = (cidx * NSA_CMP_STRIDE + (NSA_CMP_LEN - 1) <= pos) & (cidx < N_CMP_PAD - 1)
    valid_cf = valid_c.astype(F32)
    cbias = jnp.where(valid_c, 0.0, NEG_INF).astype(F32)
    s = (_qk(q, kc_ref[0]).reshape(nh, tq, N_CMP_PAD) + cbias[None]).reshape(rows, N_CMP_PAD)
    p = (jnp.exp2(s - jnp.max(s, axis=-1, keepdims=True)).reshape(nh, tq, N_CMP_PAD) * valid_cf[None])
    p = p.reshape(rows, N_CMP_PAD)
    l = jnp.sum(p, axis=-1, keepdims=True)
    p = p / jnp.where(l > 0.0, l, 1.0)
    o_cmp = jnp.dot(p.astype(BF16), vc_ref[0], preferred_element_type=F32)
    psum = jnp.sum(p.reshape(nh, tq, N_CMP_PAD), axis=0)

    cover_t = cover_ref[...]
    imp = sum(jnp.dot(cover_t, part, preferred_element_type=F32) for part in _split3(psum.T))

    pos_t = i * tq + lax.broadcasted_iota(jnp.int32, (N_SEL_BLOCKS, tq), 1)
    jidx = lax.broadcasted_iota(jnp.int32, (N_SEL_BLOCKS, tq), 0)
    qblk = pos_t // NSA_SEL_BLOCK
    valid_s = jidx * NSA_SEL_BLOCK <= pos_t
    forced = (jidx == 0) | (jidx == qblk) | (jidx == qblk - 1)
    score = jnp.where(valid_s, jnp.where(forced, NSA_FORCE_SCORE, imp), -1.0)
    rank = jnp.zeros((N_SEL_BLOCKS, tq), jnp.int32)
    for ii in range(N_SEL_BLOCKS):
        si = score[ii:ii + 1, :]
        rank = rank + ((si > score) | ((si == score) & (ii < jidx))).astype(jnp.int32)
    sel_t = (rank < NSA_TOP_N).astype(F32)
    sel = jnp.concatenate([sel_t, jnp.zeros((LANES - N_SEL_BLOCKS, tq), F32)], axis=0).T.astype(BF16)

    n_causal = per * (i + 1)

    def fill(j, carry):
        hit = jnp.dot(sel, expand_ref[j], preferred_element_type=F32)
        kpos = j * tk + lax.broadcasted_iota(jnp.int32, (tq, tk), 1)
        bias_scr[j] = jnp.where((hit > 0.5) & (kpos <= pos), 0.0, NEG_INF).astype(F32)
        return carry

    lax.fori_loop(0, n_causal, fill, 0)

    o_slc = _attend(q, ks_ref, vs_ref, lambda j: bias_scr[j], 0, n_causal, s_scr, tk=tk, groups=nh)
    far = wbias_ref.shape[0] - per
    w_lo = jnp.maximum(per * i - far, 0)
    o_win = _attend(q, kw_ref, vw_ref, lambda j: wbias_ref[per * i - j + per - 1], w_lo, n_causal - w_lo, s_scr,
                    tk=tk, groups=nh)

    gates = jax.nn.sigmoid(gate_ref[...].astype(F32))
    for h in range(nh):
        sl = slice(h * tq, (h + 1) * tq)
        g0 = NSA_BRANCHES * h
        out = (gates[:, g0:g0 + 1] * o_cmp[sl] + gates[:, g0 + 1:g0 + 2] * o_slc[sl]
               + gates[:, g0 + 2:g0 + 3] * o_win[sl])
        o_ref[:, h * LANES:(h + 1) * LANES] = out.astype(o_ref.dtype)


def _nsa(hproj, kc, vc, cover, expand, wbias, nb):
    t = NSA_TQ
    nq = SEQ // t
    nk = SEQ // NSA_T
    wide = NSA_HEADS * HEAD_DIM
    q_spec = lambda h: pl.BlockSpec((t, LANES), lambda b, i: (b * nq + i, BLK_NQ + h))
    kv_spec = lambda blk: pl.BlockSpec((SEQ, LANES), lambda b, i: (b, blk))
    cspec = pl.BlockSpec((1, N_CMP_PAD, HEAD_DIM), lambda b, i: (b, 0, 0))
    full = lambda shape: pl.BlockSpec(shape, lambda b, i: (0,) * len(shape))
    return pl.pallas_call(
        _nsa_kernel,
        out_shape=jax.ShapeDtypeStruct((nb * SEQ, wide), BF16),
        grid=(nb, nq),
        in_specs=[q_spec(h) for h in range(NSA_HEADS)]
        + [cspec, cspec, kv_spec(BLK_NKS), kv_spec(BLK_NVS), kv_spec(BLK_NKW), kv_spec(BLK_NVW),
           pl.BlockSpec((t, LANES), lambda b, i: (b * nq + i, BLK_GATE)),
           full(cover.shape), full(expand.shape), full(wbias.shape)],
        out_specs=pl.BlockSpec((t, wide), lambda b, i: (b * nq + i, 0)),
        scratch_shapes=[pltpu.VMEM((nk, t, NSA_T), F32), pltpu.VMEM((nk, NSA_HEADS * t, NSA_T), F32)],
        compiler_params=_cparams(("arbitrary", "arbitrary")),
        name="nsa",
    )(*([hproj] * NSA_HEADS), kc, vc, hproj, hproj, hproj, hproj, hproj, cover, expand, wbias)


def _layer_norm(y, g, b):
    mu = jnp.mean(y, axis=-1, keepdims=True)
    var = jnp.mean(jnp.square(y - mu), axis=-1, keepdims=True)
    return (y - mu) * lax.rsqrt(var + LN_EPS) * g + b


def _lane_min(x):
    return jnp.min(x, axis=-1, keepdims=True)


def _lane_max(x):
    return jnp.max(x, axis=-1, keepdims=True)


def _route(logits):
    lane = lax.broadcasted_iota(jnp.int32, logits.shape, 1)
    lane_f = lane.astype(F32)
    far = float(LANES)
    is_grp = lane < N_GROUPS
    lg = jnp.where(is_grp, logits, NEG_INF)
    eg = jnp.where(is_grp, jnp.exp(lg - _lane_max(lg)), 0.0)
    prob = eg / jnp.sum(eg, axis=-1, keepdims=True)
    p_g = _lane_max(prob)
    g_idx = _lane_min(jnp.where(is_grp & (prob == p_g), lane_f, far))
    e_lo = N_GROUPS + EXPERTS_PER_GROUP * g_idx
    in_grp = (lane_f >= e_lo) & (lane_f < e_lo + EXPERTS_PER_GROUP)
    le = jnp.where(in_grp, logits, NEG_INF)
    v1 = _lane_max(le)
    i1 = _lane_min(jnp.where(in_grp & (le == v1), lane_f, far))
    rest = in_grp & (lane_f != i1)
    le2 = jnp.where(rest, logits, NEG_INF)
    v2 = _lane_max(le2)
    i2 = _lane_min(jnp.where(rest & (le2 == v2), lane_f, far))
    e21 = jnp.exp(v2 - v1)
    den = 1.0 + e21
    gate1 = p_g * (1.0 / den)
    gate2 = p_g * (e21 / den)
    out = jnp.where(lane == 0, i1 - N_GROUPS, 0.0)
    out = jnp.where(lane == 1, i2 - N_GROUPS, out)
    out = jnp.where(lane == 2, gate1, out)
    out = jnp.where(lane == 3, gate2, out)
    return out


def _outproj_kernel(om_ref, od_ref, on_ref, w_ref, x_ref, g_ref, b_ref, wr_ref, x1_ref, route_ref):
    n_mla = MLA_HEADS * HEAD_DIM
    n_dil = DIL_HEADS * HEAD_DIM
    for hh in range(OUTPROJ_TM // ROW_TM):
        rows = slice(hh * ROW_TM, (hh + 1) * ROW_TM)
        mix = jnp.dot(om_ref[rows, :], w_ref[0, 0:n_mla, :], preferred_element_type=F32)
        mix = mix + jnp.dot(od_ref[rows, :], w_ref[0, n_mla:n_mla + n_dil, :], preferred_element_type=F32)
        mix = mix + jnp.dot(on_ref[rows, :], w_ref[0, n_mla + n_dil:, :], preferred_element_type=F32)
        x1 = _layer_norm(DN_ALPHA * x_ref[rows, :] + mix, g_ref[...], b_ref[...])
        x1_ref[rows, :] = x1
        xh = x1.astype(BF16)
        xm = (x1 - xh.astype(F32)).astype(BF16)
        both = jnp.dot(xh, wr_ref[...], preferred_element_type=F32)
        logits = both[:, 0:LANES] + (both[:, LANES:2 * LANES]
                                     + jnp.dot(xm, wr_ref[:, 0:LANES], preferred_element_type=F32))
        route_ref[rows, :] = _route(logits)


def _outproj(o_mla, o_dil, o_nsa, w_out_s, layer, x, g, b, wr3):
    t = x.shape[0]
    tm = OUTPROJ_TM
    full = lambda shape: pl.BlockSpec(shape, lambda i: (0,) * len(shape))
    row = lambda w: pl.BlockSpec((tm, w), lambda i: (i, 0))
    return pl.pallas_call(
        _outproj_kernel,
        out_shape=(jax.ShapeDtypeStruct((t, D_MODEL), F32), jax.ShapeDtypeStruct((t, LANES), F32)),
        grid=(t // tm,),
        in_specs=[row(o_mla.shape[1]), row(o_dil.shape[1]), row(o_nsa.shape[1]),
                  pl.BlockSpec((1,) + w_out_s.shape[1:], lambda i: (layer, 0, 0)),
                  row(D_MODEL), full((1, D_MODEL)), full((1, D_MODEL)), full(wr3.shape)],
        out_specs=(row(D_MODEL), row(LANES)),
        compiler_params=_cparams(("arbitrary",)),
        name="outproj",
    )(o_mla, o_dil, o_nsa, w_out_s, x, g, b, wr3)


def _dispatch_kernel(fill_ref, pos_ref, x1_ref, xb_hbm, stage, zbuf, sem, zsem):
    i = pl.program_id(0)
    n = pl.num_programs(0)
    tm = ROW_TM
    slot = i % 2

    def drain(s):
        for _ in range(MOE_TOP_K):
            pltpu.make_async_copy(stage.at[s], xb_hbm.at[pl.ds(0, tm)], sem.at[s]).wait()

    def zero_copy(e):
        start = pl.multiple_of(fill_ref[e], MOE_ROWS)
        return pltpu.make_async_copy(zbuf, xb_hbm.at[pl.ds(start, MOE_ROWS)], zsem)

    @pl.when(i == 0)
    def _():
        zbuf[...] = jnp.zeros_like(zbuf)
        for e in range(N_EXPERTS):
            @pl.when(fill_ref[e] >= 0)
            def _():
                zero_copy(e).start()
        for e in range(N_EXPERTS):
            @pl.when(fill_ref[e] >= 0)
            def _():
                zero_copy(e).wait()

        def tail_copy(b):
            start = pl.multiple_of(b * MOE_ROWS, MOE_ROWS)
            return pltpu.make_async_copy(zbuf, xb_hbm.at[pl.ds(start, MOE_ROWS)], zsem)

        n_blocks = xb_hbm.shape[0] // MOE_ROWS
        lax.fori_loop(fill_ref[N_EXPERTS], n_blocks, lambda b, c: (tail_copy(b).start(), c)[1], 0)
        lax.fori_loop(fill_ref[N_EXPERTS], n_blocks, lambda b, c: (tail_copy(b).wait(), c)[1], 0)

    @pl.when(i >= 2)
    def _():
        drain(slot)

    stage[slot] = x1_ref[...]

    for r in range(tm):
        for k in range(MOE_TOP_K):
            p = pos_ref[0, 0, MOE_TOP_K * r + k]
            pltpu.make_async_copy(stage.at[slot, pl.ds(r, 1)], xb_hbm.at[pl.ds(p, 1)],
                                  sem.at[slot]).start(priority=k % 2)

    @pl.when(i == n - 1)
    def _():
        drain(1 - slot)
        drain(slot)


def _dispatch(fill, pos3, x1, n_rows):
    t = x1.shape[0]
    tm = ROW_TM
    grid_spec = pltpu.PrefetchScalarGridSpec(
        num_scalar_prefetch=1,
        grid=(t // tm,),
        in_specs=[pl.BlockSpec((1, 1, MOE_TOP_K * tm), lambda i, fl: (i, 0, 0), memory_space=pltpu.SMEM),
                  pl.BlockSpec((tm, D_MODEL), lambda i, fl: (i, 0))],
        out_specs=pl.BlockSpec(memory_space=pl.ANY),
        scratch_shapes=[pltpu.VMEM((2, tm, D_MODEL), F32), pltpu.VMEM((MOE_ROWS, D_MODEL), F32),
                        pltpu.SemaphoreType.DMA((2,)), pltpu.SemaphoreType.DMA(())],
    )
    return pl.pallas_call(
        _dispatch_kernel,
        out_shape=jax.ShapeDtypeStruct((n_rows, D_MODEL), F32),
        grid_spec=grid_spec,
        compiler_params=_cparams(("arbitrary",)),
        name="dispatch",
    )(fill, pos3, x1)


def _experts_kernel(blk_ex_ref, first_ref, nxt_ref, nused_ref, x_ref, wg_hbm, wu_hbm, wd_hbm, y_ref,
                    wg_f, wu_f, wd_f, wg_b, wu_b, wd_b, sem, *, layer):
    b = pl.program_id(0)
    nused = nused_ref[0]

    def weight_copies(e, s):
        return (pltpu.make_async_copy(wg_hbm.at[layer, e], wg_f.at[s], sem.at[s]),
                pltpu.make_async_copy(wu_hbm.at[layer, e], wu_f.at[s], sem.at[s]),
                pltpu.make_async_copy(wd_hbm.at[layer, e], wd_f.at[s], sem.at[s]))

    @pl.when(b == 0)
    def _():
        for cp in weight_copies(blk_ex_ref[0], 0):
            cp.start()

    @pl.when(b < nused)
    def _():
        first = first_ref[b]

        @pl.when(first > 0)
        def _():
            s = first - 1
            for cp in weight_copies(blk_ex_ref[b], s):
                cp.wait()

            @pl.when(nxt_ref[b] >= 0)
            def _():
                for cp in weight_copies(nxt_ref[b], 1 - s):
                    cp.start()

            wg_b[...] = wg_f[s].astype(BF16)
            wu_b[...] = wu_f[s].astype(BF16)
            wd_b[...] = wd_f[s].astype(BF16)

        rows = x_ref[...].astype(BF16)
        gate = jnp.dot(rows, wg_b[...], preferred_element_type=F32)
        up = jnp.dot(rows, wu_b[...], preferred_element_type=F32)
        hid = (gate * jax.nn.sigmoid(gate) * up).astype(BF16)
        y_ref[...] = jnp.dot(hid, wd_b[...], preferred_element_type=F32)

    @pl.when(b >= nused)
    def _():
        y_ref[...] = jnp.zeros_like(y_ref)


def _experts(blk_ex, first, nxt, nused, xb_rows, w_gate, w_up, w_down, layer):
    n_blocks = xb_rows.shape[0] // MOE_ROWS
    hbm = pl.BlockSpec(memory_space=pl.ANY)
    grid_spec = pltpu.PrefetchScalarGridSpec(
        num_scalar_prefetch=4,
        grid=(n_blocks,),
        in_specs=[pl.BlockSpec((MOE_ROWS, D_MODEL), lambda b, ex, fi, nx, nu: (jnp.minimum(b, nu[0] - 1), 0)),
                  hbm, hbm, hbm],
        out_specs=pl.BlockSpec((MOE_ROWS, D_MODEL), lambda b, ex, fi, nx, nu: (b, 0)),
        scratch_shapes=[pltpu.VMEM((2, D_MODEL, EXPERT_FF), F32), pltpu.VMEM((2, D_MODEL, EXPERT_FF), F32),
                        pltpu.VMEM((2, EXPERT_FF, D_MODEL), F32),
                        pltpu.VMEM((D_MODEL, EXPERT_FF), BF16), pltpu.VMEM((D_MODEL, EXPERT_FF), BF16),
                        pltpu.VMEM((EXPERT_FF, D_MODEL), BF16), pltpu.SemaphoreType.DMA((2,))],
    )
    return pl.pallas_call(
        functools.partial(_experts_kernel, layer=layer),
        out_shape=jax.ShapeDtypeStruct((n_blocks * MOE_ROWS, D_MODEL), F32),
        grid_spec=grid_spec,
        compiler_params=_cparams(("arbitrary",)),
        name="experts",
    )(blk_ex, first, nxt, nused, xb_rows, w_gate, w_up, w_down)


def _combine_kernel(pos_cur_ref, pos_nxt_ref, y_hbm, x1_ref, route_ref, g_ref, b_ref, o_ref, ob_ref, ybuf, sem):
    i = pl.program_id(0)
    n = pl.num_programs(0)
    tm = ROW_TM

    def issue(pos_ref, slot):
        for r in range(tm):
            for k in range(MOE_TOP_K):
                p = pos_ref[0, 0, MOE_TOP_K * r + k]
                pltpu.make_async_copy(y_hbm.at[pl.ds(p, 1)], ybuf.at[slot, k, pl.ds(r, 1)],
                                      sem.at[slot]).start(priority=k % 2)

    @pl.when(i == 0)
    def _():
        issue(pos_cur_ref, 0)

    @pl.when(i + 1 < n)
    def _():
        issue(pos_nxt_ref, (i + 1) % 2)

    slot = i % 2
    for k in range(MOE_TOP_K):
        pltpu.make_async_copy(y_hbm.at[pl.ds(0, tm)], ybuf.at[slot, k], sem.at[slot]).wait()
    route = route_ref[...]
    ffn = route[:, 2:3] * ybuf[slot, 0] + route[:, 3:4] * ybuf[slot, 1]
    x2 = _layer_norm(DN_ALPHA * x1_ref[...] + ffn, g_ref[...], b_ref[...])
    o_ref[...] = x2
    ob_ref[...] = x2.astype(ob_ref.dtype)


def _combine(pos3, yb, x1, route, g, b):
    t = x1.shape[0]
    tm = ROW_TM
    nt = t // tm
    pos_spec = lambda shift: pl.BlockSpec((1, 1, MOE_TOP_K * tm), lambda i: (jnp.minimum(i + shift, nt - 1), 0, 0),
                                          memory_space=pltpu.SMEM)
    row = lambda w: pl.BlockSpec((tm, w), lambda i: (i, 0))
    full = lambda shape: pl.BlockSpec(shape, lambda i: (0,) * len(shape))
    return pl.pallas_call(
        _combine_kernel,
        out_shape=(jax.ShapeDtypeStruct((t, D_MODEL), F32), jax.ShapeDtypeStruct((t, D_MODEL), BF16)),
        grid=(nt,),
        in_specs=[pos_spec(0), pos_spec(1), pl.BlockSpec(memory_space=pl.ANY), row(D_MODEL), row(LANES),
                  full((1, D_MODEL)), full((1, D_MODEL))],
        out_specs=(row(D_MODEL), row(D_MODEL)),
        scratch_shapes=[pltpu.VMEM((2, MOE_TOP_K, tm, D_MODEL), F32), pltpu.SemaphoreType.DMA((2,))],
        compiler_params=_cparams(("arbitrary",)),
        name="combine",
    )(pos3, pos3, yb, x1, route, g, b)


def _dispatch_plan(route, t):
    m = t * MOE_TOP_K
    n_blocks = -(-m // MOE_ROWS) + N_EXPERTS
    chunk = MOE_ROWS
    ex = route[:, 0:MOE_TOP_K].reshape(m // chunk, chunk, 1)
    onehot = (ex == jnp.arange(N_EXPERTS, dtype=F32)[None, None, :]).astype(F32)
    tri = jnp.asarray(np.tril(np.ones((chunk, chunk), np.float32)))
    within = jnp.einsum('ij,cje->cie', tri, onehot)
    totals = jnp.sum(onehot, axis=1)
    before = jnp.cumsum(totals, axis=0) - totals
    counts = jnp.sum(totals, axis=0)
    padded = jnp.ceil(counts / MOE_ROWS) * MOE_ROWS
    pad_end = jnp.cumsum(padded)
    pad_start = pad_end - padded
    row = jnp.sum(onehot * (within - 1.0 + before[:, None, :] + pad_start[None, None, :]), axis=-1)
    pos = row.astype(jnp.int32).reshape(m)
    blk_start = jnp.arange(n_blocks, dtype=F32) * MOE_ROWS
    blk_ex = jnp.minimum(jnp.sum((pad_end[None, :] <= blk_start[:, None]).astype(jnp.int32), axis=1),
                         N_EXPERTS - 1).astype(jnp.int32)
    nused = (pad_end[-1] / MOE_ROWS).astype(jnp.int32).reshape(1)
    fill = jnp.concatenate([jnp.where(padded > 0, pad_end - MOE_ROWS, -1.0).astype(jnp.int32), nused])
    nonempty = padded > 0
    slot_e = (jnp.cumsum(nonempty.astype(jnp.int32)) - 1) % 2
    expert_ids = jnp.arange(N_EXPERTS, dtype=jnp.int32)
    later = lax.cummin(jnp.where(nonempty, expert_ids, N_EXPERTS), axis=0, reverse=True)
    nxt_e = jnp.concatenate([later[1:], jnp.full((1,), N_EXPERTS, jnp.int32)])
    nxt_e = jnp.where(nxt_e >= N_EXPERTS, -1, nxt_e)
    is_first = (blk_start == pad_start[blk_ex]) & (jnp.arange(n_blocks) < nused[0])
    first = jnp.where(is_first, 1 + slot_e[blk_ex], 0).astype(jnp.int32)
    return blk_ex, first, nxt_e[blk_ex].astype(jnp.int32), nused, fill, pos


W_HALF = LANES // 2


def _shift_kernel(a_ref, b_ref, tail_ref, o_ref):
    m = pl.program_id(0)
    n_src = IN_COLS // LANES
    nl = o_ref.shape[0]
    d = o_ref.shape[2]
    nc = d // LANES
    a2, b2, t2 = a_ref, b_ref, tail_ref

    def slab(ref4, col0, c, l):
        return ref4[pl.ds(col0, W_HALF), c, l, :].astype(o_ref.dtype)

    def copy(ref_top, top0, ref_bot, bot0):
        for l in range(nl):
            for c in range(nc):
                o_ref[l, 0:W_HALF, c * LANES:(c + 1) * LANES] = slab(ref_top, top0, c, l)
                if ref_bot is None:
                    o_ref[l, W_HALF:LANES, c * LANES:(c + 1) * LANES] = jnp.zeros((W_HALF, LANES), o_ref.dtype)
                else:
                    o_ref[l, W_HALF:LANES, c * LANES:(c + 1) * LANES] = slab(ref_bot, bot0, c, l)

    @pl.when(m < n_src)
    def _():
        copy(a2, 0, b2, 0)

    @pl.when(m == n_src)
    def _():
        copy(a2, 0, t2, 0)

    @pl.when(m == n_src + 1)
    def _():
        copy(t2, W_HALF, None, 0)


def _shift_w_in(w_in):
    nl, d, _ = w_in.shape
    n_src = IN_COLS // LANES
    n_keep = (IN_SHIFT_FROM + MLA_ROPE_DIM) // LANES
    last = n_src * (LANES // W_HALF) - 1
    nc = d // LANES
    view = lambda w: jnp.transpose(w.reshape(nl, nc, LANES, w.shape[-1]), (3, 1, 0, 2))
    w_t = view(w_in)
    tail_t = view(jnp.pad(w_in[:, :, n_src * LANES:], ((0, 0), (0, 0), (0, (n_src + 1) * LANES - IN_COLS))))
    first = lambda m: jnp.where(m < n_keep, 2 * m, 2 * m - 1)
    return pl.pallas_call(
        _shift_kernel,
        out_shape=jax.ShapeDtypeStruct((nl, IN_COLS_PAD, d), BF16),
        grid=(N_IN_BLOCKS,),
        in_specs=[pl.BlockSpec((W_HALF, nc, nl, LANES), lambda m: (jnp.minimum(first(m), last), 0, 0, 0)),
                  pl.BlockSpec((W_HALF, nc, nl, LANES), lambda m: (jnp.minimum(first(m) + 1, last), 0, 0, 0)),
                  pl.BlockSpec((LANES, nc, nl, LANES), lambda m: (0, 0, 0, 0))],
        out_specs=pl.BlockSpec((nl, LANES, d), lambda m: (0, m, 0)),
        compiler_params=_cparams(("arbitrary",)),
        name="shiftw",
    )(w_t, w_t, tail_t)


def _permute_w_q_up(w_q_up):
    w = w_q_up.reshape(MLA_Q_RANK, MLA_HEADS, MLA_NOPE_DIM + MLA_ROPE_DIM)
    zero = jnp.zeros((MLA_Q_RANK, MLA_HEADS, MLA_QK_PAD - MLA_NOPE_DIM - MLA_ROPE_DIM), w.dtype)
    w = jnp.concatenate([w[:, :, :MLA_NOPE_DIM], zero, w[:, :, MLA_NOPE_DIM:]], axis=-1)
    return w.reshape(MLA_Q_RANK, MLA_HEADS * MLA_QK_PAD).astype(BF16)


def _split_w_kv_up(w_kv_up):
    w = w_kv_up.reshape(MLA_KV_RANK, MLA_HEADS, MLA_NOPE_DIM + HEAD_DIM)
    wkn = w[:, :, :MLA_NOPE_DIM].reshape(MLA_KV_RANK, MLA_HEADS * MLA_NOPE_DIM)
    wv = w[:, :, MLA_NOPE_DIM:].reshape(MLA_KV_RANK, MLA_HEADS * HEAD_DIM)
    return wkn.astype(BF16), wv.astype(BF16)


def _router_slices(w_grp, w_exp):
    w = jnp.concatenate([w_grp, w_exp, jnp.zeros((D_MODEL, LANES - N_GROUPS - N_EXPERTS), F32)], axis=1)
    hi = w.astype(BF16)
    mid = (w - hi.astype(F32)).astype(BF16)
    return jnp.concatenate([hi, mid], axis=1)


def _mixer(xb, nb, layer, w_in_s, q_lat_norm, w_q_up, kv_lat_norm, w_kv_up,
           cmp_pos_k, cmp_w1_k, cmp_w2_k, cmp_pos_v, cmp_w1_v, cmp_w2_v, consts):
    hproj = _inproj(xb, w_in_s, layer, consts["rope32"])
    wkn, wv = _split_w_kv_up(w_kv_up)
    q_mla, k_mla, v_mla = _mlaprep(hproj, q_lat_norm.reshape(1, -1), kv_lat_norm.reshape(1, -1),
                                   _permute_w_q_up(w_q_up), wkn, wv, consts["rope64"])
    wide = NSA_CMP_STRIDE * HEAD_DIM
    tk2 = hproj[:, BLK_NKC * LANES:(BLK_NKC + 1) * LANES].reshape(nb, N_CMP_PAD, wide)
    tv2 = hproj[:, BLK_NVC * LANES:(BLK_NVC + 1) * LANES].reshape(nb, N_CMP_PAD, wide)
    kc, vc = _compress(tk2, tv2, cmp_pos_k.reshape(2, wide), cmp_pos_v.reshape(2, wide),
                       cmp_w1_k.astype(BF16), cmp_w2_k.astype(BF16), cmp_w1_v.astype(BF16), cmp_w2_v.astype(BF16),
                       consts["rope_cmp"])
    o_mla = _flash(q_mla, k_mla, v_mla, consts["bias_causal"], nb=nb, heads=MLA_HEADS, dqk=MLA_QK_PAD,
                   dv=HEAD_DIM, q_blk0=0, k_blk0=0, v_blk0=0)
    o_dil = _flash(hproj, hproj, hproj, consts["bias_dil"], nb=nb, heads=DIL_HEADS, dqk=HEAD_DIM, dv=HEAD_DIM,
                   q_blk0=BLK_DQ, k_blk0=BLK_DK, v_blk0=BLK_DV)
    o_nsa = _nsa(hproj, kc, vc, consts["cover"], consts["expand"], consts["bias_win"], nb)
    return o_mla, o_dil, o_nsa


def _make_consts():
    pos = np.arange(SEQ)
    slot = np.arange(N_CMP_PAD)
    cmp_start = slot * NSA_CMP_STRIDE
    cmp_end = cmp_start + NSA_CMP_LEN - 1
    sel_start = np.arange(N_SEL_BLOCKS) * NSA_SEL_BLOCK
    cover = ((cmp_start[:, None] < sel_start[None, :] + NSA_SEL_BLOCK)
             & (cmp_start[:, None] + NSA_CMP_LEN > sel_start[None, :])
             & (slot[:, None] < N_CMP_PAD - 1)).astype(np.float32)
    expand = (np.arange(LANES)[:, None] == (pos // NSA_SEL_BLOCK)[None, :]).astype(np.float32)
    expand = expand.reshape(LANES, SEQ // NSA_T, NSA_T).transpose(1, 0, 2)
    win_diags = range(-(NSA_TQ // NSA_T - 1), -(-NSA_WINDOW // NSA_T) + 1)
    return {
        "rope32": _rope_tables(pos, ROT_DIM, True),
        "rope64": _rope_tables(pos, MLA_ROPE_DIM, False, KROPE_LANE0),
        "rope_cmp": _rope_tables(cmp_end, ROT_DIM, True),
        "bias_causal": _distance_bias(_causal_mult, (0, 1), FLASH_T),
        "bias_dil": _distance_bias(_dilated_mult, range(SEQ // FLASH_T), FLASH_T),
        "bias_win": _distance_bias(_window_mult, win_diags, NSA_T, NSA_TQ),
        "cover": jnp.asarray(cover.T, BF16),
        "expand": jnp.asarray(expand, BF16),
    }


def kernel(x, w_in, q_lat_norm, w_q_up, kv_lat_norm, w_kv_up, cmp_pos_k, cmp_w1_k, cmp_w2_k, cmp_pos_v, cmp_w1_v,
           cmp_w2_v, w_out, ln1_g, ln1_b, w_grp, w_exp, w_gate, w_up, w_down, ln2_g, ln2_b):
    nb, s, d = x.shape
    assert s == SEQ and d == D_MODEL
    t = nb * s
    consts = _make_consts()
    w_in_s = _shift_w_in(w_in)
    w_out_s = w_out.astype(BF16)
    xf = x.reshape(t, d)
    xb = xf.astype(BF16)
    for l in range(DEPTH):
        o_mla, o_dil, o_nsa = _mixer(xb, nb, l, w_in_s, q_lat_norm[l], w_q_up[l], kv_lat_norm[l], w_kv_up[l],
                                     cmp_pos_k[l], cmp_w1_k[l], cmp_w2_k[l], cmp_pos_v[l], cmp_w1_v[l],
                                     cmp_w2_v[l], consts)
        x1, route = _outproj(o_mla, o_dil, o_nsa, w_out_s, l, xf, ln1_g[l].reshape(1, d),
                             ln1_b[l].reshape(1, d), _router_slices(w_grp[l], w_exp[l]))
        blk_ex, first, nxt, nused, fill, pos = _dispatch_plan(route, t)
        pos3 = pos.reshape(t // ROW_TM, 1, MOE_TOP_K * ROW_TM)
        n_rows = (-(-t * MOE_TOP_K // MOE_ROWS) + N_EXPERTS) * MOE_ROWS
        yb = _experts(blk_ex, first, nxt, nused, _dispatch(fill, pos3, x1, n_rows), w_gate, w_up, w_down, l)
        xf, xb = _combine(pos3, yb, x1, route,
                          ln2_g[l].reshape(1, d), ln2_b[l].reshape(1, d))
    return xf.reshape(nb, s, d)
```
